```python
import jax, jax.numpy as jnp
from jax import lax
import numpy as np

D_MODEL = 4096
BATCH = 4
SEQ = 2048
DEPTH = 1

MEM_LEN = 256
MIX_WIDTH = D_MODEL
RET_WIDTH = MIX_WIDTH // 2
RET_HEADS = 8
RET_DH = RET_WIDTH // RET_HEADS
RET_CHUNK = 128
ROPE_BASE = 10000.0
GM_WIDTH = MIX_WIDTH - RET_WIDTH
GM_GROUPS = 8
GM_CG = GM_WIDTH // GM_GROUPS
GM_CHUNK = 128
XA_HEADS = 4
XA_DH = 256
XA_WIDTH = XA_HEADS * XA_DH
MOE_GROUPS = 8
MOE_PER_GROUP = 8
MOE_EXPERTS = MOE_GROUPS * MOE_PER_GROUP
MOE_TOPK = 2
MOE_FF = 512
MOE_BLOCK = 128
NORM_EPS = 1e-6
IN_COLS = 4 * RET_WIDTH + 2 * GM_WIDTH

kernel_name = "hybrid_retention_gmlp_hmoe_layer"


def standardize(x):
    xf = x.astype(jnp.float32)
    mu = jnp.mean(xf, axis=-1, keepdims=True)
    var = jnp.mean(jnp.square(xf - mu), axis=-1, keepdims=True)
    return ((xf - mu) * lax.rsqrt(var + NORM_EPS)).astype(x.dtype)


def layer_norm(x, g, b):
    return standardize(x) * g + b


def rms_norm(x, g):
    xf = x.astype(jnp.float32)
    xf = xf * lax.rsqrt(jnp.mean(xf * xf, axis=-1, keepdims=True) + NORM_EPS)
    return xf.astype(x.dtype) * g


def rotary(t, cos, sin):
    half = t.shape[-1] // 2
    t1, t2 = t[..., :half], t[..., half:]
    return jnp.concatenate([t1 * cos - t2 * sin, t1 * sin + t2 * cos], axis=-1)


def retention_chunkwise(q, k, v):
    B, S, H, dk = q.shape
    dv = v.shape[-1]
    C = RET_CHUNK
    NC = S // C
    dt = q.dtype
    log_gamma = jnp.log(1.0 - jnp.exp2(-5.0 - jnp.arange(H, dtype=jnp.float32)))
    pos = jnp.arange(C, dtype=jnp.float32)
    diff = pos[:, None] - pos[None, :]
    decay = jnp.where(diff >= 0, jnp.exp(jnp.maximum(diff, 0.0) * log_gamma[:, None, None]), 0.0).astype(dt)
    q_decay = jnp.exp((pos + 1.0) * log_gamma[:, None]).astype(dt)
    k_decay = jnp.exp((C - 1.0 - pos) * log_gamma[:, None]).astype(dt)
    chunk_decay = jnp.exp(C * log_gamma).astype(dt)

    def to_chunks(t):
        return t.reshape(B, NC, C, H, t.shape[-1]).transpose(1, 0, 3, 2, 4)

    qc, kc, vc = to_chunks(q), to_chunks(k), to_chunks(v)
    inner = jnp.einsum('nbhqd,nbhkd->nbhqk', qc, kc) * decay
    inner = jnp.einsum('nbhqk,nbhke->nbhqe', inner, vc)

    def step(state, chunk):
        q_i, k_i, v_i = chunk
        cross = jnp.einsum('bhqd,bhde->bhqe', q_i * q_decay[None, :, :, None], state)
        state = state * chunk_decay[None, :, None, None] + jnp.einsum(
            'bhkd,bhke->bhde', k_i * k_decay[None, :, :, None], v_i)
        return state, cross

    state0 = jnp.zeros((B, H, dk, dv), dt)
    _, cross = lax.scan(step, state0, (qc, kc, vc))
    out = inner + cross
    return out.transpose(1, 0, 3, 2, 4).reshape(B, S, H, dv)


def chunked_gmlp(u, v, ln_g, ln_b, ws, bs):
    B, S, W = v.shape
    C = GM_CHUNK
    NC = S // C
    u = jax.nn.gelu(u)
    v = layer_norm(jax.nn.gelu(v), ln_g, ln_b)
    mask = jnp.tril(jnp.ones((C, C), dtype=bool))
    w = jnp.where(mask[None], ws, 0.0)
    vc = v.reshape(B, NC, C, GM_GROUPS, GM_CG)
    spatial = jnp.einsum('gts,bnsgc->bntgc', w, vc) + bs.T[None, None, :, :, None]
    return u * spatial.reshape(B, S, W)


def token_mixer(xn, cos, sin, w_in, ret_norm_g, gm_ln_g, gm_ln_b, gm_ws, gm_bs, w_out):
    B, S, _ = xn.shape
    R, G = RET_WIDTH, GM_WIDTH
    q, k, v, g, u, vs = jnp.split(xn @ w_in, [R, 2 * R, 3 * R, 4 * R, 4 * R + G], axis=-1)

    def heads(t):
        return t.reshape(B, S, RET_HEADS, RET_DH)

    q = rotary(heads(q), cos, sin)
    k = rotary(heads(k), cos, sin) * (RET_DH ** -0.5)
    ret = retention_chunkwise(q, k, heads(v))
    ret = standardize(ret) * ret_norm_g.reshape(RET_HEADS, RET_DH)
    ret = jax.nn.silu(g) * ret.reshape(B, S, R)
    gm = chunked_gmlp(u, vs, gm_ln_g, gm_ln_b, gm_ws, gm_bs)
    return jnp.concatenate([ret, gm], axis=-1) @ w_out


def memory_cross_attention(hn, mn, wq, wkv, wo):
    B, S, _ = hn.shape
    M = mn.shape[1]
    q = (hn @ wq).reshape(B, S, XA_HEADS, XA_DH)
    k, v = jnp.split(mn @ wkv, 2, axis=-1)
    k = k.reshape(B, M, XA_HEADS, XA_DH)
    v = v.reshape(B, M, XA_HEADS, XA_DH)
    s = jnp.einsum('bshd,bmhd->bhsm', q, k).astype(jnp.float32) * (XA_DH ** -0.5)
    p = jax.nn.softmax(s, axis=-1).astype(v.dtype)
    o = jnp.einsum('bhsm,bmhd->bshd', p, v).reshape(B, S, XA_WIDTH)
    return o @ wo


def hierarchical_moe(xn, grp_w, grp_b, exp_w, exp_b, w_gate, w_up, w_down):
    B, S, D = xn.shape
    N = B * S
    dt = xn.dtype
    xt = xn.reshape(N, D)
    grp_prob = jax.nn.softmax((xt @ grp_w + grp_b).astype(jnp.float32), axis=-1)
    grp_idx = jnp.argmax(grp_prob, axis=-1)
    grp_gate = jnp.take_along_axis(grp_prob, grp_idx[:, None], axis=-1)
    exp_logits = (xt @ exp_w + exp_b).astype(jnp.float32).reshape(N, MOE_GROUPS, MOE_PER_GROUP)
    in_group = jnp.take_along_axis(exp_logits, grp_idx[:, None, None], axis=1)[:, 0]
    top_val, top_idx = lax.top_k(in_group, MOE_TOPK)
    gates = (jax.nn.softmax(top_val, axis=-1) * grp_gate).astype(dt)
    expert_id = (grp_idx[:, None] * MOE_PER_GROUP + top_idx).astype(jnp.int32)

    A = N * MOE_TOPK
    e_flat = expert_id.reshape(A)
    t_flat = jnp.repeat(jnp.arange(N, dtype=jnp.int32), MOE_TOPK)
    g_flat = gates.reshape(A)
    order = jnp.argsort(e_flat)
    e_s, t_s, g_s = e_flat[order], t_flat[order], g_flat[order]
    counts = jax.ops.segment_sum(jnp.ones((A,), jnp.int32), e_flat, num_segments=MOE_EXPERTS)
    starts = jnp.cumsum(counts) - counts
    padded = (counts + MOE_BLOCK - 1) // MOE_BLOCK * MOE_BLOCK
    pad_end = jnp.cumsum(padded)
    pad_start = pad_end - padded
    dest = pad_start[e_s] + (jnp.arange(A, dtype=jnp.int32) - starts[e_s])
    n_blocks = -(-A // MOE_BLOCK) + MOE_EXPERTS
    n_rows = n_blocks * MOE_BLOCK
    row_tok = jnp.zeros((n_rows,), jnp.int32).at[dest].set(t_s)
    row_gate = jnp.zeros((n_rows,), dt).at[dest].set(g_s)
    block_start = jnp.arange(n_blocks, dtype=jnp.int32) * MOE_BLOCK
    block_exp = jnp.minimum(jnp.searchsorted(pad_end, block_start, side='right'), MOE_EXPERTS - 1)

    def expert_block(args):
        e, toks, g = args
        xb = xt[toks]
        hdn = jax.nn.silu(xb @ w_gate[e]) * (xb @ w_up[e])
        return (hdn @ w_down[e]) * g[:, None]

    y = lax.map(expert_block, (block_exp, row_tok.reshape(n_blocks, MOE_BLOCK),
                               row_gate.reshape(n_blocks, MOE_BLOCK)))
    out = jnp.zeros((N, D), y.dtype).at[row_tok].add(y.reshape(n_rows, D))
    return out.reshape(B, S, D)


def setup_inputs(seed: int = 0) -> dict:
    key = jax.random.key(seed)
    ks = jax.random.split(key, 25)

    def nrm(k, shape, scale):
        return jax.random.normal(k, shape, jnp.float32) * scale

    def gain(k, shape):
        return 1.0 + 0.1 * jax.random.normal(k, shape, jnp.float32)

    D = D_MODEL
    positions = (jax.random.randint(ks[2], (BATCH, 1), 0, 4096, dtype=jnp.int32)
                 + jnp.arange(SEQ, dtype=jnp.int32)[None, :])
    return {
        "x": nrm(ks[0], (BATCH, SEQ, D), 1.0),
        "mem": nrm(ks[1], (BATCH, MEM_LEN, D), 1.0),
        "positions": positions,
        "norm_mix_g": gain(ks[3], (D,)),
        "w_in": nrm(ks[4], (D, IN_COLS), D ** -0.5),
        "ret_norm_g": gain(ks[5], (RET_WIDTH,)),
        "gm_ln_g": gain(ks[6], (GM_WIDTH,)),
        "gm_ln_b": nrm(ks[7], (GM_WIDTH,), 0.02),
        "gm_ws": nrm(ks[8], (GM_GROUPS, GM_CHUNK, GM_CHUNK), GM_CHUNK ** -0.5),
        "gm_bs": gain(ks[9], (GM_GROUPS, GM_CHUNK)),
        "w_out": nrm(ks[10], (MIX_WIDTH, D), MIX_WIDTH ** -0.5),
        "norm_xa_g": gain(ks[11], (D,)),
        "norm_mem_g": gain(ks[12], (D,)),
        "xa_wq": nrm(ks[13], (D, XA_WIDTH), D ** -0.5),
        "xa_wkv": nrm(ks[14], (D, 2 * XA_WIDTH), D ** -0.5),
        "xa_wo": nrm(ks[15], (XA_WIDTH, D), XA_WIDTH ** -0.5),
        "norm_moe_g": gain(ks[16], (D,)),
        "router_grp_w": nrm(ks[17], (D, MOE_GROUPS), D ** -0.5),
        "router_grp_b": nrm(ks[18], (MOE_GROUPS,), 0.01),
        "router_exp_w": nrm(ks[19], (D, MOE_EXPERTS), D ** -0.5),
        "router_exp_b": nrm(ks[20], (MOE_EXPERTS,), 0.01),
        "moe_w_gate": nrm(ks[21], (MOE_EXPERTS, D, MOE_FF), D ** -0.5),
        "moe_w_up": nrm(ks[22], (MOE_EXPERTS, D, MOE_FF), D ** -0.5),
        "moe_w_down": nrm(ks[23], (MOE_EXPERTS, MOE_FF, D), MOE_FF ** -0.5),
        "norm_final_g": gain(ks[24], (D,)),
    }


def reference(x, mem, positions, norm_mix_g, w_in, ret_norm_g, gm_ln_g, gm_ln_b, gm_ws, gm_bs,
              w_out, norm_xa_g, norm_mem_g, xa_wq, xa_wkv, xa_wo, norm_moe_g,
              router_grp_w, router_grp_b, router_exp_w, router_exp_b,
              moe_w_gate, moe_w_up, moe_w_down, norm_final_g):
    inv_freq = ROPE_BASE ** (-jnp.arange(0, RET_DH, 2, dtype=jnp.float32) / RET_DH)
    ang = positions.astype(jnp.float32)[..., None] * inv_freq
    cos = jnp.cos(ang)[:, :, None, :].astype(x.dtype)
    sin = jnp.sin(ang)[:, :, None, :].astype(x.dtype)
    h = x
    for _layer in range(DEPTH):
        h = h + token_mixer(rms_norm(h, norm_mix_g), cos, sin, w_in, ret_norm_g,
                            gm_ln_g, gm_ln_b, gm_ws, gm_bs, w_out)
        h = h + memory_cross_attention(rms_norm(h, norm_xa_g), rms_norm(mem, norm_mem_g),
                                       xa_wq, xa_wkv, xa_wo)
        h = h + hierarchical_moe(rms_norm(h, norm_moe_g), router_grp_w, router_grp_b,
                                 router_exp_w, router_exp_b, moe_w_gate, moe_w_up, moe_w_down)
    return rms_norm(h, norm_final_g)
```

```python
import functools

import jax
import jax.numpy as jnp
from jax import lax
from jax.experimental import pallas as pl
from jax.experimental.pallas import tpu as pltpu

NORM_EPS = 1e-6
RET_HEADS = 8
ROPE_BASE = 10000.0
GM_GROUPS = 8
GM_CHUNK = 128
XA_HEADS = 4
MOE_GROUPS = 8
MOE_PER_GROUP = 8
MOE_TOPK = 2

V7X_LANES = 128
V7X_VMEM_BYTES = 64 * 1024 * 1024
VMEM_LIMIT_BYTES = 56 * 1024 * 1024

RET_BLOCK = 512
MOE_ROW_TILE = 256
MOE_FF_SPLIT = 2

F32 = jnp.float32
BF16 = jnp.bfloat16
U32 = jnp.uint32
I32 = jnp.int32


def _params(sem):
    return pltpu.CompilerParams(dimension_semantics=sem, vmem_limit_bytes=VMEM_LIMIT_BYTES)


def _tile(dim, target):
    t = min(dim, target)
    while dim % t:
        t -= V7X_LANES
    assert t > 0, (dim, target)
    return t


def _pack_halves(x_f32):
    k = x_f32.shape[-1] // 2
    bits = lax.bitcast_convert_type(x_f32.astype(BF16).astype(F32), U32)
    return (bits[:, k:] & jnp.uint32(0xFFFF0000)) | (bits[:, :k] >> 16)


def _unpack_halves(w_u32):
    lo = lax.bitcast_convert_type(w_u32 << 16, F32)
    hi = lax.bitcast_convert_type(w_u32 & jnp.uint32(0xFFFF0000), F32)
    return lo, hi


def _norm_matmul_body(x_ref, g_ref, w_ref, o_ref, xn_ref):
    @pl.when(pl.program_id(1) == 0)
    def _():
        x = x_ref[...]
        ms = jnp.mean(x * x, axis=-1, keepdims=True)
        xn_ref[...] = ((x * lax.rsqrt(ms + NORM_EPS)) * g_ref[...]).astype(BF16)

    o_ref[...] = jnp.dot(xn_ref[...], w_ref[...], preferred_element_type=F32).astype(o_ref.dtype)


def norm_matmul(x, g, w, *, tm, tn):
    m, k = x.shape
    n = w.shape[1]
    tm, tn = _tile(m, tm), _tile(n, tn)
    return pl.pallas_call(
        _norm_matmul_body,
        grid=(m // tm, n // tn),
        in_specs=[
            pl.BlockSpec((tm, k), lambda i, j: (i, 0)),
            pl.BlockSpec((1, k), lambda i, j: (0, 0)),
            pl.BlockSpec((k, tn), lambda i, j: (0, j)),
        ],
        out_specs=pl.BlockSpec((tm, tn), lambda i, j: (i, j)),
        out_shape=jax.ShapeDtypeStruct((m, n), BF16),
        scratch_shapes=[pltpu.VMEM((tm, k), BF16)],
        compiler_params=_params(("parallel", "arbitrary")),
        name="norm_matmul",
    )(x, g.reshape(1, k), w)


def _retention_body(lg_ref, q_ref, k_ref, v_ref, g_ref, cos_ref, sin_ref, gn_ref, o_ref, state_ref, *, blk, dk):
    h = pl.program_id(1)
    c = pl.program_id(2)
    lg = lg_ref[h]

    @pl.when(c == 0)
    def _():
        state_ref[...] = jnp.zeros_like(state_ref)

    half = dk // 2
    cos = cos_ref[...]
    sin = sin_ref[...]

    def rot(t):
        t1, t2 = t[:, :half], t[:, half:]
        return jnp.concatenate([t1 * cos - t2 * sin, t1 * sin + t2 * cos], axis=-1)

    qr = rot(q_ref[...].astype(F32))
    kr = rot(k_ref[...].astype(F32)) * (dk ** -0.5)
    v = v_ref[...]

    pos = lax.broadcasted_iota(I32, (blk, 1), 0).astype(F32)
    q_dec = jnp.exp((pos + 1.0) * lg)
    k_dec = jnp.exp((blk - 1.0 - pos) * lg)
    blk_dec = jnp.exp(jnp.full((1, dk), blk * lg, F32))

    s = lax.dot_general(qr.astype(BF16), kr.astype(BF16), (((1,), (1,)), ((), ())), preferred_element_type=F32)
    ri = lax.broadcasted_iota(I32, (blk, blk), 0)
    ci = lax.broadcasted_iota(I32, (blk, blk), 1)
    diff = (ri - ci).astype(F32)
    dec = jnp.where(diff >= 0.0, jnp.exp(jnp.maximum(diff, 0.0) * lg), 0.0)
    inner = jnp.dot((s * dec).astype(BF16), v, preferred_element_type=F32)

    state = state_ref[...]
    cross = jnp.dot((qr * q_dec).astype(BF16), state.astype(BF16), preferred_element_type=F32)
    kd_t = jnp.transpose(kr * k_dec).astype(BF16)
    state_ref[...] = state * blk_dec + jnp.dot(kd_t, v, preferred_element_type=F32)

    out = inner + cross
    mu = jnp.mean(out, axis=-1, keepdims=True)
    cen = out - mu
    var = jnp.mean(cen * cen, axis=-1, keepdims=True)
    y = cen * lax.rsqrt(var + NORM_EPS) * gn_ref[...]
    gate = g_ref[...].astype(F32)
    o_ref[...] = (y * (gate * jax.nn.sigmoid(gate))).astype(o_ref.dtype)


def retention(proj, cos, sin, ret_norm_g, *, batch, seq, ret_width):
    heads = RET_HEADS
    dk = ret_width // heads
    blk = min(RET_BLOCK, seq)
    nblk = seq // blk
    hb = ret_width // dk
    log_gamma = jnp.log(1.0 - jnp.exp2(-5.0 - jnp.arange(heads, dtype=F32)))

    def col(seg):
        return pl.BlockSpec((blk, dk), lambda b, h, c, lg, seg=seg: (b * nblk + c, seg * hb + h))

    rowspec = pl.BlockSpec((blk, dk // 2), lambda b, h, c, lg: (b * nblk + c, 0))
    return pl.pallas_call(
        functools.partial(_retention_body, blk=blk, dk=dk),
        grid_spec=pltpu.PrefetchScalarGridSpec(
            num_scalar_prefetch=1,
            grid=(batch, heads, nblk),
            in_specs=[col(0), col(1), col(2), col(3), rowspec, rowspec,
                      pl.BlockSpec((1, dk), lambda b, h, c, lg: (0, h))],
            out_specs=pl.BlockSpec((blk, dk), lambda b, h, c, lg: (b * nblk + c, h)),
            scratch_shapes=[pltpu.VMEM((dk, dk), F32)],
        ),
        out_shape=jax.ShapeDtypeStruct((batch * seq, ret_width), BF16),
        compiler_params=_params(("parallel", "parallel", "arbitrary")),
        name="retention",
    )(log_gamma, proj, proj, proj, proj, cos, sin, ret_norm_g.reshape(1, ret_width))


def _gmlp_body(u_ref, v_ref, lng_ref, lnb_ref, ws_ref, bst_ref, o_ref, vn_ref, *, rows, groups, cg, chunk):
    v = jax.nn.gelu(v_ref[...].astype(F32))
    mu = jnp.mean(v, axis=-1, keepdims=True)
    cen = v - mu
    var = jnp.mean(cen * cen, axis=-1, keepdims=True)
    vn_ref[...] = (cen * lax.rsqrt(var + NORM_EPS) * lng_ref[...] + lnb_ref[...]).astype(BF16)

    ri = lax.broadcasted_iota(I32, (chunk, chunk), 0)
    ci = lax.broadcasted_iota(I32, (chunk, chunk), 1)
    causal = ri >= ci
    for g in range(groups):
        w = jnp.where(causal, ws_ref[g], 0.0).astype(BF16)
        bias = bst_ref[:, g:g + 1]
        cols = slice(g * cg, (g + 1) * cg)
        for t in range(rows // chunk):
            rws = slice(t * chunk, (t + 1) * chunk)
            sp = jnp.dot(w, vn_ref[rws, cols], preferred_element_type=F32) + bias
            u = jax.nn.gelu(u_ref[rws, cols].astype(F32))
            o_ref[rws, cols] = (u * sp).astype(o_ref.dtype)


def gmlp(proj, ln_g, ln_b, ws, bs, *, n_rows, gm_width, u_block, v_block):
    groups, chunk = GM_GROUPS, GM_CHUNK
    cg = gm_width // groups
    rows = 2 * chunk
    return pl.pallas_call(
        functools.partial(_gmlp_body, rows=rows, groups=groups, cg=cg, chunk=chunk),
        grid=(n_rows // rows,),
        in_specs=[
            pl.BlockSpec((rows, gm_width), lambda i: (i, u_block)),
            pl.BlockSpec((rows, gm_width), lambda i: (i, v_block)),
            pl.BlockSpec((1, gm_width), lambda i: (0, 0)),
            pl.BlockSpec((1, gm_width), lambda i: (0, 0)),
            pl.BlockSpec((groups, chunk, chunk), lambda i: (0, 0, 0)),
            pl.BlockSpec((chunk, groups), lambda i: (0, 0)),
        ],
        out_specs=pl.BlockSpec((rows, gm_width), lambda i: (i, 0)),
        out_shape=jax.ShapeDtypeStruct((n_rows, gm_width), BF16),
        scratch_shapes=[pltpu.VMEM((rows, gm_width), BF16)],
        compiler_params=_params(("parallel",)),
        name="gmlp",
    )(proj, proj, ln_g.reshape(1, gm_width), ln_b.reshape(1, gm_width), ws, bs.T)


def _matmul_residual_body(*refs, n_parts):
    a_refs, w_refs = refs[:n_parts], refs[n_parts:2 * n_parts]
    r_ref, o_ref = refs[2 * n_parts], refs[2 * n_parts + 1]
    acc = r_ref[...]
    for a_ref, w_ref in zip(a_refs, w_refs):
        acc = acc + jnp.dot(a_ref[...], w_ref[...], preferred_element_type=F32)
    o_ref[...] = acc


def matmul_residual(parts, w, res, *, tm, tn):
    m, n = res.shape
    tm, tn = _tile(m, tm), _tile(n, tn)
    kp = parts[0].shape[1]
    n_parts = len(parts)
    a_specs = [pl.BlockSpec((tm, kp), lambda i, j: (i, 0)) for _ in parts]
    w_specs = [pl.BlockSpec((kp, tn), lambda i, j, p=p: (p, j)) for p in range(n_parts)]
    return pl.pallas_call(
        functools.partial(_matmul_residual_body, n_parts=n_parts),
        grid=(m // tm, n // tn),
        in_specs=a_specs + w_specs + [pl.BlockSpec((tm, tn), lambda i, j: (i, j))],
        out_specs=pl.BlockSpec((tm, tn), lambda i, j: (i, j)),
        out_shape=jax.ShapeDtypeStruct((m, n), F32),
        compiler_params=_params(("parallel", "parallel")),
        name="matmul_residual",
    )(*parts, *([w] * n_parts), res)


def _cross_attention_body(q_ref, kv_ref, o_ref, *, heads, dh):
    width = heads * dh
    for h in range(heads):
        q = q_ref[:, h * dh:(h + 1) * dh]
        k = kv_ref[:, h * dh:(h + 1) * dh]
        v = kv_ref[:, width + h * dh:width + (h + 1) * dh]
        s = lax.dot_general(q, k, (((1,), (1,)), ((), ())), preferred_element_type=F32) * (dh ** -0.5)
        e = jnp.exp(s - jnp.max(s, axis=-1, keepdims=True))
        p = e / jnp.sum(e, axis=-1, keepdims=True)
        o_ref[:, h * dh:(h + 1) * dh] = jnp.dot(p.astype(BF16), v, preferred_element_type=F32).astype(o_ref.dtype)


def cross_attention(q, kv, *, batch, seq, mem_len, tq):
    heads = XA_HEADS
    width = q.shape[1]
    dh = width // heads
    tq = min(tq, seq)
    nq = seq // tq
    return pl.pallas_call(
        functools.partial(_cross_attention_body, heads=heads, dh=dh),
        grid=(batch, nq),
        in_specs=[
            pl.BlockSpec((tq, width), lambda b, i: (b * nq + i, 0)),
            pl.BlockSpec((mem_len, 2 * width), lambda b, i: (b, 0)),
        ],
        out_specs=pl.BlockSpec((tq, width), lambda b, i: (b * nq + i, 0)),
        out_shape=jax.ShapeDtypeStruct((batch * seq, width), BF16),
        compiler_params=_params(("parallel", "parallel")),
        name="cross_attention",
    )(q, kv)


def _router_body(h_ref, g_ref, wr_ref, br_ref, xp_ref, meta_ref, cnt_ref, base_ref, *, tm, n_grp, per):
    @pl.when(pl.program_id(0) == 0)
    def _():
        base_ref[...] = jnp.zeros_like(base_ref)

    x = h_ref[...]
    ms = jnp.mean(x * x, axis=-1, keepdims=True)
    xn = (x * lax.rsqrt(ms + NORM_EPS)) * g_ref[...]
    xp_ref[...] = _pack_halves(xn)

    logits = jnp.dot(xn, wr_ref[...], preferred_element_type=F32, precision=lax.Precision.HIGHEST) + br_ref[...]
    lane = lax.broadcasted_iota(I32, logits.shape, 1)
    neg = jnp.float32(-1e30)
    big = jnp.int32(V7X_LANES)

    gl = jnp.where(lane < n_grp, logits, neg)
    gmax = jnp.max(gl, axis=-1, keepdims=True)
    gidx = jnp.min(jnp.where(gl == gmax, lane, big), axis=-1, keepdims=True)
    grp_gate = 1.0 / jnp.sum(jnp.exp(gl - gmax), axis=-1, keepdims=True)

    lo = n_grp + gidx * per
    el = jnp.where((lane >= lo) & (lane < lo + per), logits, neg)
    v1 = jnp.max(el, axis=-1, keepdims=True)
    i1 = jnp.min(jnp.where(el == v1, lane, big), axis=-1, keepdims=True)
    el2 = jnp.where(lane == i1, neg, el)
    v2 = jnp.max(el2, axis=-1, keepdims=True)
    i2 = jnp.min(jnp.where(el2 == v2, lane, big), axis=-1, keepdims=True)
    t = jnp.exp(v2 - v1)
    den = 1.0 + t
    g1 = grp_gate / den
    g2 = grp_gate * (t / den)

    oh1 = jnp.where(lane == i1, 1.0, 0.0)
    oh2 = jnp.where(lane == i2, 1.0, 0.0)
    ri = lax.broadcasted_iota(I32, (tm, tm), 0)
    ci = lax.broadcasted_iota(I32, (tm, tm), 1)
    lower = jnp.where(ri > ci, 1.0, 0.0).astype(BF16)
    pre1 = jnp.dot(lower, oh1.astype(BF16), preferred_element_type=F32)
    pre2 = jnp.dot(lower, oh2.astype(BF16), preferred_element_type=F32)
    cnt1 = jnp.sum(oh1, axis=0, keepdims=True)
    cnt2 = jnp.sum(oh2, axis=0, keepdims=True)
    base = base_ref[...]
    rank1 = jnp.sum(oh1 * (pre1 + base), axis=-1, keepdims=True)
    rank2 = jnp.sum(oh2 * (pre2 + base + cnt1), axis=-1, keepdims=True)
    total = base + cnt1 + cnt2
    base_ref[...] = total
    cnt_ref[...] = total

    e1 = (i1 - n_grp).astype(F32)
    e2 = (i2 - n_grp).astype(F32)
    meta = jnp.zeros(logits.shape, F32)
    for idx, val in enumerate((e1, e2, rank1, rank2, g1, g2)):
        meta = jnp.where(lane == idx, val, meta)
    meta_ref[...] = meta


def moe_router(h, g, wr, br, *, tm):
    n, d = h.shape
    tm = min(tm, n)
    return pl.pallas_call(
        functools.partial(_router_body, tm=tm, n_grp=MOE_GROUPS, per=MOE_PER_GROUP),
        grid=(n // tm,),
        in_specs=[
            pl.BlockSpec((tm, d), lambda i: (i, 0)),
            pl.BlockSpec((1, d), lambda i: (0, 0)),
            pl.BlockSpec((d, V7X_LANES), lambda i: (0, 0)),
            pl.BlockSpec((1, V7X_LANES), lambda i: (0, 0)),
        ],
        out_specs=[
            pl.BlockSpec((tm, d // 2), lambda i: (i, 0)),
            pl.BlockSpec((tm, V7X_LANES), lambda i: (i, 0)),
            pl.BlockSpec((1, V7X_LANES), lambda i: (0, 0)),
        ],
        out_shape=[
            jax.ShapeDtypeStruct((n, d // 2), U32),
            jax.ShapeDtypeStruct((n, V7X_LANES), F32),
            jax.ShapeDtypeStruct((1, V7X_LANES), F32),
        ],
        scratch_shapes=[pltpu.VMEM((1, V7X_LANES), F32)],
        compiler_params=_params(("arbitrary",)),
        name="moe_router",
    )(h, g.reshape(1, d), wr, br)


DISPATCH_TOKENS_PER_WAVE = 64


def _dispatch_copy(x_hbm, o_hbm, dest_ref, sem, tok, k):
    return pltpu.make_async_copy(x_hbm.at[pl.ds(tok, 1)], o_hbm.at[pl.ds(dest_ref[MOE_TOPK * tok + k], 1)], sem)


def _dispatch_body(dest_ref, x_hbm, init_hbm, o_hbm, sem, *, wave):
    del init_hbm
    base = pl.program_id(0) * wave

    def start(r, carry):
        for k in range(MOE_TOPK):
            _dispatch_copy(x_hbm, o_hbm, dest_ref, sem, base + r, k).start()
        return carry

    def wait(r, carry):
        for k in range(MOE_TOPK):
            _dispatch_copy(x_hbm, o_hbm, dest_ref, sem, base + r, k).wait()
        return carry

    lax.fori_loop(0, wave, start, 0)
    lax.fori_loop(0, wave, wait, 0)


def moe_dispatch(xp, dest_flat, n_sorted_rows):
    n, kw = xp.shape
    wave = min(DISPATCH_TOKENS_PER_WAVE, n)
    init = jnp.zeros((n_sorted_rows, kw), xp.dtype)
    return pl.pallas_call(
        functools.partial(_dispatch_body, wave=wave),
        grid_spec=pltpu.PrefetchScalarGridSpec(
            num_scalar_prefetch=1,
            grid=(n // wave,),
            in_specs=[pl.BlockSpec(memory_space=pl.ANY), pl.BlockSpec(memory_space=pl.ANY)],
            out_specs=pl.BlockSpec(memory_space=pl.ANY),
            scratch_shapes=[pltpu.SemaphoreType.DMA(())],
        ),
        out_shape=jax.ShapeDtypeStruct((n_sorted_rows, kw), xp.dtype),
        input_output_aliases={2: 0},
        compiler_params=_params(("arbitrary",)),
        name="moe_dispatch",
    )(dest_flat, xp, init)


def _experts_body(exp_ref, src_ref, nact_ref, x_ref, wg_ref, wu_ref, wd_ref, o_ref,
                  wgb_ref, wub_ref, wdb_ref):
    del src_ref
    i = pl.program_id(1)
    active = i < nact_ref[0]
    prev = exp_ref[jnp.maximum(i - 1, 0)]
    new_expert = jnp.logical_or(i == 0, exp_ref[i] != prev)

    @pl.when(jnp.logical_and(active, new_expert))
    def _():
        wgb_ref[...] = wg_ref[0].astype(BF16)
        wub_ref[...] = wu_ref[0].astype(BF16)
        wdb_ref[...] = wd_ref[0].astype(BF16)

    @pl.when(active)
    def _():
        lo, hi = _unpack_halves(x_ref[...])
        kh = lo.shape[1]
        xl, xh = lo.astype(BF16), hi.astype(BF16)

        def proj(w_ref):
            return (jnp.dot(xl, w_ref[:kh, :], preferred_element_type=F32)
                    + jnp.dot(xh, w_ref[kh:, :], preferred_element_type=F32))

        hg = proj(wgb_ref)
        hu = proj(wub_ref)
        hdn = (hg * jax.nn.sigmoid(hg) * hu).astype(BF16)
        y = jnp.dot(hdn, wdb_ref[...], preferred_element_type=F32)
        o_ref[0] = _pack_halves(y)

    @pl.when(jnp.logical_not(active))
    def _():
        o_ref[...] = jnp.zeros_like(o_ref)


def moe_experts(xs, w_gate, w_up, w_down, blk_exp, blk_src, nact, *, n_blocks, tm):
    n_exp, d, ff = w_gate.shape
    nf = MOE_FF_SPLIT if ff % (MOE_FF_SPLIT * V7X_LANES) == 0 else 1
    ffc = ff // nf
    kw = xs.shape[1]
    return pl.pallas_call(
        _experts_body,
        grid_spec=pltpu.PrefetchScalarGridSpec(
            num_scalar_prefetch=3,
            grid=(nf, n_blocks),
            in_specs=[
                pl.BlockSpec((tm, kw), lambda f, i, e, s, n: (s[i], 0)),
                pl.BlockSpec((1, d, ffc), lambda f, i, e, s, n: (e[i], 0, f)),
                pl.BlockSpec((1, d, ffc), lambda f, i, e, s, n: (e[i], 0, f)),
                pl.BlockSpec((1, ffc, d), lambda f, i, e, s, n: (e[i], f, 0)),
            ],
            out_specs=pl.BlockSpec((1, tm, kw), lambda f, i, e, s, n: (f, i, 0)),
            scratch_shapes=[pltpu.VMEM((d, ffc), BF16), pltpu.VMEM((d, ffc), BF16), pltpu.VMEM((ffc, d), BF16)],
        ),
        out_shape=jax.ShapeDtypeStruct((nf, n_blocks * tm, kw), U32),
        compiler_params=_params(("arbitrary", "arbitrary")),
        name="moe_experts",
    )(blk_exp, blk_src, nact, xs, w_gate, w_up, w_down)


def _combine_copy(ys_hbm, buf_ref, dest_ref, sem, f, k, tok, r):
    return pltpu.make_async_copy(
        ys_hbm.at[f, pl.ds(dest_ref[MOE_TOPK * tok + k], 1)], buf_ref.at[f * MOE_TOPK + k, pl.ds(r, 1)], sem)


def _combine_body(dest_ref, h_ref, meta_ref, g_ref, ys_hbm, o_ref, buf_ref, sem, *, tb, nf):
    base = pl.program_id(0) * tb

    def start(r, carry):
        for f in range(nf):
            for k in range(MOE_TOPK):
                _combine_copy(ys_hbm, buf_ref, dest_ref, sem, f, k, base + r, r).start()
        return carry

    def wait(r, carry):
        for f in range(nf):
            for k in range(MOE_TOPK):
                _combine_copy(ys_hbm, buf_ref, dest_ref, sem, f, k, base + r, r).wait()
        return carry

    lax.fori_loop(0, tb, start, 0)
    lax.fori_loop(0, tb, wait, 0)

    meta = meta_ref[...]
    gates = [meta[:, 4 + k:5 + k] for k in range(MOE_TOPK)]
    acc_lo = None
    acc_hi = None
    for f in range(nf):
        for k in range(MOE_TOPK):
            lo, hi = _unpack_halves(buf_ref[f * MOE_TOPK + k])
            lo, hi = lo * gates[k], hi * gates[k]
            acc_lo = lo if acc_lo is None else acc_lo + lo
            acc_hi = hi if acc_hi is None else acc_hi + hi
    kh = acc_lo.shape[1]
    h_lo = h_ref[:, :kh] + acc_lo
    h_hi = h_ref[:, kh:] + acc_hi
    ms = (jnp.sum(h_lo * h_lo, axis=-1, keepdims=True) + jnp.sum(h_hi * h_hi, axis=-1, keepdims=True)) / (2 * kh)
    scale = lax.rsqrt(ms + NORM_EPS)
    o_ref[:, :kh] = (h_lo * scale) * g_ref[:, :kh]
    o_ref[:, kh:] = (h_hi * scale) * g_ref[:, kh:]


def moe_combine(h, meta, g, ys, dest_flat, *, tb):
    n, d = h.shape
    nf, _, kw = ys.shape
    tb = min(tb, n)
    return pl.pallas_call(
        functools.partial(_combine_body, tb=tb, nf=nf),
        grid_spec=pltpu.PrefetchScalarGridSpec(
            num_scalar_prefetch=1,
            grid=(n // tb,),
            in_specs=[
                pl.BlockSpec((tb, d), lambda i, dst: (i, 0)),
                pl.BlockSpec((tb, V7X_LANES), lambda i, dst: (i, 0)),
                pl.BlockSpec((1, d), lambda i, dst: (0, 0)),
                pl.BlockSpec(memory_space=pl.ANY),
            ],
            out_specs=pl.BlockSpec((tb, d), lambda i, dst: (i, 0)),
            scratch_shapes=[pltpu.VMEM((nf * MOE_TOPK, tb, kw), U32), pltpu.SemaphoreType.DMA(())],
        ),
        out_shape=jax.ShapeDtypeStruct((n, d), F32),
        compiler_params=_params(("arbitrary",)),
        name="moe_combine",
    )(dest_flat, h, meta, g.reshape(1, d), ys)


def _moe_plan(meta, counts, *, tm, n_blocks):
    n_exp = counts.shape[0]
    expert = meta[:, 0:MOE_TOPK].astype(I32)
    rank = meta[:, MOE_TOPK:2 * MOE_TOPK].astype(I32)
    padded = (counts + tm - 1) // tm * tm
    pad_end = jnp.cumsum(padded)
    pad_start = pad_end - padded
    dest = (pad_start[expert] + rank).reshape(-1)
    nact = pad_end[-1] // tm
    blk = jnp.arange(n_blocks, dtype=I32)
    blk_exp = jnp.minimum(jnp.searchsorted(pad_end, blk * tm, side="right"), n_exp - 1).astype(I32)
    active = blk < nact
    last = jnp.maximum(nact - 1, 0)
    blk_src = jnp.where(active, blk, last).astype(I32)
    blk_exp = jnp.where(active, blk_exp, blk_exp[last]).astype(I32)
    return dest.astype(I32), blk_exp, blk_src, nact.astype(I32).reshape(1)


def kernel(x, mem, positions, norm_mix_g, w_in, ret_norm_g, gm_ln_g, gm_ln_b, gm_ws, gm_bs, w_out, norm_xa_g, norm_mem_g, xa_wq, xa_wkv, xa_wo, norm_moe_g, router_grp_w, router_grp_b, router_exp_w, router_exp_b, moe_w_gate, moe_w_up, moe_w_down, norm_final_g):
    batch, seq, d = x.shape
    mem_len = mem.shape[1]
    n = batch * seq
    ret_width = ret_norm_g.shape[0]
    gm_width = gm_ln_g.shape[0]
    assert ret_width == gm_width and w_in.shape[1] == 4 * ret_width + 2 * gm_width
    dk = ret_width // RET_HEADS
    n_exp = moe_w_gate.shape[0]
    assert n_exp == MOE_GROUPS * MOE_PER_GROUP and MOE_GROUPS + n_exp <= V7X_LANES

    inv_freq = ROPE_BASE ** (-jnp.arange(0, dk, 2, dtype=F32) / dk)
    ang = positions.astype(F32).reshape(n, 1) * inv_freq
    cos, sin = jnp.cos(ang), jnp.sin(ang)

    x2 = x.reshape(n, d)
    proj = norm_matmul(x2, norm_mix_g, w_in.astype(BF16), tm=512, tn=1024)
    ret = retention(proj, cos, sin, ret_norm_g, batch=batch, seq=seq, ret_width=ret_width)
    gm = gmlp(proj, gm_ln_g, gm_ln_b, gm_ws, gm_bs, n_rows=n, gm_width=gm_width,
              u_block=4 * ret_width // gm_width, v_block=4 * ret_width // gm_width + 1)
    h1 = matmul_residual([ret, gm], w_out.astype(BF16), x2, tm=512, tn=1024)

    q = norm_matmul(h1, norm_xa_g, xa_wq.astype(BF16), tm=512, tn=1024)
    kv = norm_matmul(mem.reshape(batch * mem_len, d), norm_mem_g, xa_wkv.astype(BF16), tm=512, tn=1024)
    o = cross_attention(q, kv, batch=batch, seq=seq, mem_len=mem_len, tq=512)
    h2 = matmul_residual([o], xa_wo.astype(BF16), h1, tm=512, tn=1024)

    pad = V7X_LANES - MOE_GROUPS - n_exp
    wr = jnp.concatenate([router_grp_w, router_exp_w, jnp.zeros((d, pad), F32)], axis=1)
    br = jnp.concatenate([router_grp_b, router_exp_b, jnp.zeros((pad,), F32)]).reshape(1, V7X_LANES)
    xp, meta, cnt = moe_router(h2, norm_moe_g, wr, br, tm=256)

    tm = MOE_ROW_TILE
    n_blocks = -(-(n * MOE_TOPK) // tm) + n_exp
    counts = cnt[0, MOE_GROUPS:MOE_GROUPS + n_exp].astype(I32)
    dest, blk_exp, blk_src, nact = _moe_plan(meta, counts, tm=tm, n_blocks=n_blocks)
    xs = moe_dispatch(xp, dest, n_blocks * tm)
    ys = moe_experts(xs, moe_w_gate, moe_w_up, moe_w_down, blk_exp, blk_src, nact, n_blocks=n_blocks, tm=tm)
    y = moe_combine(h2, meta, norm_final_g, ys, dest, tb=128)
    return y.reshape(batch, seq, d)
```

```python
import functools

import jax
import jax.numpy as jnp
from jax import lax
from jax.experimental import pallas as pl
from jax.experimental.pallas import tpu as pltpu

NORM_EPS = 1e-6
RET_HEADS = 8
ROPE_BASE = 10000.0
GM_GROUPS = 8
GM_CHUNK = 128
XA_HEADS = 4
MOE_GROUPS = 8
MOE_PER_GROUP = 8
MOE_TOPK = 2

V7X_LANES = 128
V7X_VMEM_BYTES = 64 * 1024 * 1024
VMEM_LIMIT_BYTES = 56 * 1024 * 1024

RET_BLOCK = 512
MOE_ROW_TILE = 256
MOE_FF_SPLIT = 2

F32 = jnp.float32
BF16 = jnp.bfloat16
U32 = jnp.uint32
I32 = jnp.int32


def _params(sem):
    return pltpu.CompilerParams(dimension_semantics=sem, vmem_limit_bytes=VMEM_LIMIT_BYTES)


def _tile(dim, target):
    t = min(dim, target)
    while dim % t:
        t -= V7X_LANES
    assert t > 0, (dim, target)
    return t


def _pack_halves(x_f32):
    k = x_f32.shape[-1] // 2
    bits = lax.bitcast_convert_type(x_f32.astype(BF16).astype(F32), U32)
    return (bits[:, k:] & jnp.uint32(0xFFFF0000)) | (bits[:, :k] >> 16)


def _unpack_halves(w_u32):
    lo = lax.bitcast_convert_type(w_u32 << 16, F32)
    hi = lax.bitcast_convert_type(w_u32 & jnp.uint32(0xFFFF0000), F32)
    return lo, hi


def _norm_matmul_body(x_ref, g_ref, w_ref, o_ref, xn_ref):
    @pl.when(pl.program_id(1) == 0)
    def _():
        x = x_ref[...]
        ms = jnp.mean(x * x, axis=-1, keepdims=True)
        xn_ref[...] = ((x * lax.rsqrt(ms + NORM_EPS)) * g_ref[...]).astype(BF16)

    o_ref[...] = jnp.dot(xn_ref[...], w_ref[...], preferred_element_type=F32).astype(o_ref.dtype)


def norm_matmul(x, g, w, *, tm, tn):
    m, k = x.shape
    n = w.shape[1]
    tm, tn = _tile(m, tm), _tile(n, tn)
    return pl.pallas_call(
        _norm_matmul_body,
        grid=(m // tm, n // tn),
        in_specs=[
            pl.BlockSpec((tm, k), lambda i, j: (i, 0)),
            pl.BlockSpec((1, k), lambda i, j: (0, 0)),
            pl.BlockSpec((k, tn), lambda i, j: (0, j)),
        ],
        out_specs=pl.BlockSpec((tm, tn), lambda i, j: (i, j)),
        out_shape=jax.ShapeDtypeStruct((m, n), BF16),
        scratch_shapes=[pltpu.VMEM((tm, k), BF16)],
        compiler_params=_params(("parallel", "arbitrary")),
        name="norm_matmul",
    )(x, g.reshape(1, k), w)


def _retention_body(lg_ref, q_ref, k_ref, v_ref, g_ref, cos_ref, sin_ref, gn_ref, o_ref, state_ref, *, blk, dk):
    h = pl.program_id(1)
    c = pl.program_id(2)
    lg = lg_ref[h]

    @pl.when(c == 0)
    def _():
        state_ref[...] = jnp.zeros_like(state_ref)

    half = dk // 2
    cos = cos_ref[...]
    sin = sin_ref[...]

    def rot(t):
        t1, t2 = t[:, :half], t[:, half:]
        return jnp.concatenate([t1 * cos - t2 * sin, t1 * sin + t2 * cos], axis=-1)

    qr = rot(q_ref[...].astype(F32))
    kr = rot(k_ref[...].astype(F32)) * (dk ** -0.5)
    v = v_ref[...]

    pos = lax.broadcasted_iota(I32, (blk, 1), 0).astype(F32)
    q_dec = jnp.exp((pos + 1.0) * lg)
    k_dec = jnp.exp((blk - 1.0 - pos) * lg)
    blk_dec = jnp.exp(jnp.full((1, dk), blk * lg, F32))

    s = lax.dot_general(qr.astype(BF16), kr.astype(BF16), (((1,), (1,)), ((), ())), preferred_element_type=F32)
    ri = lax.broadcasted_iota(I32, (blk, blk), 0)
    ci = lax.broadcasted_iota(I32, (blk, blk), 1)
    diff = (ri - ci).astype(F32)
    dec = jnp.where(diff >= 0.0, jnp.exp(jnp.maximum(diff, 0.0) * lg), 0.0)
    inner = jnp.dot((s * dec).astype(BF16), v, preferred_element_type=F32)

    state = state_ref[...]
    cross = jnp.dot((qr * q_dec).astype(BF16), state.astype(BF16), preferred_element_type=F32)
    kd_t = jnp.transpose(kr * k_dec).astype(BF16)
    state_ref[...] = state * blk_dec + jnp.dot(kd_t, v, preferred_element_type=F32)

    out = inner + cross
    mu = jnp.mean(out, axis=-1, keepdims=True)
    cen = out - mu
    var = jnp.mean(cen * cen, axis=-1, keepdims=True)
    y = cen * lax.rsqrt(var + NORM_EPS) * gn_ref[...]
    gate = g_ref[...].astype(F32)
    o_ref[...] = (y * (gate * jax.nn.sigmoid(gate))).astype(o_ref.dtype)


def retention(proj, cos, sin, ret_norm_g, *, batch, seq, ret_width):
    heads = RET_HEADS
    dk = ret_width // heads
    blk = min(RET_BLOCK, seq)
    nblk = seq // blk
    hb = ret_width // dk
    log_gamma = jnp.log(1.0 - jnp.exp2(-5.0 - jnp.arange(heads, dtype=F32)))

    def col(seg):
        return pl.BlockSpec((blk, dk), lambda b, h, c, lg, seg=seg: (b * nblk + c, seg * hb + h))

    rowspec = pl.BlockSpec((blk, dk // 2), lambda b, h, c, lg: (b * nblk + c, 0))
    return pl.pallas_call(
        functools.partial(_retention_body, blk=blk, dk=dk),
        grid_spec=pltpu.PrefetchScalarGridSpec(
            num_scalar_prefetch=1,
            grid=(batch, heads, nblk),
            in_specs=[col(0), col(1), col(2), col(3), rowspec, rowspec,
                      pl.BlockSpec((1, dk), lambda b, h, c, lg: (0, h))],
            out_specs=pl.BlockSpec((blk, dk), lambda b, h, c, lg: (b * nblk + c, h)),
            scratch_shapes=[pltpu.VMEM((dk, dk), F32)],
        ),
        out_shape=jax.ShapeDtypeStruct((batch * seq, ret_width), BF16),
        compiler_params=_params(("parallel", "parallel", "arbitrary")),
        name="retention",
    )(log_gamma, proj, proj, proj, proj, cos, sin, ret_norm_g.reshape(1, ret_width))


def _gmlp_body(u_ref, v_ref, lng_ref, lnb_ref, ws_ref, bst_ref, o_ref, vn_ref, *, rows, groups, cg, chunk):
    v = jax.nn.gelu(v_ref[...].astype(F32))
    mu = jnp.mean(v, axis=-1, keepdims=True)
    cen = v - mu
    var = jnp.mean(cen * cen, axis=-1, keepdims=True)
    vn_ref[...] = (cen * lax.rsqrt(var + NORM_EPS) * lng_ref[...] + lnb_ref[...]).astype(BF16)

    ri = lax.broadcasted_iota(I32, (chunk, chunk), 0)
    ci = lax.broadcasted_iota(I32, (chunk, chunk), 1)
    causal = ri >= ci
    for g in range(groups):
        w = jnp.where(causal, ws_ref[g], 0.0).astype(BF16)
        bias = bst_ref[:, g:g + 1]
        cols = slice(g * cg, (g + 1) * cg)
        for t in range(rows // chunk):
            rws = slice(t * chunk, (t + 1) * chunk)
            sp = jnp.dot(w, vn_ref[rws, cols], preferred_element_type=F32) + bias
            u = jax.nn.gelu(u_ref[rws, cols].astype(F32))
            o_ref[rws, cols] = (u * sp).astype(o_ref.dtype)


def gmlp(proj, ln_g, ln_b, ws, bs, *, n_rows, gm_width, u_block, v_block):
    groups, chunk = GM_GROUPS, GM_CHUNK
    cg = gm_width // groups
    rows = 2 * chunk
    return pl.pallas_call(
        functools.partial(_gmlp_body, rows=rows, groups=groups, cg=cg, chunk=chunk),
        grid=(n_rows // rows,),
        in_specs=[
            pl.BlockSpec((rows, gm_width), lambda i: (i, u_block)),
            pl.BlockSpec((rows, gm_width), lambda i: (i, v_block)),
            pl.BlockSpec((1, gm_width), lambda i: (0, 0)),
            pl.BlockSpec((1, gm_width), lambda i: (0, 0)),
            pl.BlockSpec((groups, chunk, chunk), lambda i: (0, 0, 0)),
            pl.BlockSpec((chunk, groups), lambda i: (0, 0)),
        ],
        out_specs=pl.BlockSpec((rows, gm_width), lambda i: (i, 0)),
        out_shape=jax.ShapeDtypeStruct((n_rows, gm_width), BF16),
        scratch_shapes=[pltpu.VMEM((rows, gm_width), BF16)],
        compiler_params=_params(("parallel",)),
        name="gmlp",
    )(proj, proj, ln_g.reshape(1, gm_width), ln_b.reshape(1, gm_width), ws, bs.T)


def _matmul_residual_body(*refs, n_parts):
    a_refs, w_refs = refs[:n_parts], refs[n_parts:2 * n_parts]
    r_ref, o_ref = refs[2 * n_parts], refs[2 * n_parts + 1]
    acc = r_ref[...]
    for a_ref, w_ref in zip(a_refs, w_refs):
        acc = acc + jnp.dot(a_ref[...], w_ref[...], preferred_element_type=F32)
    o_ref[...] = acc


def matmul_residual(parts, w, res, *, tm, tn):
    m, n = res.shape
    tm, tn = _tile(m, tm), _tile(n, tn)
    kp = parts[0].shape[1]
    n_parts = len(parts)
    a_specs = [pl.BlockSpec((tm, kp), lambda i, j: (i, 0)) for _ in parts]
    w_specs = [pl.BlockSpec((kp, tn), lambda i, j, p=p: (p, j)) for p in range(n_parts)]
    return pl.pallas_call(
        functools.partial(_matmul_residual_body, n_parts=n_parts),
        grid=(m // tm, n // tn),
        in_specs=a_specs + w_specs + [pl.BlockSpec((tm, tn), lambda i, j: (i, j))],
        out_specs=pl.BlockSpec((tm, tn), lambda i, j: (i, j)),
        out_shape=jax.ShapeDtypeStruct((m, n), F32),
        compiler_params=_params(("parallel", "parallel")),
        name="matmul_residual",
    )(*parts, *([w] * n_parts), res)


def _cross_attention_body(q_ref, kv_ref, o_ref, *, heads, dh):
    width = heads * dh
    for h in range(heads):
        q = q_ref[:, h * dh:(h + 1) * dh]
        k = kv_ref[:, h * dh:(h + 1) * dh]
        v = kv_ref[:, width + h * dh:width + (h + 1) * dh]
        s = lax.dot_general(q, k, (((1,), (1,)), ((), ())), preferred_element_type=F32) * (dh ** -0.5)
        e = jnp.exp(s - jnp.max(s, axis=-1, keepdims=True))
        p = e / jnp.sum(e, axis=-1, keepdims=True)
        o_ref[:, h * dh:(h + 1) * dh] = jnp.dot(p.astype(BF16), v, preferred_element_type=F32).astype(o_ref.dtype)


def cross_attention(q, kv, *, batch, seq, mem_len, tq):
    heads = XA_HEADS
    width = q.shape[1]
    dh = width // heads
    tq = min(tq, seq)
    nq = seq // tq
    return pl.pallas_call(
        functools.partial(_cross_attention_body, heads=heads, dh=dh),
        grid=(batch, nq),
        in_specs=[
            pl.BlockSpec((tq, width), lambda b, i: (b * nq + i, 0)),
            pl.BlockSpec((mem_len, 2 * width), lambda b, i: (b, 0)),
        ],
        out_specs=pl.BlockSpec((tq, width), lambda b, i: (b * nq + i, 0)),
        out_shape=jax.ShapeDtypeStruct((batch * seq, width), BF16),
        compiler_params=_params(("parallel", "parallel")),
        name="cross_attention",
    )(q, kv)


def _router_body(h_ref, g_ref, wr_ref, br_ref, xp_ref, meta_ref, cnt_ref, base_ref, *, tm, n_grp, per):
    @pl.when(pl.program_id(0) == 0)
    def _():
        base_ref[...] = jnp.zeros_like(base_ref)

    x = h_ref[...]
    ms = jnp.mean(x * x, axis=-1, keepdims=True)
    xn = (x * lax.rsqrt(ms + NORM_EPS)) * g_ref[...]
    xp_ref[...] = _pack_halves(xn)

    logits = jnp.dot(xn, wr_ref[...], preferred_element_type=F32, precision=lax.Precision.HIGHEST) + br_ref[...]
    lane = lax.broadcasted_iota(I32, logits.shape, 1)
    neg = jnp.float32(-1e30)
    big = jnp.int32(V7X_LANES)

    gl = jnp.where(lane < n_grp, logits, neg)
    gmax = jnp.max(gl, axis=-1, keepdims=True)
    gidx = jnp.min(jnp.where(gl == gmax, lane, big), axis=-1, keepdims=True)
    grp_gate = 1.0 / jnp.sum(jnp.exp(gl - gmax), axis=-1, keepdims=True)

    lo = n_grp + gidx * per
    el = jnp.where((lane >= lo) & (lane < lo + per), logits, neg)
    v1 = jnp.max(el, axis=-1, keepdims=True)
    i1 = jnp.min(jnp.where(el == v1, lane, big), axis=-1, keepdims=True)
    el2 = jnp.where(lane == i1, neg, el)
    v2 = jnp.max(el2, axis=-1, keepdims=True)
    i2 = jnp.min(jnp.where(el2 == v2, lane, big), axis=-1, keepdims=True)
    t = jnp.exp(v2 - v1)
    den = 1.0 + t
    g1 = grp_gate / den
    g2 = grp_gate * (t / den)

    oh1 = jnp.where(lane == i1, 1.0, 0.0)
    oh2 = jnp.where(lane == i2, 1.0, 0.0)
    ri = lax.broadcasted_iota(I32, (tm, tm), 0)
    ci = lax.broadcasted_iota(I32, (tm, tm), 1)
    lower = jnp.where(ri > ci, 1.0, 0.0).astype(BF16)
    pre1 = jnp.dot(lower, oh1.astype(BF16), preferred_element_type=F32)
    pre2 = jnp.dot(lower, oh2.astype(BF16), preferred_element_type=F32)
    cnt1 = jnp.sum(oh1, axis=0, keepdims=True)
    cnt2 = jnp.sum(oh2, axis=0, keepdims=True)
    base = base_ref[...]
    rank1 = jnp.sum(oh1 * (pre1 + base), axis=-1, keepdims=True)
    rank2 = jnp.sum(oh2 * (pre2 + base + cnt1), axis=-1, keepdims=True)
    total = base + cnt1 + cnt2
    base_ref[...] = total
    cnt_ref[...] = total

    e1 = (i1 - n_grp).astype(F32)
    e2 = (i2 - n_grp).astype(F32)
    meta = jnp.zeros(logits.shape, F32)
    for idx, val in enumerate((e1, e2, rank1, rank2, g1, g2)):
        meta = jnp.where(lane == idx, val, meta)
    meta_ref[...] = meta


def moe_router(h, g, wr, br, *, tm):
    n, d = h.shape
    tm = min(tm, n)
    return pl.pallas_call(
        functools.partial(_router_body, tm=tm, n_grp=MOE_GROUPS, per=MOE_PER_GROUP),
        grid=(n // tm,),
        in_specs=[
            pl.BlockSpec((tm, d), lambda i: (i, 0)),
            pl.BlockSpec((1, d), lambda i: (0, 0)),
            pl.BlockSpec((d, V7X_LANES), lambda i: (0, 0)),
            pl.BlockSpec((1, V7X_LANES), lambda i: (0, 0)),
        ],
        out_specs=[
            pl.BlockSpec((tm, d // 2), lambda i: (i, 0)),
            pl.BlockSpec((tm, V7X_LANES), lambda i: (i, 0)),
            pl.BlockSpec((1, V7X_LANES), lambda i: (0, 0)),
        ],
        out_shape=[
            jax.ShapeDtypeStruct((n, d // 2), U32),
            jax.ShapeDtypeStruct((n, V7X_LANES), F32),
            jax.ShapeDtypeStruct((1, V7X_LANES), F32),
        ],
        scratch_shapes=[pltpu.VMEM((1, V7X_LANES), F32)],
        compiler_params=_params(("arbitrary",)),
        name="moe_router",
    )(h, g.reshape(1, d), wr, br)


DISPATCH_TOKENS_PER_STEP = 256


def _dispatch_copy(x_ref, o_hbm, dest_ref, sem, base, r, k):
    return pltpu.make_async_copy(
        x_ref.at[pl.ds(r, 1)], o_hbm.at[pl.ds(dest_ref[MOE_TOPK * (base + r) + k], 1)], sem)


def _dispatch_body(dest_ref, x_ref, init_hbm, o_hbm, sem, *, tb):
    del init_hbm
    base = pl.program_id(0) * tb

    def start(r, carry):
        for k in range(MOE_TOPK):
            _dispatch_copy(x_ref, o_hbm, dest_ref, sem, base, r, k).start()
        return carry

    def wait(r, carry):
        for k in range(MOE_TOPK):
            _dispatch_copy(x_ref, o_hbm, dest_ref, sem, base, r, k).wait()
        return carry

    lax.fori_loop(0, tb, start, 0)
    lax.fori_loop(0, tb, wait, 0)


def moe_dispatch(xp, dest_flat, n_sorted_rows):
    n, kw = xp.shape
    tb = _tile(n, DISPATCH_TOKENS_PER_STEP)
    init = jnp.zeros((n_sorted_rows, kw), xp.dtype)
    return pl.pallas_call(
        functools.partial(_dispatch_body, tb=tb),
        grid_spec=pltpu.PrefetchScalarGridSpec(
            num_scalar_prefetch=1,
            grid=(n // tb,),
            in_specs=[pl.BlockSpec((tb, kw), lambda i, dst: (i, 0)), pl.BlockSpec(memory_space=pl.ANY)],
            out_specs=pl.BlockSpec(memory_space=pl.ANY),
            scratch_shapes=[pltpu.SemaphoreType.DMA(())],
        ),
        out_shape=jax.ShapeDtypeStruct((n_sorted_rows, kw), xp.dtype),
        input_output_aliases={2: 0},
        compiler_params=_params(("arbitrary",)),
        name="moe_dispatch",
    )(dest_flat, xp, init)


def _experts_body(exp_ref, src_ref, nact_ref, x_ref, wg_ref, wu_ref, wd_ref, o_ref,
                  wgb_ref, wub_ref, wdb_ref):
    del src_ref
    i = pl.program_id(1)
    active = i < nact_ref[0]
    prev = exp_ref[jnp.maximum(i - 1, 0)]
    new_expert = jnp.logical_or(i == 0, exp_ref[i] != prev)

    @pl.when(jnp.logical_and(active, new_expert))
    def _():
        wgb_ref[...] = wg_ref[0].astype(BF16)
        wub_ref[...] = wu_ref[0].astype(BF16)
        wdb_ref[...] = wd_ref[0].astype(BF16)

    @pl.when(active)
    def _():
        lo, hi = _unpack_halves(x_ref[...])
        kh = lo.shape[1]
        xl, xh = lo.astype(BF16), hi.astype(BF16)

        def proj(w_ref):
            return (jnp.dot(xl, w_ref[:kh, :], preferred_element_type=F32)
                    + jnp.dot(xh, w_ref[kh:, :], preferred_element_type=F32))

        hg = proj(wgb_ref)
        hu = proj(wub_ref)
        hdn = (hg * jax.nn.sigmoid(hg) * hu).astype(BF16)
        y = jnp.dot(hdn, wdb_ref[...], preferred_element_type=F32)
        o_ref[0] = _pack_halves(y)

    @pl.when(jnp.logical_not(active))
    def _():
        o_ref[...] = jnp.zeros_like(o_ref)


def moe_experts(xs, w_gate, w_up, w_down, blk_exp, blk_src, nact, *, n_blocks, tm):
    n_exp, d, ff = w_gate.shape
    nf = MOE_FF_SPLIT if ff % (MOE_FF_SPLIT * V7X_LANES) == 0 else 1
    ffc = ff // nf
    kw = xs.shape[1]
    return pl.pallas_call(
        _experts_body,
        grid_spec=pltpu.PrefetchScalarGridSpec(
            num_scalar_prefetch=3,
            grid=(nf, n_blocks),
            in_specs=[
                pl.BlockSpec((tm, kw), lambda f, i, e, s, n: (s[i], 0)),
                pl.BlockSpec((1, d, ffc), lambda f, i, e, s, n: (e[i], 0, f)),
                pl.BlockSpec((1, d, ffc), lambda f, i, e, s, n: (e[i], 0, f)),
                pl.BlockSpec((1, ffc, d), lambda f, i, e, s, n: (e[i], f, 0)),
            ],
            out_specs=pl.BlockSpec((1, tm, kw), lambda f, i, e, s, n: (f, i, 0)),
            scratch_shapes=[pltpu.VMEM((d, ffc), BF16), pltpu.VMEM((d, ffc), BF16), pltpu.VMEM((ffc, d), BF16)],
        ),
        out_shape=jax.ShapeDtypeStruct((nf, n_blocks * tm, kw), U32),
        compiler_params=_params(("arbitrary", "arbitrary")),
        name="moe_experts",
    )(blk_exp, blk_src, nact, xs, w_gate, w_up, w_down)


def _combine_copy(ys_hbm, buf_ref, dest_ref, sem, f, k, tok, r):
    return pltpu.make_async_copy(
        ys_hbm.at[f, pl.ds(dest_ref[MOE_TOPK * tok + k], 1)], buf_ref.at[f * MOE_TOPK + k, pl.ds(r, 1)], sem)


def _combine_body(dest_ref, h_ref, meta_ref, g_ref, ys_hbm, o_ref, buf_ref, sem, *, tb, nf):
    base = pl.program_id(0) * tb

    def start(r, carry):
        for f in range(nf):
            for k in range(MOE_TOPK):
                _combine_copy(ys_hbm, buf_ref, dest_ref, sem, f, k, base + r, r).start()
        return carry

    def wait(r, carry):
        for f in range(nf):
            for k in range(MOE_TOPK):
                _combine_copy(ys_hbm, buf_ref, dest_ref, sem, f, k, base + r, r).wait()
        return carry

    lax.fori_loop(0, tb, start, 0)
    lax.fori_loop(0, tb, wait, 0)

    meta = meta_ref[...]
    gates = [meta[:, 4 + k:5 + k] for k in range(MOE_TOPK)]
    acc_lo = None
    acc_hi = None
    for f in range(nf):
        for k in range(MOE_TOPK):
            lo, hi = _unpack_halves(buf_ref[f * MOE_TOPK + k])
            lo, hi = lo * gates[k], hi * gates[k]
            acc_lo = lo if acc_lo is None else acc_lo + lo
            acc_hi = hi if acc_hi is None else acc_hi + hi
    kh = acc_lo.shape[1]
    h_lo = h_ref[:, :kh] + acc_lo
    h_hi = h_ref[:, kh:] + acc_hi
    ms = (jnp.sum(h_lo * h_lo, axis=-1, keepdims=True) + jnp.sum(h_hi * h_hi, axis=-1, keepdims=True)) / (2 * kh)
    scale = lax.rsqrt(ms + NORM_EPS)
    o_ref[:, :kh] = (h_lo * scale) * g_ref[:, :kh]
    o_ref[:, kh:] = (h_hi * scale) * g_ref[:, kh:]


def moe_combine(h, meta, g, ys, dest_flat, *, tb):
    n, d = h.shape
    nf, _, kw = ys.shape
    tb = min(tb, n)
    return pl.pallas_call(
        functools.partial(_combine_body, tb=tb, nf=nf),
        grid_spec=pltpu.PrefetchScalarGridSpec(
            num_scalar_prefetch=1,
            grid=(n // tb,),
            in_specs=[
                pl.BlockSpec((tb, d), lambda i, dst: (i, 0)),
                pl.BlockSpec((tb, V7X_LANES), lambda i, dst: (i, 0)),
                pl.BlockSpec((1, d), lambda i, dst: (0, 0)),
                pl.BlockSpec(memory_space=pl.ANY),
            ],
            out_specs=pl.BlockSpec((tb, d), lambda i, dst: (i, 0)),
            scratch_shapes=[pltpu.VMEM((nf * MOE_TOPK, tb, kw), U32), pltpu.SemaphoreType.DMA(())],
        ),
        out_shape=jax.ShapeDtypeStruct((n, d), F32),
        compiler_params=_params(("arbitrary",)),
        name="moe_combine",
    )(dest_flat, h, meta, g.reshape(1, d), ys)


def _moe_plan(meta, counts, *, tm, n_blocks):
    n_exp = counts.shape[0]
    expert = meta[:, 0:MOE_TOPK].astype(I32)
    rank = meta[:, MOE_TOPK:2 * MOE_TOPK].astype(I32)
    padded = (counts + tm - 1) // tm * tm
    pad_end = jnp.cumsum(padded)
    pad_start = pad_end - padded
    dest = (pad_start[expert] + rank).reshape(-1)
    nact = pad_end[-1] // tm
    blk = jnp.arange(n_blocks, dtype=I32)
    blk_exp = jnp.minimum(jnp.searchsorted(pad_end, blk * tm, side="right"), n_exp - 1).astype(I32)
    active = blk < nact
    last = jnp.maximum(nact - 1, 0)
    blk_src = jnp.where(active, blk, last).astype(I32)
    blk_exp = jnp.where(active, blk_exp, blk_exp[last]).astype(I32)
    return dest.astype(I32), blk_exp, blk_src, nact.astype(I32).reshape(1)


def kernel(x, mem, positions, norm_mix_g, w_in, ret_norm_g, gm_ln_g, gm_ln_b, gm_ws, gm_bs, w_out, norm_xa_g, norm_mem_g, xa_wq, xa_wkv, xa_wo, norm_moe_g, router_grp_w, router_grp_b, router_exp_w, router_exp_b, moe_w_gate, moe_w_up, moe_w_down, norm_final_g):
    batch, seq, d = x.shape
    mem_len = mem.shape[1]
    n = batch * seq
    ret_width = ret_norm_g.shape[0]
    gm_width = gm_ln_g.shape[0]
    assert ret_width == gm_width and w_in.shape[1] == 4 * ret_width + 2 * gm_width
    dk = ret_width // RET_HEADS
    n_exp = moe_w_gate.shape[0]
    assert n_exp == MOE_GROUPS * MOE_PER_GROUP and MOE_GROUPS + n_exp <= V7X_LANES

    inv_freq = ROPE_BASE ** (-jnp.arange(0, dk, 2, dtype=F32) / dk)
    ang = positions.astype(F32).reshape(n, 1) * inv_freq
    cos, sin = jnp.cos(ang), jnp.sin(ang)

    x2 = x.reshape(n, d)
    proj = norm_matmul(x2, norm_mix_g, w_in.astype(BF16), tm=512, tn=1024)
    ret = retention(proj, cos, sin, ret_norm_g, batch=batch, seq=seq, ret_width=ret_width)
    gm = gmlp(proj, gm_ln_g, gm_ln_b, gm_ws, gm_bs, n_rows=n, gm_width=gm_width,
              u_block=4 * ret_width // gm_width, v_block=4 * ret_width // gm_width + 1)
    h1 = matmul_residual([ret, gm], w_out.astype(BF16), x2, tm=512, tn=1024)

    q = norm_matmul(h1, norm_xa_g, xa_wq.astype(BF16), tm=512, tn=1024)
    kv = norm_matmul(mem.reshape(batch * mem_len, d), norm_mem_g, xa_wkv.astype(BF16), tm=512, tn=1024)
    o = cross_attention(q, kv, batch=batch, seq=seq, mem_len=mem_len, tq=512)
    h2 = matmul_residual([o], xa_wo.astype(BF16), h1, tm=512, tn=1024)

    pad = V7X_LANES - MOE_GROUPS - n_exp
    wr = jnp.concatenate([router_grp_w, router_exp_w, jnp.zeros((d, pad), F32)], axis=1)
    br = jnp.concatenate([router_grp_b, router_exp_b, jnp.zeros((pad,), F32)]).reshape(1, V7X_LANES)
    xp, meta, cnt = moe_router(h2, norm_moe_g, wr, br, tm=256)

    tm = MOE_ROW_TILE
    n_blocks = -(-(n * MOE_TOPK) // tm) + n_exp
    counts = cnt[0, MOE_GROUPS:MOE_GROUPS + n_exp].astype(I32)
    dest, blk_exp, blk_src, nact = _moe_plan(meta, counts, tm=tm, n_blocks=n_blocks)
    xs = moe_dispatch(xp, dest, n_blocks * tm)
    ys = moe_experts(xs, moe_w_gate, moe_w_up, moe_w_down, blk_exp, blk_src, nact, n_blocks=n_blocks, tm=tm)
    y = moe_combine(h2, meta, norm_final_g, ys, dest, tb=128)
    return y.reshape(batch, seq, d)
```

```python
import functools

import jax
import jax.numpy as jnp
from jax import lax
from jax.experimental import pallas as pl
from jax.experimental.pallas import tpu as pltpu

NORM_EPS = 1e-6
RET_HEADS = 8
ROPE_BASE = 10000.0
GM_GROUPS = 8
GM_CHUNK = 128
XA_HEADS = 4
MOE_GROUPS = 8
MOE_PER_GROUP = 8
MOE_TOPK = 2

V7X_LANES = 128
V7X_VMEM_BYTES = 64 * 1024 * 1024
VMEM_LIMIT_BYTES = 56 * 1024 * 1024

RET_BLOCK = 512
MOE_ROW_BLOCK = 512

F32 = jnp.float32
BF16 = jnp.bfloat16
U32 = jnp.uint32
I32 = jnp.int32


def _params(sem):
    return pltpu.CompilerParams(dimension_semantics=sem, vmem_limit_bytes=VMEM_LIMIT_BYTES)


def _tile(dim, target):
    t = min(dim, target)
    while dim % t:
        t -= V7X_LANES
    assert t > 0, (dim, target)
    return t


def _pack_halves(x_f32):
    k = x_f32.shape[-1] // 2
    bits = lax.bitcast_convert_type(x_f32.astype(BF16).astype(F32), U32)
    return (bits[:, k:] & jnp.uint32(0xFFFF0000)) | (bits[:, :k] >> 16)


def _unpack_halves(w_u32):
    lo = lax.bitcast_convert_type(w_u32 << 16, F32)
    hi = lax.bitcast_convert_type(w_u32 & jnp.uint32(0xFFFF0000), F32)
    return lo, hi


def _norm_matmul_body(x_ref, g_ref, w_ref, o_ref, xn_ref):
    @pl.when(pl.program_id(1) == 0)
    def _():
        x = x_ref[...]
        ms = jnp.mean(x * x, axis=-1, keepdims=True)
        xn_ref[...] = ((x * lax.rsqrt(ms + NORM_EPS)) * g_ref[...]).astype(BF16)

    o_ref[...] = jnp.dot(xn_ref[...], w_ref[...], preferred_element_type=F32).astype(o_ref.dtype)


def norm_matmul(x, g, w, *, tm, tn):
    m, k = x.shape
    n = w.shape[1]
    tm, tn = _tile(m, tm), _tile(n, tn)
    return pl.pallas_call(
        _norm_matmul_body,
        grid=(m // tm, n // tn),
        in_specs=[
            pl.BlockSpec((tm, k), lambda i, j: (i, 0)),
            pl.BlockSpec((1, k), lambda i, j: (0, 0)),
            pl.BlockSpec((k, tn), lambda i, j: (0, j)),
        ],
        out_specs=pl.BlockSpec((tm, tn), lambda i, j: (i, j)),
        out_shape=jax.ShapeDtypeStruct((m, n), BF16),
        scratch_shapes=[pltpu.VMEM((tm, k), BF16)],
        compiler_params=_params(("parallel", "arbitrary")),
        name="norm_matmul",
    )(x, g.reshape(1, k), w)


def _retention_body(lg_ref, q_ref, k_ref, v_ref, g_ref, cos_ref, sin_ref, gn_ref, o_ref, state_ref, *, blk, dk):
    h = pl.program_id(1)
    c = pl.program_id(2)
    lg = lg_ref[h]

    @pl.when(c == 0)
    def _():
        state_ref[...] = jnp.zeros_like(state_ref)

    half = dk // 2
    cos = cos_ref[...]
    sin = sin_ref[...]

    def rot(t):
        t1, t2 = t[:, :half], t[:, half:]
        return jnp.concatenate([t1 * cos - t2 * sin, t1 * sin + t2 * cos], axis=-1)

    qr = rot(q_ref[...].astype(F32))
    kr = rot(k_ref[...].astype(F32)) * (dk ** -0.5)
    v = v_ref[...]

    pos = lax.broadcasted_iota(I32, (blk, 1), 0).astype(F32)
    q_dec = jnp.exp((pos + 1.0) * lg)
    k_dec = jnp.exp((blk - 1.0 - pos) * lg)
    blk_dec = jnp.exp(jnp.full((1, dk), blk * lg, F32))

    s = lax.dot_general(qr.astype(BF16), kr.astype(BF16), (((1,), (1,)), ((), ())), preferred_element_type=F32)
    ri = lax.broadcasted_iota(I32, (blk, blk), 0)
    ci = lax.broadcasted_iota(I32, (blk, blk), 1)
    diff = (ri - ci).astype(F32)
    dec = jnp.where(diff >= 0.0, jnp.exp(jnp.maximum(diff, 0.0) * lg), 0.0)
    inner = jnp.dot((s * dec).astype(BF16), v, preferred_element_type=F32)

    state = state_ref[...]
    cross = jnp.dot((qr * q_dec).astype(BF16), state.astype(BF16), preferred_element_type=F32)
    kd_t = jnp.transpose(kr * k_dec).astype(BF16)
    state_ref[...] = state * blk_dec + jnp.dot(kd_t, v, preferred_element_type=F32)

    out = inner + cross
    mu = jnp.mean(out, axis=-1, keepdims=True)
    cen = out - mu
    var = jnp.mean(cen * cen, axis=-1, keepdims=True)
    y = cen * lax.rsqrt(var + NORM_EPS) * gn_ref[...]
    gate = g_ref[...].astype(F32)
    o_ref[...] = (y * (gate * jax.nn.sigmoid(gate))).astype(o_ref.dtype)


def retention(proj, cos, sin, ret_norm_g, *, batch, seq, ret_width):
    heads = RET_HEADS
    dk = ret_width // heads
    blk = min(RET_BLOCK, seq)
    nblk = seq // blk
    hb = ret_width // dk
    log_gamma = jnp.log(1.0 - jnp.exp2(-5.0 - jnp.arange(heads, dtype=F32)))

    def col(seg):
        return pl.BlockSpec((blk, dk), lambda b, h, c, lg, seg=seg: (b * nblk + c, seg * hb + h))

    rowspec = pl.BlockSpec((blk, dk // 2), lambda b, h, c, lg: (b * nblk + c, 0))
    return pl.pallas_call(
        functools.partial(_retention_body, blk=blk, dk=dk),
        grid_spec=pltpu.PrefetchScalarGridSpec(
            num_scalar_prefetch=1,
            grid=(batch, heads, nblk),
            in_specs=[col(0), col(1), col(2), col(3), rowspec, rowspec,
                      pl.BlockSpec((1, dk), lambda b, h, c, lg: (0, h))],
            out_specs=pl.BlockSpec((blk, dk), lambda b, h, c, lg: (b * nblk + c, h)),
            scratch_shapes=[pltpu.VMEM((dk, dk), F32)],
        ),
        out_shape=jax.ShapeDtypeStruct((batch * seq, ret_width), BF16),
        compiler_params=_params(("parallel", "parallel", "arbitrary")),
        name="retention",
    )(log_gamma, proj, proj, proj, proj, cos, sin, ret_norm_g.reshape(1, ret_width))


def _gmlp_body(u_ref, v_ref, lng_ref, lnb_ref, ws_ref, bst_ref, o_ref, vn_ref, *, rows, groups, cg, chunk):
    v = jax.nn.gelu(v_ref[...].astype(F32))
    mu = jnp.mean(v, axis=-1, keepdims=True)
    cen = v - mu
    var = jnp.mean(cen * cen, axis=-1, keepdims=True)
    vn_ref[...] = (cen * lax.rsqrt(var + NORM_EPS) * lng_ref[...] + lnb_ref[...]).astype(BF16)

    ri = lax.broadcasted_iota(I32, (chunk, chunk), 0)
    ci = lax.broadcasted_iota(I32, (chunk, chunk), 1)
    causal = ri >= ci
    for g in range(groups):
        w = jnp.where(causal, ws_ref[g], 0.0).astype(BF16)
        bias = bst_ref[:, g:g + 1]
        cols = slice(g * cg, (g + 1) * cg)
        for t in range(rows // chunk):
            rws = slice(t * chunk, (t + 1) * chunk)
            sp = jnp.dot(w, vn_ref[rws, cols], preferred_element_type=F32) + bias
            u = jax.nn.gelu(u_ref[rws, cols].astype(F32))
            o_ref[rws, cols] = (u * sp).astype(o_ref.dtype)


def gmlp(proj, ln_g, ln_b, ws, bs, *, n_rows, gm_width, u_block, v_block):
    groups, chunk = GM_GROUPS, GM_CHUNK
    cg = gm_width // groups
    rows = 2 * chunk
    return pl.pallas_call(
        functools.partial(_gmlp_body, rows=rows, groups=groups, cg=cg, chunk=chunk),
        grid=(n_rows // rows,),
        in_specs=[
            pl.BlockSpec((rows, gm_width), lambda i: (i, u_block)),
            pl.BlockSpec((rows, gm_width), lambda i: (i, v_block)),
            pl.BlockSpec((1, gm_width), lambda i: (0, 0)),
            pl.BlockSpec((1, gm_width), lambda i: (0, 0)),
            pl.BlockSpec((groups, chunk, chunk), lambda i: (0, 0, 0)),
            pl.BlockSpec((chunk, groups), lambda i: (0, 0)),
        ],
        out_specs=pl.BlockSpec((rows, gm_width), lambda i: (i, 0)),
        out_shape=jax.ShapeDtypeStruct((n_rows, gm_width), BF16),
        scratch_shapes=[pltpu.VMEM((rows, gm_width), BF16)],
        compiler_params=_params(("parallel",)),
        name="gmlp",
    )(proj, proj, ln_g.reshape(1, gm_width), ln_b.reshape(1, gm_width), ws, bs.T)


def _matmul_residual_body(*refs, n_parts):
    a_refs, w_refs = refs[:n_parts], refs[n_parts:2 * n_parts]
    r_ref, o_ref = refs[2 * n_parts], refs[2 * n_parts + 1]
    acc = r_ref[...]
    for a_ref, w_ref in zip(a_refs, w_refs):
        acc = acc + jnp.dot(a_ref[...], w_ref[...], preferred_element_type=F32)
    o_ref[...] = acc


def matmul_residual(parts, w, res, *, tm, tn):
    m, n = res.shape
    tm, tn = _tile(m, tm), _tile(n, tn)
    kp = parts[0].shape[1]
    n_parts = len(parts)
    a_specs = [pl.BlockSpec((tm, kp), lambda i, j: (i, 0)) for _ in parts]
    w_specs = [pl.BlockSpec((kp, tn), lambda i, j, p=p: (p, j)) for p in range(n_parts)]
    return pl.pallas_call(
        functools.partial(_matmul_residual_body, n_parts=n_parts),
        grid=(m // tm, n // tn),
        in_specs=a_specs + w_specs + [pl.BlockSpec((tm, tn), lambda i, j: (i, j))],
        out_specs=pl.BlockSpec((tm, tn), lambda i, j: (i, j)),
        out_shape=jax.ShapeDtypeStruct((m, n), F32),
        compiler_params=_params(("parallel", "parallel")),
        name="matmul_residual",
    )(*parts, *([w] * n_parts), res)


def _cross_attention_body(q_ref, kv_ref, o_ref, *, heads, dh):
    width = heads * dh
    for h in range(heads):
        q = q_ref[:, h * dh:(h + 1) * dh]
        k = kv_ref[:, h * dh:(h + 1) * dh]
        v = kv_ref[:, width + h * dh:width + (h + 1) * dh]
        s = lax.dot_general(q, k, (((1,), (1,)), ((), ())), preferred_element_type=F32) * (dh ** -0.5)
        e = jnp.exp(s - jnp.max(s, axis=-1, keepdims=True))
        p = e / jnp.sum(e, axis=-1, keepdims=True)
        o_ref[:, h * dh:(h + 1) * dh] = jnp.dot(p.astype(BF16), v, preferred_element_type=F32).astype(o_ref.dtype)


def cross_attention(q, kv, *, batch, seq, mem_len, tq):
    heads = XA_HEADS
    width = q.shape[1]
    dh = width // heads
    tq = min(tq, seq)
    nq = seq // tq
    return pl.pallas_call(
        functools.partial(_cross_attention_body, heads=heads, dh=dh),
        grid=(batch, nq),
        in_specs=[
            pl.BlockSpec((tq, width), lambda b, i: (b * nq + i, 0)),
            pl.BlockSpec((mem_len, 2 * width), lambda b, i: (b, 0)),
        ],
        out_specs=pl.BlockSpec((tq, width), lambda b, i: (b * nq + i, 0)),
        out_shape=jax.ShapeDtypeStruct((batch * seq, width), BF16),
        compiler_params=_params(("parallel", "parallel")),
        name="cross_attention",
    )(q, kv)


def _router_body(h_ref, g_ref, wr_ref, br_ref, xp_ref, meta_ref, cnt_ref, base_ref, *, tm, n_grp, per):
    @pl.when(pl.program_id(0) == 0)
    def _():
        base_ref[...] = jnp.zeros_like(base_ref)

    x = h_ref[...]
    ms = jnp.mean(x * x, axis=-1, keepdims=True)
    xn = (x * lax.rsqrt(ms + NORM_EPS)) * g_ref[...]
    xp_ref[...] = _pack_halves(xn)

    logits = jnp.dot(xn, wr_ref[...], preferred_element_type=F32, precision=lax.Precision.HIGHEST) + br_ref[...]
    lane = lax.broadcasted_iota(I32, logits.shape, 1)
    neg = jnp.float32(-1e30)
    big = jnp.int32(V7X_LANES)

    gl = jnp.where(lane < n_grp, logits, neg)
    gmax = jnp.max(gl, axis=-1, keepdims=True)
    gidx = jnp.min(jnp.where(gl == gmax, lane, big), axis=-1, keepdims=True)
    grp_gate = 1.0 / jnp.sum(jnp.exp(gl - gmax), axis=-1, keepdims=True)

    lo = n_grp + gidx * per
    el = jnp.where((lane >= lo) & (lane < lo + per), logits, neg)
    v1 = jnp.max(el, axis=-1, keepdims=True)
    i1 = jnp.min(jnp.where(el == v1, lane, big), axis=-1, keepdims=True)
    el2 = jnp.where(lane == i1, neg, el)
    v2 = jnp.max(el2, axis=-1, keepdims=True)
    i2 = jnp.min(jnp.where(el2 == v2, lane, big), axis=-1, keepdims=True)
    t = jnp.exp(v2 - v1)
    den = 1.0 + t
    g1 = grp_gate / den
    g2 = grp_gate * (t / den)

    oh1 = jnp.where(lane == i1, 1.0, 0.0)
    oh2 = jnp.where(lane == i2, 1.0, 0.0)
    ri = lax.broadcasted_iota(I32, (tm, tm), 0)
    ci = lax.broadcasted_iota(I32, (tm, tm), 1)
    lower = jnp.where(ri > ci, 1.0, 0.0).astype(BF16)
    pre1 = jnp.dot(lower, oh1.astype(BF16), preferred_element_type=F32)
    pre2 = jnp.dot(lower, oh2.astype(BF16), preferred_element_type=F32)
    cnt1 = jnp.sum(oh1, axis=0, keepdims=True)
    cnt2 = jnp.sum(oh2, axis=0, keepdims=True)
    base = base_ref[...]
    rank1 = jnp.sum(oh1 * (pre1 + base), axis=-1, keepdims=True)
    rank2 = jnp.sum(oh2 * (pre2 + base + cnt1), axis=-1, keepdims=True)
    total = base + cnt1 + cnt2
    base_ref[...] = total
    cnt_ref[...] = total

    e1 = (i1 - n_grp).astype(F32)
    e2 = (i2 - n_grp).astype(F32)
    meta = jnp.zeros(logits.shape, F32)
    for idx, val in enumerate((e1, e2, rank1, rank2, g1, g2)):
        meta = jnp.where(lane == idx, val, meta)
    meta_ref[...] = meta


def moe_router(h, g, wr, br, *, tm):
    n, d = h.shape
    tm = min(tm, n)
    return pl.pallas_call(
        functools.partial(_router_body, tm=tm, n_grp=MOE_GROUPS, per=MOE_PER_GROUP),
        grid=(n // tm,),
        in_specs=[
            pl.BlockSpec((tm, d), lambda i: (i, 0)),
            pl.BlockSpec((1, d), lambda i: (0, 0)),
            pl.BlockSpec((d, V7X_LANES), lambda i: (0, 0)),
            pl.BlockSpec((1, V7X_LANES), lambda i: (0, 0)),
        ],
        out_specs=[
            pl.BlockSpec((tm, d // 2), lambda i: (i, 0)),
            pl.BlockSpec((tm, V7X_LANES), lambda i: (i, 0)),
            pl.BlockSpec((1, V7X_LANES), lambda i: (0, 0)),
        ],
        out_shape=[
            jax.ShapeDtypeStruct((n, d // 2), U32),
            jax.ShapeDtypeStruct((n, V7X_LANES), F32),
            jax.ShapeDtypeStruct((1, V7X_LANES), F32),
        ],
        scratch_shapes=[pltpu.VMEM((1, V7X_LANES), F32)],
        compiler_params=_params(("arbitrary",)),
        name="moe_router",
    )(h, g.reshape(1, d), wr, br)


DISPATCH_TOKENS_PER_STEP = 256


def _dispatch_copy(x_ref, o_hbm, dest_ref, sem, base, r, k):
    return pltpu.make_async_copy(
        x_ref.at[pl.ds(r, 1)], o_hbm.at[pl.ds(dest_ref[MOE_TOPK * (base + r) + k], 1)], sem)


def _fill_copies(fstart_ref, flen_ref, z_ref, o_hbm, sem, *, n_ranges, rb, wait):
    sub = 8

    def issue(src, dst):
        cp = pltpu.make_async_copy(src, dst, sem)
        if wait:
            cp.wait()
        else:
            cp.start()

    def per_range(i, carry):
        length = flen_ref[i]
        first = fstart_ref[i]
        off = first + length
        size = rb
        while size >= sub:
            take = (length & size) != 0
            off = off - jnp.where(take, size, 0)

            @pl.when(take)
            def _(off=off, size=size):
                issue(z_ref.at[pl.ds(0, size)], o_hbm.at[pl.ds(pl.multiple_of(off, sub), size)])

            size //= 2
        for j in range(sub - 1):
            @pl.when(j < (length & (sub - 1)))
            def _(j=j):
                issue(z_ref.at[pl.ds(0, 1)], o_hbm.at[pl.ds(first + j, 1)])
        return carry

    lax.fori_loop(0, n_ranges, per_range, 0)


def _dispatch_body(dest_ref, fstart_ref, flen_ref, x_ref, o_hbm, z_ref, sem, fill_sem, *, tb, n_ranges, rb):
    step = pl.program_id(0)
    base = step * tb

    @pl.when(step == 0)
    def _():
        z_ref[...] = jnp.zeros_like(z_ref)
        _fill_copies(fstart_ref, flen_ref, z_ref, o_hbm, fill_sem, n_ranges=n_ranges, rb=rb, wait=False)

    def start(r, carry):
        for k in range(MOE_TOPK):
            _dispatch_copy(x_ref, o_hbm, dest_ref, sem, base, r, k).start()
        return carry

    def wait(r, carry):
        for k in range(MOE_TOPK):
            _dispatch_copy(x_ref, o_hbm, dest_ref, sem, base, r, k).wait()
        return carry

    lax.fori_loop(0, tb, start, 0)
    lax.fori_loop(0, tb, wait, 0)

    @pl.when(step == pl.num_programs(0) - 1)
    def _():
        _fill_copies(fstart_ref, flen_ref, z_ref, o_hbm, fill_sem, n_ranges=n_ranges, rb=rb, wait=True)


def moe_dispatch(xp, dest_flat, fill_start, fill_len, *, n_sorted_rows, rb):
    n, kw = xp.shape
    tb = _tile(n, DISPATCH_TOKENS_PER_STEP)
    n_ranges = fill_start.shape[0]
    return pl.pallas_call(
        functools.partial(_dispatch_body, tb=tb, n_ranges=n_ranges, rb=rb),
        grid_spec=pltpu.PrefetchScalarGridSpec(
            num_scalar_prefetch=3,
            grid=(n // tb,),
            in_specs=[pl.BlockSpec((tb, kw), lambda i, dst, fs, fl: (i, 0))],
            out_specs=pl.BlockSpec(memory_space=pl.ANY),
            scratch_shapes=[pltpu.VMEM((rb, kw), xp.dtype), pltpu.SemaphoreType.DMA(()),
                            pltpu.SemaphoreType.DMA(())],
        ),
        out_shape=jax.ShapeDtypeStruct((n_sorted_rows, kw), xp.dtype),
        compiler_params=_params(("arbitrary",)),
        name="moe_dispatch",
    )(dest_flat, fill_start, fill_len, xp)


EXPERT_STEPS = 4


def _experts_body(exp_ref, nxt_ref, src_ref, nact_ref, x_ref, wg_ref, wu_ref, wdl_ref, wdh_ref, o_ref,
                  xl_ref, xh_ref, hg_ref, hu_ref, hd_ref):
    del exp_ref, nxt_ref, src_ref
    v = pl.program_id(0)
    s = pl.program_id(1)
    active = v < nact_ref[0]

    @pl.when(jnp.logical_and(active, s == 0))
    def _():
        lo, hi = _unpack_halves(x_ref[...])
        xl_ref[...] = lo.astype(BF16)
        xh_ref[...] = hi.astype(BF16)
        hg_ref[...] = jnp.dot(xl_ref[...], wg_ref[0].astype(BF16), preferred_element_type=F32)
        hu_ref[...] = jnp.dot(xl_ref[...], wu_ref[0].astype(BF16), preferred_element_type=F32)

    @pl.when(jnp.logical_and(active, s == 1))
    def _():
        hg = hg_ref[...] + jnp.dot(xh_ref[...], wg_ref[0].astype(BF16), preferred_element_type=F32)
        hu = hu_ref[...] + jnp.dot(xh_ref[...], wu_ref[0].astype(BF16), preferred_element_type=F32)
        hd_ref[...] = (hg * jax.nn.sigmoid(hg) * hu).astype(BF16)

    @pl.when(jnp.logical_and(active, s >= 2))
    def _():
        y_lo = jnp.dot(hd_ref[...], wdl_ref[0].astype(BF16), preferred_element_type=F32)
        y_hi = jnp.dot(hd_ref[...], wdh_ref[0].astype(BF16), preferred_element_type=F32)
        o_ref[...] = _pack_halves(jnp.concatenate([y_lo, y_hi], axis=-1))

    @pl.when(jnp.logical_not(active))
    def _():
        o_ref[...] = jnp.zeros_like(o_ref)


def moe_experts(xs, w_gate, w_up, w_down, blk_exp, blk_next, blk_src, nact, *, n_blocks, rb):
    n_exp, d, ff = w_gate.shape
    kw = xs.shape[1]
    assert d == 2 * kw
    oc = kw // 2

    def act(v, n):
        return v < n[0]

    def w_in_map(v, s, e, nx, sr, n):
        first = jnp.logical_and(act(v, n), s < 2)
        return (jnp.where(first, e[v], nx[v]), jnp.where(first, s, 0), 0)

    def w_down_map(half):
        def index(v, s, e, nx, sr, n):
            j = jnp.where(act(v, n), jnp.maximum(s - 2, 0), 1)
            return (e[v], 0, 2 * half + j)
        return index

    def out_map(v, s, e, nx, sr, n):
        return (v, jnp.maximum(s - 2, 0))

    return pl.pallas_call(
        _experts_body,
        grid_spec=pltpu.PrefetchScalarGridSpec(
            num_scalar_prefetch=4,
            grid=(n_blocks, EXPERT_STEPS),
            in_specs=[
                pl.BlockSpec((rb, kw), lambda v, s, e, nx, sr, n: (sr[v], 0)),
                pl.BlockSpec((1, d // 2, ff), w_in_map),
                pl.BlockSpec((1, d // 2, ff), w_in_map),
                pl.BlockSpec((1, ff, oc), w_down_map(0)),
                pl.BlockSpec((1, ff, oc), w_down_map(1)),
            ],
            out_specs=pl.BlockSpec((rb, oc), out_map),
            scratch_shapes=[pltpu.VMEM((rb, kw), BF16), pltpu.VMEM((rb, kw), BF16),
                            pltpu.VMEM((rb, ff), F32), pltpu.VMEM((rb, ff), F32), pltpu.VMEM((rb, ff), BF16)],
        ),
        out_shape=jax.ShapeDtypeStruct((n_blocks * rb, kw), U32),
        compiler_params=_params(("arbitrary", "arbitrary")),
        name="moe_experts",
    )(blk_exp, blk_next, blk_src, nact, xs, w_gate, w_up, w_down, w_down)


def _combine_copy(ys_hbm, buf_ref, dest_ref, sem, k, tok, r):
    return pltpu.make_async_copy(
        ys_hbm.at[pl.ds(dest_ref[MOE_TOPK * tok + k], 1)], buf_ref.at[k, pl.ds(r, 1)], sem)


def _combine_body(dest_ref, h_ref, meta_ref, g_ref, ys_hbm, o_ref, buf_ref, sem, *, tb):
    base = pl.program_id(0) * tb

    def start(r, carry):
        for k in range(MOE_TOPK):
            _combine_copy(ys_hbm, buf_ref, dest_ref, sem, k, base + r, r).start()
        return carry

    def wait(r, carry):
        for k in range(MOE_TOPK):
            _combine_copy(ys_hbm, buf_ref, dest_ref, sem, k, base + r, r).wait()
        return carry

    lax.fori_loop(0, tb, start, 0)
    lax.fori_loop(0, tb, wait, 0)

    meta = meta_ref[...]
    acc_lo = None
    acc_hi = None
    for k in range(MOE_TOPK):
        gate = meta[:, 2 * MOE_TOPK + k:2 * MOE_TOPK + k + 1]
        lo, hi = _unpack_halves(buf_ref[k])
        lo, hi = lo * gate, hi * gate
        acc_lo = lo if acc_lo is None else acc_lo + lo
        acc_hi = hi if acc_hi is None else acc_hi + hi
    kh = acc_lo.shape[1]
    h_lo = h_ref[:, :kh] + acc_lo
    h_hi = h_ref[:, kh:] + acc_hi
    ms = (jnp.sum(h_lo * h_lo, axis=-1, keepdims=True) + jnp.sum(h_hi * h_hi, axis=-1, keepdims=True)) / (2 * kh)
    scale = lax.rsqrt(ms + NORM_EPS)
    o_ref[:, :kh] = (h_lo * scale) * g_ref[:, :kh]
    o_ref[:, kh:] = (h_hi * scale) * g_ref[:, kh:]


def moe_combine(h, meta, g, ys, dest_flat, *, tb):
    n, d = h.shape
    kw = ys.shape[1]
    tb = min(tb, n)
    return pl.pallas_call(
        functools.partial(_combine_body, tb=tb),
        grid_spec=pltpu.PrefetchScalarGridSpec(
            num_scalar_prefetch=1,
            grid=(n // tb,),
            in_specs=[
                pl.BlockSpec((tb, d), lambda i, dst: (i, 0)),
                pl.BlockSpec((tb, V7X_LANES), lambda i, dst: (i, 0)),
                pl.BlockSpec((1, d), lambda i, dst: (0, 0)),
                pl.BlockSpec(memory_space=pl.ANY),
            ],
            out_specs=pl.BlockSpec((tb, d), lambda i, dst: (i, 0)),
            scratch_shapes=[pltpu.VMEM((MOE_TOPK, tb, kw), U32), pltpu.SemaphoreType.DMA(())],
        ),
        out_shape=jax.ShapeDtypeStruct((n, d), F32),
        compiler_params=_params(("arbitrary",)),
        name="moe_combine",
    )(dest_flat, h, meta, g.reshape(1, d), ys)


def _moe_plan(meta, counts, *, rb, n_blocks):
    n_exp = counts.shape[0]
    expert = meta[:, 0:MOE_TOPK].astype(I32)
    rank = meta[:, MOE_TOPK:2 * MOE_TOPK].astype(I32)
    padded = (counts + rb - 1) // rb * rb
    pad_end = jnp.cumsum(padded)
    pad_start = pad_end - padded
    dest = (pad_start[expert] + rank).reshape(-1).astype(I32)
    nact = (pad_end[-1] // rb).astype(I32)
    blk = jnp.arange(n_blocks, dtype=I32)
    blk_exp = jnp.minimum(jnp.searchsorted(pad_end, blk * rb, side="right"), n_exp - 1).astype(I32)
    active = blk < nact
    last = jnp.maximum(nact - 1, 0)
    blk_src = jnp.where(active, blk, last).astype(I32)
    blk_exp = jnp.where(active, blk_exp, blk_exp[last]).astype(I32)
    blk_next = blk_exp[jnp.minimum(blk + 1, last)]
    fill_start = jnp.concatenate([pad_start + counts, blk * rb]).astype(I32)
    fill_len = jnp.concatenate([padded - counts, jnp.where(active, 0, rb)]).astype(I32)
    return dest, blk_exp, blk_next, blk_src, nact.reshape(1), fill_start, fill_len


def kernel(x, mem, positions, norm_mix_g, w_in, ret_norm_g, gm_ln_g, gm_ln_b, gm_ws, gm_bs, w_out, norm_xa_g, norm_mem_g, xa_wq, xa_wkv, xa_wo, norm_moe_g, router_grp_w, router_grp_b, router_exp_w, router_exp_b, moe_w_gate, moe_w_up, moe_w_down, norm_final_g):
    batch, seq, d = x.shape
    mem_len = mem.shape[1]
    n = batch * seq
    ret_width = ret_norm_g.shape[0]
    gm_width = gm_ln_g.shape[0]
    assert ret_width == gm_width and w_in.shape[1] == 4 * ret_width + 2 * gm_width
    dk = ret_width // RET_HEADS
    n_exp = moe_w_gate.shape[0]
    assert n_exp == MOE_GROUPS * MOE_PER_GROUP and MOE_GROUPS + n_exp <= V7X_LANES

    inv_freq = ROPE_BASE ** (-jnp.arange(0, dk, 2, dtype=F32) / dk)
    ang = positions.astype(F32).reshape(n, 1) * inv_freq
    cos, sin = jnp.cos(ang), jnp.sin(ang)

    x2 = x.reshape(n, d)
    proj = norm_matmul(x2, norm_mix_g, w_in.astype(BF16), tm=512, tn=1024)
    ret = retention(proj, cos, sin, ret_norm_g, batch=batch, seq=seq, ret_width=ret_width)
    gm = gmlp(proj, gm_ln_g, gm_ln_b, gm_ws, gm_bs, n_rows=n, gm_width=gm_width,
              u_block=4 * ret_width // gm_width, v_block=4 * ret_width // gm_width + 1)
    h1 = matmul_residual([ret, gm], w_out.astype(BF16), x2, tm=512, tn=1024)

    q = norm_matmul(h1, norm_xa_g, xa_wq.astype(BF16), tm=512, tn=1024)
    kv = norm_matmul(mem.reshape(batch * mem_len, d), norm_mem_g, xa_wkv.astype(BF16), tm=512, tn=1024)
    o = cross_attention(q, kv, batch=batch, seq=seq, mem_len=mem_len, tq=512)
    h2 = matmul_residual([o], xa_wo.astype(BF16), h1, tm=512, tn=1024)

    pad = V7X_LANES - MOE_GROUPS - n_exp
    wr = jnp.concatenate([router_grp_w, router_exp_w, jnp.zeros((d, pad), F32)], axis=1)
    br = jnp.concatenate([router_grp_b, router_exp_b, jnp.zeros((pad,), F32)]).reshape(1, V7X_LANES)
    xp, meta, cnt = moe_router(h2, norm_moe_g, wr, br, tm=256)

    rb = MOE_ROW_BLOCK
    n_blocks = -(-(n * MOE_TOPK) // rb) + n_exp
    counts = cnt[0, MOE_GROUPS:MOE_GROUPS + n_exp].astype(I32)
    dest, blk_exp, blk_next, blk_src, nact, fill_start, fill_len = _moe_plan(meta, counts, rb=rb, n_blocks=n_blocks)
    xs = moe_dispatch(xp, dest, fill_start, fill_len, n_sorted_rows=n_blocks * rb, rb=rb)
    ys = moe_experts(xs, moe_w_gate, moe_w_up, moe_w_down, blk_exp, blk_next, blk_src, nact,
                     n_blocks=n_blocks, rb=rb)
    y = moe_combine(h2, meta, norm_final_g, ys, dest, tb=128)
    return y.reshape(batch, seq, d)
```

```python
import functools

import jax
import jax.numpy as jnp
from jax import lax
from jax.experimental import pallas as pl
from jax.experimental.pallas import tpu as pltpu

NORM_EPS = 1e-6
RET_HEADS = 8
ROPE_BASE = 10000.0
GM_GROUPS = 8
GM_CHUNK = 128
XA_HEADS = 4
MOE_GROUPS = 8
MOE_PER_GROUP = 8
MOE_TOPK = 2

V7X_LANES = 128
V7X_VMEM_BYTES = 64 * 1024 * 1024
VMEM_LIMIT_BYTES = 56 * 1024 * 1024

RET_BLOCK = 512
MOE_ROW_BLOCK = 512

F32 = jnp.float32
BF16 = jnp.bfloat16
U32 = jnp.uint32
I32 = jnp.int32


def _params(sem):
    return pltpu.CompilerParams(dimension_semantics=sem, vmem_limit_bytes=VMEM_LIMIT_BYTES)


def _tile(dim, target):
    t = min(dim, target)
    while dim % t:
        t -= V7X_LANES
    assert t > 0, (dim, target)
    return t


def _pack_halves(x_f32):
    k = x_f32.shape[-1] // 2
    bits = lax.bitcast_convert_type(x_f32.astype(BF16).astype(F32), U32)
    return (bits[:, k:] & jnp.uint32(0xFFFF0000)) | (bits[:, :k] >> 16)


def _unpack_halves(w_u32):
    lo = lax.bitcast_convert_type(w_u32 << 16, F32)
    hi = lax.bitcast_convert_type(w_u32 & jnp.uint32(0xFFFF0000), F32)
    return lo, hi


def _norm_matmul_body(x_ref, g_ref, w_ref, o_ref, xn_ref):
    @pl.when(pl.program_id(1) == 0)
    def _():
        x = x_ref[...]
        ms = jnp.mean(x * x, axis=-1, keepdims=True)
        xn_ref[...] = ((x * lax.rsqrt(ms + NORM_EPS)) * g_ref[...]).astype(BF16)

    o_ref[...] = jnp.dot(xn_ref[...], w_ref[...], preferred_element_type=F32).astype(o_ref.dtype)


def norm_matmul(x, g, w, *, tm, tn):
    m, k = x.shape
    n = w.shape[1]
    tm, tn = _tile(m, tm), _tile(n, tn)
    return pl.pallas_call(
        _norm_matmul_body,
        grid=(m // tm, n // tn),
        in_specs=[
            pl.BlockSpec((tm, k), lambda i, j: (i, 0)),
            pl.BlockSpec((1, k), lambda i, j: (0, 0)),
            pl.BlockSpec((k, tn), lambda i, j: (0, j)),
        ],
        out_specs=pl.BlockSpec((tm, tn), lambda i, j: (i, j)),
        out_shape=jax.ShapeDtypeStruct((m, n), BF16),
        scratch_shapes=[pltpu.VMEM((tm, k), BF16)],
        compiler_params=_params(("parallel", "arbitrary")),
        name="norm_matmul",
    )(x, g.reshape(1, k), w)


def _retention_body(lg_ref, q_ref, k_ref, v_ref, g_ref, cos_ref, sin_ref, gn_ref, o_ref, state_ref, *, blk, dk):
    h = pl.program_id(1)
    c = pl.program_id(2)
    lg = lg_ref[h]

    @pl.when(c == 0)
    def _():
        state_ref[...] = jnp.zeros_like(state_ref)

    half = dk // 2
    cos = cos_ref[...]
    sin = sin_ref[...]

    def rot(t):
        t1, t2 = t[:, :half], t[:, half:]
        return jnp.concatenate([t1 * cos - t2 * sin, t1 * sin + t2 * cos], axis=-1)

    qr = rot(q_ref[...].astype(F32))
    kr = rot(k_ref[...].astype(F32)) * (dk ** -0.5)
    v = v_ref[...]

    pos = lax.broadcasted_iota(I32, (blk, 1), 0).astype(F32)
    q_dec = jnp.exp((pos + 1.0) * lg)
    k_dec = jnp.exp((blk - 1.0 - pos) * lg)
    blk_dec = jnp.exp(jnp.full((1, dk), blk * lg, F32))

    s = lax.dot_general(qr.astype(BF16), kr.astype(BF16), (((1,), (1,)), ((), ())), preferred_element_type=F32)
    ri = lax.broadcasted_iota(I32, (blk, blk), 0)
    ci = lax.broadcasted_iota(I32, (blk, blk), 1)
    diff = (ri - ci).astype(F32)
    dec = jnp.where(diff >= 0.0, jnp.exp(jnp.maximum(diff, 0.0) * lg), 0.0)
    inner = jnp.dot((s * dec).astype(BF16), v, preferred_element_type=F32)

    state = state_ref[...]
    cross = jnp.dot((qr * q_dec).astype(BF16), state.astype(BF16), preferred_element_type=F32)
    kd_t = jnp.transpose(kr * k_dec).astype(BF16)
    state_ref[...] = state * blk_dec + jnp.dot(kd_t, v, preferred_element_type=F32)

    out = inner + cross
    mu = jnp.mean(out, axis=-1, keepdims=True)
    cen = out - mu
    var = jnp.mean(cen * cen, axis=-1, keepdims=True)
    y = cen * lax.rsqrt(var + NORM_EPS) * gn_ref[...]
    gate = g_ref[...].astype(F32)
    o_ref[...] = (y * (gate * jax.nn.sigmoid(gate))).astype(o_ref.dtype)


def retention(proj, cos, sin, ret_norm_g, *, batch, seq, ret_width):
    heads = RET_HEADS
    dk = ret_width // heads
    blk = min(RET_BLOCK, seq)
    nblk = seq // blk
    hb = ret_width // dk
    log_gamma = jnp.log(1.0 - jnp.exp2(-5.0 - jnp.arange(heads, dtype=F32)))

    def col(seg):
        return pl.BlockSpec((blk, dk), lambda b, h, c, lg, seg=seg: (b * nblk + c, seg * hb + h))

    rowspec = pl.BlockSpec((blk, dk // 2), lambda b, h, c, lg: (b * nblk + c, 0))
    return pl.pallas_call(
        functools.partial(_retention_body, blk=blk, dk=dk),
        grid_spec=pltpu.PrefetchScalarGridSpec(
            num_scalar_prefetch=1,
            grid=(batch, heads, nblk),
            in_specs=[col(0), col(1), col(2), col(3), rowspec, rowspec,
                      pl.BlockSpec((1, dk), lambda b, h, c, lg: (0, h))],
            out_specs=pl.BlockSpec((blk, dk), lambda b, h, c, lg: (b * nblk + c, h)),
            scratch_shapes=[pltpu.VMEM((dk, dk), F32)],
        ),
        out_shape=jax.ShapeDtypeStruct((batch * seq, ret_width), BF16),
        compiler_params=_params(("parallel", "parallel", "arbitrary")),
        name="retention",
    )(log_gamma, proj, proj, proj, proj, cos, sin, ret_norm_g.reshape(1, ret_width))


def _gmlp_body(u_ref, v_ref, lng_ref, lnb_ref, ws_ref, bst_ref, o_ref, vn_ref, *, rows, groups, cg, chunk):
    v = jax.nn.gelu(v_ref[...].astype(F32))
    mu = jnp.mean(v, axis=-1, keepdims=True)
    cen = v - mu
    var = jnp.mean(cen * cen, axis=-1, keepdims=True)
    vn_ref[...] = (cen * lax.rsqrt(var + NORM_EPS) * lng_ref[...] + lnb_ref[...]).astype(BF16)

    ri = lax.broadcasted_iota(I32, (chunk, chunk), 0)
    ci = lax.broadcasted_iota(I32, (chunk, chunk), 1)
    causal = ri >= ci
    for g in range(groups):
        w = jnp.where(causal, ws_ref[g], 0.0).astype(BF16)
        bias = bst_ref[:, g:g + 1]
        cols = slice(g * cg, (g + 1) * cg)
        for t in range(rows // chunk):
            rws = slice(t * chunk, (t + 1) * chunk)
            sp = jnp.dot(w, vn_ref[rws, cols], preferred_element_type=F32) + bias
            u = jax.nn.gelu(u_ref[rws, cols].astype(F32))
            o_ref[rws, cols] = (u * sp).astype(o_ref.dtype)


def gmlp(proj, ln_g, ln_b, ws, bs, *, n_rows, gm_width, u_block, v_block):
    groups, chunk = GM_GROUPS, GM_CHUNK
    cg = gm_width // groups
    rows = 2 * chunk
    return pl.pallas_call(
        functools.partial(_gmlp_body, rows=rows, groups=groups, cg=cg, chunk=chunk),
        grid=(n_rows // rows,),
        in_specs=[
            pl.BlockSpec((rows, gm_width), lambda i: (i, u_block)),
            pl.BlockSpec((rows, gm_width), lambda i: (i, v_block)),
            pl.BlockSpec((1, gm_width), lambda i: (0, 0)),
            pl.BlockSpec((1, gm_width), lambda i: (0, 0)),
            pl.BlockSpec((groups, chunk, chunk), lambda i: (0, 0, 0)),
            pl.BlockSpec((chunk, groups), lambda i: (0, 0)),
        ],
        out_specs=pl.BlockSpec((rows, gm_width), lambda i: (i, 0)),
        out_shape=jax.ShapeDtypeStruct((n_rows, gm_width), BF16),
        scratch_shapes=[pltpu.VMEM((rows, gm_width), BF16)],
        compiler_params=_params(("parallel",)),
        name="gmlp",
    )(proj, proj, ln_g.reshape(1, gm_width), ln_b.reshape(1, gm_width), ws, bs.T)


def _matmul_residual_body(*refs, n_parts):
    a_refs, w_refs = refs[:n_parts], refs[n_parts:2 * n_parts]
    r_ref, o_ref = refs[2 * n_parts], refs[2 * n_parts + 1]
    acc = r_ref[...]
    for a_ref, w_ref in zip(a_refs, w_refs):
        acc = acc + jnp.dot(a_ref[...], w_ref[...], preferred_element_type=F32)
    o_ref[...] = acc


def matmul_residual(parts, w, res, *, tm, tn):
    m, n = res.shape
    tm, tn = _tile(m, tm), _tile(n, tn)
    kp = parts[0].shape[1]
    n_parts = len(parts)
    a_specs = [pl.BlockSpec((tm, kp), lambda i, j: (i, 0)) for _ in parts]
    w_specs = [pl.BlockSpec((kp, tn), lambda i, j, p=p: (p, j)) for p in range(n_parts)]
    return pl.pallas_call(
        functools.partial(_matmul_residual_body, n_parts=n_parts),
        grid=(m // tm, n // tn),
        in_specs=a_specs + w_specs + [pl.BlockSpec((tm, tn), lambda i, j: (i, j))],
        out_specs=pl.BlockSpec((tm, tn), lambda i, j: (i, j)),
        out_shape=jax.ShapeDtypeStruct((m, n), F32),
        compiler_params=_params(("parallel", "parallel")),
        name="matmul_residual",
    )(*parts, *([w] * n_parts), res)


def _cross_attention_body(q_ref, kv_ref, o_ref, *, heads, dh):
    width = heads * dh
    for h in range(heads):
        q = q_ref[:, h * dh:(h + 1) * dh]
        k = kv_ref[:, h * dh:(h + 1) * dh]
        v = kv_ref[:, width + h * dh:width + (h + 1) * dh]
        s = lax.dot_general(q, k, (((1,), (1,)), ((), ())), preferred_element_type=F32) * (dh ** -0.5)
        e = jnp.exp(s - jnp.max(s, axis=-1, keepdims=True))
        p = e / jnp.sum(e, axis=-1, keepdims=True)
        o_ref[:, h * dh:(h + 1) * dh] = jnp.dot(p.astype(BF16), v, preferred_element_type=F32).astype(o_ref.dtype)


def cross_attention(q, kv, *, batch, seq, mem_len, tq):
    heads = XA_HEADS
    width = q.shape[1]
    dh = width // heads
    tq = min(tq, seq)
    nq = seq // tq
    return pl.pallas_call(
        functools.partial(_cross_attention_body, heads=heads, dh=dh),
        grid=(batch, nq),
        in_specs=[
            pl.BlockSpec((tq, width), lambda b, i: (b * nq + i, 0)),
            pl.BlockSpec((mem_len, 2 * width), lambda b, i: (b, 0)),
        ],
        out_specs=pl.BlockSpec((tq, width), lambda b, i: (b * nq + i, 0)),
        out_shape=jax.ShapeDtypeStruct((batch * seq, width), BF16),
        compiler_params=_params(("parallel", "parallel")),
        name="cross_attention",
    )(q, kv)


def _router_body(h_ref, g_ref, wr_ref, br_ref, xp_ref, meta_ref, meta_t_ref, cnt_ref, base_ref, *, tm, n_grp, per):
    @pl.when(pl.program_id(0) == 0)
    def _():
        base_ref[...] = jnp.zeros_like(base_ref)

    x = h_ref[...]
    ms = jnp.mean(x * x, axis=-1, keepdims=True)
    xn = (x * lax.rsqrt(ms + NORM_EPS)) * g_ref[...]
    xp_ref[...] = _pack_halves(xn)

    x_hi = xn.astype(BF16)
    x_lo = (xn - x_hi.astype(F32)).astype(BF16)
    both = jnp.dot(x_hi, wr_ref[...], preferred_element_type=F32)
    corr = jnp.dot(x_lo, wr_ref[:, :V7X_LANES], preferred_element_type=F32)
    logits = both[:, :V7X_LANES] + (both[:, V7X_LANES:] + corr) + br_ref[...]
    lane = lax.broadcasted_iota(I32, logits.shape, 1)
    neg = jnp.float32(-1e30)
    big = jnp.int32(V7X_LANES)

    gl = jnp.where(lane < n_grp, logits, neg)
    gmax = jnp.max(gl, axis=-1, keepdims=True)
    gidx = jnp.min(jnp.where(gl == gmax, lane, big), axis=-1, keepdims=True)
    grp_gate = 1.0 / jnp.sum(jnp.exp(gl - gmax), axis=-1, keepdims=True)

    lo = n_grp + gidx * per
    el = jnp.where((lane >= lo) & (lane < lo + per), logits, neg)
    v1 = jnp.max(el, axis=-1, keepdims=True)
    i1 = jnp.min(jnp.where(el == v1, lane, big), axis=-1, keepdims=True)
    el2 = jnp.where(lane == i1, neg, el)
    v2 = jnp.max(el2, axis=-1, keepdims=True)
    i2 = jnp.min(jnp.where(el2 == v2, lane, big), axis=-1, keepdims=True)
    t = jnp.exp(v2 - v1)
    den = 1.0 + t
    g1 = grp_gate / den
    g2 = grp_gate * (t / den)

    oh1 = jnp.where(lane == i1, 1.0, 0.0)
    oh2 = jnp.where(lane == i2, 1.0, 0.0)
    ri = lax.broadcasted_iota(I32, (tm, tm), 0)
    ci = lax.broadcasted_iota(I32, (tm, tm), 1)
    lower = jnp.where(ri > ci, 1.0, 0.0).astype(BF16)
    pre1 = jnp.dot(lower, oh1.astype(BF16), preferred_element_type=F32)
    pre2 = jnp.dot(lower, oh2.astype(BF16), preferred_element_type=F32)
    cnt1 = jnp.sum(oh1, axis=0, keepdims=True)
    cnt2 = jnp.sum(oh2, axis=0, keepdims=True)
    base = base_ref[...]
    rank1 = jnp.sum(oh1 * (pre1 + base), axis=-1, keepdims=True)
    rank2 = jnp.sum(oh2 * (pre2 + base + cnt1), axis=-1, keepdims=True)
    total = base + cnt1 + cnt2
    base_ref[...] = total
    cnt_ref[...] = total

    e1 = (i1 - n_grp).astype(F32)
    e2 = (i2 - n_grp).astype(F32)
    meta = jnp.zeros(logits.shape, F32)
    for idx, val in enumerate((e1, e2, rank1, rank2, g1, g2)):
        meta = jnp.where(lane == idx, val, meta)
    meta_ref[...] = meta
    meta_t_ref[...] = jnp.transpose(meta)[:META_ROWS, :]


META_ROWS = 8


def moe_router(h, g, wr, br, *, tm):
    n, d = h.shape
    tm = min(tm, n)
    return pl.pallas_call(
        functools.partial(_router_body, tm=tm, n_grp=MOE_GROUPS, per=MOE_PER_GROUP),
        grid=(n // tm,),
        in_specs=[
            pl.BlockSpec((tm, d), lambda i: (i, 0)),
            pl.BlockSpec((1, d), lambda i: (0, 0)),
            pl.BlockSpec((d, 2 * V7X_LANES), lambda i: (0, 0)),
            pl.BlockSpec((1, V7X_LANES), lambda i: (0, 0)),
        ],
        out_specs=[
            pl.BlockSpec((tm, d // 2), lambda i: (i, 0)),
            pl.BlockSpec((tm, V7X_LANES), lambda i: (i, 0)),
            pl.BlockSpec((META_ROWS, tm), lambda i: (0, i)),
            pl.BlockSpec((1, V7X_LANES), lambda i: (0, 0)),
        ],
        out_shape=[
            jax.ShapeDtypeStruct((n, d // 2), U32),
            jax.ShapeDtypeStruct((n, V7X_LANES), F32),
            jax.ShapeDtypeStruct((META_ROWS, n), F32),
            jax.ShapeDtypeStruct((1, V7X_LANES), F32),
        ],
        scratch_shapes=[pltpu.VMEM((1, V7X_LANES), F32)],
        compiler_params=_params(("arbitrary",)),
        name="moe_router",
    )(h, g.reshape(1, d), wr, br)


DISPATCH_TOKENS_PER_STEP = 256


def _dispatch_copy(x_ref, o_hbm, dest_ref, sem, base, r, k):
    n_tok = dest_ref.shape[0] // MOE_TOPK
    return pltpu.make_async_copy(
        x_ref.at[pl.ds(r, 1)], o_hbm.at[pl.ds(dest_ref[k * n_tok + base + r], 1)], sem)


def _fill_copies(fstart_ref, flen_ref, z_ref, o_hbm, sem, *, n_ranges, rb, wait):
    sub = 8

    def issue(src, dst):
        cp = pltpu.make_async_copy(src, dst, sem)
        if wait:
            cp.wait()
        else:
            cp.start()

    def per_range(i, carry):
        length = flen_ref[i]
        first = fstart_ref[i]
        off = first + length
        size = rb
        while size >= sub:
            take = (length & size) != 0
            off = off - jnp.where(take, size, 0)

            @pl.when(take)
            def _(off=off, size=size):
                issue(z_ref.at[pl.ds(0, size)], o_hbm.at[pl.ds(pl.multiple_of(off, sub), size)])

            size //= 2
        for j in range(sub - 1):
            @pl.when(j < (length & (sub - 1)))
            def _(j=j):
                issue(z_ref.at[pl.ds(0, 1)], o_hbm.at[pl.ds(first + j, 1)])
        return carry

    lax.fori_loop(0, n_ranges, per_range, 0)


def _dispatch_body(dest_ref, fstart_ref, flen_ref, x_ref, o_hbm, z_ref, sem, fill_sem, *, tb, n_ranges, rb):
    step = pl.program_id(0)
    base = step * tb

    @pl.when(step == 0)
    def _():
        z_ref[...] = jnp.zeros_like(z_ref)
        _fill_copies(fstart_ref, flen_ref, z_ref, o_hbm, fill_sem, n_ranges=n_ranges, rb=rb, wait=False)

    def start(r, carry):
        for k in range(MOE_TOPK):
            _dispatch_copy(x_ref, o_hbm, dest_ref, sem, base, r, k).start()
        return carry

    def wait(r, carry):
        for k in range(MOE_TOPK):
            _dispatch_copy(x_ref, o_hbm, dest_ref, sem, base, r, k).wait()
        return carry

    lax.fori_loop(0, tb, start, 0)
    lax.fori_loop(0, tb, wait, 0)

    @pl.when(step == pl.num_programs(0) - 1)
    def _():
        _fill_copies(fstart_ref, flen_ref, z_ref, o_hbm, fill_sem, n_ranges=n_ranges, rb=rb, wait=True)


def moe_dispatch(xp, dest_flat, fill_start, fill_len, *, n_sorted_rows, rb):
    n, kw = xp.shape
    tb = _tile(n, DISPATCH_TOKENS_PER_STEP)
    n_ranges = fill_start.shape[0]
    return pl.pallas_call(
        functools.partial(_dispatch_body, tb=tb, n_ranges=n_ranges, rb=rb),
        grid_spec=pltpu.PrefetchScalarGridSpec(
            num_scalar_prefetch=3,
            grid=(n // tb,),
            in_specs=[pl.BlockSpec((tb, kw), lambda i, dst, fs, fl: (i, 0))],
            out_specs=pl.BlockSpec(memory_space=pl.ANY),
            scratch_shapes=[pltpu.VMEM((rb, kw), xp.dtype), pltpu.SemaphoreType.DMA(()),
                            pltpu.SemaphoreType.DMA(())],
        ),
        out_shape=jax.ShapeDtypeStruct((n_sorted_rows, kw), xp.dtype),
        compiler_params=_params(("arbitrary",)),
        name="moe_dispatch",
    )(dest_flat, fill_start, fill_len, xp)


EXPERT_STEPS = 4


def _experts_body(exp_ref, nxt_ref, src_ref, rows_ref, nact_ref, x_ref, wg_ref, wu_ref, wdl_ref, wdh_ref, o_ref,
                  xl_ref, xh_ref, hg_ref, hu_ref, hd_ref, wa_ref, wb_ref, wc_ref, wd_ref, *, rb):
    del exp_ref, nxt_ref, src_ref
    v = pl.program_id(0)
    s = pl.program_id(1)
    active = v < nact_ref[0]
    half = rb // 2
    halves = [(pl.ds(0, half), None), (pl.ds(half, half), rows_ref[v] > half)]

    def for_halves(fn):
        for rows, cond in halves:
            if cond is None:
                fn(rows)
            else:
                pl.when(cond)(functools.partial(fn, rows))

    @pl.when(jnp.logical_and(active, s < 2))
    def _():
        wa_ref[...] = wg_ref[0].astype(BF16)
        wb_ref[...] = wu_ref[0].astype(BF16)

    @pl.when(jnp.logical_and(active, s == 0))
    def _():
        def step0(rows):
            lo, hi = _unpack_halves(x_ref[rows, :])
            xl = lo.astype(BF16)
            xl_ref[rows, :] = xl
            xh_ref[rows, :] = hi.astype(BF16)
            hg_ref[rows, :] = jnp.dot(xl, wa_ref[...], preferred_element_type=F32)
            hu_ref[rows, :] = jnp.dot(xl, wb_ref[...], preferred_element_type=F32)
        for_halves(step0)

    @pl.when(jnp.logical_and(active, s == 1))
    def _():
        def step1(rows):
            xh = xh_ref[rows, :]
            hg = hg_ref[rows, :] + jnp.dot(xh, wa_ref[...], preferred_element_type=F32)
            hu = hu_ref[rows, :] + jnp.dot(xh, wb_ref[...], preferred_element_type=F32)
            hd_ref[rows, :] = (hg * jax.nn.sigmoid(hg) * hu).astype(BF16)
        for_halves(step1)

    @pl.when(jnp.logical_and(active, s >= 2))
    def _():
        wc_ref[...] = wdl_ref[0].astype(BF16)
        wd_ref[...] = wdh_ref[0].astype(BF16)

        def step2(rows):
            hd = hd_ref[rows, :]
            y_lo = jnp.dot(hd, wc_ref[...], preferred_element_type=F32)
            y_hi = jnp.dot(hd, wd_ref[...], preferred_element_type=F32)
            o_ref[rows, :] = _pack_halves(jnp.concatenate([y_lo, y_hi], axis=-1))

        o_ref[pl.ds(half, half), :] = jnp.zeros((half, o_ref.shape[1]), o_ref.dtype)
        for_halves(step2)

    @pl.when(jnp.logical_not(active))
    def _():
        o_ref[...] = jnp.zeros_like(o_ref)


def moe_experts(xs, w_gate, w_up, w_down, blk_exp, blk_next, blk_src, blk_rows, nact, *, n_blocks, rb):
    n_exp, d, ff = w_gate.shape
    kw = xs.shape[1]
    assert d == 2 * kw
    oc = kw // 2

    def act(v, n):
        return v < n[0]

    def w_in_map(v, s, e, nx, sr, rw, n):
        first = jnp.logical_and(act(v, n), s < 2)
        return (jnp.where(first, e[v], nx[v]), jnp.where(first, s, 0), 0)

    def w_down_map(half):
        def index(v, s, e, nx, sr, rw, n):
            j = jnp.where(act(v, n), jnp.maximum(s - 2, 0), 1)
            return (e[v], 0, 2 * half + j)
        return index

    def out_map(v, s, e, nx, sr, rw, n):
        return (v, jnp.maximum(s - 2, 0))

    return pl.pallas_call(
        functools.partial(_experts_body, rb=rb),
        grid_spec=pltpu.PrefetchScalarGridSpec(
            num_scalar_prefetch=5,
            grid=(n_blocks, EXPERT_STEPS),
            in_specs=[
                pl.BlockSpec((rb, kw), lambda v, s, e, nx, sr, rw, n: (sr[v], 0)),
                pl.BlockSpec((1, d // 2, ff), w_in_map),
                pl.BlockSpec((1, d // 2, ff), w_in_map),
                pl.BlockSpec((1, ff, oc), w_down_map(0)),
                pl.BlockSpec((1, ff, oc), w_down_map(1)),
            ],
            out_specs=pl.BlockSpec((rb, oc), out_map),
            scratch_shapes=[pltpu.VMEM((rb, kw), BF16), pltpu.VMEM((rb, kw), BF16),
                            pltpu.VMEM((rb, ff), F32), pltpu.VMEM((rb, ff), F32), pltpu.VMEM((rb, ff), BF16),
                            pltpu.VMEM((d // 2, ff), BF16), pltpu.VMEM((d // 2, ff), BF16),
                            pltpu.VMEM((ff, oc), BF16), pltpu.VMEM((ff, oc), BF16)],
        ),
        out_shape=jax.ShapeDtypeStruct((n_blocks * rb, kw), U32),
        compiler_params=_params(("arbitrary", "arbitrary")),
        name="moe_experts",
    )(blk_exp, blk_next, blk_src, blk_rows, nact, xs, w_gate, w_up, w_down, w_down)


def _combine_copy(ys_hbm, buf_ref, dest_ref, sem, k, tok, r):
    n_tok = dest_ref.shape[0] // MOE_TOPK
    return pltpu.make_async_copy(
        ys_hbm.at[pl.ds(dest_ref[k * n_tok + tok], 1)], buf_ref.at[k, pl.ds(r, 1)], sem)


def _combine_body(dest_ref, h_ref, meta_ref, g_ref, ys_hbm, o_ref, buf_ref, sem, *, tb):
    base = pl.program_id(0) * tb

    def start(r, carry):
        for k in range(MOE_TOPK):
            _combine_copy(ys_hbm, buf_ref, dest_ref, sem, k, base + r, r).start()
        return carry

    def wait(r, carry):
        for k in range(MOE_TOPK):
            _combine_copy(ys_hbm, buf_ref, dest_ref, sem, k, base + r, r).wait()
        return carry

    lax.fori_loop(0, tb, start, 0)
    lax.fori_loop(0, tb, wait, 0)

    meta = meta_ref[...]
    acc_lo = None
    acc_hi = None
    for k in range(MOE_TOPK):
        gate = meta[:, 2 * MOE_TOPK + k:2 * MOE_TOPK + k + 1]
        lo, hi = _unpack_halves(buf_ref[k])
        lo, hi = lo * gate, hi * gate
        acc_lo = lo if acc_lo is None else acc_lo + lo
        acc_hi = hi if acc_hi is None else acc_hi + hi
    kh = acc_lo.shape[1]
    h_lo = h_ref[:, :kh] + acc_lo
    h_hi = h_ref[:, kh:] + acc_hi
    ms = (jnp.sum(h_lo * h_lo, axis=-1, keepdims=True) + jnp.sum(h_hi * h_hi, axis=-1, keepdims=True)) / (2 * kh)
    scale = lax.rsqrt(ms + NORM_EPS)
    o_ref[:, :kh] = (h_lo * scale) * g_ref[:, :kh]
    o_ref[:, kh:] = (h_hi * scale) * g_ref[:, kh:]


def moe_combine(h, meta, g, ys, dest_flat, *, tb):
    n, d = h.shape
    kw = ys.shape[1]
    tb = min(tb, n)
    return pl.pallas_call(
        functools.partial(_combine_body, tb=tb),
        grid_spec=pltpu.PrefetchScalarGridSpec(
            num_scalar_prefetch=1,
            grid=(n // tb,),
            in_specs=[
                pl.BlockSpec((tb, d), lambda i, dst: (i, 0)),
                pl.BlockSpec((tb, V7X_LANES), lambda i, dst: (i, 0)),
                pl.BlockSpec((1, d), lambda i, dst: (0, 0)),
                pl.BlockSpec(memory_space=pl.ANY),
            ],
            out_specs=pl.BlockSpec((tb, d), lambda i, dst: (i, 0)),
            scratch_shapes=[pltpu.VMEM((MOE_TOPK, tb, kw), U32), pltpu.SemaphoreType.DMA(())],
        ),
        out_shape=jax.ShapeDtypeStruct((n, d), F32),
        compiler_params=_params(("arbitrary",)),
        name="moe_combine",
    )(dest_flat, h, meta, g.reshape(1, d), ys)


def _moe_plan(meta_t, counts, *, rb, n_blocks):
    n_exp = counts.shape[0]
    expert = meta_t[0:MOE_TOPK].astype(I32)
    rank = meta_t[MOE_TOPK:2 * MOE_TOPK].astype(I32)
    padded = (counts + rb - 1) // rb * rb
    pad_end = jnp.cumsum(padded)
    pad_start = pad_end - padded
    dest = (pad_start[expert] + rank).reshape(-1).astype(I32)
    nact = (pad_end[-1] // rb).astype(I32)
    blk = jnp.arange(n_blocks, dtype=I32)
    blk_exp = jnp.minimum(jnp.searchsorted(pad_end, blk * rb, side="right"), n_exp - 1).astype(I32)
    active = blk < nact
    last = jnp.maximum(nact - 1, 0)
    blk_src = jnp.where(active, blk, last).astype(I32)
    blk_exp = jnp.where(active, blk_exp, blk_exp[last]).astype(I32)
    blk_next = blk_exp[jnp.minimum(blk + 1, last)]
    blk_rows = jnp.where(active, jnp.clip(pad_start[blk_exp] + counts[blk_exp] - blk * rb, 0, rb), 0).astype(I32)
    fill_start = jnp.concatenate([pad_start + counts, blk * rb]).astype(I32)
    fill_len = jnp.concatenate([padded - counts, jnp.where(active, 0, rb)]).astype(I32)
    return dest, blk_exp, blk_next, blk_src, blk_rows, nact.reshape(1), fill_start, fill_len


def kernel(x, mem, positions, norm_mix_g, w_in, ret_norm_g, gm_ln_g, gm_ln_b, gm_ws, gm_bs, w_out, norm_xa_g, norm_mem_g, xa_wq, xa_wkv, xa_wo, norm_moe_g, router_grp_w, router_grp_b, router_exp_w, router_exp_b, moe_w_gate, moe_w_up, moe_w_down, norm_final_g):
    batch, seq, d = x.shape
    mem_len = mem.shape[1]
    n = batch * seq
    ret_width = ret_norm_g.shape[0]
    gm_width = gm_ln_g.shape[0]
    assert ret_width == gm_width and w_in.shape[1] == 4 * ret_width + 2 * gm_width
    dk = ret_width // RET_HEADS
    n_exp = moe_w_gate.shape[0]
    assert n_exp == MOE_GROUPS * MOE_PER_GROUP and MOE_GROUPS + n_exp <= V7X_LANES

    inv_freq = ROPE_BASE ** (-jnp.arange(0, dk, 2, dtype=F32) / dk)
    ang = positions.astype(F32).reshape(n, 1) * inv_freq
    cos, sin = jnp.cos(ang), jnp.sin(ang)

    x2 = x.reshape(n, d)
    proj = norm_matmul(x2, norm_mix_g, w_in.astype(BF16), tm=512, tn=1024)
    ret = retention(proj, cos, sin, ret_norm_g, batch=batch, seq=seq, ret_width=ret_width)
    gm = gmlp(proj, gm_ln_g, gm_ln_b, gm_ws, gm_bs, n_rows=n, gm_width=gm_width,
              u_block=4 * ret_width // gm_width, v_block=4 * ret_width // gm_width + 1)
    h1 = matmul_residual([ret, gm], w_out.astype(BF16), x2, tm=512, tn=1024)

    q = norm_matmul(h1, norm_xa_g, xa_wq.astype(BF16), tm=512, tn=1024)
    kv = norm_matmul(mem.reshape(batch * mem_len, d), norm_mem_g, xa_wkv.astype(BF16), tm=512, tn=1024)
    o = cross_attention(q, kv, batch=batch, seq=seq, mem_len=mem_len, tq=512)
    h2 = matmul_residual([o], xa_wo.astype(BF16), h1, tm=512, tn=1024)

    pad = V7X_LANES - MOE_GROUPS - n_exp
    wr = jnp.concatenate([router_grp_w, router_exp_w, jnp.zeros((d, pad), F32)], axis=1)
    br = jnp.concatenate([router_grp_b, router_exp_b, jnp.zeros((pad,), F32)]).reshape(1, V7X_LANES)
    wr_hi = wr.astype(BF16)
    wr_lo = (wr - wr_hi.astype(F32)).astype(BF16)
    xp, meta, meta_t, cnt = moe_router(h2, norm_moe_g, jnp.concatenate([wr_hi, wr_lo], axis=1), br, tm=256)

    rb = MOE_ROW_BLOCK
    n_blocks = -(-(n * MOE_TOPK) // rb) + n_exp
    counts = cnt[0, MOE_GROUPS:MOE_GROUPS + n_exp].astype(I32)
    dest, blk_exp, blk_next, blk_src, blk_rows, nact, fill_start, fill_len = _moe_plan(
        meta_t, counts, rb=rb, n_blocks=n_blocks)
    xs = moe_dispatch(xp, dest, fill_start, fill_len, n_sorted_rows=n_blocks * rb, rb=rb)
    ys = moe_experts(xs, moe_w_gate, moe_w_up, moe_w_down, blk_exp, blk_next, blk_src, blk_rows, nact,
                     n_blocks=n_blocks, rb=rb)
    y = moe_combine(h2, meta, norm_final_g, ys, dest, tb=128)
    return y.reshape(batch, seq, d)
```

```python
import functools

import jax
import jax.numpy as jnp
from jax import lax
from jax.experimental import pallas as pl
from jax.experimental.pallas import tpu as pltpu

NORM_EPS = 1e-6
RET_HEADS = 8
ROPE_BASE = 10000.0
GM_GROUPS = 8
GM_CHUNK = 128
XA_HEADS = 4
MOE_GROUPS = 8
MOE_PER_GROUP = 8
MOE_TOPK = 2

V7X_LANES = 128
V7X_VMEM_BYTES = 64 * 1024 * 1024
VMEM_LIMIT_BYTES = 56 * 1024 * 1024

RET_BLOCK = 512
MOE_ROW_BLOCK = 512

F32 = jnp.float32
BF16 = jnp.bfloat16
U32 = jnp.uint32
I32 = jnp.int32


def _params(sem):
    return pltpu.CompilerParams(dimension_semantics=sem, vmem_limit_bytes=VMEM_LIMIT_BYTES)


def _tile(dim, target):
    t = min(dim, target)
    while dim % t:
        t -= V7X_LANES
    assert t > 0, (dim, target)
    return t


def _pack_halves(x_f32):
    k = x_f32.shape[-1] // 2
    bits = lax.bitcast_convert_type(x_f32.astype(BF16).astype(F32), U32)
    return (bits[:, k:] & jnp.uint32(0xFFFF0000)) | (bits[:, :k] >> 16)


def _unpack_halves(w_u32):
    lo = lax.bitcast_convert_type(w_u32 << 16, F32)
    hi = lax.bitcast_convert_type(w_u32 & jnp.uint32(0xFFFF0000), F32)
    return lo, hi


def _norm_matmul_body(x_ref, g_ref, w_ref, o_ref, xn_ref):
    @pl.when(pl.program_id(1) == 0)
    def _():
        x = x_ref[...]
        ms = jnp.mean(x * x, axis=-1, keepdims=True)
        xn_ref[...] = ((x * lax.rsqrt(ms + NORM_EPS)) * g_ref[...]).astype(BF16)

    o_ref[...] = jnp.dot(xn_ref[...], w_ref[...], preferred_element_type=F32).astype(o_ref.dtype)


def norm_matmul(x, g, w, *, tm, tn):
    m, k = x.shape
    n = w.shape[1]
    tm, tn = _tile(m, tm), _tile(n, tn)
    return pl.pallas_call(
        _norm_matmul_body,
        grid=(m // tm, n // tn),
        in_specs=[
            pl.BlockSpec((tm, k), lambda i, j: (i, 0)),
            pl.BlockSpec((1, k), lambda i, j: (0, 0)),
            pl.BlockSpec((k, tn), lambda i, j: (0, j)),
        ],
        out_specs=pl.BlockSpec((tm, tn), lambda i, j: (i, j)),
        out_shape=jax.ShapeDtypeStruct((m, n), BF16),
        scratch_shapes=[pltpu.VMEM((tm, k), BF16)],
        compiler_params=_params(("parallel", "arbitrary")),
        name="norm_matmul",
    )(x, g.reshape(1, k), w)


def _retention_body(lg_ref, q_ref, k_ref, v_ref, g_ref, cos_ref, sin_ref, gn_ref, o_ref, state_ref, *, blk, dk):
    h = pl.program_id(1)
    c = pl.program_id(2)
    lg = lg_ref[h]

    @pl.when(c == 0)
    def _():
        state_ref[...] = jnp.zeros_like(state_ref)

    half = dk // 2
    cos = cos_ref[...]
    sin = sin_ref[...]

    def rot(t):
        t1, t2 = t[:, :half], t[:, half:]
        return jnp.concatenate([t1 * cos - t2 * sin, t1 * sin + t2 * cos], axis=-1)

    qr = rot(q_ref[...].astype(F32))
    kr = rot(k_ref[...].astype(F32)) * (dk ** -0.5)
    v = v_ref[...]

    pos = lax.broadcasted_iota(I32, (blk, 1), 0).astype(F32)
    q_dec = jnp.exp((pos + 1.0) * lg)
    k_dec = jnp.exp((blk - 1.0 - pos) * lg)
    blk_dec = jnp.exp(jnp.full((1, dk), blk * lg, F32))

    s = lax.dot_general(qr.astype(BF16), kr.astype(BF16), (((1,), (1,)), ((), ())), preferred_element_type=F32)
    ri = lax.broadcasted_iota(I32, (blk, blk), 0)
    ci = lax.broadcasted_iota(I32, (blk, blk), 1)
    diff = (ri - ci).astype(F32)
    dec = jnp.where(diff >= 0.0, jnp.exp(jnp.maximum(diff, 0.0) * lg), 0.0)
    inner = jnp.dot((s * dec).astype(BF16), v, preferred_element_type=F32)

    state = state_ref[...]
    cross = jnp.dot((qr * q_dec).astype(BF16), state.astype(BF16), preferred_element_type=F32)
    kd_t = jnp.transpose(kr * k_dec).astype(BF16)
    state_ref[...] = state * blk_dec + jnp.dot(kd_t, v, preferred_element_type=F32)

    out = inner + cross
    mu = jnp.mean(out, axis=-1, keepdims=True)
    cen = out - mu
    var = jnp.mean(cen * cen, axis=-1, keepdims=True)
    y = cen * lax.rsqrt(var + NORM_EPS) * gn_ref[...]
    gate = g_ref[...].astype(F32)
    o_ref[...] = (y * (gate * jax.nn.sigmoid(gate))).astype(o_ref.dtype)


def retention(proj, cos, sin, ret_norm_g, *, batch, seq, ret_width):
    heads = RET_HEADS
    dk = ret_width // heads
    blk = min(RET_BLOCK, seq)
    nblk = seq // blk
    hb = ret_width // dk
    log_gamma = jnp.log(1.0 - jnp.exp2(-5.0 - jnp.arange(heads, dtype=F32)))

    def col(seg):
        return pl.BlockSpec((blk, dk), lambda b, h, c, lg, seg=seg: (b * nblk + c, seg * hb + h))

    rowspec = pl.BlockSpec((blk, dk // 2), lambda b, h, c, lg: (b * nblk + c, 0))
    return pl.pallas_call(
        functools.partial(_retention_body, blk=blk, dk=dk),
        grid_spec=pltpu.PrefetchScalarGridSpec(
            num_scalar_prefetch=1,
            grid=(batch, heads, nblk),
            in_specs=[col(0), col(1), col(2), col(3), rowspec, rowspec,
                      pl.BlockSpec((1, dk), lambda b, h, c, lg: (0, h))],
            out_specs=pl.BlockSpec((blk, dk), lambda b, h, c, lg: (b * nblk + c, h)),
            scratch_shapes=[pltpu.VMEM((dk, dk), F32)],
        ),
        out_shape=jax.ShapeDtypeStruct((batch * seq, ret_width), BF16),
        compiler_params=_params(("parallel", "parallel", "arbitrary")),
        name="retention",
    )(log_gamma, proj, proj, proj, proj, cos, sin, ret_norm_g.reshape(1, ret_width))


def _gmlp_body(u_ref, v_ref, lng_ref, lnb_ref, ws_ref, bst_ref, o_ref, vn_ref, *, rows, groups, cg, chunk):
    v = jax.nn.gelu(v_ref[...].astype(F32))
    mu = jnp.mean(v, axis=-1, keepdims=True)
    cen = v - mu
    var = jnp.mean(cen * cen, axis=-1, keepdims=True)
    vn_ref[...] = (cen * lax.rsqrt(var + NORM_EPS) * lng_ref[...] + lnb_ref[...]).astype(BF16)

    ri = lax.broadcasted_iota(I32, (chunk, chunk), 0)
    ci = lax.broadcasted_iota(I32, (chunk, chunk), 1)
    causal = ri >= ci
    for g in range(groups):
        w = jnp.where(causal, ws_ref[g], 0.0).astype(BF16)
        bias = bst_ref[:, g:g + 1]
        cols = slice(g * cg, (g + 1) * cg)
        for t in range(rows // chunk):
            rws = slice(t * chunk, (t + 1) * chunk)
            sp = jnp.dot(w, vn_ref[rws, cols], preferred_element_type=F32) + bias
            u = jax.nn.gelu(u_ref[rws, cols].astype(F32))
            o_ref[rws, cols] = (u * sp).astype(o_ref.dtype)


def gmlp(proj, ln_g, ln_b, ws, bs, *, n_rows, gm_width, u_block, v_block):
    groups, chunk = GM_GROUPS, GM_CHUNK
    cg = gm_width // groups
    rows = 2 * chunk
    return pl.pallas_call(
        functools.partial(_gmlp_body, rows=rows, groups=groups, cg=cg, chunk=chunk),
        grid=(n_rows // rows,),
        in_specs=[
            pl.BlockSpec((rows, gm_width), lambda i: (i, u_block)),
            pl.BlockSpec((rows, gm_width), lambda i: (i, v_block)),
            pl.BlockSpec((1, gm_width), lambda i: (0, 0)),
            pl.BlockSpec((1, gm_width), lambda i: (0, 0)),
            pl.BlockSpec((groups, chunk, chunk), lambda i: (0, 0, 0)),
            pl.BlockSpec((chunk, groups), lambda i: (0, 0)),
        ],
        out_specs=pl.BlockSpec((rows, gm_width), lambda i: (i, 0)),
        out_shape=jax.ShapeDtypeStruct((n_rows, gm_width), BF16),
        scratch_shapes=[pltpu.VMEM((rows, gm_width), BF16)],
        compiler_params=_params(("parallel",)),
        name="gmlp",
    )(proj, proj, ln_g.reshape(1, gm_width), ln_b.reshape(1, gm_width), ws, bs.T)


def _matmul_residual_body(*refs, n_parts):
    a_refs, w_refs = refs[:n_parts], refs[n_parts:2 * n_parts]
    r_ref, o_ref = refs[2 * n_parts], refs[2 * n_parts + 1]
    acc = r_ref[...]
    for a_ref, w_ref in zip(a_refs, w_refs):
        acc = acc + jnp.dot(a_ref[...], w_ref[...], preferred_element_type=F32)
    o_ref[...] = acc


def matmul_residual(parts, w, res, *, tm, tn):
    m, n = res.shape
    tm, tn = _tile(m, tm), _tile(n, tn)
    kp = parts[0].shape[1]
    n_parts = len(parts)
    a_specs = [pl.BlockSpec((tm, kp), lambda i, j: (i, 0)) for _ in parts]
    w_specs = [pl.BlockSpec((kp, tn), lambda i, j, p=p: (p, j)) for p in range(n_parts)]
    return pl.pallas_call(
        functools.partial(_matmul_residual_body, n_parts=n_parts),
        grid=(m // tm, n // tn),
        in_specs=a_specs + w_specs + [pl.BlockSpec((tm, tn), lambda i, j: (i, j))],
        out_specs=pl.BlockSpec((tm, tn), lambda i, j: (i, j)),
        out_shape=jax.ShapeDtypeStruct((m, n), F32),
        compiler_params=_params(("parallel", "parallel")),
        name="matmul_residual",
    )(*parts, *([w] * n_parts), res)


def _cross_attention_body(q_ref, kv_ref, o_ref, *, heads, dh):
    width = heads * dh
    for h in range(heads):
        q = q_ref[:, h * dh:(h + 1) * dh]
        k = kv_ref[:, h * dh:(h + 1) * dh]
        v = kv_ref[:, width + h * dh:width + (h + 1) * dh]
        s = lax.dot_general(q, k, (((1,), (1,)), ((), ())), preferred_element_type=F32) * (dh ** -0.5)
        e = jnp.exp(s - jnp.max(s, axis=-1, keepdims=True))
        p = e / jnp.sum(e, axis=-1, keepdims=True)
        o_ref[:, h * dh:(h + 1) * dh] = jnp.dot(p.astype(BF16), v, preferred_element_type=F32).astype(o_ref.dtype)


def cross_attention(q, kv, *, batch, seq, mem_len, tq):
    heads = XA_HEADS
    width = q.shape[1]
    dh = width // heads
    tq = min(tq, seq)
    nq = seq // tq
    return pl.pallas_call(
        functools.partial(_cross_attention_body, heads=heads, dh=dh),
        grid=(batch, nq),
        in_specs=[
            pl.BlockSpec((tq, width), lambda b, i: (b * nq + i, 0)),
            pl.BlockSpec((mem_len, 2 * width), lambda b, i: (b, 0)),
        ],
        out_specs=pl.BlockSpec((tq, width), lambda b, i: (b * nq + i, 0)),
        out_shape=jax.ShapeDtypeStruct((batch * seq, width), BF16),
        compiler_params=_params(("parallel", "parallel")),
        name="cross_attention",
    )(q, kv)


def _router_body(h_ref, g_ref, wr_ref, br_ref, xp_ref, meta_ref, meta_t_ref, cnt_ref, base_ref, *, tm, n_grp, per):
    @pl.when(pl.program_id(0) == 0)
    def _():
        base_ref[...] = jnp.zeros_like(base_ref)

    x = h_ref[...]
    ms = jnp.mean(x * x, axis=-1, keepdims=True)
    xn = (x * lax.rsqrt(ms + NORM_EPS)) * g_ref[...]
    xp_ref[...] = _pack_halves(xn)

    x_hi = xn.astype(BF16)
    x_lo = (xn - x_hi.astype(F32)).astype(BF16)
    both = jnp.dot(x_hi, wr_ref[...], preferred_element_type=F32)
    corr = jnp.dot(x_lo, wr_ref[:, :V7X_LANES], preferred_element_type=F32)
    logits = both[:, :V7X_LANES] + (both[:, V7X_LANES:] + corr) + br_ref[...]
    lane = lax.broadcasted_iota(I32, logits.shape, 1)
    neg = jnp.float32(-1e30)
    big = jnp.int32(V7X_LANES)

    gl = jnp.where(lane < n_grp, logits, neg)
    gmax = jnp.max(gl, axis=-1, keepdims=True)
    gidx = jnp.min(jnp.where(gl == gmax, lane, big), axis=-1, keepdims=True)
    grp_gate = 1.0 / jnp.sum(jnp.exp(gl - gmax), axis=-1, keepdims=True)

    lo = n_grp + gidx * per
    el = jnp.where((lane >= lo) & (lane < lo + per), logits, neg)
    v1 = jnp.max(el, axis=-1, keepdims=True)
    i1 = jnp.min(jnp.where(el == v1, lane, big), axis=-1, keepdims=True)
    el2 = jnp.where(lane == i1, neg, el)
    v2 = jnp.max(el2, axis=-1, keepdims=True)
    i2 = jnp.min(jnp.where(el2 == v2, lane, big), axis=-1, keepdims=True)
    t = jnp.exp(v2 - v1)
    den = 1.0 + t
    g1 = grp_gate / den
    g2 = grp_gate * (t / den)

    oh1 = jnp.where(lane == i1, 1.0, 0.0)
    oh2 = jnp.where(lane == i2, 1.0, 0.0)
    ri = lax.broadcasted_iota(I32, (tm, tm), 0)
    ci = lax.broadcasted_iota(I32, (tm, tm), 1)
    lower = jnp.where(ri > ci, 1.0, 0.0).astype(BF16)
    pre1 = jnp.dot(lower, oh1.astype(BF16), preferred_element_type=F32)
    pre2 = jnp.dot(lower, oh2.astype(BF16), preferred_element_type=F32)
    cnt1 = jnp.sum(oh1, axis=0, keepdims=True)
    cnt2 = jnp.sum(oh2, axis=0, keepdims=True)
    base = base_ref[...]
    rank1 = jnp.sum(oh1 * (pre1 + base), axis=-1, keepdims=True)
    rank2 = jnp.sum(oh2 * (pre2 + base + cnt1), axis=-1, keepdims=True)
    total = base + cnt1 + cnt2
    base_ref[...] = total
    cnt_ref[...] = total

    e1 = (i1 - n_grp).astype(F32)
    e2 = (i2 - n_grp).astype(F32)
    meta = jnp.zeros(logits.shape, F32)
    for idx, val in enumerate((e1, e2, rank1, rank2, g1, g2)):
        meta = jnp.where(lane == idx, val, meta)
    meta_ref[...] = meta
    meta_t_ref[...] = jnp.transpose(meta)[:META_ROWS, :]


META_ROWS = 8


def moe_router(h, g, wr, br, *, tm):
    n, d = h.shape
    tm = min(tm, n)
    return pl.pallas_call(
        functools.partial(_router_body, tm=tm, n_grp=MOE_GROUPS, per=MOE_PER_GROUP),
        grid=(n // tm,),
        in_specs=[
            pl.BlockSpec((tm, d), lambda i: (i, 0)),
            pl.BlockSpec((1, d), lambda i: (0, 0)),
            pl.BlockSpec((d, 2 * V7X_LANES), lambda i: (0, 0)),
            pl.BlockSpec((1, V7X_LANES), lambda i: (0, 0)),
        ],
        out_specs=[
            pl.BlockSpec((tm, d // 2), lambda i: (i, 0)),
            pl.BlockSpec((tm, V7X_LANES), lambda i: (i, 0)),
            pl.BlockSpec((META_ROWS, tm), lambda i: (0, i)),
            pl.BlockSpec((1, V7X_LANES), lambda i: (0, 0)),
        ],
        out_shape=[
            jax.ShapeDtypeStruct((n, d // 2), U32),
            jax.ShapeDtypeStruct((n, V7X_LANES), F32),
            jax.ShapeDtypeStruct((META_ROWS, n), F32),
            jax.ShapeDtypeStruct((1, V7X_LANES), F32),
        ],
        scratch_shapes=[pltpu.VMEM((1, V7X_LANES), F32)],
        compiler_params=_params(("arbitrary",)),
        name="moe_router",
    )(h, g.reshape(1, d), wr, br)


MOE_CODE_SHIFT = 6
assert MOE_GROUPS * MOE_PER_GROUP == 1 << MOE_CODE_SHIFT


def _order_body(code_ref, start_ref, order_ref):
    def body(a, carry):
        code = code_ref[a]
        order_ref[start_ref[code & ((1 << MOE_CODE_SHIFT) - 1)] + (code >> MOE_CODE_SHIFT)] = a
        return carry

    lax.fori_loop(0, code_ref.shape[0], body, 0, unroll=8)


def moe_order(code, start):
    return pl.pallas_call(
        _order_body,
        grid_spec=pltpu.PrefetchScalarGridSpec(
            num_scalar_prefetch=2,
            grid=(1,),
            in_specs=[],
            out_specs=pl.BlockSpec(memory_space=pltpu.SMEM),
        ),
        out_shape=jax.ShapeDtypeStruct(code.shape, I32),
        compiler_params=_params(("arbitrary",)),
        name="moe_order",
    )(code, start)


EXPERT_STEPS = 4
assert MOE_TOPK == 2


def _experts_body(exp_ref, nxt_ref, rows_ref, off_ref, order_ref, nact_ref,
                  xp_hbm, wg_ref, wu_ref, wdl_ref, wdh_ref, ya_hbm,
                  xbuf_ref, ybuf_ref, xl_ref, xh_ref, hg_ref, hu_ref, hd_ref, wa_ref, wb_ref, wc_ref, wd_ref,
                  gsem, ssem, *, rb, n_tok, n_blocks):
    del exp_ref, nxt_ref
    v = pl.program_id(0)
    s = pl.program_id(1)
    nact = nact_ref[0]
    active = v < nact
    slot = lax.rem(v, 2)
    half = rb // 2
    quarter = rb // EXPERT_STEPS
    oc = wc_ref.shape[1]

    def decode(blk, r):
        a = order_ref[off_ref[blk] + r]
        k = jnp.where(a >= n_tok, 1, 0)
        return k, a - k * n_tok

    def gather_copy(blk, buf, r):
        _, tok = decode(blk, r)
        return pltpu.make_async_copy(xp_hbm.at[pl.ds(tok, 1)], xbuf_ref.at[buf, pl.ds(r, 1)], gsem.at[buf])

    def scatter_copy(blk, r):
        k, tok = decode(blk, r)
        return pltpu.make_async_copy(ybuf_ref.at[pl.ds(r, 1)], ya_hbm.at[k, pl.ds(tok, 1)], ssem)

    def for_rows(lo, hi, fn):
        def body(r, carry):
            fn(r)
            return carry
        lax.fori_loop(lo, hi, body, 0)

    @pl.when(jnp.logical_and(v == 0, s == 0))
    def _():
        xbuf_ref[...] = jnp.zeros_like(xbuf_ref)
        for_rows(0, rows_ref[0], lambda r: gather_copy(0, 0, r).start())

    @pl.when(jnp.logical_and(active, s == 0))
    def _():
        for_rows(0, rows_ref[v], lambda r: gather_copy(v, slot, r).wait())

    @pl.when(jnp.logical_and(active, v + 1 < nact))
    def _():
        nxt = jnp.minimum(v + 1, n_blocks - 1)
        for_rows(s * quarter, jnp.minimum((s + 1) * quarter, rows_ref[nxt]),
                 lambda r: gather_copy(nxt, 1 - slot, r).start())

    @pl.when(jnp.logical_and(active, jnp.logical_and(s == 2, v >= 1)))
    def _():
        prev = jnp.maximum(v - 1, 0)
        for_rows(0, rows_ref[prev], lambda r: scatter_copy(prev, r).wait())

    halves = [(pl.ds(0, half), None), (pl.ds(half, half), rows_ref[v] > half)]

    def for_halves(fn):
        for rows, cond in halves:
            if cond is None:
                fn(rows)
            else:
                pl.when(cond)(functools.partial(fn, rows))

    @pl.when(jnp.logical_and(active, s < 2))
    def _():
        wa_ref[...] = wg_ref[0].astype(BF16)
        wb_ref[...] = wu_ref[0].astype(BF16)

    @pl.when(jnp.logical_and(active, s == 0))
    def _():
        def step0(rows):
            lo, hi = _unpack_halves(xbuf_ref[slot, rows, :])
            xl = lo.astype(BF16)
            xl_ref[rows, :] = xl
            xh_ref[rows, :] = hi.astype(BF16)
            hg_ref[rows, :] = jnp.dot(xl, wa_ref[...], preferred_element_type=F32)
            hu_ref[rows, :] = jnp.dot(xl, wb_ref[...], preferred_element_type=F32)
        for_halves(step0)

    @pl.when(jnp.logical_and(active, s == 1))
    def _():
        def step1(rows):
            xh = xh_ref[rows, :]
            hg = hg_ref[rows, :] + jnp.dot(xh, wa_ref[...], preferred_element_type=F32)
            hu = hu_ref[rows, :] + jnp.dot(xh, wb_ref[...], preferred_element_type=F32)
            hd_ref[rows, :] = (hg * jax.nn.sigmoid(hg) * hu).astype(BF16)
        for_halves(step1)

    def down(cols):
        wc_ref[...] = wdl_ref[0].astype(BF16)
        wd_ref[...] = wdh_ref[0].astype(BF16)

        def step2(rows):
            hd = hd_ref[rows, :]
            y_lo = jnp.dot(hd, wc_ref[...], preferred_element_type=F32)
            y_hi = jnp.dot(hd, wd_ref[...], preferred_element_type=F32)
            ybuf_ref[rows, cols] = _pack_halves(jnp.concatenate([y_lo, y_hi], axis=-1))
        for_halves(step2)

    @pl.when(jnp.logical_and(active, s == 2))
    def _():
        down(pl.ds(0, oc))

    @pl.when(jnp.logical_and(active, s == 3))
    def _():
        down(pl.ds(oc, oc))
        for_rows(0, rows_ref[v], lambda r: scatter_copy(v, r).start())

    @pl.when(jnp.logical_and(v == n_blocks - 1, s == EXPERT_STEPS - 1))
    def _():
        last = jnp.maximum(nact - 1, 0)
        for_rows(0, rows_ref[last], lambda r: scatter_copy(last, r).wait())


def moe_experts(xp, w_gate, w_up, w_down, blk_exp, blk_next, blk_rows, blk_off, order, nact, *, n_blocks, rb):
    n_exp, d, ff = w_gate.shape
    n_tok, kw = xp.shape
    assert d == 2 * kw
    oc = kw // 2

    def w_in_map(v, s, e, nx, rw, of, od, n):
        first = jnp.logical_and(v < n[0], s < 2)
        return (jnp.where(first, e[v], nx[v]), jnp.where(first, s, 0), 0)

    def w_down_map(half):
        def index(v, s, e, nx, rw, of, od, n):
            j = jnp.where(v < n[0], jnp.maximum(s - 2, 0), 1)
            return (e[v], 0, 2 * half + j)
        return index

    return pl.pallas_call(
        functools.partial(_experts_body, rb=rb, n_tok=n_tok, n_blocks=n_blocks),
        grid_spec=pltpu.PrefetchScalarGridSpec(
            num_scalar_prefetch=6,
            grid=(n_blocks, EXPERT_STEPS),
            in_specs=[
                pl.BlockSpec(memory_space=pl.ANY),
                pl.BlockSpec((1, d // 2, ff), w_in_map),
                pl.BlockSpec((1, d // 2, ff), w_in_map),
                pl.BlockSpec((1, ff, oc), w_down_map(0)),
                pl.BlockSpec((1, ff, oc), w_down_map(1)),
            ],
            out_specs=pl.BlockSpec(memory_space=pl.ANY),
            scratch_shapes=[pltpu.VMEM((2, rb, kw), U32), pltpu.VMEM((rb, kw), U32),
                            pltpu.VMEM((rb, kw), BF16), pltpu.VMEM((rb, kw), BF16),
                            pltpu.VMEM((rb, ff), F32), pltpu.VMEM((rb, ff), F32), pltpu.VMEM((rb, ff), BF16),
                            pltpu.VMEM((d // 2, ff), BF16), pltpu.VMEM((d // 2, ff), BF16),
                            pltpu.VMEM((ff, oc), BF16), pltpu.VMEM((ff, oc), BF16),
                            pltpu.SemaphoreType.DMA((2,)), pltpu.SemaphoreType.DMA(())],
        ),
        out_shape=jax.ShapeDtypeStruct((MOE_TOPK, n_tok, kw), U32),
        compiler_params=_params(("arbitrary", "arbitrary")),
        name="moe_experts",
    )(blk_exp, blk_next, blk_rows, blk_off, order, nact, xp, w_gate, w_up, w_down, w_down)


def _combine_body(h_ref, meta_ref, g_ref, y0_ref, y1_ref, o_ref):
    meta = meta_ref[...]
    acc_lo = None
    acc_hi = None
    for k, y_ref in enumerate((y0_ref, y1_ref)):
        gate = meta[:, 2 * MOE_TOPK + k:2 * MOE_TOPK + k + 1]
        lo, hi = _unpack_halves(y_ref[0])
        lo, hi = lo * gate, hi * gate
        acc_lo = lo if acc_lo is None else acc_lo + lo
        acc_hi = hi if acc_hi is None else acc_hi + hi
    kh = acc_lo.shape[1]
    h_lo = h_ref[:, :kh] + acc_lo
    h_hi = h_ref[:, kh:] + acc_hi
    ms = (jnp.sum(h_lo * h_lo, axis=-1, keepdims=True) + jnp.sum(h_hi * h_hi, axis=-1, keepdims=True)) / (2 * kh)
    scale = lax.rsqrt(ms + NORM_EPS)
    o_ref[:, :kh] = (h_lo * scale) * g_ref[:, :kh]
    o_ref[:, kh:] = (h_hi * scale) * g_ref[:, kh:]


def moe_combine(h, meta, g, ya, *, tb):
    n, d = h.shape
    kw = ya.shape[2]
    tb = _tile(n, tb)

    def slot(k):
        return pl.BlockSpec((1, tb, kw), lambda i, k=k: (k, i, 0))

    return pl.pallas_call(
        _combine_body,
        grid=(n // tb,),
        in_specs=[
            pl.BlockSpec((tb, d), lambda i: (i, 0)),
            pl.BlockSpec((tb, V7X_LANES), lambda i: (i, 0)),
            pl.BlockSpec((1, d), lambda i: (0, 0)),
            slot(0), slot(1),
        ],
        out_specs=pl.BlockSpec((tb, d), lambda i: (i, 0)),
        out_shape=jax.ShapeDtypeStruct((n, d), F32),
        compiler_params=_params(("parallel",)),
        name="moe_combine",
    )(h, meta, g.reshape(1, d), ya, ya)


def _moe_plan(counts, *, rb, n_blocks):
    n_exp = counts.shape[0]
    per_exp = (counts + rb - 1) // rb
    blk_end = jnp.cumsum(per_exp)
    start = jnp.cumsum(counts) - counts
    nact = blk_end[-1].astype(I32)
    blk = jnp.arange(n_blocks, dtype=I32)
    blk_exp = jnp.minimum(jnp.searchsorted(blk_end, blk, side="right"), n_exp - 1).astype(I32)
    active = blk < nact
    last = jnp.maximum(nact - 1, 0)
    blk_exp = jnp.where(active, blk_exp, blk_exp[last]).astype(I32)
    blk_next = blk_exp[jnp.minimum(blk + 1, last)]
    j = blk - (blk_end[blk_exp] - per_exp[blk_exp])
    blk_off = jnp.where(active, start[blk_exp] + j * rb, 0).astype(I32)
    blk_rows = jnp.where(active, jnp.clip(counts[blk_exp] - j * rb, 0, rb), 0).astype(I32)
    return start.astype(I32), blk_exp, blk_next, blk_rows, blk_off, nact.reshape(1)


def kernel(x, mem, positions, norm_mix_g, w_in, ret_norm_g, gm_ln_g, gm_ln_b, gm_ws, gm_bs, w_out, norm_xa_g, norm_mem_g, xa_wq, xa_wkv, xa_wo, norm_moe_g, router_grp_w, router_grp_b, router_exp_w, router_exp_b, moe_w_gate, moe_w_up, moe_w_down, norm_final_g):
    batch, seq, d = x.shape
    mem_len = mem.shape[1]
    n = batch * seq
    ret_width = ret_norm_g.shape[0]
    gm_width = gm_ln_g.shape[0]
    assert ret_width == gm_width and w_in.shape[1] == 4 * ret_width + 2 * gm_width
    dk = ret_width // RET_HEADS
    n_exp = moe_w_gate.shape[0]
    assert n_exp == MOE_GROUPS * MOE_PER_GROUP and MOE_GROUPS + n_exp <= V7X_LANES

    inv_freq = ROPE_BASE ** (-jnp.arange(0, dk, 2, dtype=F32) / dk)
    ang = positions.astype(F32).reshape(n, 1) * inv_freq
    cos, sin = jnp.cos(ang), jnp.sin(ang)

    x2 = x.reshape(n, d)
    proj = norm_matmul(x2, norm_mix_g, w_in.astype(BF16), tm=512, tn=1024)
    ret = retention(proj, cos, sin, ret_norm_g, batch=batch, seq=seq, ret_width=ret_width)
    gm = gmlp(proj, gm_ln_g, gm_ln_b, gm_ws, gm_bs, n_rows=n, gm_width=gm_width,
              u_block=4 * ret_width // gm_width, v_block=4 * ret_width // gm_width + 1)
    h1 = matmul_residual([ret, gm], w_out.astype(BF16), x2, tm=512, tn=1024)

    q = norm_matmul(h1, norm_xa_g, xa_wq.astype(BF16), tm=512, tn=1024)
    kv = norm_matmul(mem.reshape(batch * mem_len, d), norm_mem_g, xa_wkv.astype(BF16), tm=512, tn=1024)
    o = cross_attention(q, kv, batch=batch, seq=seq, mem_len=mem_len, tq=512)
    h2 = matmul_residual([o], xa_wo.astype(BF16), h1, tm=512, tn=1024)

    pad = V7X_LANES - MOE_GROUPS - n_exp
    wr = jnp.concatenate([router_grp_w, router_exp_w, jnp.zeros((d, pad), F32)], axis=1)
    br = jnp.concatenate([router_grp_b, router_exp_b, jnp.zeros((pad,), F32)]).reshape(1, V7X_LANES)
    wr_hi = wr.astype(BF16)
    wr_lo = (wr - wr_hi.astype(F32)).astype(BF16)
    xp, meta, meta_t, cnt = moe_router(h2, norm_moe_g, jnp.concatenate([wr_hi, wr_lo], axis=1), br, tm=256)

    rb = MOE_ROW_BLOCK
    n_blocks = -(-(n * MOE_TOPK) // rb) + n_exp
    counts = cnt[0, MOE_GROUPS:MOE_GROUPS + n_exp].astype(I32)
    start, blk_exp, blk_next, blk_rows, blk_off, nact = _moe_plan(counts, rb=rb, n_blocks=n_blocks)
    code = (meta_t[MOE_TOPK:2 * MOE_TOPK].astype(I32) * (1 << MOE_CODE_SHIFT)
            + meta_t[0:MOE_TOPK].astype(I32)).reshape(-1)
    order = moe_order(code, start)
    ya = moe_experts(xp, moe_w_gate, moe_w_up, moe_w_down, blk_exp, blk_next, blk_rows, blk_off, order, nact,
                     n_blocks=n_blocks, rb=rb)
    y = moe_combine(h2, meta, norm_final_g, ya, tb=256)
    return y.reshape(batch, seq, d)
```

```python
import functools

import jax
import jax.numpy as jnp
from jax import lax
from jax.experimental import pallas as pl
from jax.experimental.pallas import tpu as pltpu

NORM_EPS = 1e-6
RET_HEADS = 8
ROPE_BASE = 10000.0
GM_GROUPS = 8
GM_CHUNK = 128
XA_HEADS = 4
MOE_GROUPS = 8
MOE_PER_GROUP = 8
MOE_TOPK = 2

V7X_LANES = 128
V7X_VMEM_BYTES = 64 * 1024 * 1024
VMEM_LIMIT_BYTES = 56 * 1024 * 1024

RET_BLOCK = 512
MOE_ROW_BLOCK = 512

F32 = jnp.float32
BF16 = jnp.bfloat16
U32 = jnp.uint32
I32 = jnp.int32


def _params(sem):
    return pltpu.CompilerParams(dimension_semantics=sem, vmem_limit_bytes=VMEM_LIMIT_BYTES)


def _tile(dim, target):
    t = min(dim, target)
    while dim % t:
        t -= V7X_LANES
    assert t > 0, (dim, target)
    return t


def _pack_halves(x_f32):
    k = x_f32.shape[-1] // 2
    bits = lax.bitcast_convert_type(x_f32.astype(BF16).astype(F32), U32)
    return (bits[:, k:] & jnp.uint32(0xFFFF0000)) | (bits[:, :k] >> 16)


def _unpack_halves(w_u32):
    lo = lax.bitcast_convert_type(w_u32 << 16, F32)
    hi = lax.bitcast_convert_type(w_u32 & jnp.uint32(0xFFFF0000), F32)
    return lo, hi


def _norm_matmul_body(x_ref, g_ref, w_ref, o_ref, xn_ref):
    @pl.when(pl.program_id(1) == 0)
    def _():
        x = x_ref[...]
        ms = jnp.mean(x * x, axis=-1, keepdims=True)
        xn_ref[...] = ((x * lax.rsqrt(ms + NORM_EPS)) * g_ref[...]).astype(BF16)

    o_ref[...] = jnp.dot(xn_ref[...], w_ref[...], preferred_element_type=F32).astype(o_ref.dtype)


def norm_matmul(x, g, w, *, tm, tn):
    m, k = x.shape
    n = w.shape[1]
    tm, tn = _tile(m, tm), _tile(n, tn)
    return pl.pallas_call(
        _norm_matmul_body,
        grid=(m // tm, n // tn),
        in_specs=[
            pl.BlockSpec((tm, k), lambda i, j: (i, 0)),
            pl.BlockSpec((1, k), lambda i, j: (0, 0)),
            pl.BlockSpec((k, tn), lambda i, j: (0, j)),
        ],
        out_specs=pl.BlockSpec((tm, tn), lambda i, j: (i, j)),
        out_shape=jax.ShapeDtypeStruct((m, n), BF16),
        scratch_shapes=[pltpu.VMEM((tm, k), BF16)],
        compiler_params=_params(("parallel", "arbitrary")),
        name="norm_matmul",
    )(x, g.reshape(1, k), w)


def _retention_body(lg_ref, q_ref, k_ref, v_ref, g_ref, cos_ref, sin_ref, gn_ref, o_ref, state_ref, *, blk, dk):
    h = pl.program_id(1)
    c = pl.program_id(2)
    lg = lg_ref[h]

    @pl.when(c == 0)
    def _():
        state_ref[...] = jnp.zeros_like(state_ref)

    half = dk // 2
    cos = cos_ref[...]
    sin = sin_ref[...]

    def rot(t):
        t1, t2 = t[:, :half], t[:, half:]
        return jnp.concatenate([t1 * cos - t2 * sin, t1 * sin + t2 * cos], axis=-1)

    qr = rot(q_ref[...].astype(F32))
    kr = rot(k_ref[...].astype(F32)) * (dk ** -0.5)
    v = v_ref[...]

    pos = lax.broadcasted_iota(I32, (blk, 1), 0).astype(F32)
    q_dec = jnp.exp((pos + 1.0) * lg)
    k_dec = jnp.exp((blk - 1.0 - pos) * lg)
    blk_dec = jnp.exp(jnp.full((1, dk), blk * lg, F32))

    s = lax.dot_general(qr.astype(BF16), kr.astype(BF16), (((1,), (1,)), ((), ())), preferred_element_type=F32)
    ri = lax.broadcasted_iota(I32, (blk, blk), 0)
    ci = lax.broadcasted_iota(I32, (blk, blk), 1)
    diff = (ri - ci).astype(F32)
    dec = jnp.where(diff >= 0.0, jnp.exp(jnp.maximum(diff, 0.0) * lg), 0.0)
    inner = jnp.dot((s * dec).astype(BF16), v, preferred_element_type=F32)

    state = state_ref[...]
    cross = jnp.dot((qr * q_dec).astype(BF16), state.astype(BF16), preferred_element_type=F32)
    kd_t = jnp.transpose(kr * k_dec).astype(BF16)
    state_ref[...] = state * blk_dec + jnp.dot(kd_t, v, preferred_element_type=F32)

    out = inner + cross
    mu = jnp.mean(out, axis=-1, keepdims=True)
    cen = out - mu
    var = jnp.mean(cen * cen, axis=-1, keepdims=True)
    y = cen * lax.rsqrt(var + NORM_EPS) * gn_ref[...]
    gate = g_ref[...].astype(F32)
    o_ref[...] = (y * (gate * jax.nn.sigmoid(gate))).astype(o_ref.dtype)


def retention(proj, cos, sin, ret_norm_g, *, batch, seq, ret_width):
    heads = RET_HEADS
    dk = ret_width // heads
    blk = min(RET_BLOCK, seq)
    nblk = seq // blk
    hb = ret_width // dk
    log_gamma = jnp.log(1.0 - jnp.exp2(-5.0 - jnp.arange(heads, dtype=F32)))

    def col(seg):
        return pl.BlockSpec((blk, dk), lambda b, h, c, lg, seg=seg: (b * nblk + c, seg * hb + h))

    rowspec = pl.BlockSpec((blk, dk // 2), lambda b, h, c, lg: (b * nblk + c, 0))
    return pl.pallas_call(
        functools.partial(_retention_body, blk=blk, dk=dk),
        grid_spec=pltpu.PrefetchScalarGridSpec(
            num_scalar_prefetch=1,
            grid=(batch, heads, nblk),
            in_specs=[col(0), col(1), col(2), col(3), rowspec, rowspec,
                      pl.BlockSpec((1, dk), lambda b, h, c, lg: (0, h))],
            out_specs=pl.BlockSpec((blk, dk), lambda b, h, c, lg: (b * nblk + c, h)),
            scratch_shapes=[pltpu.VMEM((dk, dk), F32)],
        ),
        out_shape=jax.ShapeDtypeStruct((batch * seq, ret_width), BF16),
        compiler_params=_params(("parallel", "parallel", "arbitrary")),
        name="retention",
    )(log_gamma, proj, proj, proj, proj, cos, sin, ret_norm_g.reshape(1, ret_width))


def _gmlp_body(u_ref, v_ref, lng_ref, lnb_ref, ws_ref, bst_ref, o_ref, vn_ref, *, rows, groups, cg, chunk):
    v = jax.nn.gelu(v_ref[...].astype(F32))
    mu = jnp.mean(v, axis=-1, keepdims=True)
    cen = v - mu
    var = jnp.mean(cen * cen, axis=-1, keepdims=True)
    vn_ref[...] = (cen * lax.rsqrt(var + NORM_EPS) * lng_ref[...] + lnb_ref[...]).astype(BF16)

    ri = lax.broadcasted_iota(I32, (chunk, chunk), 0)
    ci = lax.broadcasted_iota(I32, (chunk, chunk), 1)
    causal = ri >= ci
    for g in range(groups):
        w = jnp.where(causal, ws_ref[g], 0.0).astype(BF16)
        bias = bst_ref[:, g:g + 1]
        cols = slice(g * cg, (g + 1) * cg)
        for t in range(rows // chunk):
            rws = slice(t * chunk, (t + 1) * chunk)
            sp = jnp.dot(w, vn_ref[rws, cols], preferred_element_type=F32) + bias
            u = jax.nn.gelu(u_ref[rws, cols].astype(F32))
            o_ref[rws, cols] = (u * sp).astype(o_ref.dtype)


def gmlp(proj, ln_g, ln_b, ws, bs, *, n_rows, gm_width, u_block, v_block):
    groups, chunk = GM_GROUPS, GM_CHUNK
    cg = gm_width // groups
    rows = 2 * chunk
    return pl.pallas_call(
        functools.partial(_gmlp_body, rows=rows, groups=groups, cg=cg, chunk=chunk),
        grid=(n_rows // rows,),
        in_specs=[
            pl.BlockSpec((rows, gm_width), lambda i: (i, u_block)),
            pl.BlockSpec((rows, gm_width), lambda i: (i, v_block)),
            pl.BlockSpec((1, gm_width), lambda i: (0, 0)),
            pl.BlockSpec((1, gm_width), lambda i: (0, 0)),
            pl.BlockSpec((groups, chunk, chunk), lambda i: (0, 0, 0)),
            pl.BlockSpec((chunk, groups), lambda i: (0, 0)),
        ],
        out_specs=pl.BlockSpec((rows, gm_width), lambda i: (i, 0)),
        out_shape=jax.ShapeDtypeStruct((n_rows, gm_width), BF16),
        scratch_shapes=[pltpu.VMEM((rows, gm_width), BF16)],
        compiler_params=_params(("parallel",)),
        name="gmlp",
    )(proj, proj, ln_g.reshape(1, gm_width), ln_b.reshape(1, gm_width), ws, bs.T)


def _matmul_residual_body(*refs, n_parts):
    a_refs, w_refs = refs[:n_parts], refs[n_parts:2 * n_parts]
    r_ref, o_ref = refs[2 * n_parts], refs[2 * n_parts + 1]
    acc = r_ref[...]
    for a_ref, w_ref in zip(a_refs, w_refs):
        acc = acc + jnp.dot(a_ref[...], w_ref[...], preferred_element_type=F32)
    o_ref[...] = acc


def matmul_residual(parts, w, res, *, tm, tn):
    m, n = res.shape
    tm, tn = _tile(m, tm), _tile(n, tn)
    kp = parts[0].shape[1]
    n_parts = len(parts)
    a_specs = [pl.BlockSpec((tm, kp), lambda i, j: (i, 0)) for _ in parts]
    w_specs = [pl.BlockSpec((kp, tn), lambda i, j, p=p: (p, j)) for p in range(n_parts)]
    return pl.pallas_call(
        functools.partial(_matmul_residual_body, n_parts=n_parts),
        grid=(m // tm, n // tn),
        in_specs=a_specs + w_specs + [pl.BlockSpec((tm, tn), lambda i, j: (i, j))],
        out_specs=pl.BlockSpec((tm, tn), lambda i, j: (i, j)),
        out_shape=jax.ShapeDtypeStruct((m, n), F32),
        compiler_params=_params(("parallel", "parallel")),
        name="matmul_residual",
    )(*parts, *([w] * n_parts), res)


def _cross_attention_body(q_ref, kv_ref, o_ref, *, heads, dh):
    width = heads * dh
    for h in range(heads):
        q = q_ref[:, h * dh:(h + 1) * dh]
        k = kv_ref[:, h * dh:(h + 1) * dh]
        v = kv_ref[:, width + h * dh:width + (h + 1) * dh]
        s = lax.dot_general(q, k, (((1,), (1,)), ((), ())), preferred_element_type=F32) * (dh ** -0.5)
        e = jnp.exp(s - jnp.max(s, axis=-1, keepdims=True))
        p = e / jnp.sum(e, axis=-1, keepdims=True)
        o_ref[:, h * dh:(h + 1) * dh] = jnp.dot(p.astype(BF16), v, preferred_element_type=F32).astype(o_ref.dtype)


def cross_attention(q, kv, *, batch, seq, mem_len, tq):
    heads = XA_HEADS
    width = q.shape[1]
    dh = width // heads
    tq = min(tq, seq)
    nq = seq // tq
    return pl.pallas_call(
        functools.partial(_cross_attention_body, heads=heads, dh=dh),
        grid=(batch, nq),
        in_specs=[
            pl.BlockSpec((tq, width), lambda b, i: (b * nq + i, 0)),
            pl.BlockSpec((mem_len, 2 * width), lambda b, i: (b, 0)),
        ],
        out_specs=pl.BlockSpec((tq, width), lambda b, i: (b * nq + i, 0)),
        out_shape=jax.ShapeDtypeStruct((batch * seq, width), BF16),
        compiler_params=_params(("parallel", "parallel")),
        name="cross_attention",
    )(q, kv)


def _router_body(h_ref, g_ref, wr_ref, br_ref, xp_ref, meta_ref, meta_t_ref, cnt_ref, base_ref, *, tm, n_grp, per):
    @pl.when(pl.program_id(0) == 0)
    def _():
        base_ref[...] = jnp.zeros_like(base_ref)

    x = h_ref[...]
    ms = jnp.mean(x * x, axis=-1, keepdims=True)
    xn = (x * lax.rsqrt(ms + NORM_EPS)) * g_ref[...]
    xp_ref[...] = _pack_halves(xn)

    x_hi = xn.astype(BF16)
    x_lo = (xn - x_hi.astype(F32)).astype(BF16)
    both = jnp.dot(x_hi, wr_ref[...], preferred_element_type=F32)
    corr = jnp.dot(x_lo, wr_ref[:, :V7X_LANES], preferred_element_type=F32)
    logits = both[:, :V7X_LANES] + (both[:, V7X_LANES:] + corr) + br_ref[...]
    lane = lax.broadcasted_iota(I32, logits.shape, 1)
    neg = jnp.float32(-1e30)
    big = jnp.int32(V7X_LANES)

    gl = jnp.where(lane < n_grp, logits, neg)
    gmax = jnp.max(gl, axis=-1, keepdims=True)
    gidx = jnp.min(jnp.where(gl == gmax, lane, big), axis=-1, keepdims=True)
    grp_gate = 1.0 / jnp.sum(jnp.exp(gl - gmax), axis=-1, keepdims=True)

    lo = n_grp + gidx * per
    el = jnp.where((lane >= lo) & (lane < lo + per), logits, neg)
    v1 = jnp.max(el, axis=-1, keepdims=True)
    i1 = jnp.min(jnp.where(el == v1, lane, big), axis=-1, keepdims=True)
    el2 = jnp.where(lane == i1, neg, el)
    v2 = jnp.max(el2, axis=-1, keepdims=True)
    i2 = jnp.min(jnp.where(el2 == v2, lane, big), axis=-1, keepdims=True)
    t = jnp.exp(v2 - v1)
    den = 1.0 + t
    g1 = grp_gate / den
    g2 = grp_gate * (t / den)

    oh1 = jnp.where(lane == i1, 1.0, 0.0)
    oh2 = jnp.where(lane == i2, 1.0, 0.0)
    ri = lax.broadcasted_iota(I32, (tm, tm), 0)
    ci = lax.broadcasted_iota(I32, (tm, tm), 1)
    lower = jnp.where(ri > ci, 1.0, 0.0).astype(BF16)
    pre1 = jnp.dot(lower, oh1.astype(BF16), preferred_element_type=F32)
    pre2 = jnp.dot(lower, oh2.astype(BF16), preferred_element_type=F32)
    cnt1 = jnp.sum(oh1, axis=0, keepdims=True)
    cnt2 = jnp.sum(oh2, axis=0, keepdims=True)
    base = base_ref[...]
    rank1 = jnp.sum(oh1 * (pre1 + base), axis=-1, keepdims=True)
    rank2 = jnp.sum(oh2 * (pre2 + base + cnt1), axis=-1, keepdims=True)
    total = base + cnt1 + cnt2
    base_ref[...] = total
    cnt_ref[...] = total

    e1 = (i1 - n_grp).astype(F32)
    e2 = (i2 - n_grp).astype(F32)
    meta = jnp.zeros(logits.shape, F32)
    for idx, val in enumerate((e1, e2, rank1, rank2, g1, g2)):
        meta = jnp.where(lane == idx, val, meta)
    meta_ref[...] = meta
    meta_t_ref[...] = jnp.transpose(meta)[:META_ROWS, :]


META_ROWS = 8


def moe_router(h, g, wr, br, *, tm):
    n, d = h.shape
    tm = min(tm, n)
    return pl.pallas_call(
        functools.partial(_router_body, tm=tm, n_grp=MOE_GROUPS, per=MOE_PER_GROUP),
        grid=(n // tm,),
        in_specs=[
            pl.BlockSpec((tm, d), lambda i: (i, 0)),
            pl.BlockSpec((1, d), lambda i: (0, 0)),
            pl.BlockSpec((d, 2 * V7X_LANES), lambda i: (0, 0)),
            pl.BlockSpec((1, V7X_LANES), lambda i: (0, 0)),
        ],
        out_specs=[
            pl.BlockSpec((tm, d // 2), lambda i: (i, 0)),
            pl.BlockSpec((tm, V7X_LANES), lambda i: (i, 0)),
            pl.BlockSpec((META_ROWS, tm), lambda i: (0, i)),
            pl.BlockSpec((1, V7X_LANES), lambda i: (0, 0)),
        ],
        out_shape=[
            jax.ShapeDtypeStruct((n, d // 2), U32),
            jax.ShapeDtypeStruct((n, V7X_LANES), F32),
            jax.ShapeDtypeStruct((META_ROWS, n), F32),
            jax.ShapeDtypeStruct((1, V7X_LANES), F32),
        ],
        scratch_shapes=[pltpu.VMEM((1, V7X_LANES), F32)],
        compiler_params=_params(("arbitrary",)),
        name="moe_router",
    )(h, g.reshape(1, d), wr, br)


MOE_CODE_SHIFT = 6
assert MOE_GROUPS * MOE_PER_GROUP == 1 << MOE_CODE_SHIFT


def _order_body(code_ref, start_ref, order_ref):
    def body(a, carry):
        code = code_ref[a]
        order_ref[start_ref[code & ((1 << MOE_CODE_SHIFT) - 1)] + (code >> MOE_CODE_SHIFT)] = a
        return carry

    lax.fori_loop(0, code_ref.shape[0], body, 0, unroll=8)


def moe_order(code, start):
    return pl.pallas_call(
        _order_body,
        grid_spec=pltpu.PrefetchScalarGridSpec(
            num_scalar_prefetch=2,
            grid=(1,),
            in_specs=[],
            out_specs=pl.BlockSpec(memory_space=pltpu.SMEM),
        ),
        out_shape=jax.ShapeDtypeStruct(code.shape, I32),
        compiler_params=_params(("arbitrary",)),
        name="moe_order",
    )(code, start)


EXPERT_STEPS = 4
ROW_DMA_UNROLL = 8
assert MOE_TOPK == 2


def _experts_body(exp_ref, nxt_ref, rows_ref, off_ref, order_ref, nact_ref,
                  xp_hbm, wg_ref, wu_ref, wdl_ref, wdh_ref, ya_hbm,
                  xbuf_ref, ybuf_ref, xl_ref, xh_ref, hg_ref, hu_ref, hd_ref, wa_ref, wb_ref, wc_ref, wd_ref,
                  gsem, ssem, *, rb, n_tok, n_blocks):
    del exp_ref, nxt_ref
    v = pl.program_id(0)
    s = pl.program_id(1)
    nact = nact_ref[0]
    active = v < nact
    slot = lax.rem(v, 2)
    half = rb // 2
    quarter = rb // EXPERT_STEPS
    oc = wc_ref.shape[1]

    def decode(base, r):
        a = order_ref[base + r]
        k = jnp.where(a >= n_tok, 1, 0)
        return k, a - k * n_tok

    def gather_copy(base, buf, r):
        _, tok = decode(base, r)
        return pltpu.make_async_copy(xp_hbm.at[pl.ds(tok, 1)], xbuf_ref.at[buf, pl.ds(r, 1)], gsem.at[buf])

    def scatter_copy(base, r):
        k, tok = decode(base, r)
        return pltpu.make_async_copy(ybuf_ref.at[pl.ds(r, 1)], ya_hbm.at[k, pl.ds(tok, 1)], ssem)

    def for_rows(lo, hi, fn):
        groups = lax.shift_right_logical(jnp.maximum(hi - lo, 0), ROW_DMA_UNROLL.bit_length() - 1)

        def group(g, carry):
            for u in range(ROW_DMA_UNROLL):
                fn(lo + g * ROW_DMA_UNROLL + u)
            return carry

        def single(r, carry):
            fn(r)
            return carry

        lax.fori_loop(0, groups, group, 0)
        lax.fori_loop(lo + groups * ROW_DMA_UNROLL, hi, single, 0)

    @pl.when(jnp.logical_and(v == 0, s == 0))
    def _():
        xbuf_ref[...] = jnp.zeros_like(xbuf_ref)
        base = off_ref[0]
        for_rows(0, rows_ref[0], lambda r: gather_copy(base, 0, r).start())

    @pl.when(jnp.logical_and(active, s == 0))
    def _():
        base = off_ref[v]
        for_rows(0, rows_ref[v], lambda r: gather_copy(base, slot, r).wait())

    @pl.when(jnp.logical_and(active, v + 1 < nact))
    def _():
        nxt = jnp.minimum(v + 1, n_blocks - 1)
        base = off_ref[nxt]
        for_rows(s * quarter, jnp.minimum((s + 1) * quarter, rows_ref[nxt]),
                 lambda r: gather_copy(base, 1 - slot, r).start())

    @pl.when(jnp.logical_and(active, jnp.logical_and(s == 2, v >= 1)))
    def _():
        prev = jnp.maximum(v - 1, 0)
        base = off_ref[prev]
        for_rows(0, rows_ref[prev], lambda r: scatter_copy(base, r).wait())

    halves = [(pl.ds(0, half), None), (pl.ds(half, half), rows_ref[v] > half)]

    def for_halves(fn):
        for rows, cond in halves:
            if cond is None:
                fn(rows)
            else:
                pl.when(cond)(functools.partial(fn, rows))

    @pl.when(jnp.logical_and(active, s < 2))
    def _():
        wa_ref[...] = wg_ref[0].astype(BF16)
        wb_ref[...] = wu_ref[0].astype(BF16)

    @pl.when(jnp.logical_and(active, s == 0))
    def _():
        def step0(rows):
            lo, hi = _unpack_halves(xbuf_ref[slot, rows, :])
            xl = lo.astype(BF16)
            xl_ref[rows, :] = xl
            xh_ref[rows, :] = hi.astype(BF16)
            hg_ref[rows, :] = jnp.dot(xl, wa_ref[...], preferred_element_type=F32)
            hu_ref[rows, :] = jnp.dot(xl, wb_ref[...], preferred_element_type=F32)
        for_halves(step0)

    @pl.when(jnp.logical_and(active, s == 1))
    def _():
        def step1(rows):
            xh = xh_ref[rows, :]
            hg = hg_ref[rows, :] + jnp.dot(xh, wa_ref[...], preferred_element_type=F32)
            hu = hu_ref[rows, :] + jnp.dot(xh, wb_ref[...], preferred_element_type=F32)
            hd_ref[rows, :] = (hg * jax.nn.sigmoid(hg) * hu).astype(BF16)
        for_halves(step1)

    def down(cols):
        wc_ref[...] = wdl_ref[0].astype(BF16)
        wd_ref[...] = wdh_ref[0].astype(BF16)

        def step2(rows):
            hd = hd_ref[rows, :]
            y_lo = jnp.dot(hd, wc_ref[...], preferred_element_type=F32)
            y_hi = jnp.dot(hd, wd_ref[...], preferred_element_type=F32)
            ybuf_ref[rows, cols] = _pack_halves(jnp.concatenate([y_lo, y_hi], axis=-1))
        for_halves(step2)

    @pl.when(jnp.logical_and(active, s == 2))
    def _():
        down(pl.ds(0, oc))

    @pl.when(jnp.logical_and(active, s == 3))
    def _():
        down(pl.ds(oc, oc))
        base = off_ref[v]
        for_rows(0, rows_ref[v], lambda r: scatter_copy(base, r).start())

    @pl.when(jnp.logical_and(v == n_blocks - 1, s == EXPERT_STEPS - 1))
    def _():
        last = jnp.maximum(nact - 1, 0)
        base = off_ref[last]
        for_rows(0, rows_ref[last], lambda r: scatter_copy(base, r).wait())


def moe_experts(xp, w_gate, w_up, w_down, blk_exp, blk_next, blk_rows, blk_off, order, nact, *, n_blocks, rb):
    n_exp, d, ff = w_gate.shape
    n_tok, kw = xp.shape
    assert d == 2 * kw
    oc = kw // 2

    def w_in_map(v, s, e, nx, rw, of, od, n):
        first = jnp.logical_and(v < n[0], s < 2)
        return (jnp.where(first, e[v], nx[v]), jnp.where(first, s, 0), 0)

    def w_down_map(half):
        def index(v, s, e, nx, rw, of, od, n):
            j = jnp.where(v < n[0], jnp.maximum(s - 2, 0), 1)
            return (e[v], 0, 2 * half + j)
        return index

    return pl.pallas_call(
        functools.partial(_experts_body, rb=rb, n_tok=n_tok, n_blocks=n_blocks),
        grid_spec=pltpu.PrefetchScalarGridSpec(
            num_scalar_prefetch=6,
            grid=(n_blocks, EXPERT_STEPS),
            in_specs=[
                pl.BlockSpec(memory_space=pl.ANY),
                pl.BlockSpec((1, d // 2, ff), w_in_map),
                pl.BlockSpec((1, d // 2, ff), w_in_map),
                pl.BlockSpec((1, ff, oc), w_down_map(0)),
                pl.BlockSpec((1, ff, oc), w_down_map(1)),
            ],
            out_specs=pl.BlockSpec(memory_space=pl.ANY),
            scratch_shapes=[pltpu.VMEM((2, rb, kw), U32), pltpu.VMEM((rb, kw), U32),
                            pltpu.VMEM((rb, kw), BF16), pltpu.VMEM((rb, kw), BF16),
                            pltpu.VMEM((rb, ff), F32), pltpu.VMEM((rb, ff), F32), pltpu.VMEM((rb, ff), BF16),
                            pltpu.VMEM((d // 2, ff), BF16), pltpu.VMEM((d // 2, ff), BF16),
                            pltpu.VMEM((ff, oc), BF16), pltpu.VMEM((ff, oc), BF16),
                            pltpu.SemaphoreType.DMA((2,)), pltpu.SemaphoreType.DMA(())],
        ),
        out_shape=jax.ShapeDtypeStruct((MOE_TOPK, n_tok, kw), U32),
        compiler_params=_params(("arbitrary", "arbitrary")),
        name="moe_experts",
    )(blk_exp, blk_next, blk_rows, blk_off, order, nact, xp, w_gate, w_up, w_down, w_down)


def _combine_body(h_ref, meta_ref, g_ref, y0_ref, y1_ref, o_ref):
    meta = meta_ref[...]
    acc_lo = None
    acc_hi = None
    for k, y_ref in enumerate((y0_ref, y1_ref)):
        gate = meta[:, 2 * MOE_TOPK + k:2 * MOE_TOPK + k + 1]
        lo, hi = _unpack_halves(y_ref[0])
        lo, hi = lo * gate, hi * gate
        acc_lo = lo if acc_lo is None else acc_lo + lo
        acc_hi = hi if acc_hi is None else acc_hi + hi
    kh = acc_lo.shape[1]
    h_lo = h_ref[:, :kh] + acc_lo
    h_hi = h_ref[:, kh:] + acc_hi
    ms = (jnp.sum(h_lo * h_lo, axis=-1, keepdims=True) + jnp.sum(h_hi * h_hi, axis=-1, keepdims=True)) / (2 * kh)
    scale = lax.rsqrt(ms + NORM_EPS)
    o_ref[:, :kh] = (h_lo * scale) * g_ref[:, :kh]
    o_ref[:, kh:] = (h_hi * scale) * g_ref[:, kh:]


def moe_combine(h, meta, g, ya, *, tb):
    n, d = h.shape
    kw = ya.shape[2]
    tb = _tile(n, tb)

    def slot(k):
        return pl.BlockSpec((1, tb, kw), lambda i, k=k: (k, i, 0))

    return pl.pallas_call(
        _combine_body,
        grid=(n // tb,),
        in_specs=[
            pl.BlockSpec((tb, d), lambda i: (i, 0)),
            pl.BlockSpec((tb, V7X_LANES), lambda i: (i, 0)),
            pl.BlockSpec((1, d), lambda i: (0, 0)),
            slot(0), slot(1),
        ],
        out_specs=pl.BlockSpec((tb, d), lambda i: (i, 0)),
        out_shape=jax.ShapeDtypeStruct((n, d), F32),
        compiler_params=_params(("parallel",)),
        name="moe_combine",
    )(h, meta, g.reshape(1, d), ya, ya)


def _moe_plan(counts, *, rb, n_blocks):
    n_exp = counts.shape[0]
    per_exp = (counts + rb - 1) // rb
    blk_end = jnp.cumsum(per_exp)
    start = jnp.cumsum(counts) - counts
    nact = blk_end[-1].astype(I32)
    blk = jnp.arange(n_blocks, dtype=I32)
    blk_exp = jnp.minimum(jnp.searchsorted(blk_end, blk, side="right"), n_exp - 1).astype(I32)
    active = blk < nact
    last = jnp.maximum(nact - 1, 0)
    blk_exp = jnp.where(active, blk_exp, blk_exp[last]).astype(I32)
    blk_next = blk_exp[jnp.minimum(blk + 1, last)]
    j = blk - (blk_end[blk_exp] - per_exp[blk_exp])
    blk_off = jnp.where(active, start[blk_exp] + j * rb, 0).astype(I32)
    blk_rows = jnp.where(active, jnp.clip(counts[blk_exp] - j * rb, 0, rb), 0).astype(I32)
    return start.astype(I32), blk_exp, blk_next, blk_rows, blk_off, nact.reshape(1)


def kernel(x, mem, positions, norm_mix_g, w_in, ret_norm_g, gm_ln_g, gm_ln_b, gm_ws, gm_bs, w_out, norm_xa_g, norm_mem_g, xa_wq, xa_wkv, xa_wo, norm_moe_g, router_grp_w, router_grp_b, router_exp_w, router_exp_b, moe_w_gate, moe_w_up, moe_w_down, norm_final_g):
    batch, seq, d = x.shape
    mem_len = mem.shape[1]
    n = batch * seq
    ret_width = ret_norm_g.shape[0]
    gm_width = gm_ln_g.shape[0]
    assert ret_width == gm_width and w_in.shape[1] == 4 * ret_width + 2 * gm_width
    dk = ret_width // RET_HEADS
    n_exp = moe_w_gate.shape[0]
    assert n_exp == MOE_GROUPS * MOE_PER_GROUP and MOE_GROUPS + n_exp <= V7X_LANES

    inv_freq = ROPE_BASE ** (-jnp.arange(0, dk, 2, dtype=F32) / dk)
    ang = positions.astype(F32).reshape(n, 1) * inv_freq
    cos, sin = jnp.cos(ang), jnp.sin(ang)

    x2 = x.reshape(n, d)
    proj = norm_matmul(x2, norm_mix_g, w_in.astype(BF16), tm=512, tn=1024)
    ret = retention(proj, cos, sin, ret_norm_g, batch=batch, seq=seq, ret_width=ret_width)
    gm = gmlp(proj, gm_ln_g, gm_ln_b, gm_ws, gm_bs, n_rows=n, gm_width=gm_width,
              u_block=4 * ret_width // gm_width, v_block=4 * ret_width // gm_width + 1)
    h1 = matmul_residual([ret, gm], w_out.astype(BF16), x2, tm=512, tn=1024)

    q = norm_matmul(h1, norm_xa_g, xa_wq.astype(BF16), tm=512, tn=1024)
    kv = norm_matmul(mem.reshape(batch * mem_len, d), norm_mem_g, xa_wkv.astype(BF16), tm=512, tn=1024)
    o = cross_attention(q, kv, batch=batch, seq=seq, mem_len=mem_len, tq=512)
    h2 = matmul_residual([o], xa_wo.astype(BF16), h1, tm=512, tn=1024)

    pad = V7X_LANES - MOE_GROUPS - n_exp
    wr = jnp.concatenate([router_grp_w, router_exp_w, jnp.zeros((d, pad), F32)], axis=1)
    br = jnp.concatenate([router_grp_b, router_exp_b, jnp.zeros((pad,), F32)]).reshape(1, V7X_LANES)
    wr_hi = wr.astype(BF16)
    wr_lo = (wr - wr_hi.astype(F32)).astype(BF16)
    xp, meta, meta_t, cnt = moe_router(h2, norm_moe_g, jnp.concatenate([wr_hi, wr_lo], axis=1), br, tm=256)

    rb = MOE_ROW_BLOCK
    n_blocks = -(-(n * MOE_TOPK) // rb) + n_exp
    counts = cnt[0, MOE_GROUPS:MOE_GROUPS + n_exp].astype(I32)
    start, blk_exp, blk_next, blk_rows, blk_off, nact = _moe_plan(counts, rb=rb, n_blocks=n_blocks)
    code = (meta_t[MOE_TOPK:2 * MOE_TOPK].astype(I32) * (1 << MOE_CODE_SHIFT)
            + meta_t[0:MOE_TOPK].astype(I32)).reshape(-1)
    order = moe_order(code, start)
    ya = moe_experts(xp, moe_w_gate, moe_w_up, moe_w_down, blk_exp, blk_next, blk_rows, blk_off, order, nact,
                     n_blocks=n_blocks, rb=rb)
    y = moe_combine(h2, meta, norm_final_g, ya, tb=256)
    return y.reshape(batch, seq, d)
```

```python
import functools

import jax
import jax.numpy as jnp
from jax import lax
from jax.experimental import pallas as pl
from jax.experimental.pallas import tpu as pltpu

NORM_EPS = 1e-6
RET_HEADS = 8
ROPE_BASE = 10000.0
GM_GROUPS = 8
GM_CHUNK = 128
XA_HEADS = 4
MOE_GROUPS = 8
MOE_PER_GROUP = 8
MOE_TOPK = 2

V7X_LANES = 128
V7X_VMEM_BYTES = 64 * 1024 * 1024
VMEM_LIMIT_BYTES = 56 * 1024 * 1024

RET_BLOCK = 512
MOE_ROW_BLOCK = 512

F32 = jnp.float32
BF16 = jnp.bfloat16
U32 = jnp.uint32
I32 = jnp.int32


def _params(sem):
    return pltpu.CompilerParams(dimension_semantics=sem, vmem_limit_bytes=VMEM_LIMIT_BYTES)


def _tile(dim, target):
    t = min(dim, target)
    while dim % t:
        t -= V7X_LANES
    assert t > 0, (dim, target)
    return t


def _pack_halves(x_f32):
    k = x_f32.shape[-1] // 2
    bits = lax.bitcast_convert_type(x_f32.astype(BF16).astype(F32), U32)
    return (bits[:, k:] & jnp.uint32(0xFFFF0000)) | (bits[:, :k] >> 16)


def _unpack_halves(w_u32):
    lo = lax.bitcast_convert_type(w_u32 << 16, F32)
    hi = lax.bitcast_convert_type(w_u32 & jnp.uint32(0xFFFF0000), F32)
    return lo, hi


def _norm_matmul_body(x_ref, g_ref, w_ref, o_ref, xn_ref):
    @pl.when(pl.program_id(1) == 0)
    def _():
        x = x_ref[...]
        ms = jnp.mean(x * x, axis=-1, keepdims=True)
        xn_ref[...] = ((x * lax.rsqrt(ms + NORM_EPS)) * g_ref[...]).astype(BF16)

    o_ref[...] = jnp.dot(xn_ref[...], w_ref[...], preferred_element_type=F32).astype(o_ref.dtype)


def norm_matmul(x, g, w, *, tm, tn):
    m, k = x.shape
    n = w.shape[1]
    tm, tn = _tile(m, tm), _tile(n, tn)
    return pl.pallas_call(
        _norm_matmul_body,
        grid=(m // tm, n // tn),
        in_specs=[
            pl.BlockSpec((tm, k), lambda i, j: (i, 0)),
            pl.BlockSpec((1, k), lambda i, j: (0, 0)),
            pl.BlockSpec((k, tn), lambda i, j: (0, j)),
        ],
        out_specs=pl.BlockSpec((tm, tn), lambda i, j: (i, j)),
        out_shape=jax.ShapeDtypeStruct((m, n), BF16),
        scratch_shapes=[pltpu.VMEM((tm, k), BF16)],
        compiler_params=_params(("parallel", "arbitrary")),
        name="norm_matmul",
    )(x, g.reshape(1, k), w)


def _retention_body(lg_ref, q_ref, k_ref, v_ref, g_ref, cos_ref, sin_ref, gn_ref, o_ref, state_ref, *, blk, dk):
    h = pl.program_id(1)
    c = pl.program_id(2)
    lg = lg_ref[h]

    @pl.when(c == 0)
    def _():
        state_ref[...] = jnp.zeros_like(state_ref)

    half = dk // 2
    cos = cos_ref[...]
    sin = sin_ref[...]

    def rot(t):
        t1, t2 = t[:, :half], t[:, half:]
        return jnp.concatenate([t1 * cos - t2 * sin, t1 * sin + t2 * cos], axis=-1)

    qr = rot(q_ref[...].astype(F32))
    kr = rot(k_ref[...].astype(F32)) * (dk ** -0.5)
    v = v_ref[...]

    pos = lax.broadcasted_iota(I32, (blk, 1), 0).astype(F32)
    q_dec = jnp.exp((pos + 1.0) * lg)
    k_dec = jnp.exp((blk - 1.0 - pos) * lg)
    blk_dec = jnp.exp(jnp.full((1, dk), blk * lg, F32))

    s = lax.dot_general(qr.astype(BF16), kr.astype(BF16), (((1,), (1,)), ((), ())), preferred_element_type=F32)
    ri = lax.broadcasted_iota(I32, (blk, blk), 0)
    ci = lax.broadcasted_iota(I32, (blk, blk), 1)
    diff = (ri - ci).astype(F32)
    dec = jnp.where(diff >= 0.0, jnp.exp(jnp.maximum(diff, 0.0) * lg), 0.0)
    inner = jnp.dot((s * dec).astype(BF16), v, preferred_element_type=F32)

    state = state_ref[...]
    cross = jnp.dot((qr * q_dec).astype(BF16), state.astype(BF16), preferred_element_type=F32)
    kd_t = jnp.transpose(kr * k_dec).astype(BF16)
    state_ref[...] = state * blk_dec + jnp.dot(kd_t, v, preferred_element_type=F32)

    out = inner + cross
    mu = jnp.mean(out, axis=-1, keepdims=True)
    cen = out - mu
    var = jnp.mean(cen * cen, axis=-1, keepdims=True)
    y = cen * lax.rsqrt(var + NORM_EPS) * gn_ref[...]
    gate = g_ref[...].astype(F32)
    o_ref[...] = (y * (gate * jax.nn.sigmoid(gate))).astype(o_ref.dtype)


def retention(proj, cos, sin, ret_norm_g, *, batch, seq, ret_width):
    heads = RET_HEADS
    dk = ret_width // heads
    blk = min(RET_BLOCK, seq)
    nblk = seq // blk
    hb = ret_width // dk
    log_gamma = jnp.log(1.0 - jnp.exp2(-5.0 - jnp.arange(heads, dtype=F32)))

    def col(seg):
        return pl.BlockSpec((blk, dk), lambda b, h, c, lg, seg=seg: (b * nblk + c, seg * hb + h))

    rowspec = pl.BlockSpec((blk, dk // 2), lambda b, h, c, lg: (b * nblk + c, 0))
    return pl.pallas_call(
        functools.partial(_retention_body, blk=blk, dk=dk),
        grid_spec=pltpu.PrefetchScalarGridSpec(
            num_scalar_prefetch=1,
            grid=(batch, heads, nblk),
            in_specs=[col(0), col(1), col(2), col(3), rowspec, rowspec,
                      pl.BlockSpec((1, dk), lambda b, h, c, lg: (0, h))],
            out_specs=pl.BlockSpec((blk, dk), lambda b, h, c, lg: (b * nblk + c, h)),
            scratch_shapes=[pltpu.VMEM((dk, dk), F32)],
        ),
        out_shape=jax.ShapeDtypeStruct((batch * seq, ret_width), BF16),
        compiler_params=_params(("parallel", "parallel", "arbitrary")),
        name="retention",
    )(log_gamma, proj, proj, proj, proj, cos, sin, ret_norm_g.reshape(1, ret_width))


def _gmlp_body(u_ref, v_ref, lng_ref, lnb_ref, ws_ref, bst_ref, o_ref, vn_ref, *, rows, groups, cg, chunk):
    v = jax.nn.gelu(v_ref[...].astype(F32))
    mu = jnp.mean(v, axis=-1, keepdims=True)
    cen = v - mu
    var = jnp.mean(cen * cen, axis=-1, keepdims=True)
    vn_ref[...] = (cen * lax.rsqrt(var + NORM_EPS) * lng_ref[...] + lnb_ref[...]).astype(BF16)

    ri = lax.broadcasted_iota(I32, (chunk, chunk), 0)
    ci = lax.broadcasted_iota(I32, (chunk, chunk), 1)
    causal = ri >= ci
    for g in range(groups):
        w = jnp.where(causal, ws_ref[g], 0.0).astype(BF16)
        bias = bst_ref[:, g:g + 1]
        cols = slice(g * cg, (g + 1) * cg)
        for t in range(rows // chunk):
            rws = slice(t * chunk, (t + 1) * chunk)
            sp = jnp.dot(w, vn_ref[rws, cols], preferred_element_type=F32) + bias
            u = jax.nn.gelu(u_ref[rws, cols].astype(F32))
            o_ref[rws, cols] = (u * sp).astype(o_ref.dtype)


def gmlp(proj, ln_g, ln_b, ws, bs, *, n_rows, gm_width, u_block, v_block):
    groups, chunk = GM_GROUPS, GM_CHUNK
    cg = gm_width // groups
    rows = 2 * chunk
    return pl.pallas_call(
        functools.partial(_gmlp_body, rows=rows, groups=groups, cg=cg, chunk=chunk),
        grid=(n_rows // rows,),
        in_specs=[
            pl.BlockSpec((rows, gm_width), lambda i: (i, u_block)),
            pl.BlockSpec((rows, gm_width), lambda i: (i, v_block)),
            pl.BlockSpec((1, gm_width), lambda i: (0, 0)),
            pl.BlockSpec((1, gm_width), lambda i: (0, 0)),
            pl.BlockSpec((groups, chunk, chunk), lambda i: (0, 0, 0)),
            pl.BlockSpec((chunk, groups), lambda i: (0, 0)),
        ],
        out_specs=pl.BlockSpec((rows, gm_width), lambda i: (i, 0)),
        out_shape=jax.ShapeDtypeStruct((n_rows, gm_width), BF16),
        scratch_shapes=[pltpu.VMEM((rows, gm_width), BF16)],
        compiler_params=_params(("parallel",)),
        name="gmlp",
    )(proj, proj, ln_g.reshape(1, gm_width), ln_b.reshape(1, gm_width), ws, bs.T)


def _matmul_residual_body(*refs, n_parts):
    a_refs, w_refs = refs[:n_parts], refs[n_parts:2 * n_parts]
    r_ref, o_ref = refs[2 * n_parts], refs[2 * n_parts + 1]
    acc = r_ref[...]
    for a_ref, w_ref in zip(a_refs, w_refs):
        acc = acc + jnp.dot(a_ref[...], w_ref[...], preferred_element_type=F32)
    o_ref[...] = acc


def matmul_residual(parts, w, res, *, tm, tn):
    m, n = res.shape
    tm, tn = _tile(m, tm), _tile(n, tn)
    kp = parts[0].shape[1]
    n_parts = len(parts)
    a_specs = [pl.BlockSpec((tm, kp), lambda i, j: (i, 0)) for _ in parts]
    w_specs = [pl.BlockSpec((kp, tn), lambda i, j, p=p: (p, j)) for p in range(n_parts)]
    return pl.pallas_call(
        functools.partial(_matmul_residual_body, n_parts=n_parts),
        grid=(m // tm, n // tn),
        in_specs=a_specs + w_specs + [pl.BlockSpec((tm, tn), lambda i, j: (i, j))],
        out_specs=pl.BlockSpec((tm, tn), lambda i, j: (i, j)),
        out_shape=jax.ShapeDtypeStruct((m, n), F32),
        compiler_params=_params(("parallel", "parallel")),
        name="matmul_residual",
    )(*parts, *([w] * n_parts), res)


def _cross_attention_body(q_ref, kv_ref, o_ref, *, heads, dh):
    width = heads * dh
    for h in range(heads):
        q = q_ref[:, h * dh:(h + 1) * dh]
        k = kv_ref[:, h * dh:(h + 1) * dh]
        v = kv_ref[:, width + h * dh:width + (h + 1) * dh]
        s = lax.dot_general(q, k, (((1,), (1,)), ((), ())), preferred_element_type=F32) * (dh ** -0.5)
        e = jnp.exp(s - jnp.max(s, axis=-1, keepdims=True))
        p = e / jnp.sum(e, axis=-1, keepdims=True)
        o_ref[:, h * dh:(h + 1) * dh] = jnp.dot(p.astype(BF16), v, preferred_element_type=F32).astype(o_ref.dtype)


def cross_attention(q, kv, *, batch, seq, mem_len, tq):
    heads = XA_HEADS
    width = q.shape[1]
    dh = width // heads
    tq = min(tq, seq)
    nq = seq // tq
    return pl.pallas_call(
        functools.partial(_cross_attention_body, heads=heads, dh=dh),
        grid=(batch, nq),
        in_specs=[
            pl.BlockSpec((tq, width), lambda b, i: (b * nq + i, 0)),
            pl.BlockSpec((mem_len, 2 * width), lambda b, i: (b, 0)),
        ],
        out_specs=pl.BlockSpec((tq, width), lambda b, i: (b * nq + i, 0)),
        out_shape=jax.ShapeDtypeStruct((batch * seq, width), BF16),
        compiler_params=_params(("parallel", "parallel")),
        name="cross_attention",
    )(q, kv)


def _router_body(h_ref, g_ref, wr_ref, br_ref, xp_ref, meta_ref, meta_t_ref, cnt_ref, base_ref, *, tm, n_grp, per):
    @pl.when(pl.program_id(0) == 0)
    def _():
        base_ref[...] = jnp.zeros_like(base_ref)

    x = h_ref[...]
    ms = jnp.mean(x * x, axis=-1, keepdims=True)
    xn = (x * lax.rsqrt(ms + NORM_EPS)) * g_ref[...]
    xp_ref[...] = _pack_halves(xn)

    x_hi = xn.astype(BF16)
    x_lo = (xn - x_hi.astype(F32)).astype(BF16)
    both = jnp.dot(x_hi, wr_ref[...], preferred_element_type=F32)
    corr = jnp.dot(x_lo, wr_ref[:, :V7X_LANES], preferred_element_type=F32)
    logits = both[:, :V7X_LANES] + (both[:, V7X_LANES:] + corr) + br_ref[...]
    lane = lax.broadcasted_iota(I32, logits.shape, 1)
    neg = jnp.float32(-1e30)
    big = jnp.int32(V7X_LANES)

    gl = jnp.where(lane < n_grp, logits, neg)
    gmax = jnp.max(gl, axis=-1, keepdims=True)
    gidx = jnp.min(jnp.where(gl == gmax, lane, big), axis=-1, keepdims=True)
    grp_gate = 1.0 / jnp.sum(jnp.exp(gl - gmax), axis=-1, keepdims=True)

    lo = n_grp + gidx * per
    el = jnp.where((lane >= lo) & (lane < lo + per), logits, neg)
    v1 = jnp.max(el, axis=-1, keepdims=True)
    i1 = jnp.min(jnp.where(el == v1, lane, big), axis=-1, keepdims=True)
    el2 = jnp.where(lane == i1, neg, el)
    v2 = jnp.max(el2, axis=-1, keepdims=True)
    i2 = jnp.min(jnp.where(el2 == v2, lane, big), axis=-1, keepdims=True)
    t = jnp.exp(v2 - v1)
    den = 1.0 + t
    g1 = grp_gate / den
    g2 = grp_gate * (t / den)

    oh1 = jnp.where(lane == i1, 1.0, 0.0)
    oh2 = jnp.where(lane == i2, 1.0, 0.0)
    ri = lax.broadcasted_iota(I32, (tm, tm), 0)
    ci = lax.broadcasted_iota(I32, (tm, tm), 1)
    lower = jnp.where(ri > ci, 1.0, 0.0).astype(BF16)
    pre1 = jnp.dot(lower, oh1.astype(BF16), preferred_element_type=F32)
    pre2 = jnp.dot(lower, oh2.astype(BF16), preferred_element_type=F32)
    cnt1 = jnp.sum(oh1, axis=0, keepdims=True)
    cnt2 = jnp.sum(oh2, axis=0, keepdims=True)
    base = base_ref[...]
    rank1 = jnp.sum(oh1 * (pre1 + base), axis=-1, keepdims=True)
    rank2 = jnp.sum(oh2 * (pre2 + base + cnt1), axis=-1, keepdims=True)
    total = base + cnt1 + cnt2
    base_ref[...] = total
    cnt_ref[...] = total

    e1 = (i1 - n_grp).astype(F32)
    e2 = (i2 - n_grp).astype(F32)
    meta = jnp.zeros(logits.shape, F32)
    for idx, val in enumerate((e1, e2, rank1, rank2, g1, g2)):
        meta = jnp.where(lane == idx, val, meta)
    meta_ref[...] = meta
    meta_t_ref[...] = jnp.transpose(meta)[:META_ROWS, :]


META_ROWS = 8


def moe_router(h, g, wr, br, *, tm):
    n, d = h.shape
    tm = min(tm, n)
    return pl.pallas_call(
        functools.partial(_router_body, tm=tm, n_grp=MOE_GROUPS, per=MOE_PER_GROUP),
        grid=(n // tm,),
        in_specs=[
            pl.BlockSpec((tm, d), lambda i: (i, 0)),
            pl.BlockSpec((1, d), lambda i: (0, 0)),
            pl.BlockSpec((d, 2 * V7X_LANES), lambda i: (0, 0)),
            pl.BlockSpec((1, V7X_LANES), lambda i: (0, 0)),
        ],
        out_specs=[
            pl.BlockSpec((tm, d // 2), lambda i: (i, 0)),
            pl.BlockSpec((tm, V7X_LANES), lambda i: (i, 0)),
            pl.BlockSpec((META_ROWS, tm), lambda i: (0, i)),
            pl.BlockSpec((1, V7X_LANES), lambda i: (0, 0)),
        ],
        out_shape=[
            jax.ShapeDtypeStruct((n, d // 2), U32),
            jax.ShapeDtypeStruct((n, V7X_LANES), F32),
            jax.ShapeDtypeStruct((META_ROWS, n), F32),
            jax.ShapeDtypeStruct((1, V7X_LANES), F32),
        ],
        scratch_shapes=[pltpu.VMEM((1, V7X_LANES), F32)],
        compiler_params=_params(("arbitrary",)),
        name="moe_router",
    )(h, g.reshape(1, d), wr, br)


MOE_CODE_SHIFT = 6
assert MOE_GROUPS * MOE_PER_GROUP == 1 << MOE_CODE_SHIFT


def _order_body(code_ref, start_ref, order_ref):
    def body(a, carry):
        code = code_ref[a]
        order_ref[start_ref[code & ((1 << MOE_CODE_SHIFT) - 1)] + (code >> MOE_CODE_SHIFT)] = a
        return carry

    lax.fori_loop(0, code_ref.shape[0], body, 0, unroll=8)


def moe_order(code, start):
    return pl.pallas_call(
        _order_body,
        grid_spec=pltpu.PrefetchScalarGridSpec(
            num_scalar_prefetch=2,
            grid=(1,),
            in_specs=[],
            out_specs=pl.BlockSpec(memory_space=pltpu.SMEM),
        ),
        out_shape=jax.ShapeDtypeStruct(code.shape, I32),
        compiler_params=_params(("arbitrary",)),
        name="moe_order",
    )(code, start)


EXPERT_STEPS = 4
ROW_DMA_UNROLL = 8
assert MOE_TOPK == 2


def _experts_body(exp_ref, nxt_ref, rows_ref, off_ref, order_ref, nact_ref,
                  xp_hbm, wg_hbm, wu_hbm, wdn_hbm, ya_hbm,
                  xbuf_ref, ybuf_ref, xl_ref, xh_ref, hg_ref, hu_ref, hd_ref, wa_ref, wb_ref, wc_ref, wd_ref,
                  ring_in, ring_dn, gsem, ssem, sem_in, sem_dn, *, rb, n_tok, n_blocks):
    v = pl.program_id(0)
    s = pl.program_id(1)
    nact = nact_ref[0]
    active = v < nact
    slot = lax.rem(v, 2)
    half = rb // 2
    quarter = rb // EXPERT_STEPS
    oc = wc_ref.shape[1]
    dq = ring_in.shape[1]

    def in_copy(e, i):
        src = (wg_hbm, wg_hbm, wu_hbm, wu_hbm)[i % 4]
        q = 2 * (i // 4) + i % 2
        return pltpu.make_async_copy(src.at[e, pl.ds(q * dq, dq), :], ring_in.at[i], sem_in.at[i])

    def dn_copy(e, i):
        col = (i % 2) * 2 * oc + (i // 2) * oc
        return pltpu.make_async_copy(wdn_hbm.at[e, :, pl.ds(col, oc)], ring_dn.at[i], sem_dn.at[i])

    def refill(copy, slots):
        @pl.when(v + 1 < nact)
        def _():
            for i in slots:
                copy(nxt_ref[v], i).start()

    def decode(base, r):
        a = order_ref[base + r]
        k = jnp.where(a >= n_tok, 1, 0)
        return k, a - k * n_tok

    def gather_copy(base, buf, r):
        _, tok = decode(base, r)
        return pltpu.make_async_copy(xp_hbm.at[pl.ds(tok, 1)], xbuf_ref.at[buf, pl.ds(r, 1)], gsem.at[buf])

    def scatter_copy(base, r):
        k, tok = decode(base, r)
        return pltpu.make_async_copy(ybuf_ref.at[pl.ds(r, 1)], ya_hbm.at[k, pl.ds(tok, 1)], ssem)

    def for_rows(lo, hi, fn):
        groups = lax.shift_right_logical(jnp.maximum(hi - lo, 0), ROW_DMA_UNROLL.bit_length() - 1)

        def group(g, carry):
            for u in range(ROW_DMA_UNROLL):
                fn(lo + g * ROW_DMA_UNROLL + u)
            return carry

        def single(r, carry):
            fn(r)
            return carry

        lax.fori_loop(0, groups, group, 0)
        lax.fori_loop(lo + groups * ROW_DMA_UNROLL, hi, single, 0)

    @pl.when(jnp.logical_and(v == 0, s == 0))
    def _():
        for i in range(ring_in.shape[0]):
            in_copy(exp_ref[0], i).start()
        for i in range(ring_dn.shape[0]):
            dn_copy(exp_ref[0], i).start()
        xbuf_ref[...] = jnp.zeros_like(xbuf_ref)
        base = off_ref[0]
        for_rows(0, rows_ref[0], lambda r: gather_copy(base, 0, r).start())

    @pl.when(jnp.logical_and(active, s == 0))
    def _():
        base = off_ref[v]
        for_rows(0, rows_ref[v], lambda r: gather_copy(base, slot, r).wait())

    @pl.when(jnp.logical_and(active, v + 1 < nact))
    def _():
        nxt = jnp.minimum(v + 1, n_blocks - 1)
        base = off_ref[nxt]
        for_rows(s * quarter, jnp.minimum((s + 1) * quarter, rows_ref[nxt]),
                 lambda r: gather_copy(base, 1 - slot, r).start())

    @pl.when(jnp.logical_and(active, jnp.logical_and(s == 2, v >= 1)))
    def _():
        prev = jnp.maximum(v - 1, 0)
        base = off_ref[prev]
        for_rows(0, rows_ref[prev], lambda r: scatter_copy(base, r).wait())

    halves = [(pl.ds(0, half), None), (pl.ds(half, half), rows_ref[v] > half)]

    def for_halves(fn):
        for rows, cond in halves:
            if cond is None:
                fn(rows)
            else:
                pl.when(cond)(functools.partial(fn, rows))

    def load_in(step):
        slots = range(4 * step, 4 * step + 4)
        for i in slots:
            in_copy(exp_ref[v], i).wait()
        for i in slots:
            dst = (wa_ref, wa_ref, wb_ref, wb_ref)[i % 4]
            dst[pl.ds((i % 2) * dq, dq), :] = ring_in[i].astype(BF16)
        refill(in_copy, slots)

    def load_dn(step):
        slots = range(2 * (step - 2), 2 * (step - 2) + 2)
        for i in slots:
            dn_copy(exp_ref[v], i).wait()
        for i in slots:
            (wc_ref, wd_ref)[i % 2][...] = ring_dn[i].astype(BF16)
        refill(dn_copy, slots)

    @pl.when(jnp.logical_and(active, s == 0))
    def _():
        load_in(0)

        def step0(rows):
            lo, hi = _unpack_halves(xbuf_ref[slot, rows, :])
            xl = lo.astype(BF16)
            xl_ref[rows, :] = xl
            xh_ref[rows, :] = hi.astype(BF16)
            hg_ref[rows, :] = jnp.dot(xl, wa_ref[...], preferred_element_type=F32)
            hu_ref[rows, :] = jnp.dot(xl, wb_ref[...], preferred_element_type=F32)
        for_halves(step0)

    @pl.when(jnp.logical_and(active, s == 1))
    def _():
        load_in(1)

        def step1(rows):
            xh = xh_ref[rows, :]
            hg = hg_ref[rows, :] + jnp.dot(xh, wa_ref[...], preferred_element_type=F32)
            hu = hu_ref[rows, :] + jnp.dot(xh, wb_ref[...], preferred_element_type=F32)
            hd_ref[rows, :] = (hg * jax.nn.sigmoid(hg) * hu).astype(BF16)
        for_halves(step1)

    def down(step, cols):
        load_dn(step)

        def step2(rows):
            hd = hd_ref[rows, :]
            y_lo = jnp.dot(hd, wc_ref[...], preferred_element_type=F32)
            y_hi = jnp.dot(hd, wd_ref[...], preferred_element_type=F32)
            ybuf_ref[rows, cols] = _pack_halves(jnp.concatenate([y_lo, y_hi], axis=-1))
        for_halves(step2)

    @pl.when(jnp.logical_and(active, s == 2))
    def _():
        down(2, pl.ds(0, oc))

    @pl.when(jnp.logical_and(active, s == 3))
    def _():
        down(3, pl.ds(oc, oc))
        base = off_ref[v]
        for_rows(0, rows_ref[v], lambda r: scatter_copy(base, r).start())

    @pl.when(jnp.logical_and(v == n_blocks - 1, s == EXPERT_STEPS - 1))
    def _():
        last = jnp.maximum(nact - 1, 0)
        base = off_ref[last]
        for_rows(0, rows_ref[last], lambda r: scatter_copy(base, r).wait())


def moe_experts(xp, w_gate, w_up, w_down, blk_exp, blk_next, blk_rows, blk_off, order, nact, *, n_blocks, rb):
    n_exp, d, ff = w_gate.shape
    n_tok, kw = xp.shape
    assert d == 2 * kw
    oc = kw // 2
    n_in, n_dn = 8, 4
    any_spec = pl.BlockSpec(memory_space=pl.ANY)
    return pl.pallas_call(
        functools.partial(_experts_body, rb=rb, n_tok=n_tok, n_blocks=n_blocks),
        grid_spec=pltpu.PrefetchScalarGridSpec(
            num_scalar_prefetch=6,
            grid=(n_blocks, EXPERT_STEPS),
            in_specs=[any_spec, any_spec, any_spec, any_spec],
            out_specs=any_spec,
            scratch_shapes=[pltpu.VMEM((2, rb, kw), U32), pltpu.VMEM((rb, kw), U32),
                            pltpu.VMEM((rb, kw), BF16), pltpu.VMEM((rb, kw), BF16),
                            pltpu.VMEM((rb, ff), F32), pltpu.VMEM((rb, ff), F32), pltpu.VMEM((rb, ff), BF16),
                            pltpu.VMEM((d // 2, ff), BF16), pltpu.VMEM((d // 2, ff), BF16),
                            pltpu.VMEM((ff, oc), BF16), pltpu.VMEM((ff, oc), BF16),
                            pltpu.VMEM((n_in, d // 4, ff), F32), pltpu.VMEM((n_dn, ff, oc), F32),
                            pltpu.SemaphoreType.DMA((2,)), pltpu.SemaphoreType.DMA(()),
                            pltpu.SemaphoreType.DMA((n_in,)), pltpu.SemaphoreType.DMA((n_dn,))],
        ),
        out_shape=jax.ShapeDtypeStruct((MOE_TOPK, n_tok, kw), U32),
        compiler_params=_params(("arbitrary", "arbitrary")),
        name="moe_experts",
    )(blk_exp, blk_next, blk_rows, blk_off, order, nact, xp, w_gate, w_up, w_down)


def _combine_body(h_ref, meta_ref, g_ref, y0_ref, y1_ref, o_ref):
    meta = meta_ref[...]
    acc_lo = None
    acc_hi = None
    for k, y_ref in enumerate((y0_ref, y1_ref)):
        gate = meta[:, 2 * MOE_TOPK + k:2 * MOE_TOPK + k + 1]
        lo, hi = _unpack_halves(y_ref[0])
        lo, hi = lo * gate, hi * gate
        acc_lo = lo if acc_lo is None else acc_lo + lo
        acc_hi = hi if acc_hi is None else acc_hi + hi
    kh = acc_lo.shape[1]
    h_lo = h_ref[:, :kh] + acc_lo
    h_hi = h_ref[:, kh:] + acc_hi
    ms = (jnp.sum(h_lo * h_lo, axis=-1, keepdims=True) + jnp.sum(h_hi * h_hi, axis=-1, keepdims=True)) / (2 * kh)
    scale = lax.rsqrt(ms + NORM_EPS)
    o_ref[:, :kh] = (h_lo * scale) * g_ref[:, :kh]
    o_ref[:, kh:] = (h_hi * scale) * g_ref[:, kh:]


def moe_combine(h, meta, g, ya, *, tb):
    n, d = h.shape
    kw = ya.shape[2]
    tb = _tile(n, tb)

    def slot(k):
        return pl.BlockSpec((1, tb, kw), lambda i, k=k: (k, i, 0))

    return pl.pallas_call(
        _combine_body,
        grid=(n // tb,),
        in_specs=[
            pl.BlockSpec((tb, d), lambda i: (i, 0)),
            pl.BlockSpec((tb, V7X_LANES), lambda i: (i, 0)),
            pl.BlockSpec((1, d), lambda i: (0, 0)),
            slot(0), slot(1),
        ],
        out_specs=pl.BlockSpec((tb, d), lambda i: (i, 0)),
        out_shape=jax.ShapeDtypeStruct((n, d), F32),
        compiler_params=_params(("parallel",)),
        name="moe_combine",
    )(h, meta, g.reshape(1, d), ya, ya)


def _moe_plan(counts, *, rb, n_blocks):
    n_exp = counts.shape[0]
    per_exp = (counts + rb - 1) // rb
    blk_end = jnp.cumsum(per_exp)
    start = jnp.cumsum(counts) - counts
    nact = blk_end[-1].astype(I32)
    blk = jnp.arange(n_blocks, dtype=I32)
    blk_exp = jnp.minimum(jnp.searchsorted(blk_end, blk, side="right"), n_exp - 1).astype(I32)
    active = blk < nact
    last = jnp.maximum(nact - 1, 0)
    blk_exp = jnp.where(active, blk_exp, blk_exp[last]).astype(I32)
    blk_next = blk_exp[jnp.minimum(blk + 1, last)]
    j = blk - (blk_end[blk_exp] - per_exp[blk_exp])
    blk_off = jnp.where(active, start[blk_exp] + j * rb, 0).astype(I32)
    blk_rows = jnp.where(active, jnp.clip(counts[blk_exp] - j * rb, 0, rb), 0).astype(I32)
    return start.astype(I32), blk_exp, blk_next, blk_rows, blk_off, nact.reshape(1)


def kernel(x, mem, positions, norm_mix_g, w_in, ret_norm_g, gm_ln_g, gm_ln_b, gm_ws, gm_bs, w_out, norm_xa_g, norm_mem_g, xa_wq, xa_wkv, xa_wo, norm_moe_g, router_grp_w, router_grp_b, router_exp_w, router_exp_b, moe_w_gate, moe_w_up, moe_w_down, norm_final_g):
    batch, seq, d = x.shape
    mem_len = mem.shape[1]
    n = batch * seq
    ret_width = ret_norm_g.shape[0]
    gm_width = gm_ln_g.shape[0]
    assert ret_width == gm_width and w_in.shape[1] == 4 * ret_width + 2 * gm_width
    dk = ret_width // RET_HEADS
    n_exp = moe_w_gate.shape[0]
    assert n_exp == MOE_GROUPS * MOE_PER_GROUP and MOE_GROUPS + n_exp <= V7X_LANES

    inv_freq = ROPE_BASE ** (-jnp.arange(0, dk, 2, dtype=F32) / dk)
    ang = positions.astype(F32).reshape(n, 1) * inv_freq
    cos, sin = jnp.cos(ang), jnp.sin(ang)

    x2 = x.reshape(n, d)
    proj = norm_matmul(x2, norm_mix_g, w_in.astype(BF16), tm=512, tn=1024)
    ret = retention(proj, cos, sin, ret_norm_g, batch=batch, seq=seq, ret_width=ret_width)
    gm = gmlp(proj, gm_ln_g, gm_ln_b, gm_ws, gm_bs, n_rows=n, gm_width=gm_width,
              u_block=4 * ret_width // gm_width, v_block=4 * ret_width // gm_width + 1)
    h1 = matmul_residual([ret, gm], w_out.astype(BF16), x2, tm=512, tn=1024)

    q = norm_matmul(h1, norm_xa_g, xa_wq.astype(BF16), tm=512, tn=1024)
    kv = norm_matmul(mem.reshape(batch * mem_len, d), norm_mem_g, xa_wkv.astype(BF16), tm=512, tn=1024)
    o = cross_attention(q, kv, batch=batch, seq=seq, mem_len=mem_len, tq=512)
    h2 = matmul_residual([o], xa_wo.astype(BF16), h1, tm=512, tn=1024)

    pad = V7X_LANES - MOE_GROUPS - n_exp
    wr = jnp.concatenate([router_grp_w, router_exp_w, jnp.zeros((d, pad), F32)], axis=1)
    br = jnp.concatenate([router_grp_b, router_exp_b, jnp.zeros((pad,), F32)]).reshape(1, V7X_LANES)
    wr_hi = wr.astype(BF16)
    wr_lo = (wr - wr_hi.astype(F32)).astype(BF16)
    xp, meta, meta_t, cnt = moe_router(h2, norm_moe_g, jnp.concatenate([wr_hi, wr_lo], axis=1), br, tm=256)

    rb = MOE_ROW_BLOCK
    n_blocks = -(-(n * MOE_TOPK) // rb) + n_exp
    counts = cnt[0, MOE_GROUPS:MOE_GROUPS + n_exp].astype(I32)
    start, blk_exp, blk_next, blk_rows, blk_off, nact = _moe_plan(counts, rb=rb, n_blocks=n_blocks)
    code = (meta_t[MOE_TOPK:2 * MOE_TOPK].astype(I32) * (1 << MOE_CODE_SHIFT)
            + meta_t[0:MOE_TOPK].astype(I32)).reshape(-1)
    order = moe_order(code, start)
    ya = moe_experts(xp, moe_w_gate, moe_w_up, moe_w_down, blk_exp, blk_next, blk_rows, blk_off, order, nact,
                     n_blocks=n_blocks, rb=rb)
    y = moe_combine(h2, meta, norm_final_g, ya, tb=256)
    return y.reshape(batch, seq, d)
```

```python
import functools

import jax
import jax.numpy as jnp
from jax import lax
from jax.experimental import pallas as pl
from jax.experimental.pallas import tpu as pltpu

NORM_EPS = 1e-6
RET_HEADS = 8
ROPE_BASE = 10000.0
GM_GROUPS = 8
GM_CHUNK = 128
XA_HEADS = 4
MOE_GROUPS = 8
MOE_PER_GROUP = 8
MOE_TOPK = 2

V7X_LANES = 128
V7X_VMEM_BYTES = 64 * 1024 * 1024
VMEM_LIMIT_BYTES = 56 * 1024 * 1024

RET_BLOCK = 512
MOE_ROW_BLOCK = 512

F32 = jnp.float32
BF16 = jnp.bfloat16
U32 = jnp.uint32
I32 = jnp.int32


def _params(sem):
    return pltpu.CompilerParams(dimension_semantics=sem, vmem_limit_bytes=VMEM_LIMIT_BYTES)


def _tile(dim, target):
    t = min(dim, target)
    while dim % t:
        t -= V7X_LANES
    assert t > 0, (dim, target)
    return t


def _pack_halves(x_f32):
    k = x_f32.shape[-1] // 2
    bits = lax.bitcast_convert_type(x_f32.astype(BF16).astype(F32), U32)
    return (bits[:, k:] & jnp.uint32(0xFFFF0000)) | (bits[:, :k] >> 16)


def _unpack_halves(w_u32):
    lo = lax.bitcast_convert_type(w_u32 << 16, F32)
    hi = lax.bitcast_convert_type(w_u32 & jnp.uint32(0xFFFF0000), F32)
    return lo, hi


def _norm_matmul_body(x_ref, g_ref, w_ref, o_ref, xn_ref):
    @pl.when(pl.program_id(1) == 0)
    def _():
        x = x_ref[...]
        ms = jnp.mean(x * x, axis=-1, keepdims=True)
        xn_ref[...] = ((x * lax.rsqrt(ms + NORM_EPS)) * g_ref[...]).astype(BF16)

    o_ref[...] = jnp.dot(xn_ref[...], w_ref[...], preferred_element_type=F32).astype(o_ref.dtype)


def norm_matmul(x, g, w, *, tm, tn):
    m, k = x.shape
    n = w.shape[1]
    tm, tn = _tile(m, tm), _tile(n, tn)
    return pl.pallas_call(
        _norm_matmul_body,
        grid=(m // tm, n // tn),
        in_specs=[
            pl.BlockSpec((tm, k), lambda i, j: (i, 0)),
            pl.BlockSpec((1, k), lambda i, j: (0, 0)),
            pl.BlockSpec((k, tn), lambda i, j: (0, j)),
        ],
        out_specs=pl.BlockSpec((tm, tn), lambda i, j: (i, j)),
        out_shape=jax.ShapeDtypeStruct((m, n), BF16),
        scratch_shapes=[pltpu.VMEM((tm, k), BF16)],
        compiler_params=_params(("parallel", "arbitrary")),
        name="norm_matmul",
    )(x, g.reshape(1, k), w)


def _retention_body(lg_ref, q_ref, k_ref, v_ref, g_ref, cos_ref, sin_ref, gn_ref, o_ref, state_ref, *, blk, dk):
    h = pl.program_id(1)
    c = pl.program_id(2)
    lg = lg_ref[h]

    @pl.when(c == 0)
    def _():
        state_ref[...] = jnp.zeros_like(state_ref)

    half = dk // 2
    cos = cos_ref[...]
    sin = sin_ref[...]

    def rot(t):
        t1, t2 = t[:, :half], t[:, half:]
        return jnp.concatenate([t1 * cos - t2 * sin, t1 * sin + t2 * cos], axis=-1)

    qr = rot(q_ref[...].astype(F32))
    kr = rot(k_ref[...].astype(F32)) * (dk ** -0.5)
    v = v_ref[...]

    pos = lax.broadcasted_iota(I32, (blk, 1), 0).astype(F32)
    q_dec = jnp.exp((pos + 1.0) * lg)
    k_dec = jnp.exp((blk - 1.0 - pos) * lg)
    blk_dec = jnp.exp(jnp.full((1, dk), blk * lg, F32))

    s = lax.dot_general(qr.astype(BF16), kr.astype(BF16), (((1,), (1,)), ((), ())), preferred_element_type=F32)
    ri = lax.broadcasted_iota(I32, (blk, blk), 0)
    ci = lax.broadcasted_iota(I32, (blk, blk), 1)
    diff = (ri - ci).astype(F32)
    dec = jnp.where(diff >= 0.0, jnp.exp(jnp.maximum(diff, 0.0) * lg), 0.0)
    inner = jnp.dot((s * dec).astype(BF16), v, preferred_element_type=F32)

    state = state_ref[...]
    cross = jnp.dot((qr * q_dec).astype(BF16), state.astype(BF16), preferred_element_type=F32)
    kd_t = jnp.transpose(kr * k_dec).astype(BF16)
    state_ref[...] = state * blk_dec + jnp.dot(kd_t, v, preferred_element_type=F32)

    out = inner + cross
    mu = jnp.mean(out, axis=-1, keepdims=True)
    cen = out - mu
    var = jnp.mean(cen * cen, axis=-1, keepdims=True)
    y = cen * lax.rsqrt(var + NORM_EPS) * gn_ref[...]
    gate = g_ref[...].astype(F32)
    o_ref[...] = (y * (gate * jax.nn.sigmoid(gate))).astype(o_ref.dtype)


def retention(proj, cos, sin, ret_norm_g, *, batch, seq, ret_width):
    heads = RET_HEADS
    dk = ret_width // heads
    blk = min(RET_BLOCK, seq)
    nblk = seq // blk
    hb = ret_width // dk
    log_gamma = jnp.log(1.0 - jnp.exp2(-5.0 - jnp.arange(heads, dtype=F32)))

    def col(seg):
        return pl.BlockSpec((blk, dk), lambda b, h, c, lg, seg=seg: (b * nblk + c, seg * hb + h))

    rowspec = pl.BlockSpec((blk, dk // 2), lambda b, h, c, lg: (b * nblk + c, 0))
    return pl.pallas_call(
        functools.partial(_retention_body, blk=blk, dk=dk),
        grid_spec=pltpu.PrefetchScalarGridSpec(
            num_scalar_prefetch=1,
            grid=(batch, heads, nblk),
            in_specs=[col(0), col(1), col(2), col(3), rowspec, rowspec,
                      pl.BlockSpec((1, dk), lambda b, h, c, lg: (0, h))],
            out_specs=pl.BlockSpec((blk, dk), lambda b, h, c, lg: (b * nblk + c, h)),
            scratch_shapes=[pltpu.VMEM((dk, dk), F32)],
        ),
        out_shape=jax.ShapeDtypeStruct((batch * seq, ret_width), BF16),
        compiler_params=_params(("parallel", "parallel", "arbitrary")),
        name="retention",
    )(log_gamma, proj, proj, proj, proj, cos, sin, ret_norm_g.reshape(1, ret_width))


def _gmlp_body(u_ref, v_ref, lng_ref, lnb_ref, ws_ref, bst_ref, o_ref, vn_ref, *, rows, groups, cg, chunk):
    v = jax.nn.gelu(v_ref[...].astype(F32))
    mu = jnp.mean(v, axis=-1, keepdims=True)
    cen = v - mu
    var = jnp.mean(cen * cen, axis=-1, keepdims=True)
    vn_ref[...] = (cen * lax.rsqrt(var + NORM_EPS) * lng_ref[...] + lnb_ref[...]).astype(BF16)

    ri = lax.broadcasted_iota(I32, (chunk, chunk), 0)
    ci = lax.broadcasted_iota(I32, (chunk, chunk), 1)
    causal = ri >= ci
    for g in range(groups):
        w = jnp.where(causal, ws_ref[g], 0.0).astype(BF16)
        bias = bst_ref[:, g:g + 1]
        cols = slice(g * cg, (g + 1) * cg)
        for t in range(rows // chunk):
            rws = slice(t * chunk, (t + 1) * chunk)
            sp = jnp.dot(w, vn_ref[rws, cols], preferred_element_type=F32) + bias
            u = jax.nn.gelu(u_ref[rws, cols].astype(F32))
            o_ref[rws, cols] = (u * sp).astype(o_ref.dtype)


def gmlp(proj, ln_g, ln_b, ws, bs, *, n_rows, gm_width, u_block, v_block):
    groups, chunk = GM_GROUPS, GM_CHUNK
    cg = gm_width // groups
    rows = 2 * chunk
    return pl.pallas_call(
        functools.partial(_gmlp_body, rows=rows, groups=groups, cg=cg, chunk=chunk),
        grid=(n_rows // rows,),
        in_specs=[
            pl.BlockSpec((rows, gm_width), lambda i: (i, u_block)),
            pl.BlockSpec((rows, gm_width), lambda i: (i, v_block)),
            pl.BlockSpec((1, gm_width), lambda i: (0, 0)),
            pl.BlockSpec((1, gm_width), lambda i: (0, 0)),
            pl.BlockSpec((groups, chunk, chunk), lambda i: (0, 0, 0)),
            pl.BlockSpec((chunk, groups), lambda i: (0, 0)),
        ],
        out_specs=pl.BlockSpec((rows, gm_width), lambda i: (i, 0)),
        out_shape=jax.ShapeDtypeStruct((n_rows, gm_width), BF16),
        scratch_shapes=[pltpu.VMEM((rows, gm_width), BF16)],
        compiler_params=_params(("parallel",)),
        name="gmlp",
    )(proj, proj, ln_g.reshape(1, gm_width), ln_b.reshape(1, gm_width), ws, bs.T)


def _matmul_residual_body(*refs, n_parts):
    a_refs, w_refs = refs[:n_parts], refs[n_parts:2 * n_parts]
    r_ref, o_ref = refs[2 * n_parts], refs[2 * n_parts + 1]
    acc = r_ref[...]
    for a_ref, w_ref in zip(a_refs, w_refs):
        acc = acc + jnp.dot(a_ref[...], w_ref[...], preferred_element_type=F32)
    o_ref[...] = acc


def matmul_residual(parts, w, res, *, tm, tn):
    m, n = res.shape
    tm, tn = _tile(m, tm), _tile(n, tn)
    kp = parts[0].shape[1]
    n_parts = len(parts)
    a_specs = [pl.BlockSpec((tm, kp), lambda i, j: (i, 0)) for _ in parts]
    w_specs = [pl.BlockSpec((kp, tn), lambda i, j, p=p: (p, j)) for p in range(n_parts)]
    return pl.pallas_call(
        functools.partial(_matmul_residual_body, n_parts=n_parts),
        grid=(m // tm, n // tn),
        in_specs=a_specs + w_specs + [pl.BlockSpec((tm, tn), lambda i, j: (i, j))],
        out_specs=pl.BlockSpec((tm, tn), lambda i, j: (i, j)),
        out_shape=jax.ShapeDtypeStruct((m, n), F32),
        compiler_params=_params(("parallel", "parallel")),
        name="matmul_residual",
    )(*parts, *([w] * n_parts), res)


def _cross_attention_body(q_ref, kv_ref, o_ref, *, heads, dh):
    width = heads * dh
    for h in range(heads):
        q = q_ref[:, h * dh:(h + 1) * dh]
        k = kv_ref[:, h * dh:(h + 1) * dh]
        v = kv_ref[:, width + h * dh:width + (h + 1) * dh]
        s = lax.dot_general(q, k, (((1,), (1,)), ((), ())), preferred_element_type=F32) * (dh ** -0.5)
        e = jnp.exp(s - jnp.max(s, axis=-1, keepdims=True))
        p = e / jnp.sum(e, axis=-1, keepdims=True)
        o_ref[:, h * dh:(h + 1) * dh] = jnp.dot(p.astype(BF16), v, preferred_element_type=F32).astype(o_ref.dtype)


def cross_attention(q, kv, *, batch, seq, mem_len, tq):
    heads = XA_HEADS
    width = q.shape[1]
    dh = width // heads
    tq = min(tq, seq)
    nq = seq // tq
    return pl.pallas_call(
        functools.partial(_cross_attention_body, heads=heads, dh=dh),
        grid=(batch, nq),
        in_specs=[
            pl.BlockSpec((tq, width), lambda b, i: (b * nq + i, 0)),
            pl.BlockSpec((mem_len, 2 * width), lambda b, i: (b, 0)),
        ],
        out_specs=pl.BlockSpec((tq, width), lambda b, i: (b * nq + i, 0)),
        out_shape=jax.ShapeDtypeStruct((batch * seq, width), BF16),
        compiler_params=_params(("parallel", "parallel")),
        name="cross_attention",
    )(q, kv)


def _router_body(h_ref, g_ref, wr_ref, br_ref, xp_ref, meta_ref, meta_t_ref, cnt_ref, base_ref, *, tm, n_grp, per):
    @pl.when(pl.program_id(0) == 0)
    def _():
        base_ref[...] = jnp.zeros_like(base_ref)

    x = h_ref[...]
    ms = jnp.mean(x * x, axis=-1, keepdims=True)
    xn = (x * lax.rsqrt(ms + NORM_EPS)) * g_ref[...]
    xp_ref[...] = _pack_halves(xn)

    x_hi = xn.astype(BF16)
    x_lo = (xn - x_hi.astype(F32)).astype(BF16)
    both = jnp.dot(x_hi, wr_ref[...], preferred_element_type=F32)
    corr = jnp.dot(x_lo, wr_ref[:, :V7X_LANES], preferred_element_type=F32)
    logits = both[:, :V7X_LANES] + (both[:, V7X_LANES:] + corr) + br_ref[...]
    lane = lax.broadcasted_iota(I32, logits.shape, 1)
    neg = jnp.float32(-1e30)
    big = jnp.int32(V7X_LANES)

    gl = jnp.where(lane < n_grp, logits, neg)
    gmax = jnp.max(gl, axis=-1, keepdims=True)
    gidx = jnp.min(jnp.where(gl == gmax, lane, big), axis=-1, keepdims=True)
    grp_gate = 1.0 / jnp.sum(jnp.exp(gl - gmax), axis=-1, keepdims=True)

    lo = n_grp + gidx * per
    el = jnp.where((lane >= lo) & (lane < lo + per), logits, neg)
    v1 = jnp.max(el, axis=-1, keepdims=True)
    i1 = jnp.min(jnp.where(el == v1, lane, big), axis=-1, keepdims=True)
    el2 = jnp.where(lane == i1, neg, el)
    v2 = jnp.max(el2, axis=-1, keepdims=True)
    i2 = jnp.min(jnp.where(el2 == v2, lane, big), axis=-1, keepdims=True)
    t = jnp.exp(v2 - v1)
    den = 1.0 + t
    g1 = grp_gate / den
    g2 = grp_gate * (t / den)

    oh1 = jnp.where(lane == i1, 1.0, 0.0)
    oh2 = jnp.where(lane == i2, 1.0, 0.0)
    ri = lax.broadcasted_iota(I32, (tm, tm), 0)
    ci = lax.broadcasted_iota(I32, (tm, tm), 1)
    lower = jnp.where(ri > ci, 1.0, 0.0).astype(BF16)
    pre1 = jnp.dot(lower, oh1.astype(BF16), preferred_element_type=F32)
    pre2 = jnp.dot(lower, oh2.astype(BF16), preferred_element_type=F32)
    cnt1 = jnp.sum(oh1, axis=0, keepdims=True)
    cnt2 = jnp.sum(oh2, axis=0, keepdims=True)
    base = base_ref[...]
    rank1 = jnp.sum(oh1 * (pre1 + base), axis=-1, keepdims=True)
    rank2 = jnp.sum(oh2 * (pre2 + base + cnt1), axis=-1, keepdims=True)
    total = base + cnt1 + cnt2
    base_ref[...] = total
    cnt_ref[...] = total

    e1 = (i1 - n_grp).astype(F32)
    e2 = (i2 - n_grp).astype(F32)
    meta = jnp.zeros(logits.shape, F32)
    for idx, val in enumerate((e1, e2, rank1, rank2, g1, g2)):
        meta = jnp.where(lane == idx, val, meta)
    meta_ref[...] = meta
    meta_t_ref[...] = jnp.transpose(meta)[:META_ROWS, :]


META_ROWS = 8


def moe_router(h, g, wr, br, *, tm):
    n, d = h.shape
    tm = min(tm, n)
    return pl.pallas_call(
        functools.partial(_router_body, tm=tm, n_grp=MOE_GROUPS, per=MOE_PER_GROUP),
        grid=(n // tm,),
        in_specs=[
            pl.BlockSpec((tm, d), lambda i: (i, 0)),
            pl.BlockSpec((1, d), lambda i: (0, 0)),
            pl.BlockSpec((d, 2 * V7X_LANES), lambda i: (0, 0)),
            pl.BlockSpec((1, V7X_LANES), lambda i: (0, 0)),
        ],
        out_specs=[
            pl.BlockSpec((tm, d // 2), lambda i: (i, 0)),
            pl.BlockSpec((tm, V7X_LANES), lambda i: (i, 0)),
            pl.BlockSpec((META_ROWS, tm), lambda i: (0, i)),
            pl.BlockSpec((1, V7X_LANES), lambda i: (0, 0)),
        ],
        out_shape=[
            jax.ShapeDtypeStruct((n, d // 2), U32),
            jax.ShapeDtypeStruct((n, V7X_LANES), F32),
            jax.ShapeDtypeStruct((META_ROWS, n), F32),
            jax.ShapeDtypeStruct((1, V7X_LANES), F32),
        ],
        scratch_shapes=[pltpu.VMEM((1, V7X_LANES), F32)],
        compiler_params=_params(("arbitrary",)),
        name="moe_router",
    )(h, g.reshape(1, d), wr, br)


MOE_CODE_SHIFT = 6
assert MOE_GROUPS * MOE_PER_GROUP == 1 << MOE_CODE_SHIFT


def _order_body(code_ref, start_ref, order_ref):
    def body(a, carry):
        code = code_ref[a]
        order_ref[start_ref[code & ((1 << MOE_CODE_SHIFT) - 1)] + (code >> MOE_CODE_SHIFT)] = a
        return carry

    lax.fori_loop(0, code_ref.shape[0], body, 0, unroll=8)


def moe_order(code, start):
    return pl.pallas_call(
        _order_body,
        grid_spec=pltpu.PrefetchScalarGridSpec(
            num_scalar_prefetch=2,
            grid=(1,),
            in_specs=[],
            out_specs=pl.BlockSpec(memory_space=pltpu.SMEM),
        ),
        out_shape=jax.ShapeDtypeStruct(code.shape, I32),
        compiler_params=_params(("arbitrary",)),
        name="moe_order",
    )(code, start)


EXPERT_STEPS = 4
ROW_DMA_UNROLL = 8
assert MOE_TOPK == 2


def _experts_body(exp_ref, nxt_ref, rows_ref, off_ref, order_ref, nact_ref,
                  xp_hbm, wg_hbm, wu_hbm, wdn_hbm, ya_hbm,
                  xbuf_ref, ybuf_ref, xl_ref, xh_ref, hg_ref, hu_ref, hd_ref, wa_ref, wb_ref, wc_ref, wd_ref,
                  ring_in, ring_dn, gsem, ssem, sem_in, sem_dn, *, rb, n_tok):
    nact = nact_ref[0]
    n_blocks = rows_ref.shape[0]
    half = rb // 2
    quarter = rb // EXPERT_STEPS
    oc = wc_ref.shape[1]
    dq = ring_in.shape[1]

    def in_copy(e, i):
        src = (wg_hbm, wg_hbm, wu_hbm, wu_hbm)[i % 4]
        q = 2 * (i // 4) + i % 2
        return pltpu.make_async_copy(src.at[e, pl.ds(q * dq, dq), :], ring_in.at[i], sem_in.at[i])

    def dn_copy(e, i):
        col = (i % 2) * 2 * oc + (i // 2) * oc
        return pltpu.make_async_copy(wdn_hbm.at[e, :, pl.ds(col, oc)], ring_dn.at[i], sem_dn.at[i])

    def decode(base, r):
        a = order_ref[base + r]
        k = jnp.where(a >= n_tok, 1, 0)
        return k, a - k * n_tok

    def gather_copy(base, buf, r):
        _, tok = decode(base, r)
        return pltpu.make_async_copy(xp_hbm.at[pl.ds(tok, 1)], xbuf_ref.at[buf, pl.ds(r, 1)], gsem.at[buf])

    def scatter_copy(base, r):
        k, tok = decode(base, r)
        return pltpu.make_async_copy(ybuf_ref.at[pl.ds(r, 1)], ya_hbm.at[k, pl.ds(tok, 1)], ssem)

    def for_rows(lo, hi, fn):
        groups = lax.shift_right_logical(jnp.maximum(hi - lo, 0), ROW_DMA_UNROLL.bit_length() - 1)

        def group(g, carry):
            for u in range(ROW_DMA_UNROLL):
                fn(lo + g * ROW_DMA_UNROLL + u)
            return carry

        def single(r, carry):
            fn(r)
            return carry

        lax.fori_loop(0, groups, group, 0)
        lax.fori_loop(lo + groups * ROW_DMA_UNROLL, hi, single, 0)

    for i in range(ring_in.shape[0]):
        in_copy(exp_ref[0], i).start()
    for i in range(ring_dn.shape[0]):
        dn_copy(exp_ref[0], i).start()
    xbuf_ref[...] = jnp.zeros_like(xbuf_ref)
    base0 = off_ref[0]
    for_rows(0, rows_ref[0], lambda r: gather_copy(base0, 0, r).start())

    def block(v, carry):
        slot = lax.rem(v, 2)
        has_next = v + 1 < nact
        nxt = jnp.minimum(v + 1, n_blocks - 1)
        cur_base, cur_rows = off_ref[v], rows_ref[v]
        nxt_base, nxt_rows = off_ref[nxt], jnp.where(has_next, rows_ref[nxt], 0)

        def refill(copy, slots):
            @pl.when(has_next)
            def _():
                for i in slots:
                    copy(nxt_ref[v], i).start()

        def load_in(step):
            slots = range(4 * step, 4 * step + 4)
            for i in slots:
                in_copy(exp_ref[v], i).wait()
            for i in slots:
                dst = (wa_ref, wa_ref, wb_ref, wb_ref)[i % 4]
                dst[pl.ds((i % 2) * dq, dq), :] = ring_in[i].astype(BF16)
            refill(in_copy, slots)

        def load_dn(step):
            slots = range(2 * (step - 2), 2 * (step - 2) + 2)
            for i in slots:
                dn_copy(exp_ref[v], i).wait()
            for i in slots:
                (wc_ref, wd_ref)[i % 2][...] = ring_dn[i].astype(BF16)
            refill(dn_copy, slots)

        def gather_next(step):
            for_rows(step * quarter, jnp.minimum((step + 1) * quarter, nxt_rows),
                     lambda r: gather_copy(nxt_base, 1 - slot, r).start())

        def for_halves(fn):
            fn(pl.ds(0, half))
            pl.when(cur_rows > half)(functools.partial(fn, pl.ds(half, half)))

        def step0(rows):
            lo, hi = _unpack_halves(xbuf_ref[slot, rows, :])
            xl = lo.astype(BF16)
            xl_ref[rows, :] = xl
            xh_ref[rows, :] = hi.astype(BF16)
            hg_ref[rows, :] = jnp.dot(xl, wa_ref[...], preferred_element_type=F32)
            hu_ref[rows, :] = jnp.dot(xl, wb_ref[...], preferred_element_type=F32)

        def step1(rows):
            xh = xh_ref[rows, :]
            hg = hg_ref[rows, :] + jnp.dot(xh, wa_ref[...], preferred_element_type=F32)
            hu = hu_ref[rows, :] + jnp.dot(xh, wb_ref[...], preferred_element_type=F32)
            hd_ref[rows, :] = (hg * jax.nn.sigmoid(hg) * hu).astype(BF16)

        def step23(cols, rows):
            hd = hd_ref[rows, :]
            y_lo = jnp.dot(hd, wc_ref[...], preferred_element_type=F32)
            y_hi = jnp.dot(hd, wd_ref[...], preferred_element_type=F32)
            ybuf_ref[rows, cols] = _pack_halves(jnp.concatenate([y_lo, y_hi], axis=-1))

        for_rows(0, cur_rows, lambda r: gather_copy(cur_base, slot, r).wait())
        load_in(0)
        gather_next(0)
        for_halves(step0)
        load_in(1)
        gather_next(1)
        for_halves(step1)
        @pl.when(v >= 1)
        def _():
            prev = jnp.maximum(v - 1, 0)
            prev_base = off_ref[prev]
            for_rows(0, rows_ref[prev], lambda r: scatter_copy(prev_base, r).wait())
        load_dn(2)
        gather_next(2)
        for_halves(functools.partial(step23, pl.ds(0, oc)))
        load_dn(3)
        gather_next(3)
        for_halves(functools.partial(step23, pl.ds(oc, oc)))
        for_rows(0, cur_rows, lambda r: scatter_copy(cur_base, r).start())
        return carry

    lax.fori_loop(0, nact, block, 0)

    last = jnp.maximum(nact - 1, 0)
    last_base = off_ref[last]
    for_rows(0, rows_ref[last], lambda r: scatter_copy(last_base, r).wait())


def moe_experts(xp, w_gate, w_up, w_down, blk_exp, blk_next, blk_rows, blk_off, order, nact, *, n_blocks, rb):
    n_exp, d, ff = w_gate.shape
    n_tok, kw = xp.shape
    assert d == 2 * kw
    oc = kw // 2
    n_in, n_dn = 8, 4
    any_spec = pl.BlockSpec(memory_space=pl.ANY)
    return pl.pallas_call(
        functools.partial(_experts_body, rb=rb, n_tok=n_tok),
        grid_spec=pltpu.PrefetchScalarGridSpec(
            num_scalar_prefetch=6,
            grid=(1,),
            in_specs=[any_spec, any_spec, any_spec, any_spec],
            out_specs=any_spec,
            scratch_shapes=[pltpu.VMEM((2, rb, kw), U32), pltpu.VMEM((rb, kw), U32),
                            pltpu.VMEM((rb, kw), BF16), pltpu.VMEM((rb, kw), BF16),
                            pltpu.VMEM((rb, ff), F32), pltpu.VMEM((rb, ff), F32), pltpu.VMEM((rb, ff), BF16),
                            pltpu.VMEM((d // 2, ff), BF16), pltpu.VMEM((d // 2, ff), BF16),
                            pltpu.VMEM((ff, oc), BF16), pltpu.VMEM((ff, oc), BF16),
                            pltpu.VMEM((n_in, d // 4, ff), F32), pltpu.VMEM((n_dn, ff, oc), F32),
                            pltpu.SemaphoreType.DMA((2,)), pltpu.SemaphoreType.DMA(()),
                            pltpu.SemaphoreType.DMA((n_in,)), pltpu.SemaphoreType.DMA((n_dn,))],
        ),
        out_shape=jax.ShapeDtypeStruct((MOE_TOPK, n_tok, kw), U32),
        compiler_params=_params(("arbitrary",)),
        name="moe_experts",
    )(blk_exp, blk_next, blk_rows, blk_off, order, nact, xp, w_gate, w_up, w_down)


def _combine_body(h_ref, meta_ref, g_ref, y0_ref, y1_ref, o_ref):
    meta = meta_ref[...]
    acc_lo = None
    acc_hi = None
    for k, y_ref in enumerate((y0_ref, y1_ref)):
        gate = meta[:, 2 * MOE_TOPK + k:2 * MOE_TOPK + k + 1]
        lo, hi = _unpack_halves(y_ref[0])
        lo, hi = lo * gate, hi * gate
        acc_lo = lo if acc_lo is None else acc_lo + lo
        acc_hi = hi if acc_hi is None else acc_hi + hi
    kh = acc_lo.shape[1]
    h_lo = h_ref[:, :kh] + acc_lo
    h_hi = h_ref[:, kh:] + acc_hi
    ms = (jnp.sum(h_lo * h_lo, axis=-1, keepdims=True) + jnp.sum(h_hi * h_hi, axis=-1, keepdims=True)) / (2 * kh)
    scale = lax.rsqrt(ms + NORM_EPS)
    o_ref[:, :kh] = (h_lo * scale) * g_ref[:, :kh]
    o_ref[:, kh:] = (h_hi * scale) * g_ref[:, kh:]


def moe_combine(h, meta, g, ya, *, tb):
    n, d = h.shape
    kw = ya.shape[2]
    tb = _tile(n, tb)

    def slot(k):
        return pl.BlockSpec((1, tb, kw), lambda i, k=k: (k, i, 0))

    return pl.pallas_call(
        _combine_body,
        grid=(n // tb,),
        in_specs=[
            pl.BlockSpec((tb, d), lambda i: (i, 0)),
            pl.BlockSpec((tb, V7X_LANES), lambda i: (i, 0)),
            pl.BlockSpec((1, d), lambda i: (0, 0)),
            slot(0), slot(1),
        ],
        out_specs=pl.BlockSpec((tb, d), lambda i: (i, 0)),
        out_shape=jax.ShapeDtypeStruct((n, d), F32),
        compiler_params=_params(("parallel",)),
        name="moe_combine",
    )(h, meta, g.reshape(1, d), ya, ya)


def _moe_plan(counts, *, rb, n_blocks):
    n_exp = counts.shape[0]
    per_exp = (counts + rb - 1) // rb
    blk_end = jnp.cumsum(per_exp)
    start = jnp.cumsum(counts) - counts
    nact = blk_end[-1].astype(I32)
    blk = jnp.arange(n_blocks, dtype=I32)
    blk_exp = jnp.minimum(jnp.searchsorted(blk_end, blk, side="right"), n_exp - 1).astype(I32)
    active = blk < nact
    last = jnp.maximum(nact - 1, 0)
    blk_exp = jnp.where(active, blk_exp, blk_exp[last]).astype(I32)
    blk_next = blk_exp[jnp.minimum(blk + 1, last)]
    j = blk - (blk_end[blk_exp] - per_exp[blk_exp])
    blk_off = jnp.where(active, start[blk_exp] + j * rb, 0).astype(I32)
    blk_rows = jnp.where(active, jnp.clip(counts[blk_exp] - j * rb, 0, rb), 0).astype(I32)
    return start.astype(I32), blk_exp, blk_next, blk_rows, blk_off, nact.reshape(1)


def kernel(x, mem, positions, norm_mix_g, w_in, ret_norm_g, gm_ln_g, gm_ln_b, gm_ws, gm_bs, w_out, norm_xa_g, norm_mem_g, xa_wq, xa_wkv, xa_wo, norm_moe_g, router_grp_w, router_grp_b, router_exp_w, router_exp_b, moe_w_gate, moe_w_up, moe_w_down, norm_final_g):
    batch, seq, d = x.shape
    mem_len = mem.shape[1]
    n = batch * seq
    ret_width = ret_norm_g.shape[0]
    gm_width = gm_ln_g.shape[0]
    assert ret_width == gm_width and w_in.shape[1] == 4 * ret_width + 2 * gm_width
    dk = ret_width // RET_HEADS
    n_exp = moe_w_gate.shape[0]
    assert n_exp == MOE_GROUPS * MOE_PER_GROUP and MOE_GROUPS + n_exp <= V7X_LANES

    inv_freq = ROPE_BASE ** (-jnp.arange(0, dk, 2, dtype=F32) / dk)
    ang = positions.astype(F32).reshape(n, 1) * inv_freq
    cos, sin = jnp.cos(ang), jnp.sin(ang)

    x2 = x.reshape(n, d)
    proj = norm_matmul(x2, norm_mix_g, w_in.astype(BF16), tm=512, tn=1024)
    ret = retention(proj, cos, sin, ret_norm_g, batch=batch, seq=seq, ret_width=ret_width)
    gm = gmlp(proj, gm_ln_g, gm_ln_b, gm_ws, gm_bs, n_rows=n, gm_width=gm_width,
              u_block=4 * ret_width // gm_width, v_block=4 * ret_width // gm_width + 1)
    h1 = matmul_residual([ret, gm], w_out.astype(BF16), x2, tm=512, tn=1024)

    q = norm_matmul(h1, norm_xa_g, xa_wq.astype(BF16), tm=512, tn=1024)
    kv = norm_matmul(mem.reshape(batch * mem_len, d), norm_mem_g, xa_wkv.astype(BF16), tm=512, tn=1024)
    o = cross_attention(q, kv, batch=batch, seq=seq, mem_len=mem_len, tq=512)
    h2 = matmul_residual([o], xa_wo.astype(BF16), h1, tm=512, tn=1024)

    pad = V7X_LANES - MOE_GROUPS - n_exp
    wr = jnp.concatenate([router_grp_w, router_exp_w, jnp.zeros((d, pad), F32)], axis=1)
    br = jnp.concatenate([router_grp_b, router_exp_b, jnp.zeros((pad,), F32)]).reshape(1, V7X_LANES)
    wr_hi = wr.astype(BF16)
    wr_lo = (wr - wr_hi.astype(F32)).astype(BF16)
    xp, meta, meta_t, cnt = moe_router(h2, norm_moe_g, jnp.concatenate([wr_hi, wr_lo], axis=1), br, tm=256)

    rb = MOE_ROW_BLOCK
    n_blocks = -(-(n * MOE_TOPK) // rb) + n_exp
    counts = cnt[0, MOE_GROUPS:MOE_GROUPS + n_exp].astype(I32)
    start, blk_exp, blk_next, blk_rows, blk_off, nact = _moe_plan(counts, rb=rb, n_blocks=n_blocks)
    code = (meta_t[MOE_TOPK:2 * MOE_TOPK].astype(I32) * (1 << MOE_CODE_SHIFT)
            + meta_t[0:MOE_TOPK].astype(I32)).reshape(-1)
    order = moe_order(code, start)
    ya = moe_experts(xp, moe_w_gate, moe_w_up, moe_w_down, blk_exp, blk_next, blk_rows, blk_off, order, nact,
                     n_blocks=n_blocks, rb=rb)
    y = moe_combine(h2, meta, norm_final_g, ya, tb=256)
    return y.reshape(batch, seq, d)
```

```python
import functools

import jax
import jax.numpy as jnp
from jax import lax
from jax.experimental import pallas as pl
from jax.experimental.pallas import tpu as pltpu

NORM_EPS = 1e-6
RET_HEADS = 8
ROPE_BASE = 10000.0
GM_GROUPS = 8
GM_CHUNK = 128
XA_HEADS = 4
MOE_GROUPS = 8
MOE_PER_GROUP = 8
MOE_TOPK = 2

V7X_LANES = 128
V7X_VMEM_BYTES = 64 * 1024 * 1024
VMEM_LIMIT_BYTES = 56 * 1024 * 1024

RET_BLOCK = 512
MOE_ROW_BLOCK = 512

F32 = jnp.float32
BF16 = jnp.bfloat16
U32 = jnp.uint32
I32 = jnp.int32


def _params(sem):
    return pltpu.CompilerParams(dimension_semantics=sem, vmem_limit_bytes=VMEM_LIMIT_BYTES)


def _tile(dim, target):
    t = min(dim, target)
    while dim % t:
        t -= V7X_LANES
    assert t > 0, (dim, target)
    return t


def _pack_halves(x_f32):
    k = x_f32.shape[-1] // 2
    bits = lax.bitcast_convert_type(x_f32.astype(BF16).astype(F32), U32)
    return (bits[:, k:] & jnp.uint32(0xFFFF0000)) | (bits[:, :k] >> 16)


def _unpack_halves(w_u32):
    lo = lax.bitcast_convert_type(w_u32 << 16, F32)
    hi = lax.bitcast_convert_type(w_u32 & jnp.uint32(0xFFFF0000), F32)
    return lo, hi


def _norm_matmul_body(x_ref, g_ref, w_ref, o_ref, xn_ref):
    @pl.when(pl.program_id(1) == 0)
    def _():
        x = x_ref[...]
        ms = jnp.mean(x * x, axis=-1, keepdims=True)
        xn_ref[...] = ((x * lax.rsqrt(ms + NORM_EPS)) * g_ref[...]).astype(BF16)

    o_ref[...] = jnp.dot(xn_ref[...], w_ref[...], preferred_element_type=F32).astype(o_ref.dtype)


def norm_matmul(x, g, w, *, tm, tn):
    m, k = x.shape
    n = w.shape[1]
    tm, tn = _tile(m, tm), _tile(n, tn)
    return pl.pallas_call(
        _norm_matmul_body,
        grid=(m // tm, n // tn),
        in_specs=[
            pl.BlockSpec((tm, k), lambda i, j: (i, 0)),
            pl.BlockSpec((1, k), lambda i, j: (0, 0)),
            pl.BlockSpec((k, tn), lambda i, j: (0, j)),
        ],
        out_specs=pl.BlockSpec((tm, tn), lambda i, j: (i, j)),
        out_shape=jax.ShapeDtypeStruct((m, n), BF16),
        scratch_shapes=[pltpu.VMEM((tm, k), BF16)],
        compiler_params=_params(("parallel", "arbitrary")),
        name="norm_matmul",
    )(x, g.reshape(1, k), w)


def _retention_body(lg_ref, q_ref, k_ref, v_ref, g_ref, cos_ref, sin_ref, gn_ref, o_ref, state_ref, *, blk, dk):
    h = pl.program_id(1)
    c = pl.program_id(2)
    lg = lg_ref[h]

    @pl.when(c == 0)
    def _():
        state_ref[...] = jnp.zeros_like(state_ref)

    half = dk // 2
    cos = cos_ref[...]
    sin = sin_ref[...]

    def rot(t):
        t1, t2 = t[:, :half], t[:, half:]
        return jnp.concatenate([t1 * cos - t2 * sin, t1 * sin + t2 * cos], axis=-1)

    qr = rot(q_ref[...].astype(F32))
    kr = rot(k_ref[...].astype(F32)) * (dk ** -0.5)
    v = v_ref[...]

    pos = lax.broadcasted_iota(I32, (blk, 1), 0).astype(F32)
    q_dec = jnp.exp((pos + 1.0) * lg)
    k_dec = jnp.exp((blk - 1.0 - pos) * lg)
    blk_dec = jnp.exp(jnp.full((1, dk), blk * lg, F32))

    s = lax.dot_general(qr.astype(BF16), kr.astype(BF16), (((1,), (1,)), ((), ())), preferred_element_type=F32)
    ri = lax.broadcasted_iota(I32, (blk, blk), 0)
    ci = lax.broadcasted_iota(I32, (blk, blk), 1)
    diff = (ri - ci).astype(F32)
    dec = jnp.where(diff >= 0.0, jnp.exp(jnp.maximum(diff, 0.0) * lg), 0.0)
    inner = jnp.dot((s * dec).astype(BF16), v, preferred_element_type=F32)

    state = state_ref[...]
    cross = jnp.dot((qr * q_dec).astype(BF16), state.astype(BF16), preferred_element_type=F32)
    kd_t = jnp.transpose(kr * k_dec).astype(BF16)
    state_ref[...] = state * blk_dec + jnp.dot(kd_t, v, preferred_element_type=F32)

    out = inner + cross
    mu = jnp.mean(out, axis=-1, keepdims=True)
    cen = out - mu
    var = jnp.mean(cen * cen, axis=-1, keepdims=True)
    y = cen * lax.rsqrt(var + NORM_EPS) * gn_ref[...]
    gate = g_ref[...].astype(F32)
    o_ref[...] = (y * (gate * jax.nn.sigmoid(gate))).astype(o_ref.dtype)


def retention(proj, cos, sin, ret_norm_g, *, batch, seq, ret_width):
    heads = RET_HEADS
    dk = ret_width // heads
    blk = min(RET_BLOCK, seq)
    nblk = seq // blk
    hb = ret_width // dk
    log_gamma = jnp.log(1.0 - jnp.exp2(-5.0 - jnp.arange(heads, dtype=F32)))

    def col(seg):
        return pl.BlockSpec((blk, dk), lambda b, h, c, lg, seg=seg: (b * nblk + c, seg * hb + h))

    rowspec = pl.BlockSpec((blk, dk // 2), lambda b, h, c, lg: (b * nblk + c, 0))
    return pl.pallas_call(
        functools.partial(_retention_body, blk=blk, dk=dk),
        grid_spec=pltpu.PrefetchScalarGridSpec(
            num_scalar_prefetch=1,
            grid=(batch, heads, nblk),
            in_specs=[col(0), col(1), col(2), col(3), rowspec, rowspec,
                      pl.BlockSpec((1, dk), lambda b, h, c, lg: (0, h))],
            out_specs=pl.BlockSpec((blk, dk), lambda b, h, c, lg: (b * nblk + c, h)),
            scratch_shapes=[pltpu.VMEM((dk, dk), F32)],
        ),
        out_shape=jax.ShapeDtypeStruct((batch * seq, ret_width), BF16),
        compiler_params=_params(("parallel", "parallel", "arbitrary")),
        name="retention",
    )(log_gamma, proj, proj, proj, proj, cos, sin, ret_norm_g.reshape(1, ret_width))


def _gmlp_body(u_ref, v_ref, lng_ref, lnb_ref, ws_ref, bst_ref, o_ref, vn_ref, *, rows, groups, cg, chunk):
    v = jax.nn.gelu(v_ref[...].astype(F32))
    mu = jnp.mean(v, axis=-1, keepdims=True)
    cen = v - mu
    var = jnp.mean(cen * cen, axis=-1, keepdims=True)
    vn_ref[...] = (cen * lax.rsqrt(var + NORM_EPS) * lng_ref[...] + lnb_ref[...]).astype(BF16)

    ri = lax.broadcasted_iota(I32, (chunk, chunk), 0)
    ci = lax.broadcasted_iota(I32, (chunk, chunk), 1)
    causal = ri >= ci
    for g in range(groups):
        w = jnp.where(causal, ws_ref[g], 0.0).astype(BF16)
        bias = bst_ref[:, g:g + 1]
        cols = slice(g * cg, (g + 1) * cg)
        for t in range(rows // chunk):
            rws = slice(t * chunk, (t + 1) * chunk)
            sp = jnp.dot(w, vn_ref[rws, cols], preferred_element_type=F32) + bias
            u = jax.nn.gelu(u_ref[rws, cols].astype(F32))
            o_ref[rws, cols] = (u * sp).astype(o_ref.dtype)


def gmlp(proj, ln_g, ln_b, ws, bs, *, n_rows, gm_width, u_block, v_block):
    groups, chunk = GM_GROUPS, GM_CHUNK
    cg = gm_width // groups
    rows = 2 * chunk
    return pl.pallas_call(
        functools.partial(_gmlp_body, rows=rows, groups=groups, cg=cg, chunk=chunk),
        grid=(n_rows // rows,),
        in_specs=[
            pl.BlockSpec((rows, gm_width), lambda i: (i, u_block)),
            pl.BlockSpec((rows, gm_width), lambda i: (i, v_block)),
            pl.BlockSpec((1, gm_width), lambda i: (0, 0)),
            pl.BlockSpec((1, gm_width), lambda i: (0, 0)),
            pl.BlockSpec((groups, chunk, chunk), lambda i: (0, 0, 0)),
            pl.BlockSpec((chunk, groups), lambda i: (0, 0)),
        ],
        out_specs=pl.BlockSpec((rows, gm_width), lambda i: (i, 0)),
        out_shape=jax.ShapeDtypeStruct((n_rows, gm_width), BF16),
        scratch_shapes=[pltpu.VMEM((rows, gm_width), BF16)],
        compiler_params=_params(("parallel",)),
        name="gmlp",
    )(proj, proj, ln_g.reshape(1, gm_width), ln_b.reshape(1, gm_width), ws, bs.T)


def _matmul_residual_body(*refs, n_parts):
    a_refs, w_refs = refs[:n_parts], refs[n_parts:2 * n_parts]
    r_ref, o_ref = refs[2 * n_parts], refs[2 * n_parts + 1]
    acc = r_ref[...]
    for a_ref, w_ref in zip(a_refs, w_refs):
        acc = acc + jnp.dot(a_ref[...], w_ref[...], preferred_element_type=F32)
    o_ref[...] = acc


def matmul_residual(parts, w, res, *, tm, tn):
    m, n = res.shape
    tm, tn = _tile(m, tm), _tile(n, tn)
    kp = parts[0].shape[1]
    n_parts = len(parts)
    a_specs = [pl.BlockSpec((tm, kp), lambda i, j: (i, 0)) for _ in parts]
    w_specs = [pl.BlockSpec((kp, tn), lambda i, j, p=p: (p, j)) for p in range(n_parts)]
    return pl.pallas_call(
        functools.partial(_matmul_residual_body, n_parts=n_parts),
        grid=(m // tm, n // tn),
        in_specs=a_specs + w_specs + [pl.BlockSpec((tm, tn), lambda i, j: (i, j))],
        out_specs=pl.BlockSpec((tm, tn), lambda i, j: (i, j)),
        out_shape=jax.ShapeDtypeStruct((m, n), F32),
        compiler_params=_params(("parallel", "parallel")),
        name="matmul_residual",
    )(*parts, *([w] * n_parts), res)


def _cross_attention_body(h_ref, g_ref, wq_ref, kv_ref, wo_ref, o_ref, *, heads, dh):
    width = heads * dh
    x = h_ref[...]
    ms = jnp.mean(x * x, axis=-1, keepdims=True)
    xn = ((x * lax.rsqrt(ms + NORM_EPS)) * g_ref[...]).astype(BF16)
    q = jnp.dot(xn, wq_ref[...], preferred_element_type=F32).astype(BF16)
    outs = []
    for h in range(heads):
        k = kv_ref[:, h * dh:(h + 1) * dh]
        v = kv_ref[:, width + h * dh:width + (h + 1) * dh]
        s = lax.dot_general(q[:, h * dh:(h + 1) * dh], k, (((1,), (1,)), ((), ())),
                            preferred_element_type=F32) * (dh ** -0.5)
        e = jnp.exp(s - jnp.max(s, axis=-1, keepdims=True))
        p = e / jnp.sum(e, axis=-1, keepdims=True)
        outs.append(jnp.dot(p.astype(BF16), v, preferred_element_type=F32).astype(BF16))
    o = jnp.concatenate(outs, axis=-1)
    o_ref[...] = x + jnp.dot(o, wo_ref[...], preferred_element_type=F32)


def cross_attention_block(h, g, wq, kv, wo, *, seq, mem_len, tm):
    n, d = h.shape
    heads = XA_HEADS
    width = wq.shape[1]
    dh = width // heads
    tm = _tile(seq, tm)
    per_batch = seq // tm

    def resident(shape):
        return pl.BlockSpec(shape, lambda i: (0, 0), pipeline_mode=pl.Buffered(1))

    return pl.pallas_call(
        functools.partial(_cross_attention_body, heads=heads, dh=dh),
        grid=(n // tm,),
        in_specs=[
            pl.BlockSpec((tm, d), lambda i: (i, 0)),
            resident((1, d)),
            resident((d, width)),
            pl.BlockSpec((mem_len, 2 * width), lambda i: (i // per_batch, 0)),
            resident((width, d)),
        ],
        out_specs=pl.BlockSpec((tm, d), lambda i: (i, 0)),
        out_shape=jax.ShapeDtypeStruct((n, d), F32),
        compiler_params=_params(("parallel",)),
        name="cross_attention",
    )(h, g.reshape(1, d), wq, kv, wo)


def _router_body(h_ref, g_ref, wr_ref, br_ref, xp_ref, meta_ref, meta_t_ref, cnt_ref, base_ref, *, tm, n_grp, per):
    @pl.when(pl.program_id(0) == 0)
    def _():
        base_ref[...] = jnp.zeros_like(base_ref)

    x = h_ref[...]
    ms = jnp.mean(x * x, axis=-1, keepdims=True)
    xn = (x * lax.rsqrt(ms + NORM_EPS)) * g_ref[...]
    xp_ref[...] = _pack_halves(xn)

    x_hi = xn.astype(BF16)
    x_lo = (xn - x_hi.astype(F32)).astype(BF16)
    both = jnp.dot(x_hi, wr_ref[...], preferred_element_type=F32)
    corr = jnp.dot(x_lo, wr_ref[:, :V7X_LANES], preferred_element_type=F32)
    logits = both[:, :V7X_LANES] + (both[:, V7X_LANES:] + corr) + br_ref[...]
    lane = lax.broadcasted_iota(I32, logits.shape, 1)
    neg = jnp.float32(-1e30)
    big = jnp.int32(V7X_LANES)

    gl = jnp.where(lane < n_grp, logits, neg)
    gmax = jnp.max(gl, axis=-1, keepdims=True)
    gidx = jnp.min(jnp.where(gl == gmax, lane, big), axis=-1, keepdims=True)
    grp_gate = 1.0 / jnp.sum(jnp.exp(gl - gmax), axis=-1, keepdims=True)

    lo = n_grp + gidx * per
    el = jnp.where((lane >= lo) & (lane < lo + per), logits, neg)
    v1 = jnp.max(el, axis=-1, keepdims=True)
    i1 = jnp.min(jnp.where(el == v1, lane, big), axis=-1, keepdims=True)
    el2 = jnp.where(lane == i1, neg, el)
    v2 = jnp.max(el2, axis=-1, keepdims=True)
    i2 = jnp.min(jnp.where(el2 == v2, lane, big), axis=-1, keepdims=True)
    t = jnp.exp(v2 - v1)
    den = 1.0 + t
    g1 = grp_gate / den
    g2 = grp_gate * (t / den)

    oh1 = jnp.where(lane == i1, 1.0, 0.0)
    oh2 = jnp.where(lane == i2, 1.0, 0.0)
    ri = lax.broadcasted_iota(I32, (tm, tm), 0)
    ci = lax.broadcasted_iota(I32, (tm, tm), 1)
    lower = jnp.where(ri > ci, 1.0, 0.0).astype(BF16)
    pre1 = jnp.dot(lower, oh1.astype(BF16), preferred_element_type=F32)
    pre2 = jnp.dot(lower, oh2.astype(BF16), preferred_element_type=F32)
    cnt1 = jnp.sum(oh1, axis=0, keepdims=True)
    cnt2 = jnp.sum(oh2, axis=0, keepdims=True)
    base = base_ref[...]
    rank1 = jnp.sum(oh1 * (pre1 + base), axis=-1, keepdims=True)
    rank2 = jnp.sum(oh2 * (pre2 + base + cnt1), axis=-1, keepdims=True)
    total = base + cnt1 + cnt2
    base_ref[...] = total
    cnt_ref[...] = total

    e1 = (i1 - n_grp).astype(F32)
    e2 = (i2 - n_grp).astype(F32)
    meta = jnp.zeros(logits.shape, F32)
    for idx, val in enumerate((e1, e2, rank1, rank2, g1, g2)):
        meta = jnp.where(lane == idx, val, meta)
    meta_ref[...] = meta
    meta_t_ref[...] = jnp.transpose(meta)[:META_ROWS, :]


META_ROWS = 8


def moe_router(h, g, wr, br, *, tm):
    n, d = h.shape
    tm = min(tm, n)
    return pl.pallas_call(
        functools.partial(_router_body, tm=tm, n_grp=MOE_GROUPS, per=MOE_PER_GROUP),
        grid=(n // tm,),
        in_specs=[
            pl.BlockSpec((tm, d), lambda i: (i, 0)),
            pl.BlockSpec((1, d), lambda i: (0, 0)),
            pl.BlockSpec((d, 2 * V7X_LANES), lambda i: (0, 0)),
            pl.BlockSpec((1, V7X_LANES), lambda i: (0, 0)),
        ],
        out_specs=[
            pl.BlockSpec((tm, d // 2), lambda i: (i, 0)),
            pl.BlockSpec((tm, V7X_LANES), lambda i: (i, 0)),
            pl.BlockSpec((META_ROWS, tm), lambda i: (0, i)),
            pl.BlockSpec((1, V7X_LANES), lambda i: (0, 0)),
        ],
        out_shape=[
            jax.ShapeDtypeStruct((n, d // 2), U32),
            jax.ShapeDtypeStruct((n, V7X_LANES), F32),
            jax.ShapeDtypeStruct((META_ROWS, n), F32),
            jax.ShapeDtypeStruct((1, V7X_LANES), F32),
        ],
        scratch_shapes=[pltpu.VMEM((1, V7X_LANES), F32)],
        compiler_params=_params(("arbitrary",)),
        name="moe_router",
    )(h, g.reshape(1, d), wr, br)


MOE_CODE_SHIFT = 6
assert MOE_GROUPS * MOE_PER_GROUP == 1 << MOE_CODE_SHIFT


def _order_body(code_ref, start_ref, order_ref):
    def body(a, carry):
        code = code_ref[a]
        order_ref[start_ref[code & ((1 << MOE_CODE_SHIFT) - 1)] + (code >> MOE_CODE_SHIFT)] = a
        return carry

    lax.fori_loop(0, code_ref.shape[0], body, 0, unroll=8)


def moe_order(code, start):
    return pl.pallas_call(
        _order_body,
        grid_spec=pltpu.PrefetchScalarGridSpec(
            num_scalar_prefetch=2,
            grid=(1,),
            in_specs=[],
            out_specs=pl.BlockSpec(memory_space=pltpu.SMEM),
        ),
        out_shape=jax.ShapeDtypeStruct(code.shape, I32),
        compiler_params=_params(("arbitrary",)),
        name="moe_order",
    )(code, start)


EXPERT_STEPS = 4
ROW_DMA_UNROLL = 8
assert MOE_TOPK == 2


def _experts_body(exp_ref, nxt_ref, rows_ref, off_ref, order_ref, nact_ref,
                  xp_hbm, wg_hbm, wu_hbm, wdn_hbm, ya_hbm,
                  xbuf_ref, ybuf_ref, xl_ref, xh_ref, hg_ref, hu_ref, hd_ref, wa_ref, wb_ref, wc_ref, wd_ref,
                  ring_in, ring_dn, gsem, ssem, sem_in, sem_dn, *, rb, n_tok, n_blocks):
    v = pl.program_id(0)
    s = pl.program_id(1)
    nact = nact_ref[0]
    active = v < nact
    slot = lax.rem(v, 2)
    half = rb // 2
    quarter = rb // EXPERT_STEPS
    oc = wc_ref.shape[1]
    dq = ring_in.shape[1]

    def in_copy(e, i):
        src = (wg_hbm, wg_hbm, wu_hbm, wu_hbm)[i % 4]
        q = 2 * (i // 4) + i % 2
        return pltpu.make_async_copy(src.at[e, pl.ds(q * dq, dq), :], ring_in.at[i], sem_in.at[i])

    def dn_copy(e, i):
        col = (i % 2) * 2 * oc + (i // 2) * oc
        return pltpu.make_async_copy(wdn_hbm.at[e, :, pl.ds(col, oc)], ring_dn.at[i], sem_dn.at[i])

    def refill(copy, slots):
        @pl.when(v + 1 < nact)
        def _():
            for i in slots:
                copy(nxt_ref[v], i).start()

    def decode(base, r):
        a = order_ref[base + r]
        k = jnp.where(a >= n_tok, 1, 0)
        return k, a - k * n_tok

    def gather_copy(base, buf, r):
        _, tok = decode(base, r)
        return pltpu.make_async_copy(xp_hbm.at[pl.ds(tok, 1)], xbuf_ref.at[buf, pl.ds(r, 1)], gsem.at[buf])

    def scatter_copy(base, r):
        k, tok = decode(base, r)
        return pltpu.make_async_copy(ybuf_ref.at[pl.ds(r, 1)], ya_hbm.at[k, pl.ds(tok, 1)], ssem)

    def for_rows(lo, hi, fn):
        groups = lax.shift_right_logical(jnp.maximum(hi - lo, 0), ROW_DMA_UNROLL.bit_length() - 1)

        def group(g, carry):
            for u in range(ROW_DMA_UNROLL):
                fn(lo + g * ROW_DMA_UNROLL + u)
            return carry

        def single(r, carry):
            fn(r)
            return carry

        lax.fori_loop(0, groups, group, 0)
        lax.fori_loop(lo + groups * ROW_DMA_UNROLL, hi, single, 0)

    @pl.when(jnp.logical_and(v == 0, s == 0))
    def _():
        for i in range(ring_in.shape[0]):
            in_copy(exp_ref[0], i).start()
        for i in range(ring_dn.shape[0]):
            dn_copy(exp_ref[0], i).start()
        xbuf_ref[...] = jnp.zeros_like(xbuf_ref)
        base = off_ref[0]
        for_rows(0, rows_ref[0], lambda r: gather_copy(base, 0, r).start())

    @pl.when(jnp.logical_and(active, s == 0))
    def _():
        base = off_ref[v]
        for_rows(0, rows_ref[v], lambda r: gather_copy(base, slot, r).wait())

    @pl.when(jnp.logical_and(active, v + 1 < nact))
    def _():
        nxt = jnp.minimum(v + 1, n_blocks - 1)
        base = off_ref[nxt]
        for_rows(s * quarter, jnp.minimum((s + 1) * quarter, rows_ref[nxt]),
                 lambda r: gather_copy(base, 1 - slot, r).start())

    @pl.when(jnp.logical_and(active, jnp.logical_and(s == 2, v >= 1)))
    def _():
        prev = jnp.maximum(v - 1, 0)
        base = off_ref[prev]
        for_rows(0, rows_ref[prev], lambda r: scatter_copy(base, r).wait())

    halves = [(pl.ds(0, half), None), (pl.ds(half, half), rows_ref[v] > half)]

    def for_halves(fn):
        for rows, cond in halves:
            if cond is None:
                fn(rows)
            else:
                pl.when(cond)(functools.partial(fn, rows))

    def load_in(step):
        slots = range(4 * step, 4 * step + 4)
        for i in slots:
            in_copy(exp_ref[v], i).wait()
        for i in slots:
            dst = (wa_ref, wa_ref, wb_ref, wb_ref)[i % 4]
            dst[pl.ds((i % 2) * dq, dq), :] = ring_in[i].astype(BF16)
        refill(in_copy, slots)

    def load_dn(step):
        slots = range(2 * (step - 2), 2 * (step - 2) + 2)
        for i in slots:
            dn_copy(exp_ref[v], i).wait()
        for i in slots:
            (wc_ref, wd_ref)[i % 2][...] = ring_dn[i].astype(BF16)
        refill(dn_copy, slots)

    @pl.when(jnp.logical_and(active, s == 0))
    def _():
        load_in(0)

        def step0(rows):
            lo, hi = _unpack_halves(xbuf_ref[slot, rows, :])
            xl = lo.astype(BF16)
            xl_ref[rows, :] = xl
            xh_ref[rows, :] = hi.astype(BF16)
            hg_ref[rows, :] = jnp.dot(xl, wa_ref[...], preferred_element_type=F32)
            hu_ref[rows, :] = jnp.dot(xl, wb_ref[...], preferred_element_type=F32)
        for_halves(step0)

    @pl.when(jnp.logical_and(active, s == 1))
    def _():
        load_in(1)

        def step1(rows):
            xh = xh_ref[rows, :]
            hg = hg_ref[rows, :] + jnp.dot(xh, wa_ref[...], preferred_element_type=F32)
            hu = hu_ref[rows, :] + jnp.dot(xh, wb_ref[...], preferred_element_type=F32)
            hd_ref[rows, :] = (hg * jax.nn.sigmoid(hg) * hu).astype(BF16)
        for_halves(step1)

    def down(step, cols):
        load_dn(step)

        def step2(rows):
            hd = hd_ref[rows, :]
            y_lo = jnp.dot(hd, wc_ref[...], preferred_element_type=F32)
            y_hi = jnp.dot(hd, wd_ref[...], preferred_element_type=F32)
            ybuf_ref[rows, cols] = _pack_halves(jnp.concatenate([y_lo, y_hi], axis=-1))
        for_halves(step2)

    @pl.when(jnp.logical_and(active, s == 2))
    def _():
        down(2, pl.ds(0, oc))

    @pl.when(jnp.logical_and(active, s == 3))
    def _():
        down(3, pl.ds(oc, oc))
        base = off_ref[v]
        for_rows(0, rows_ref[v], lambda r: scatter_copy(base, r).start())

    @pl.when(jnp.logical_and(v == n_blocks - 1, s == EXPERT_STEPS - 1))
    def _():
        last = jnp.maximum(nact - 1, 0)
        base = off_ref[last]
        for_rows(0, rows_ref[last], lambda r: scatter_copy(base, r).wait())


def moe_experts(xp, w_gate, w_up, w_down, blk_exp, blk_next, blk_rows, blk_off, order, nact, *, n_blocks, rb):
    n_exp, d, ff = w_gate.shape
    n_tok, kw = xp.shape
    assert d == 2 * kw
    oc = kw // 2
    n_in, n_dn = 8, 4
    any_spec = pl.BlockSpec(memory_space=pl.ANY)
    return pl.pallas_call(
        functools.partial(_experts_body, rb=rb, n_tok=n_tok, n_blocks=n_blocks),
        grid_spec=pltpu.PrefetchScalarGridSpec(
            num_scalar_prefetch=6,
            grid=(n_blocks, EXPERT_STEPS),
            in_specs=[any_spec, any_spec, any_spec, any_spec],
            out_specs=any_spec,
            scratch_shapes=[pltpu.VMEM((2, rb, kw), U32), pltpu.VMEM((rb, kw), U32),
                            pltpu.VMEM((rb, kw), BF16), pltpu.VMEM((rb, kw), BF16),
                            pltpu.VMEM((rb, ff), F32), pltpu.VMEM((rb, ff), F32), pltpu.VMEM((rb, ff), BF16),
                            pltpu.VMEM((d // 2, ff), BF16), pltpu.VMEM((d // 2, ff), BF16),
                            pltpu.VMEM((ff, oc), BF16), pltpu.VMEM((ff, oc), BF16),
                            pltpu.VMEM((n_in, d // 4, ff), F32), pltpu.VMEM((n_dn, ff, oc), F32),
                            pltpu.SemaphoreType.DMA((2,)), pltpu.SemaphoreType.DMA(()),
                            pltpu.SemaphoreType.DMA((n_in,)), pltpu.SemaphoreType.DMA((n_dn,))],
        ),
        out_shape=jax.ShapeDtypeStruct((MOE_TOPK, n_tok, kw), U32),
        compiler_params=_params(("arbitrary", "arbitrary")),
        name="moe_experts",
    )(blk_exp, blk_next, blk_rows, blk_off, order, nact, xp, w_gate, w_up, w_down)


def _combine_body(h_ref, meta_ref, g_ref, y0_ref, y1_ref, o_ref):
    meta = meta_ref[...]
    acc_lo = None
    acc_hi = None
    for k, y_ref in enumerate((y0_ref, y1_ref)):
        gate = meta[:, 2 * MOE_TOPK + k:2 * MOE_TOPK + k + 1]
        lo, hi = _unpack_halves(y_ref[0])
        lo, hi = lo * gate, hi * gate
        acc_lo = lo if acc_lo is None else acc_lo + lo
        acc_hi = hi if acc_hi is None else acc_hi + hi
    kh = acc_lo.shape[1]
    h_lo = h_ref[:, :kh] + acc_lo
    h_hi = h_ref[:, kh:] + acc_hi
    ms = (jnp.sum(h_lo * h_lo, axis=-1, keepdims=True) + jnp.sum(h_hi * h_hi, axis=-1, keepdims=True)) / (2 * kh)
    scale = lax.rsqrt(ms + NORM_EPS)
    o_ref[:, :kh] = (h_lo * scale) * g_ref[:, :kh]
    o_ref[:, kh:] = (h_hi * scale) * g_ref[:, kh:]


def moe_combine(h, meta, g, ya, *, tb):
    n, d = h.shape
    kw = ya.shape[2]
    tb = _tile(n, tb)

    def slot(k):
        return pl.BlockSpec((1, tb, kw), lambda i, k=k: (k, i, 0))

    return pl.pallas_call(
        _combine_body,
        grid=(n // tb,),
        in_specs=[
            pl.BlockSpec((tb, d), lambda i: (i, 0)),
            pl.BlockSpec((tb, V7X_LANES), lambda i: (i, 0)),
            pl.BlockSpec((1, d), lambda i: (0, 0)),
            slot(0), slot(1),
        ],
        out_specs=pl.BlockSpec((tb, d), lambda i: (i, 0)),
        out_shape=jax.ShapeDtypeStruct((n, d), F32),
        compiler_params=_params(("parallel",)),
        name="moe_combine",
    )(h, meta, g.reshape(1, d), ya, ya)


def _moe_plan(counts, *, rb, n_blocks):
    n_exp = counts.shape[0]
    per_exp = (counts + rb - 1) // rb
    blk_end = jnp.cumsum(per_exp)
    start = jnp.cumsum(counts) - counts
    nact = blk_end[-1].astype(I32)
    blk = jnp.arange(n_blocks, dtype=I32)
    blk_exp = jnp.minimum(jnp.searchsorted(blk_end, blk, side="right"), n_exp - 1).astype(I32)
    active = blk < nact
    last = jnp.maximum(nact - 1, 0)
    blk_exp = jnp.where(active, blk_exp, blk_exp[last]).astype(I32)
    blk_next = blk_exp[jnp.minimum(blk + 1, last)]
    j = blk - (blk_end[blk_exp] - per_exp[blk_exp])
    blk_off = jnp.where(active, start[blk_exp] + j * rb, 0).astype(I32)
    blk_rows = jnp.where(active, jnp.clip(counts[blk_exp] - j * rb, 0, rb), 0).astype(I32)
    return start.astype(I32), blk_exp, blk_next, blk_rows, blk_off, nact.reshape(1)


def kernel(x, mem, positions, norm_mix_g, w_in, ret_norm_g, gm_ln_g, gm_ln_b, gm_ws, gm_bs, w_out, norm_xa_g, norm_mem_g, xa_wq, xa_wkv, xa_wo, norm_moe_g, router_grp_w, router_grp_b, router_exp_w, router_exp_b, moe_w_gate, moe_w_up, moe_w_down, norm_final_g):
    batch, seq, d = x.shape
    mem_len = mem.shape[1]
    n = batch * seq
    ret_width = ret_norm_g.shape[0]
    gm_width = gm_ln_g.shape[0]
    assert ret_width == gm_width and w_in.shape[1] == 4 * ret_width + 2 * gm_width
    dk = ret_width // RET_HEADS
    n_exp = moe_w_gate.shape[0]
    assert n_exp == MOE_GROUPS * MOE_PER_GROUP and MOE_GROUPS + n_exp <= V7X_LANES

    inv_freq = ROPE_BASE ** (-jnp.arange(0, dk, 2, dtype=F32) / dk)
    ang = positions.astype(F32).reshape(n, 1) * inv_freq
    cos, sin = jnp.cos(ang), jnp.sin(ang)

    x2 = x.reshape(n, d)
    proj = norm_matmul(x2, norm_mix_g, w_in.astype(BF16), tm=512, tn=1024)
    ret = retention(proj, cos, sin, ret_norm_g, batch=batch, seq=seq, ret_width=ret_width)
    gm = gmlp(proj, gm_ln_g, gm_ln_b, gm_ws, gm_bs, n_rows=n, gm_width=gm_width,
              u_block=4 * ret_width // gm_width, v_block=4 * ret_width // gm_width + 1)
    h1 = matmul_residual([ret, gm], w_out.astype(BF16), x2, tm=512, tn=1024)

    kv = norm_matmul(mem.reshape(batch * mem_len, d), norm_mem_g, xa_wkv.astype(BF16), tm=512, tn=1024)
    h2 = cross_attention_block(h1, norm_xa_g, xa_wq.astype(BF16), kv, xa_wo.astype(BF16),
                               seq=seq, mem_len=mem_len, tm=256)

    pad = V7X_LANES - MOE_GROUPS - n_exp
    wr = jnp.concatenate([router_grp_w, router_exp_w, jnp.zeros((d, pad), F32)], axis=1)
    br = jnp.concatenate([router_grp_b, router_exp_b, jnp.zeros((pad,), F32)]).reshape(1, V7X_LANES)
    wr_hi = wr.astype(BF16)
    wr_lo = (wr - wr_hi.astype(F32)).astype(BF16)
    xp, meta, meta_t, cnt = moe_router(h2, norm_moe_g, jnp.concatenate([wr_hi, wr_lo], axis=1), br, tm=256)

    rb = MOE_ROW_BLOCK
    n_blocks = -(-(n * MOE_TOPK) // rb) + n_exp
    counts = cnt[0, MOE_GROUPS:MOE_GROUPS + n_exp].astype(I32)
    start, blk_exp, blk_next, blk_rows, blk_off, nact = _moe_plan(counts, rb=rb, n_blocks=n_blocks)
    code = (meta_t[MOE_TOPK:2 * MOE_TOPK].astype(I32) * (1 << MOE_CODE_SHIFT)
            + meta_t[0:MOE_TOPK].astype(I32)).reshape(-1)
    order = moe_order(code, start)
    ya = moe_experts(xp, moe_w_gate, moe_w_up, moe_w_down, blk_exp, blk_next, blk_rows, blk_off, order, nact,
                     n_blocks=n_blocks, rb=rb)
    y = moe_combine(h2, meta, norm_final_g, ya, tb=256)
    return y.reshape(batch, seq, d)
```

```python
import functools

import jax
import jax.numpy as jnp
from jax import lax
from jax.experimental import pallas as pl
from jax.experimental.pallas import tpu as pltpu

NORM_EPS = 1e-6
RET_HEADS = 8
ROPE_BASE = 10000.0
GM_GROUPS = 8
GM_CHUNK = 128
XA_HEADS = 4
MOE_GROUPS = 8
MOE_PER_GROUP = 8
MOE_TOPK = 2

V7X_LANES = 128
V7X_VMEM_BYTES = 64 * 1024 * 1024
VMEM_LIMIT_BYTES = 56 * 1024 * 1024

RET_BLOCK = 512
MOE_ROW_BLOCK = 512

F32 = jnp.float32
BF16 = jnp.bfloat16
U32 = jnp.uint32
I32 = jnp.int32


def _params(sem):
    return pltpu.CompilerParams(dimension_semantics=sem, vmem_limit_bytes=VMEM_LIMIT_BYTES)


def _tile(dim, target):
    t = min(dim, target)
    while dim % t:
        t -= V7X_LANES
    assert t > 0, (dim, target)
    return t


def _pack_halves(x_f32):
    k = x_f32.shape[-1] // 2
    bits = lax.bitcast_convert_type(x_f32.astype(BF16).astype(F32), U32)
    return (bits[:, k:] & jnp.uint32(0xFFFF0000)) | (bits[:, :k] >> 16)


def _unpack_halves(w_u32):
    lo = lax.bitcast_convert_type(w_u32 << 16, F32)
    hi = lax.bitcast_convert_type(w_u32 & jnp.uint32(0xFFFF0000), F32)
    return lo, hi


SIDE_CAST_SLABS = 64
BF16_SUBLANES = 16


def _side_slabs(side, steps):
    n = SIDE_CAST_SLABS
    while n > steps or any(a.shape[0] % (BF16_SUBLANES * n) for a in side):
        n //= 2
    assert n >= 1, [a.shape for a in side]
    return n


def _side_specs(side, n_j, slabs):
    return [pl.BlockSpec((a.shape[0] // slabs, a.shape[1]), lambda i, j: (jnp.minimum(i * n_j + j, slabs - 1), 0))
            for a in side]


def _side_cast(in_refs, out_refs, n_j, slabs):
    if not in_refs:
        return

    @pl.when(pl.program_id(0) * n_j + pl.program_id(1) < slabs)
    def _():
        for src, dst in zip(in_refs, out_refs):
            dst[...] = src[...].astype(dst.dtype)


def _side_out_shapes(side):
    return [jax.ShapeDtypeStruct(a.shape, BF16) for a in side]


def _norm_matmul_body(*refs, n_side, n_j, slabs):
    x_ref, g_ref, w_ref = refs[:3]
    side_in = refs[3:3 + n_side]
    o_ref = refs[3 + n_side]
    side_out = refs[4 + n_side:4 + 2 * n_side]
    xn_ref = refs[4 + 2 * n_side]

    @pl.when(pl.program_id(1) == 0)
    def _():
        x = x_ref[...]
        ms = jnp.mean(x * x, axis=-1, keepdims=True)
        xn_ref[...] = ((x * lax.rsqrt(ms + NORM_EPS)) * g_ref[...]).astype(BF16)

    o_ref[...] = jnp.dot(xn_ref[...], w_ref[...], preferred_element_type=F32).astype(o_ref.dtype)
    _side_cast(side_in, side_out, n_j, slabs)


def norm_matmul(x, g, w, *, tm, tn, side=()):
    m, k = x.shape
    n = w.shape[1]
    tm, tn = _tile(m, tm), _tile(n, tn)
    n_j = n // tn
    slabs = _side_slabs(side, (m // tm) * n_j)
    side_specs = _side_specs(side, n_j, slabs)
    out = pl.pallas_call(
        functools.partial(_norm_matmul_body, n_side=len(side), n_j=n_j, slabs=slabs),
        grid=(m // tm, n_j),
        in_specs=[
            pl.BlockSpec((tm, k), lambda i, j: (i, 0)),
            pl.BlockSpec((1, k), lambda i, j: (0, 0)),
            pl.BlockSpec((k, tn), lambda i, j: (0, j)),
        ] + side_specs,
        out_specs=[pl.BlockSpec((tm, tn), lambda i, j: (i, j))] + side_specs,
        out_shape=[jax.ShapeDtypeStruct((m, n), BF16)] + _side_out_shapes(side),
        scratch_shapes=[pltpu.VMEM((tm, k), BF16)],
        compiler_params=_params(("arbitrary", "arbitrary") if side else ("parallel", "arbitrary")),
        name="norm_matmul",
    )(x, g.reshape(1, k), w, *side)
    return out[0], out[1:]


def _retention_body(lg_ref, q_ref, k_ref, v_ref, g_ref, cos_ref, sin_ref, gn_ref, o_ref, state_ref, dec_ref,
                    *, blk, dk):
    h = pl.program_id(0)
    c = pl.program_id(2)
    lg = lg_ref[h]

    @pl.when(c == 0)
    def _():
        state_ref[...] = jnp.zeros_like(state_ref)

    @pl.when(jnp.logical_and(pl.program_id(1) == 0, c == 0))
    def _():
        ri = lax.broadcasted_iota(I32, (blk, blk), 0)
        ci = lax.broadcasted_iota(I32, (blk, blk), 1)
        diff = (ri - ci).astype(F32)
        dec_ref[...] = jnp.where(diff >= 0.0, jnp.exp(jnp.maximum(diff, 0.0) * lg), 0.0)

    half = dk // 2
    cos = cos_ref[...]
    sin = sin_ref[...]

    def rot(t):
        t1, t2 = t[:, :half], t[:, half:]
        return jnp.concatenate([t1 * cos - t2 * sin, t1 * sin + t2 * cos], axis=-1)

    qr = rot(q_ref[...].astype(F32))
    kr = rot(k_ref[...].astype(F32)) * (dk ** -0.5)
    v = v_ref[...]

    pos = lax.broadcasted_iota(I32, (blk, 1), 0).astype(F32)
    q_dec = jnp.exp((pos + 1.0) * lg)
    k_dec = jnp.exp((blk - 1.0 - pos) * lg)
    blk_dec = jnp.exp(jnp.full((1, dk), blk * lg, F32))

    s = lax.dot_general(qr.astype(BF16), kr.astype(BF16), (((1,), (1,)), ((), ())), preferred_element_type=F32)
    inner = jnp.dot((s * dec_ref[...]).astype(BF16), v, preferred_element_type=F32)

    state = state_ref[...]
    cross = jnp.dot((qr * q_dec).astype(BF16), state.astype(BF16), preferred_element_type=F32)
    kd_t = jnp.transpose(kr * k_dec).astype(BF16)
    state_ref[...] = state * blk_dec + jnp.dot(kd_t, v, preferred_element_type=F32)

    out = inner + cross
    mu = jnp.mean(out, axis=-1, keepdims=True)
    cen = out - mu
    var = jnp.mean(cen * cen, axis=-1, keepdims=True)
    y = cen * lax.rsqrt(var + NORM_EPS) * gn_ref[...]
    gate = g_ref[...].astype(F32)
    o_ref[...] = (y * (gate * jax.nn.sigmoid(gate))).astype(o_ref.dtype)


def retention(proj, cos, sin, ret_norm_g, *, batch, seq, ret_width):
    heads = RET_HEADS
    dk = ret_width // heads
    blk = min(RET_BLOCK, seq)
    nblk = seq // blk
    hb = ret_width // dk
    log_gamma = jnp.log(1.0 - jnp.exp2(-5.0 - jnp.arange(heads, dtype=F32)))

    def col(seg):
        return pl.BlockSpec((blk, dk), lambda h, b, c, lg, seg=seg: (b * nblk + c, seg * hb + h))

    rowspec = pl.BlockSpec((blk, dk // 2), lambda h, b, c, lg: (b * nblk + c, 0))
    return pl.pallas_call(
        functools.partial(_retention_body, blk=blk, dk=dk),
        grid_spec=pltpu.PrefetchScalarGridSpec(
            num_scalar_prefetch=1,
            grid=(heads, batch, nblk),
            in_specs=[col(0), col(1), col(2), col(3), rowspec, rowspec,
                      pl.BlockSpec((1, dk), lambda h, b, c, lg: (0, h))],
            out_specs=pl.BlockSpec((blk, dk), lambda h, b, c, lg: (b * nblk + c, h)),
            scratch_shapes=[pltpu.VMEM((dk, dk), F32), pltpu.VMEM((blk, blk), F32)],
        ),
        out_shape=jax.ShapeDtypeStruct((batch * seq, ret_width), BF16),
        compiler_params=_params(("arbitrary", "arbitrary", "arbitrary")),
        name="retention",
    )(log_gamma, proj, proj, proj, proj, cos, sin, ret_norm_g.reshape(1, ret_width))


def _gmlp_body(u_ref, v_ref, lng_ref, lnb_ref, ws_ref, bst_ref, o_ref, vn_ref, *, rows, groups, cg, chunk):
    v = jax.nn.gelu(v_ref[...].astype(F32))
    mu = jnp.mean(v, axis=-1, keepdims=True)
    cen = v - mu
    var = jnp.mean(cen * cen, axis=-1, keepdims=True)
    vn_ref[...] = (cen * lax.rsqrt(var + NORM_EPS) * lng_ref[...] + lnb_ref[...]).astype(BF16)

    ri = lax.broadcasted_iota(I32, (chunk, chunk), 0)
    ci = lax.broadcasted_iota(I32, (chunk, chunk), 1)
    causal = ri >= ci
    for g in range(groups):
        w = jnp.where(causal, ws_ref[g], 0.0).astype(BF16)
        bias = bst_ref[:, g:g + 1]
        cols = slice(g * cg, (g + 1) * cg)
        for t in range(rows // chunk):
            rws = slice(t * chunk, (t + 1) * chunk)
            sp = jnp.dot(w, vn_ref[rws, cols], preferred_element_type=F32) + bias
            u = jax.nn.gelu(u_ref[rws, cols].astype(F32))
            o_ref[rws, cols] = (u * sp).astype(o_ref.dtype)


def gmlp(proj, ln_g, ln_b, ws, bs, *, n_rows, gm_width, u_block, v_block):
    groups, chunk = GM_GROUPS, GM_CHUNK
    cg = gm_width // groups
    rows = 2 * chunk
    return pl.pallas_call(
        functools.partial(_gmlp_body, rows=rows, groups=groups, cg=cg, chunk=chunk),
        grid=(n_rows // rows,),
        in_specs=[
            pl.BlockSpec((rows, gm_width), lambda i: (i, u_block)),
            pl.BlockSpec((rows, gm_width), lambda i: (i, v_block)),
            pl.BlockSpec((1, gm_width), lambda i: (0, 0)),
            pl.BlockSpec((1, gm_width), lambda i: (0, 0)),
            pl.BlockSpec((groups, chunk, chunk), lambda i: (0, 0, 0)),
            pl.BlockSpec((chunk, groups), lambda i: (0, 0)),
        ],
        out_specs=pl.BlockSpec((rows, gm_width), lambda i: (i, 0)),
        out_shape=jax.ShapeDtypeStruct((n_rows, gm_width), BF16),
        scratch_shapes=[pltpu.VMEM((rows, gm_width), BF16)],
        compiler_params=_params(("parallel",)),
        name="gmlp",
    )(proj, proj, ln_g.reshape(1, gm_width), ln_b.reshape(1, gm_width), ws, bs.T)


def _matmul_residual_body(*refs, n_parts, n_side, n_j, slabs):
    a_refs, w_refs = refs[:n_parts], refs[n_parts:2 * n_parts]
    r_ref = refs[2 * n_parts]
    side_in = refs[2 * n_parts + 1:2 * n_parts + 1 + n_side]
    o_ref = refs[2 * n_parts + 1 + n_side]
    side_out = refs[2 * n_parts + 2 + n_side:]
    acc = r_ref[...]
    for a_ref, w_ref in zip(a_refs, w_refs):
        acc = acc + jnp.dot(a_ref[...], w_ref[...], preferred_element_type=F32)
    o_ref[...] = acc
    _side_cast(side_in, side_out, n_j, slabs)


def matmul_residual(parts, w, res, *, tm, tn, side=()):
    m, n = res.shape
    tm, tn = _tile(m, tm), _tile(n, tn)
    n_j = n // tn
    slabs = _side_slabs(side, (m // tm) * n_j)
    kp = parts[0].shape[1]
    n_parts = len(parts)
    a_specs = [pl.BlockSpec((tm, kp), lambda i, j: (i, 0)) for _ in parts]
    w_specs = [pl.BlockSpec((kp, tn), lambda i, j, p=p: (p, j)) for p in range(n_parts)]
    side_specs = _side_specs(side, n_j, slabs)
    out = pl.pallas_call(
        functools.partial(_matmul_residual_body, n_parts=n_parts, n_side=len(side), n_j=n_j, slabs=slabs),
        grid=(m // tm, n_j),
        in_specs=a_specs + w_specs + [pl.BlockSpec((tm, tn), lambda i, j: (i, j))] + side_specs,
        out_specs=[pl.BlockSpec((tm, tn), lambda i, j: (i, j))] + side_specs,
        out_shape=[jax.ShapeDtypeStruct((m, n), F32)] + _side_out_shapes(side),
        compiler_params=_params(("arbitrary", "arbitrary") if side else ("parallel", "parallel")),
        name="matmul_residual",
    )(*parts, *([w] * n_parts), res, *side)
    return out[0], out[1:]


def _cross_attention_body(h_ref, g_ref, wq_ref, kv_ref, wo_ref, o_ref, *, heads, dh):
    width = heads * dh
    x = h_ref[...]
    ms = jnp.mean(x * x, axis=-1, keepdims=True)
    xn = ((x * lax.rsqrt(ms + NORM_EPS)) * g_ref[...]).astype(BF16)
    q = jnp.dot(xn, wq_ref[...], preferred_element_type=F32).astype(BF16)
    outs = []
    for h in range(heads):
        k = kv_ref[:, h * dh:(h + 1) * dh]
        v = kv_ref[:, width + h * dh:width + (h + 1) * dh]
        s = lax.dot_general(q[:, h * dh:(h + 1) * dh], k, (((1,), (1,)), ((), ())),
                            preferred_element_type=F32) * (dh ** -0.5)
        e = jnp.exp(s - jnp.max(s, axis=-1, keepdims=True))
        p = e / jnp.sum(e, axis=-1, keepdims=True)
        outs.append(jnp.dot(p.astype(BF16), v, preferred_element_type=F32).astype(BF16))
    o = jnp.concatenate(outs, axis=-1)
    o_ref[...] = x + jnp.dot(o, wo_ref[...], preferred_element_type=F32)


def cross_attention_block(h, g, wq, kv, wo, *, seq, mem_len, tm):
    n, d = h.shape
    heads = XA_HEADS
    width = wq.shape[1]
    dh = width // heads
    tm = _tile(seq, tm)
    per_batch = seq // tm

    def resident(shape):
        return pl.BlockSpec(shape, lambda i: (0, 0), pipeline_mode=pl.Buffered(1))

    return pl.pallas_call(
        functools.partial(_cross_attention_body, heads=heads, dh=dh),
        grid=(n // tm,),
        in_specs=[
            pl.BlockSpec((tm, d), lambda i: (i, 0)),
            resident((1, d)),
            resident((d, width)),
            pl.BlockSpec((mem_len, 2 * width), lambda i: (i // per_batch, 0)),
            resident((width, d)),
        ],
        out_specs=pl.BlockSpec((tm, d), lambda i: (i, 0)),
        out_shape=jax.ShapeDtypeStruct((n, d), F32),
        compiler_params=_params(("parallel",)),
        name="cross_attention",
    )(h, g.reshape(1, d), wq, kv, wo)


def _router_body(h_ref, g_ref, wr_ref, br_ref, xp_ref, meta_ref, meta_t_ref, cnt_ref, base_ref, *, tm, n_grp, per):
    @pl.when(pl.program_id(0) == 0)
    def _():
        base_ref[...] = jnp.zeros_like(base_ref)

    x = h_ref[...]
    ms = jnp.mean(x * x, axis=-1, keepdims=True)
    xn = (x * lax.rsqrt(ms + NORM_EPS)) * g_ref[...]
    xp_ref[...] = _pack_halves(xn)

    x_hi = xn.astype(BF16)
    x_lo = (xn - x_hi.astype(F32)).astype(BF16)
    both = jnp.dot(x_hi, wr_ref[...], preferred_element_type=F32)
    corr = jnp.dot(x_lo, wr_ref[:, :V7X_LANES], preferred_element_type=F32)
    logits = both[:, :V7X_LANES] + (both[:, V7X_LANES:] + corr) + br_ref[...]
    lane = lax.broadcasted_iota(I32, logits.shape, 1)
    neg = jnp.float32(-1e30)
    big = jnp.int32(V7X_LANES)

    gl = jnp.where(lane < n_grp, logits, neg)
    gmax = jnp.max(gl, axis=-1, keepdims=True)
    gidx = jnp.min(jnp.where(gl == gmax, lane, big), axis=-1, keepdims=True)
    grp_gate = 1.0 / jnp.sum(jnp.exp(gl - gmax), axis=-1, keepdims=True)

    lo = n_grp + gidx * per
    el = jnp.where((lane >= lo) & (lane < lo + per), logits, neg)
    v1 = jnp.max(el, axis=-1, keepdims=True)
    i1 = jnp.min(jnp.where(el == v1, lane, big), axis=-1, keepdims=True)
    el2 = jnp.where(lane == i1, neg, el)
    v2 = jnp.max(el2, axis=-1, keepdims=True)
    i2 = jnp.min(jnp.where(el2 == v2, lane, big), axis=-1, keepdims=True)
    t = jnp.exp(v2 - v1)
    den = 1.0 + t
    g1 = grp_gate / den
    g2 = grp_gate * (t / den)

    oh1 = jnp.where(lane == i1, 1.0, 0.0)
    oh2 = jnp.where(lane == i2, 1.0, 0.0)
    ri = lax.broadcasted_iota(I32, (tm, tm), 0)
    ci = lax.broadcasted_iota(I32, (tm, tm), 1)
    lower = jnp.where(ri > ci, 1.0, 0.0).astype(BF16)
    pre1 = jnp.dot(lower, oh1.astype(BF16), preferred_element_type=F32)
    pre2 = jnp.dot(lower, oh2.astype(BF16), preferred_element_type=F32)
    cnt1 = jnp.sum(oh1, axis=0, keepdims=True)
    cnt2 = jnp.sum(oh2, axis=0, keepdims=True)
    base = base_ref[...]
    rank1 = jnp.sum(oh1 * (pre1 + base), axis=-1, keepdims=True)
    rank2 = jnp.sum(oh2 * (pre2 + base + cnt1), axis=-1, keepdims=True)
    total = base + cnt1 + cnt2
    base_ref[...] = total
    cnt_ref[...] = total

    e1 = (i1 - n_grp).astype(F32)
    e2 = (i2 - n_grp).astype(F32)
    meta = jnp.zeros(logits.shape, F32)
    for idx, val in enumerate((e1, e2, rank1, rank2, g1, g2)):
        meta = jnp.where(lane == idx, val, meta)
    meta_ref[...] = meta
    meta_t_ref[...] = jnp.transpose(meta)[:META_ROWS, :]


META_ROWS = 8


def moe_router(h, g, wr, br, *, tm):
    n, d = h.shape
    tm = min(tm, n)
    return pl.pallas_call(
        functools.partial(_router_body, tm=tm, n_grp=MOE_GROUPS, per=MOE_PER_GROUP),
        grid=(n // tm,),
        in_specs=[
            pl.BlockSpec((tm, d), lambda i: (i, 0)),
            pl.BlockSpec((1, d), lambda i: (0, 0)),
            pl.BlockSpec((d, 2 * V7X_LANES), lambda i: (0, 0)),
            pl.BlockSpec((1, V7X_LANES), lambda i: (0, 0)),
        ],
        out_specs=[
            pl.BlockSpec((tm, d // 2), lambda i: (i, 0)),
            pl.BlockSpec((tm, V7X_LANES), lambda i: (i, 0)),
            pl.BlockSpec((META_ROWS, tm), lambda i: (0, i)),
            pl.BlockSpec((1, V7X_LANES), lambda i: (0, 0)),
        ],
        out_shape=[
            jax.ShapeDtypeStruct((n, d // 2), U32),
            jax.ShapeDtypeStruct((n, V7X_LANES), F32),
            jax.ShapeDtypeStruct((META_ROWS, n), F32),
            jax.ShapeDtypeStruct((1, V7X_LANES), F32),
        ],
        scratch_shapes=[pltpu.VMEM((1, V7X_LANES), F32)],
        compiler_params=_params(("arbitrary",)),
        name="moe_router",
    )(h, g.reshape(1, d), wr, br)


MOE_CODE_SHIFT = 6
assert MOE_GROUPS * MOE_PER_GROUP == 1 << MOE_CODE_SHIFT


def _order_body(code_ref, start_ref, order_ref):
    def body(a, carry):
        code = code_ref[a]
        order_ref[start_ref[code & ((1 << MOE_CODE_SHIFT) - 1)] + (code >> MOE_CODE_SHIFT)] = a
        return carry

    lax.fori_loop(0, code_ref.shape[0], body, 0, unroll=8)


def moe_order(code, start):
    return pl.pallas_call(
        _order_body,
        grid_spec=pltpu.PrefetchScalarGridSpec(
            num_scalar_prefetch=2,
            grid=(1,),
            in_specs=[],
            out_specs=pl.BlockSpec(memory_space=pltpu.SMEM),
        ),
        out_shape=jax.ShapeDtypeStruct(code.shape, I32),
        compiler_params=_params(("arbitrary",)),
        name="moe_order",
    )(code, start)


EXPERT_STEPS = 4
ROW_DMA_UNROLL = 8
assert MOE_TOPK == 2


def _experts_body(exp_ref, nxt_ref, rows_ref, off_ref, order_ref, nact_ref,
                  xp_hbm, wg_hbm, wu_hbm, wdn_hbm, ya_hbm,
                  xbuf_ref, ybuf_ref, xl_ref, xh_ref, hg_ref, hu_ref, hd_ref, wa_ref, wb_ref, wc_ref, wd_ref,
                  ring_in, ring_dn, gsem, ssem, sem_in, sem_dn, *, rb, n_tok, n_blocks):
    v = pl.program_id(0)
    s = pl.program_id(1)
    nact = nact_ref[0]
    active = v < nact
    slot = lax.rem(v, 2)
    half = rb // 2
    quarter = rb // EXPERT_STEPS
    oc = wc_ref.shape[1]
    dq = ring_in.shape[1]

    def in_copy(e, i):
        src = (wg_hbm, wg_hbm, wu_hbm, wu_hbm)[i % 4]
        q = 2 * (i // 4) + i % 2
        return pltpu.make_async_copy(src.at[e, pl.ds(q * dq, dq), :], ring_in.at[i], sem_in.at[i])

    def dn_copy(e, i):
        col = (i % 2) * 2 * oc + (i // 2) * oc
        return pltpu.make_async_copy(wdn_hbm.at[e, :, pl.ds(col, oc)], ring_dn.at[i], sem_dn.at[i])

    def refill(copy, slots):
        @pl.when(v + 1 < nact)
        def _():
            for i in slots:
                copy(nxt_ref[v], i).start()

    def decode(base, r):
        a = order_ref[base + r]
        k = jnp.where(a >= n_tok, 1, 0)
        return k, a - k * n_tok

    def gather_copy(base, buf, r):
        _, tok = decode(base, r)
        return pltpu.make_async_copy(xp_hbm.at[pl.ds(tok, 1)], xbuf_ref.at[buf, pl.ds(r, 1)], gsem.at[buf])

    def scatter_copy(base, r):
        k, tok = decode(base, r)
        return pltpu.make_async_copy(ybuf_ref.at[pl.ds(r, 1)], ya_hbm.at[k, pl.ds(tok, 1)], ssem)

    def for_rows(lo, hi, fn):
        groups = lax.shift_right_logical(jnp.maximum(hi - lo, 0), ROW_DMA_UNROLL.bit_length() - 1)

        def group(g, carry):
            for u in range(ROW_DMA_UNROLL):
                fn(lo + g * ROW_DMA_UNROLL + u)
            return carry

        def single(r, carry):
            fn(r)
            return carry

        lax.fori_loop(0, groups, group, 0)
        lax.fori_loop(lo + groups * ROW_DMA_UNROLL, hi, single, 0)

    @pl.when(jnp.logical_and(v == 0, s == 0))
    def _():
        for i in range(ring_in.shape[0]):
            in_copy(exp_ref[0], i).start()
        for i in range(ring_dn.shape[0]):
            dn_copy(exp_ref[0], i).start()
        xbuf_ref[...] = jnp.zeros_like(xbuf_ref)
        base = off_ref[0]
        for_rows(0, rows_ref[0], lambda r: gather_copy(base, 0, r).start())

    @pl.when(jnp.logical_and(active, s == 0))
    def _():
        base = off_ref[v]
        for_rows(0, rows_ref[v], lambda r: gather_copy(base, slot, r).wait())

    @pl.when(jnp.logical_and(active, v + 1 < nact))
    def _():
        nxt = jnp.minimum(v + 1, n_blocks - 1)
        base = off_ref[nxt]
        for_rows(s * quarter, jnp.minimum((s + 1) * quarter, rows_ref[nxt]),
                 lambda r: gather_copy(base, 1 - slot, r).start())

    @pl.when(jnp.logical_and(active, jnp.logical_and(s == 2, v >= 1)))
    def _():
        prev = jnp.maximum(v - 1, 0)
        base = off_ref[prev]
        for_rows(0, rows_ref[prev], lambda r: scatter_copy(base, r).wait())

    halves = [(pl.ds(0, half), None), (pl.ds(half, half), rows_ref[v] > half)]

    def for_halves(fn):
        for rows, cond in halves:
            if cond is None:
                fn(rows)
            else:
                pl.when(cond)(functools.partial(fn, rows))

    def load_in(step):
        slots = range(4 * step, 4 * step + 4)
        for i in slots:
            in_copy(exp_ref[v], i).wait()
        for i in slots:
            dst = (wa_ref, wa_ref, wb_ref, wb_ref)[i % 4]
            dst[pl.ds((i % 2) * dq, dq), :] = ring_in[i].astype(BF16)
        refill(in_copy, slots)

    def load_dn(step):
        slots = range(2 * (step - 2), 2 * (step - 2) + 2)
        for i in slots:
            dn_copy(exp_ref[v], i).wait()
        for i in slots:
            (wc_ref, wd_ref)[i % 2][...] = ring_dn[i].astype(BF16)
        refill(dn_copy, slots)

    @pl.when(jnp.logical_and(active, s == 0))
    def _():
        load_in(0)

        def step0(rows):
            lo, hi = _unpack_halves(xbuf_ref[slot, rows, :])
            xl = lo.astype(BF16)
            xl_ref[rows, :] = xl
            xh_ref[rows, :] = hi.astype(BF16)
            hg_ref[rows, :] = jnp.dot(xl, wa_ref[...], preferred_element_type=F32)
            hu_ref[rows, :] = jnp.dot(xl, wb_ref[...], preferred_element_type=F32)
        for_halves(step0)

    @pl.when(jnp.logical_and(active, s == 1))
    def _():
        load_in(1)

        def step1(rows):
            xh = xh_ref[rows, :]
            hg = hg_ref[rows, :] + jnp.dot(xh, wa_ref[...], preferred_element_type=F32)
            hu = hu_ref[rows, :] + jnp.dot(xh, wb_ref[...], preferred_element_type=F32)
            hd_ref[rows, :] = (hg * jax.nn.sigmoid(hg) * hu).astype(BF16)
        for_halves(step1)

    def down(step, cols):
        load_dn(step)

        def step2(rows):
            hd = hd_ref[rows, :]
            y_lo = jnp.dot(hd, wc_ref[...], preferred_element_type=F32)
            y_hi = jnp.dot(hd, wd_ref[...], preferred_element_type=F32)
            ybuf_ref[rows, cols] = _pack_halves(jnp.concatenate([y_lo, y_hi], axis=-1))
        for_halves(step2)

    @pl.when(jnp.logical_and(active, s == 2))
    def _():
        down(2, pl.ds(0, oc))

    @pl.when(jnp.logical_and(active, s == 3))
    def _():
        down(3, pl.ds(oc, oc))
        base = off_ref[v]
        for_rows(0, rows_ref[v], lambda r: scatter_copy(base, r).start())

    @pl.when(jnp.logical_and(v == n_blocks - 1, s == EXPERT_STEPS - 1))
    def _():
        last = jnp.maximum(nact - 1, 0)
        base = off_ref[last]
        for_rows(0, rows_ref[last], lambda r: scatter_copy(base, r).wait())


def moe_experts(xp, w_gate, w_up, w_down, blk_exp, blk_next, blk_rows, blk_off, order, nact, *, n_blocks, rb):
    n_exp, d, ff = w_gate.shape
    n_tok, kw = xp.shape
    assert d == 2 * kw
    oc = kw // 2
    n_in, n_dn = 8, 4
    any_spec = pl.BlockSpec(memory_space=pl.ANY)
    return pl.pallas_call(
        functools.partial(_experts_body, rb=rb, n_tok=n_tok, n_blocks=n_blocks),
        grid_spec=pltpu.PrefetchScalarGridSpec(
            num_scalar_prefetch=6,
            grid=(n_blocks, EXPERT_STEPS),
            in_specs=[any_spec, any_spec, any_spec, any_spec],
            out_specs=any_spec,
            scratch_shapes=[pltpu.VMEM((2, rb, kw), U32), pltpu.VMEM((rb, kw), U32),
                            pltpu.VMEM((rb, kw), BF16), pltpu.VMEM((rb, kw), BF16),
                            pltpu.VMEM((rb, ff), F32), pltpu.VMEM((rb, ff), F32), pltpu.VMEM((rb, ff), BF16),
                            pltpu.VMEM((d // 2, ff), BF16), pltpu.VMEM((d // 2, ff), BF16),
                            pltpu.VMEM((ff, oc), BF16), pltpu.VMEM((ff, oc), BF16),
                            pltpu.VMEM((n_in, d // 4, ff), F32), pltpu.VMEM((n_dn, ff, oc), F32),
                            pltpu.SemaphoreType.DMA((2,)), pltpu.SemaphoreType.DMA(()),
                            pltpu.SemaphoreType.DMA((n_in,)), pltpu.SemaphoreType.DMA((n_dn,))],
        ),
        out_shape=jax.ShapeDtypeStruct((MOE_TOPK, n_tok, kw), U32),
        compiler_params=_params(("arbitrary", "arbitrary")),
        name="moe_experts",
    )(blk_exp, blk_next, blk_rows, blk_off, order, nact, xp, w_gate, w_up, w_down)


def _combine_body(h_ref, meta_ref, g_ref, y0_ref, y1_ref, o_ref):
    meta = meta_ref[...]
    acc_lo = None
    acc_hi = None
    for k, y_ref in enumerate((y0_ref, y1_ref)):
        gate = meta[:, 2 * MOE_TOPK + k:2 * MOE_TOPK + k + 1]
        lo, hi = _unpack_halves(y_ref[0])
        lo, hi = lo * gate, hi * gate
        acc_lo = lo if acc_lo is None else acc_lo + lo
        acc_hi = hi if acc_hi is None else acc_hi + hi
    kh = acc_lo.shape[1]
    h_lo = h_ref[:, :kh] + acc_lo
    h_hi = h_ref[:, kh:] + acc_hi
    ms = (jnp.sum(h_lo * h_lo, axis=-1, keepdims=True) + jnp.sum(h_hi * h_hi, axis=-1, keepdims=True)) / (2 * kh)
    scale = lax.rsqrt(ms + NORM_EPS)
    o_ref[:, :kh] = (h_lo * scale) * g_ref[:, :kh]
    o_ref[:, kh:] = (h_hi * scale) * g_ref[:, kh:]


def moe_combine(h, meta, g, ya, *, tb):
    n, d = h.shape
    kw = ya.shape[2]
    tb = _tile(n, tb)

    def slot(k):
        return pl.BlockSpec((1, tb, kw), lambda i, k=k: (k, i, 0))

    return pl.pallas_call(
        _combine_body,
        grid=(n // tb,),
        in_specs=[
            pl.BlockSpec((tb, d), lambda i: (i, 0)),
            pl.BlockSpec((tb, V7X_LANES), lambda i: (i, 0)),
            pl.BlockSpec((1, d), lambda i: (0, 0)),
            slot(0), slot(1),
        ],
        out_specs=pl.BlockSpec((tb, d), lambda i: (i, 0)),
        out_shape=jax.ShapeDtypeStruct((n, d), F32),
        compiler_params=_params(("parallel",)),
        name="moe_combine",
    )(h, meta, g.reshape(1, d), ya, ya)


def _moe_plan(counts, *, rb, n_blocks):
    n_exp = counts.shape[0]
    per_exp = (counts + rb - 1) // rb
    blk_end = jnp.cumsum(per_exp)
    start = jnp.cumsum(counts) - counts
    nact = blk_end[-1].astype(I32)
    blk = jnp.arange(n_blocks, dtype=I32)
    blk_exp = jnp.minimum(jnp.searchsorted(blk_end, blk, side="right"), n_exp - 1).astype(I32)
    active = blk < nact
    last = jnp.maximum(nact - 1, 0)
    blk_exp = jnp.where(active, blk_exp, blk_exp[last]).astype(I32)
    blk_next = blk_exp[jnp.minimum(blk + 1, last)]
    j = blk - (blk_end[blk_exp] - per_exp[blk_exp])
    blk_off = jnp.where(active, start[blk_exp] + j * rb, 0).astype(I32)
    blk_rows = jnp.where(active, jnp.clip(counts[blk_exp] - j * rb, 0, rb), 0).astype(I32)
    return start.astype(I32), blk_exp, blk_next, blk_rows, blk_off, nact.reshape(1)


def kernel(x, mem, positions, norm_mix_g, w_in, ret_norm_g, gm_ln_g, gm_ln_b, gm_ws, gm_bs, w_out, norm_xa_g, norm_mem_g, xa_wq, xa_wkv, xa_wo, norm_moe_g, router_grp_w, router_grp_b, router_exp_w, router_exp_b, moe_w_gate, moe_w_up, moe_w_down, norm_final_g):
    batch, seq, d = x.shape
    mem_len = mem.shape[1]
    n = batch * seq
    ret_width = ret_norm_g.shape[0]
    gm_width = gm_ln_g.shape[0]
    assert ret_width == gm_width and w_in.shape[1] == 4 * ret_width + 2 * gm_width
    dk = ret_width // RET_HEADS
    n_exp = moe_w_gate.shape[0]
    assert n_exp == MOE_GROUPS * MOE_PER_GROUP and MOE_GROUPS + n_exp <= V7X_LANES

    inv_freq = ROPE_BASE ** (-jnp.arange(0, dk, 2, dtype=F32) / dk)
    ang = positions.astype(F32).reshape(n, 1) * inv_freq
    cos, sin = jnp.cos(ang), jnp.sin(ang)

    x2 = x.reshape(n, d)
    proj, (w_out_b,) = norm_matmul(x2, norm_mix_g, w_in.astype(BF16), tm=512, tn=1024, side=(w_out,))
    ret = retention(proj, cos, sin, ret_norm_g, batch=batch, seq=seq, ret_width=ret_width)
    gm = gmlp(proj, gm_ln_g, gm_ln_b, gm_ws, gm_bs, n_rows=n, gm_width=gm_width,
              u_block=4 * ret_width // gm_width, v_block=4 * ret_width // gm_width + 1)
    h1, (wq_b, wkv_b, wo_b) = matmul_residual([ret, gm], w_out_b, x2, tm=512, tn=1024,
                                              side=(xa_wq, xa_wkv, xa_wo))

    kv, _ = norm_matmul(mem.reshape(batch * mem_len, d), norm_mem_g, wkv_b, tm=512, tn=1024)
    h2 = cross_attention_block(h1, norm_xa_g, wq_b, kv, wo_b, seq=seq, mem_len=mem_len, tm=256)

    pad = V7X_LANES - MOE_GROUPS - n_exp
    wr = jnp.concatenate([router_grp_w, router_exp_w, jnp.zeros((d, pad), F32)], axis=1)
    br = jnp.concatenate([router_grp_b, router_exp_b, jnp.zeros((pad,), F32)]).reshape(1, V7X_LANES)
    wr_hi = wr.astype(BF16)
    wr_lo = (wr - wr_hi.astype(F32)).astype(BF16)
    xp, meta, meta_t, cnt = moe_router(h2, norm_moe_g, jnp.concatenate([wr_hi, wr_lo], axis=1), br, tm=256)

    rb = MOE_ROW_BLOCK
    n_blocks = -(-(n * MOE_TOPK) // rb) + n_exp
    counts = cnt[0, MOE_GROUPS:MOE_GROUPS + n_exp].astype(I32)
    start, blk_exp, blk_next, blk_rows, blk_off, nact = _moe_plan(counts, rb=rb, n_blocks=n_blocks)
    code = (meta_t[MOE_TOPK:2 * MOE_TOPK].astype(I32) * (1 << MOE_CODE_SHIFT)
            + meta_t[0:MOE_TOPK].astype(I32)).reshape(-1)
    order = moe_order(code, start)
    ya = moe_experts(xp, moe_w_gate, moe_w_up, moe_w_down, blk_exp, blk_next, blk_rows, blk_off, order, nact,
                     n_blocks=n_blocks, rb=rb)
    y = moe_combine(h2, meta, norm_final_g, ya, tb=256)
    return y.reshape(batch, seq, d)
```

```python
import functools

import jax
import jax.numpy as jnp
from jax import lax
from jax.experimental import pallas as pl
from jax.experimental.pallas import tpu as pltpu

NORM_EPS = 1e-6
RET_HEADS = 8
ROPE_BASE = 10000.0
GM_GROUPS = 8
GM_CHUNK = 128
XA_HEADS = 4
MOE_GROUPS = 8
MOE_PER_GROUP = 8
MOE_TOPK = 2

V7X_LANES = 128
V7X_VMEM_BYTES = 64 * 1024 * 1024
VMEM_LIMIT_BYTES = 56 * 1024 * 1024

RET_BLOCK = 512
MOE_ROW_BLOCK = 512

F32 = jnp.float32
BF16 = jnp.bfloat16
U32 = jnp.uint32
I32 = jnp.int32


def _params(sem):
    return pltpu.CompilerParams(dimension_semantics=sem, vmem_limit_bytes=VMEM_LIMIT_BYTES)


def _tile(dim, target):
    t = min(dim, target)
    while dim % t:
        t -= V7X_LANES
    assert t > 0, (dim, target)
    return t


def _pack_halves(x_f32):
    k = x_f32.shape[-1] // 2
    bits = lax.bitcast_convert_type(x_f32.astype(BF16).astype(F32), U32)
    return (bits[:, k:] & jnp.uint32(0xFFFF0000)) | (bits[:, :k] >> 16)


def _unpack_halves(w_u32):
    lo = lax.bitcast_convert_type(w_u32 << 16, F32)
    hi = lax.bitcast_convert_type(w_u32 & jnp.uint32(0xFFFF0000), F32)
    return lo, hi


SIDE_CAST_SLABS = 64
BF16_SUBLANES = 16


def _side_slabs(side, steps):
    n = SIDE_CAST_SLABS
    while n > steps or any(a.shape[0] % (BF16_SUBLANES * n) for a in side):
        n //= 2
    assert n >= 1, [a.shape for a in side]
    return n


def _side_specs(side, n_j, slabs):
    return [pl.BlockSpec((a.shape[0] // slabs, a.shape[1]), lambda i, j: (jnp.minimum(i * n_j + j, slabs - 1), 0))
            for a in side]


def _side_cast(in_refs, out_refs, n_j, slabs):
    if not in_refs:
        return

    @pl.when(pl.program_id(0) * n_j + pl.program_id(1) < slabs)
    def _():
        for src, dst in zip(in_refs, out_refs):
            dst[...] = src[...].astype(dst.dtype)


def _side_out_shapes(side):
    return [jax.ShapeDtypeStruct(a.shape, BF16) for a in side]


def _norm_matmul_body(*refs, n_side, n_j, slabs):
    x_ref, g_ref, w_ref = refs[:3]
    side_in = refs[3:3 + n_side]
    o_ref = refs[3 + n_side]
    side_out = refs[4 + n_side:4 + 2 * n_side]
    xn_ref = refs[4 + 2 * n_side]

    @pl.when(pl.program_id(1) == 0)
    def _():
        x = x_ref[...]
        ms = jnp.mean(x * x, axis=-1, keepdims=True)
        xn_ref[...] = ((x * lax.rsqrt(ms + NORM_EPS)) * g_ref[...]).astype(BF16)

    o_ref[...] = jnp.dot(xn_ref[...], w_ref[...], preferred_element_type=F32).astype(o_ref.dtype)
    _side_cast(side_in, side_out, n_j, slabs)


def norm_matmul(x, g, w, *, tm, tn, side=()):
    m, k = x.shape
    n = w.shape[1]
    tm, tn = _tile(m, tm), _tile(n, tn)
    n_j = n // tn
    slabs = _side_slabs(side, (m // tm) * n_j)
    side_specs = _side_specs(side, n_j, slabs)
    out = pl.pallas_call(
        functools.partial(_norm_matmul_body, n_side=len(side), n_j=n_j, slabs=slabs),
        grid=(m // tm, n_j),
        in_specs=[
            pl.BlockSpec((tm, k), lambda i, j: (i, 0)),
            pl.BlockSpec((1, k), lambda i, j: (0, 0)),
            pl.BlockSpec((k, tn), lambda i, j: (0, j)),
        ] + side_specs,
        out_specs=[pl.BlockSpec((tm, tn), lambda i, j: (i, j))] + side_specs,
        out_shape=[jax.ShapeDtypeStruct((m, n), BF16)] + _side_out_shapes(side),
        scratch_shapes=[pltpu.VMEM((tm, k), BF16)],
        compiler_params=_params(("arbitrary", "arbitrary") if side else ("parallel", "arbitrary")),
        name="norm_matmul",
    )(x, g.reshape(1, k), w, *side)
    return out[0], out[1:]


RET_HEADS_PER_STEP = 2


def _retention_body(lg_ref, q_ref, k_ref, v_ref, g_ref, cos_ref, sin_ref, gn_ref, o_ref, state_ref, dec_ref,
                    *, blk, dk, hps):
    c = pl.program_id(2)
    first_of_pair = jnp.logical_and(pl.program_id(1) == 0, c == 0)
    half = dk // 2
    cos = cos_ref[...]
    sin = sin_ref[...]
    pos = lax.broadcasted_iota(I32, (blk, 1), 0).astype(F32)

    def rot(t):
        t1, t2 = t[:, :half], t[:, half:]
        return jnp.concatenate([t1 * cos - t2 * sin, t1 * sin + t2 * cos], axis=-1)

    @pl.when(c == 0)
    def _():
        state_ref[...] = jnp.zeros_like(state_ref)

    for j in range(hps):
        lg = lg_ref[pl.program_id(0) * hps + j]
        cols = slice(j * dk, (j + 1) * dk)

        @pl.when(first_of_pair)
        def _(j=j, lg=lg):
            ri = lax.broadcasted_iota(I32, (blk, blk), 0)
            ci = lax.broadcasted_iota(I32, (blk, blk), 1)
            diff = (ri - ci).astype(F32)
            dec_ref[j] = jnp.where(diff >= 0.0, jnp.exp(jnp.maximum(diff, 0.0) * lg), 0.0)

        qr = rot(q_ref[:, cols].astype(F32))
        kr = rot(k_ref[:, cols].astype(F32)) * (dk ** -0.5)
        v = v_ref[:, cols]

        q_dec = jnp.exp((pos + 1.0) * lg)
        k_dec = jnp.exp((blk - 1.0 - pos) * lg)
        blk_dec = jnp.exp(jnp.full((1, dk), blk * lg, F32))

        s = lax.dot_general(qr.astype(BF16), kr.astype(BF16), (((1,), (1,)), ((), ())), preferred_element_type=F32)
        inner = jnp.dot((s * dec_ref[j]).astype(BF16), v, preferred_element_type=F32)

        state = state_ref[j]
        cross = jnp.dot((qr * q_dec).astype(BF16), state.astype(BF16), preferred_element_type=F32)
        kd_t = jnp.transpose(kr * k_dec).astype(BF16)
        state_ref[j] = state * blk_dec + jnp.dot(kd_t, v, preferred_element_type=F32)

        out = inner + cross
        mu = jnp.mean(out, axis=-1, keepdims=True)
        cen = out - mu
        var = jnp.mean(cen * cen, axis=-1, keepdims=True)
        y = cen * lax.rsqrt(var + NORM_EPS) * gn_ref[:, cols]
        gate = g_ref[:, cols].astype(F32)
        o_ref[:, cols] = (y * (gate * jax.nn.sigmoid(gate))).astype(o_ref.dtype)


def retention(proj, cos, sin, ret_norm_g, *, batch, seq, ret_width):
    heads = RET_HEADS
    hps = RET_HEADS_PER_STEP
    assert heads % hps == 0
    dk = ret_width // heads
    blk = min(RET_BLOCK, seq)
    nblk = seq // blk
    hb = heads // hps
    log_gamma = jnp.log(1.0 - jnp.exp2(-5.0 - jnp.arange(heads, dtype=F32)))

    def col(seg):
        return pl.BlockSpec((blk, hps * dk), lambda h, b, c, lg, seg=seg: (b * nblk + c, seg * hb + h))

    rowspec = pl.BlockSpec((blk, dk // 2), lambda h, b, c, lg: (b * nblk + c, 0))
    return pl.pallas_call(
        functools.partial(_retention_body, blk=blk, dk=dk, hps=hps),
        grid_spec=pltpu.PrefetchScalarGridSpec(
            num_scalar_prefetch=1,
            grid=(heads // hps, batch, nblk),
            in_specs=[col(0), col(1), col(2), col(3), rowspec, rowspec,
                      pl.BlockSpec((1, hps * dk), lambda h, b, c, lg: (0, h))],
            out_specs=pl.BlockSpec((blk, hps * dk), lambda h, b, c, lg: (b * nblk + c, h)),
            scratch_shapes=[pltpu.VMEM((hps, dk, dk), F32), pltpu.VMEM((hps, blk, blk), F32)],
        ),
        out_shape=jax.ShapeDtypeStruct((batch * seq, ret_width), BF16),
        compiler_params=_params(("arbitrary", "arbitrary", "arbitrary")),
        name="retention",
    )(log_gamma, proj, proj, proj, proj, cos, sin, ret_norm_g.reshape(1, ret_width))


def _gmlp_body(u_ref, v_ref, lng_ref, lnb_ref, ws_ref, bst_ref, o_ref, vn_ref, *, rows, groups, cg, chunk):
    v = jax.nn.gelu(v_ref[...].astype(F32))
    mu = jnp.mean(v, axis=-1, keepdims=True)
    cen = v - mu
    var = jnp.mean(cen * cen, axis=-1, keepdims=True)
    vn_ref[...] = (cen * lax.rsqrt(var + NORM_EPS) * lng_ref[...] + lnb_ref[...]).astype(BF16)

    ri = lax.broadcasted_iota(I32, (chunk, chunk), 0)
    ci = lax.broadcasted_iota(I32, (chunk, chunk), 1)
    causal = ri >= ci
    for g in range(groups):
        w = jnp.where(causal, ws_ref[g], 0.0).astype(BF16)
        bias = bst_ref[:, g:g + 1]
        cols = slice(g * cg, (g + 1) * cg)
        for t in range(rows // chunk):
            rws = slice(t * chunk, (t + 1) * chunk)
            sp = jnp.dot(w, vn_ref[rws, cols], preferred_element_type=F32) + bias
            u = jax.nn.gelu(u_ref[rws, cols].astype(F32))
            o_ref[rws, cols] = (u * sp).astype(o_ref.dtype)


def gmlp(proj, ln_g, ln_b, ws, bs, *, n_rows, gm_width, u_block, v_block):
    groups, chunk = GM_GROUPS, GM_CHUNK
    cg = gm_width // groups
    rows = 2 * chunk
    return pl.pallas_call(
        functools.partial(_gmlp_body, rows=rows, groups=groups, cg=cg, chunk=chunk),
        grid=(n_rows // rows,),
        in_specs=[
            pl.BlockSpec((rows, gm_width), lambda i: (i, u_block)),
            pl.BlockSpec((rows, gm_width), lambda i: (i, v_block)),
            pl.BlockSpec((1, gm_width), lambda i: (0, 0)),
            pl.BlockSpec((1, gm_width), lambda i: (0, 0)),
            pl.BlockSpec((groups, chunk, chunk), lambda i: (0, 0, 0)),
            pl.BlockSpec((chunk, groups), lambda i: (0, 0)),
        ],
        out_specs=pl.BlockSpec((rows, gm_width), lambda i: (i, 0)),
        out_shape=jax.ShapeDtypeStruct((n_rows, gm_width), BF16),
        scratch_shapes=[pltpu.VMEM((rows, gm_width), BF16)],
        compiler_params=_params(("parallel",)),
        name="gmlp",
    )(proj, proj, ln_g.reshape(1, gm_width), ln_b.reshape(1, gm_width), ws, bs.T)


def _matmul_residual_body(*refs, n_parts, n_side, n_j, slabs):
    a_refs, w_refs = refs[:n_parts], refs[n_parts:2 * n_parts]
    r_ref = refs[2 * n_parts]
    side_in = refs[2 * n_parts + 1:2 * n_parts + 1 + n_side]
    o_ref = refs[2 * n_parts + 1 + n_side]
    side_out = refs[2 * n_parts + 2 + n_side:]
    acc = r_ref[...]
    for a_ref, w_ref in zip(a_refs, w_refs):
        acc = acc + jnp.dot(a_ref[...], w_ref[...], preferred_element_type=F32)
    o_ref[...] = acc
    _side_cast(side_in, side_out, n_j, slabs)


def matmul_residual(parts, w, res, *, tm, tn, side=()):
    m, n = res.shape
    tm, tn = _tile(m, tm), _tile(n, tn)
    n_j = n // tn
    slabs = _side_slabs(side, (m // tm) * n_j)
    kp = parts[0].shape[1]
    n_parts = len(parts)
    a_specs = [pl.BlockSpec((tm, kp), lambda i, j: (i, 0)) for _ in parts]
    w_specs = [pl.BlockSpec((kp, tn), lambda i, j, p=p: (p, j)) for p in range(n_parts)]
    side_specs = _side_specs(side, n_j, slabs)
    out = pl.pallas_call(
        functools.partial(_matmul_residual_body, n_parts=n_parts, n_side=len(side), n_j=n_j, slabs=slabs),
        grid=(m // tm, n_j),
        in_specs=a_specs + w_specs + [pl.BlockSpec((tm, tn), lambda i, j: (i, j))] + side_specs,
        out_specs=[pl.BlockSpec((tm, tn), lambda i, j: (i, j))] + side_specs,
        out_shape=[jax.ShapeDtypeStruct((m, n), F32)] + _side_out_shapes(side),
        compiler_params=_params(("arbitrary", "arbitrary") if side else ("parallel", "parallel")),
        name="matmul_residual",
    )(*parts, *([w] * n_parts), res, *side)
    return out[0], out[1:]


def _cross_attention_body(h_ref, g_ref, wq_ref, kv_ref, wo_ref, o_ref, *, heads, dh):
    width = heads * dh
    x = h_ref[...]
    ms = jnp.mean(x * x, axis=-1, keepdims=True)
    xn = ((x * lax.rsqrt(ms + NORM_EPS)) * g_ref[...]).astype(BF16)
    q = jnp.dot(xn, wq_ref[...], preferred_element_type=F32).astype(BF16)
    outs = []
    for h in range(heads):
        k = kv_ref[:, h * dh:(h + 1) * dh]
        v = kv_ref[:, width + h * dh:width + (h + 1) * dh]
        s = lax.dot_general(q[:, h * dh:(h + 1) * dh], k, (((1,), (1,)), ((), ())),
                            preferred_element_type=F32) * (dh ** -0.5)
        e = jnp.exp(s - jnp.max(s, axis=-1, keepdims=True))
        p = e / jnp.sum(e, axis=-1, keepdims=True)
        outs.append(jnp.dot(p.astype(BF16), v, preferred_element_type=F32).astype(BF16))
    o = jnp.concatenate(outs, axis=-1)
    o_ref[...] = x + jnp.dot(o, wo_ref[...], preferred_element_type=F32)


def cross_attention_block(h, g, wq, kv, wo, *, seq, mem_len, tm):
    n, d = h.shape
    heads = XA_HEADS
    width = wq.shape[1]
    dh = width // heads
    tm = _tile(seq, tm)
    per_batch = seq // tm

    def resident(shape):
        return pl.BlockSpec(shape, lambda i: (0, 0), pipeline_mode=pl.Buffered(1))

    return pl.pallas_call(
        functools.partial(_cross_attention_body, heads=heads, dh=dh),
        grid=(n // tm,),
        in_specs=[
            pl.BlockSpec((tm, d), lambda i: (i, 0)),
            resident((1, d)),
            resident((d, width)),
            pl.BlockSpec((mem_len, 2 * width), lambda i: (i // per_batch, 0)),
            resident((width, d)),
        ],
        out_specs=pl.BlockSpec((tm, d), lambda i: (i, 0)),
        out_shape=jax.ShapeDtypeStruct((n, d), F32),
        compiler_params=_params(("parallel",)),
        name="cross_attention",
    )(h, g.reshape(1, d), wq, kv, wo)


def _router_body(h_ref, g_ref, wr_ref, br_ref, xp_ref, meta_ref, meta_t_ref, cnt_ref, base_ref, *, tm, n_grp, per):
    @pl.when(pl.program_id(0) == 0)
    def _():
        base_ref[...] = jnp.zeros_like(base_ref)

    x = h_ref[...]
    ms = jnp.mean(x * x, axis=-1, keepdims=True)
    xn = (x * lax.rsqrt(ms + NORM_EPS)) * g_ref[...]
    xp_ref[...] = _pack_halves(xn)

    x_hi = xn.astype(BF16)
    x_lo = (xn - x_hi.astype(F32)).astype(BF16)
    both = jnp.dot(x_hi, wr_ref[...], preferred_element_type=F32)
    corr = jnp.dot(x_lo, wr_ref[:, :V7X_LANES], preferred_element_type=F32)
    logits = both[:, :V7X_LANES] + (both[:, V7X_LANES:] + corr) + br_ref[...]
    lane = lax.broadcasted_iota(I32, logits.shape, 1)
    neg = jnp.float32(-1e30)
    big = jnp.int32(V7X_LANES)

    gl = jnp.where(lane < n_grp, logits, neg)
    gmax = jnp.max(gl, axis=-1, keepdims=True)
    gidx = jnp.min(jnp.where(gl == gmax, lane, big), axis=-1, keepdims=True)
    grp_gate = 1.0 / jnp.sum(jnp.exp(gl - gmax), axis=-1, keepdims=True)

    lo = n_grp + gidx * per
    el = jnp.where((lane >= lo) & (lane < lo + per), logits, neg)
    v1 = jnp.max(el, axis=-1, keepdims=True)
    i1 = jnp.min(jnp.where(el == v1, lane, big), axis=-1, keepdims=True)
    el2 = jnp.where(lane == i1, neg, el)
    v2 = jnp.max(el2, axis=-1, keepdims=True)
    i2 = jnp.min(jnp.where(el2 == v2, lane, big), axis=-1, keepdims=True)
    t = jnp.exp(v2 - v1)
    den = 1.0 + t
    g1 = grp_gate / den
    g2 = grp_gate * (t / den)

    oh1 = jnp.where(lane == i1, 1.0, 0.0)
    oh2 = jnp.where(lane == i2, 1.0, 0.0)
    ri = lax.broadcasted_iota(I32, (tm, tm), 0)
    ci = lax.broadcasted_iota(I32, (tm, tm), 1)
    lower = jnp.where(ri > ci, 1.0, 0.0).astype(BF16)
    pre1 = jnp.dot(lower, oh1.astype(BF16), preferred_element_type=F32)
    pre2 = jnp.dot(lower, oh2.astype(BF16), preferred_element_type=F32)
    cnt1 = jnp.sum(oh1, axis=0, keepdims=True)
    cnt2 = jnp.sum(oh2, axis=0, keepdims=True)
    base = base_ref[...]
    rank1 = jnp.sum(oh1 * (pre1 + base), axis=-1, keepdims=True)
    rank2 = jnp.sum(oh2 * (pre2 + base + cnt1), axis=-1, keepdims=True)
    total = base + cnt1 + cnt2
    base_ref[...] = total
    cnt_ref[...] = total

    e1 = (i1 - n_grp).astype(F32)
    e2 = (i2 - n_grp).astype(F32)
    meta = jnp.zeros(logits.shape, F32)
    for idx, val in enumerate((e1, e2, rank1, rank2, g1, g2)):
        meta = jnp.where(lane == idx, val, meta)
    meta_ref[...] = meta
    meta_t_ref[...] = jnp.transpose(meta)[:META_ROWS, :]


META_ROWS = 8


def moe_router(h, g, wr, br, *, tm):
    n, d = h.shape
    tm = min(tm, n)
    return pl.pallas_call(
        functools.partial(_router_body, tm=tm, n_grp=MOE_GROUPS, per=MOE_PER_GROUP),
        grid=(n // tm,),
        in_specs=[
            pl.BlockSpec((tm, d), lambda i: (i, 0)),
            pl.BlockSpec((1, d), lambda i: (0, 0)),
            pl.BlockSpec((d, 2 * V7X_LANES), lambda i: (0, 0)),
            pl.BlockSpec((1, V7X_LANES), lambda i: (0, 0)),
        ],
        out_specs=[
            pl.BlockSpec((tm, d // 2), lambda i: (i, 0)),
            pl.BlockSpec((tm, V7X_LANES), lambda i: (i, 0)),
            pl.BlockSpec((META_ROWS, tm), lambda i: (0, i)),
            pl.BlockSpec((1, V7X_LANES), lambda i: (0, 0)),
        ],
        out_shape=[
            jax.ShapeDtypeStruct((n, d // 2), U32),
            jax.ShapeDtypeStruct((n, V7X_LANES), F32),
            jax.ShapeDtypeStruct((META_ROWS, n), F32),
            jax.ShapeDtypeStruct((1, V7X_LANES), F32),
        ],
        scratch_shapes=[pltpu.VMEM((1, V7X_LANES), F32)],
        compiler_params=_params(("arbitrary",)),
        name="moe_router",
    )(h, g.reshape(1, d), wr, br)


MOE_CODE_SHIFT = 6
assert MOE_GROUPS * MOE_PER_GROUP == 1 << MOE_CODE_SHIFT


def _order_body(code_ref, start_ref, order_ref):
    def body(a, carry):
        code = code_ref[a]
        order_ref[start_ref[code & ((1 << MOE_CODE_SHIFT) - 1)] + (code >> MOE_CODE_SHIFT)] = a
        return carry

    lax.fori_loop(0, code_ref.shape[0], body, 0, unroll=8)


def moe_order(code, start):
    return pl.pallas_call(
        _order_body,
        grid_spec=pltpu.PrefetchScalarGridSpec(
            num_scalar_prefetch=2,
            grid=(1,),
            in_specs=[],
            out_specs=pl.BlockSpec(memory_space=pltpu.SMEM),
        ),
        out_shape=jax.ShapeDtypeStruct(code.shape, I32),
        compiler_params=_params(("arbitrary",)),
        name="moe_order",
    )(code, start)


EXPERT_STEPS = 4
ROW_DMA_UNROLL = 8
WEIGHT_DMA_PRIORITY = 1
assert MOE_TOPK == 2


def _experts_body(exp_ref, nxt_ref, rows_ref, off_ref, order_ref, nact_ref,
                  xp_hbm, wg_hbm, wu_hbm, wdn_hbm, ya_hbm,
                  xbuf_ref, ybuf_ref, xl_ref, xh_ref, hg_ref, hu_ref, hd_ref, wa_ref, wb_ref, wc_ref, wd_ref,
                  ring_in, ring_dn, gsem, ssem, sem_in, sem_dn, *, rb, n_tok, n_blocks):
    v = pl.program_id(0)
    s = pl.program_id(1)
    nact = nact_ref[0]
    active = v < nact
    slot = lax.rem(v, 2)
    half = rb // 2
    quarter = rb // EXPERT_STEPS
    oc = wc_ref.shape[1]
    dq = ring_in.shape[1]

    def in_copy(e, i):
        src = (wg_hbm, wg_hbm, wu_hbm, wu_hbm)[i % 4]
        q = 2 * (i // 4) + i % 2
        return pltpu.make_async_copy(src.at[e, pl.ds(q * dq, dq), :], ring_in.at[i], sem_in.at[i])

    def dn_copy(e, i):
        col = (i % 2) * 2 * oc + (i // 2) * oc
        return pltpu.make_async_copy(wdn_hbm.at[e, :, pl.ds(col, oc)], ring_dn.at[i], sem_dn.at[i])

    def refill(copy, slots):
        @pl.when(v + 1 < nact)
        def _():
            for i in slots:
                copy(nxt_ref[v], i).start(priority=WEIGHT_DMA_PRIORITY)

    def decode(base, r):
        a = order_ref[base + r]
        k = jnp.where(a >= n_tok, 1, 0)
        return k, a - k * n_tok

    def gather_copy(base, buf, r):
        _, tok = decode(base, r)
        return pltpu.make_async_copy(xp_hbm.at[pl.ds(tok, 1)], xbuf_ref.at[buf, pl.ds(r, 1)], gsem.at[buf])

    def scatter_copy(base, r):
        k, tok = decode(base, r)
        return pltpu.make_async_copy(ybuf_ref.at[pl.ds(r, 1)], ya_hbm.at[k, pl.ds(tok, 1)], ssem)

    def for_rows(lo, hi, fn):
        groups = lax.shift_right_logical(jnp.maximum(hi - lo, 0), ROW_DMA_UNROLL.bit_length() - 1)

        def group(g, carry):
            for u in range(ROW_DMA_UNROLL):
                fn(lo + g * ROW_DMA_UNROLL + u)
            return carry

        def single(r, carry):
            fn(r)
            return carry

        lax.fori_loop(0, groups, group, 0)
        lax.fori_loop(lo + groups * ROW_DMA_UNROLL, hi, single, 0)

    @pl.when(jnp.logical_and(v == 0, s == 0))
    def _():
        for i in range(ring_in.shape[0]):
            in_copy(exp_ref[0], i).start(priority=WEIGHT_DMA_PRIORITY)
        for i in range(ring_dn.shape[0]):
            dn_copy(exp_ref[0], i).start(priority=WEIGHT_DMA_PRIORITY)
        xbuf_ref[...] = jnp.zeros_like(xbuf_ref)
        base = off_ref[0]
        for_rows(0, rows_ref[0], lambda r: gather_copy(base, 0, r).start())

    @pl.when(jnp.logical_and(active, s == 0))
    def _():
        base = off_ref[v]
        for_rows(0, rows_ref[v], lambda r: gather_copy(base, slot, r).wait())

    @pl.when(jnp.logical_and(active, v + 1 < nact))
    def _():
        nxt = jnp.minimum(v + 1, n_blocks - 1)
        base = off_ref[nxt]
        for_rows(s * quarter, jnp.minimum((s + 1) * quarter, rows_ref[nxt]),
                 lambda r: gather_copy(base, 1 - slot, r).start())

    @pl.when(jnp.logical_and(active, jnp.logical_and(s == 2, v >= 1)))
    def _():
        prev = jnp.maximum(v - 1, 0)
        base = off_ref[prev]
        for_rows(0, rows_ref[prev], lambda r: scatter_copy(base, r).wait())

    halves = [(pl.ds(0, half), None), (pl.ds(half, half), rows_ref[v] > half)]

    def for_halves(fn):
        for rows, cond in halves:
            if cond is None:
                fn(rows)
            else:
                pl.when(cond)(functools.partial(fn, rows))

    def load_in(step):
        slots = range(4 * step, 4 * step + 4)
        for i in slots:
            in_copy(exp_ref[v], i).wait()
        for i in slots:
            dst = (wa_ref, wa_ref, wb_ref, wb_ref)[i % 4]
            dst[pl.ds((i % 2) * dq, dq), :] = ring_in[i].astype(BF16)
        refill(in_copy, slots)

    def load_dn(step):
        slots = range(2 * (step - 2), 2 * (step - 2) + 2)
        for i in slots:
            dn_copy(exp_ref[v], i).wait()
        for i in slots:
            (wc_ref, wd_ref)[i % 2][...] = ring_dn[i].astype(BF16)
        refill(dn_copy, slots)

    @pl.when(jnp.logical_and(active, s == 0))
    def _():
        load_in(0)

        def step0(rows):
            lo, hi = _unpack_halves(xbuf_ref[slot, rows, :])
            xl = lo.astype(BF16)
            xl_ref[rows, :] = xl
            xh_ref[rows, :] = hi.astype(BF16)
            hg_ref[rows, :] = jnp.dot(xl, wa_ref[...], preferred_element_type=F32)
            hu_ref[rows, :] = jnp.dot(xl, wb_ref[...], preferred_element_type=F32)
        for_halves(step0)

    @pl.when(jnp.logical_and(active, s == 1))
    def _():
        load_in(1)

        def step1(rows):
            xh = xh_ref[rows, :]
            hg = hg_ref[rows, :] + jnp.dot(xh, wa_ref[...], preferred_element_type=F32)
            hu = hu_ref[rows, :] + jnp.dot(xh, wb_ref[...], preferred_element_type=F32)
            hd_ref[rows, :] = (hg * jax.nn.sigmoid(hg) * hu).astype(BF16)
        for_halves(step1)

    def down(step, cols):
        load_dn(step)

        def step2(rows):
            hd = hd_ref[rows, :]
            y_lo = jnp.dot(hd, wc_ref[...], preferred_element_type=F32)
            y_hi = jnp.dot(hd, wd_ref[...], preferred_element_type=F32)
            ybuf_ref[rows, cols] = _pack_halves(jnp.concatenate([y_lo, y_hi], axis=-1))
        for_halves(step2)

    @pl.when(jnp.logical_and(active, s == 2))
    def _():
        down(2, pl.ds(0, oc))

    @pl.when(jnp.logical_and(active, s == 3))
    def _():
        down(3, pl.ds(oc, oc))
        base = off_ref[v]
        for_rows(0, rows_ref[v], lambda r: scatter_copy(base, r).start())

    @pl.when(jnp.logical_and(v == n_blocks - 1, s == EXPERT_STEPS - 1))
    def _():
        last = jnp.maximum(nact - 1, 0)
        base = off_ref[last]
        for_rows(0, rows_ref[last], lambda r: scatter_copy(base, r).wait())


def moe_experts(xp, w_gate, w_up, w_down, blk_exp, blk_next, blk_rows, blk_off, order, nact, *, n_blocks, rb):
    n_exp, d, ff = w_gate.shape
    n_tok, kw = xp.shape
    assert d == 2 * kw
    oc = kw // 2
    n_in, n_dn = 8, 4
    any_spec = pl.BlockSpec(memory_space=pl.ANY)
    return pl.pallas_call(
        functools.partial(_experts_body, rb=rb, n_tok=n_tok, n_blocks=n_blocks),
        grid_spec=pltpu.PrefetchScalarGridSpec(
            num_scalar_prefetch=6,
            grid=(n_blocks, EXPERT_STEPS),
            in_specs=[any_spec, any_spec, any_spec, any_spec],
            out_specs=any_spec,
            scratch_shapes=[pltpu.VMEM((2, rb, kw), U32), pltpu.VMEM((rb, kw), U32),
                            pltpu.VMEM((rb, kw), BF16), pltpu.VMEM((rb, kw), BF16),
                            pltpu.VMEM((rb, ff), F32), pltpu.VMEM((rb, ff), F32), pltpu.VMEM((rb, ff), BF16),
                            pltpu.VMEM((d // 2, ff), BF16), pltpu.VMEM((d // 2, ff), BF16),
                            pltpu.VMEM((ff, oc), BF16), pltpu.VMEM((ff, oc), BF16),
                            pltpu.VMEM((n_in, d // 4, ff), F32), pltpu.VMEM((n_dn, ff, oc), F32),
                            pltpu.SemaphoreType.DMA((2,)), pltpu.SemaphoreType.DMA(()),
                            pltpu.SemaphoreType.DMA((n_in,)), pltpu.SemaphoreType.DMA((n_dn,))],
        ),
        out_shape=jax.ShapeDtypeStruct((MOE_TOPK, n_tok, kw), U32),
        compiler_params=_params(("arbitrary", "arbitrary")),
        name="moe_experts",
    )(blk_exp, blk_next, blk_rows, blk_off, order, nact, xp, w_gate, w_up, w_down)


def _combine_body(h_ref, meta_ref, g_ref, y0_ref, y1_ref, o_ref):
    meta = meta_ref[...]
    acc_lo = None
    acc_hi = None
    for k, y_ref in enumerate((y0_ref, y1_ref)):
        gate = meta[:, 2 * MOE_TOPK + k:2 * MOE_TOPK + k + 1]
        lo, hi = _unpack_halves(y_ref[0])
        lo, hi = lo * gate, hi * gate
        acc_lo = lo if acc_lo is None else acc_lo + lo
        acc_hi = hi if acc_hi is None else acc_hi + hi
    kh = acc_lo.shape[1]
    h_lo = h_ref[:, :kh] + acc_lo
    h_hi = h_ref[:, kh:] + acc_hi
    ms = (jnp.sum(h_lo * h_lo, axis=-1, keepdims=True) + jnp.sum(h_hi * h_hi, axis=-1, keepdims=True)) / (2 * kh)
    scale = lax.rsqrt(ms + NORM_EPS)
    o_ref[:, :kh] = (h_lo * scale) * g_ref[:, :kh]
    o_ref[:, kh:] = (h_hi * scale) * g_ref[:, kh:]


def moe_combine(h, meta, g, ya, *, tb):
    n, d = h.shape
    kw = ya.shape[2]
    tb = _tile(n, tb)

    def slot(k):
        return pl.BlockSpec((1, tb, kw), lambda i, k=k: (k, i, 0))

    return pl.pallas_call(
        _combine_body,
        grid=(n // tb,),
        in_specs=[
            pl.BlockSpec((tb, d), lambda i: (i, 0)),
            pl.BlockSpec((tb, V7X_LANES), lambda i: (i, 0)),
            pl.BlockSpec((1, d), lambda i: (0, 0)),
            slot(0), slot(1),
        ],
        out_specs=pl.BlockSpec((tb, d), lambda i: (i, 0)),
        out_shape=jax.ShapeDtypeStruct((n, d), F32),
        compiler_params=_params(("parallel",)),
        name="moe_combine",
    )(h, meta, g.reshape(1, d), ya, ya)


def _moe_plan(counts, *, rb, n_blocks):
    n_exp = counts.shape[0]
    per_exp = (counts + rb - 1) // rb
    blk_end = jnp.cumsum(per_exp)
    start = jnp.cumsum(counts) - counts
    nact = blk_end[-1].astype(I32)
    blk = jnp.arange(n_blocks, dtype=I32)
    blk_exp = jnp.minimum(jnp.searchsorted(blk_end, blk, side="right"), n_exp - 1).astype(I32)
    active = blk < nact
    last = jnp.maximum(nact - 1, 0)
    blk_exp = jnp.where(active, blk_exp, blk_exp[last]).astype(I32)
    blk_next = blk_exp[jnp.minimum(blk + 1, last)]
    j = blk - (blk_end[blk_exp] - per_exp[blk_exp])
    blk_off = jnp.where(active, start[blk_exp] + j * rb, 0).astype(I32)
    blk_rows = jnp.where(active, jnp.clip(counts[blk_exp] - j * rb, 0, rb), 0).astype(I32)
    return start.astype(I32), blk_exp, blk_next, blk_rows, blk_off, nact.reshape(1)


def kernel(x, mem, positions, norm_mix_g, w_in, ret_norm_g, gm_ln_g, gm_ln_b, gm_ws, gm_bs, w_out, norm_xa_g, norm_mem_g, xa_wq, xa_wkv, xa_wo, norm_moe_g, router_grp_w, router_grp_b, router_exp_w, router_exp_b, moe_w_gate, moe_w_up, moe_w_down, norm_final_g):
    batch, seq, d = x.shape
    mem_len = mem.shape[1]
    n = batch * seq
    ret_width = ret_norm_g.shape[0]
    gm_width = gm_ln_g.shape[0]
    assert ret_width == gm_width and w_in.shape[1] == 4 * ret_width + 2 * gm_width
    dk = ret_width // RET_HEADS
    n_exp = moe_w_gate.shape[0]
    assert n_exp == MOE_GROUPS * MOE_PER_GROUP and MOE_GROUPS + n_exp <= V7X_LANES

    inv_freq = ROPE_BASE ** (-jnp.arange(0, dk, 2, dtype=F32) / dk)
    ang = positions.astype(F32).reshape(n, 1) * inv_freq
    cos, sin = jnp.cos(ang), jnp.sin(ang)

    x2 = x.reshape(n, d)
    proj, (w_out_b,) = norm_matmul(x2, norm_mix_g, w_in.astype(BF16), tm=512, tn=1024, side=(w_out,))
    ret = retention(proj, cos, sin, ret_norm_g, batch=batch, seq=seq, ret_width=ret_width)
    gm = gmlp(proj, gm_ln_g, gm_ln_b, gm_ws, gm_bs, n_rows=n, gm_width=gm_width,
              u_block=4 * ret_width // gm_width, v_block=4 * ret_width // gm_width + 1)
    h1, (wq_b, wkv_b, wo_b) = matmul_residual([ret, gm], w_out_b, x2, tm=512, tn=1024,
                                              side=(xa_wq, xa_wkv, xa_wo))

    kv, _ = norm_matmul(mem.reshape(batch * mem_len, d), norm_mem_g, wkv_b, tm=512, tn=1024)
    h2 = cross_attention_block(h1, norm_xa_g, wq_b, kv, wo_b, seq=seq, mem_len=mem_len, tm=256)

    pad = V7X_LANES - MOE_GROUPS - n_exp
    wr = jnp.concatenate([router_grp_w, router_exp_w, jnp.zeros((d, pad), F32)], axis=1)
    br = jnp.concatenate([router_grp_b, router_exp_b, jnp.zeros((pad,), F32)]).reshape(1, V7X_LANES)
    wr_hi = wr.astype(BF16)
    wr_lo = (wr - wr_hi.astype(F32)).astype(BF16)
    xp, meta, meta_t, cnt = moe_router(h2, norm_moe_g, jnp.concatenate([wr_hi, wr_lo], axis=1), br, tm=256)

    rb = MOE_ROW_BLOCK
    n_blocks = -(-(n * MOE_TOPK) // rb) + n_exp
    counts = cnt[0, MOE_GROUPS:MOE_GROUPS + n_exp].astype(I32)
    start, blk_exp, blk_next, blk_rows, blk_off, nact = _moe_plan(counts, rb=rb, n_blocks=n_blocks)
    code = (meta_t[MOE_TOPK:2 * MOE_TOPK].astype(I32) * (1 << MOE_CODE_SHIFT)
            + meta_t[0:MOE_TOPK].astype(I32)).reshape(-1)
    order = moe_order(code, start)
    ya = moe_experts(xp, moe_w_gate, moe_w_up, moe_w_down, blk_exp, blk_next, blk_rows, blk_off, order, nact,
                     n_blocks=n_blocks, rb=rb)
    y = moe_combine(h2, meta, norm_final_g, ya, tb=256)
    return y.reshape(batch, seq, d)
```

```python
import functools

import jax
import jax.numpy as jnp
from jax import lax
from jax.experimental import pallas as pl
from jax.experimental.pallas import tpu as pltpu

NORM_EPS = 1e-6
RET_HEADS = 8
ROPE_BASE = 10000.0
GM_GROUPS = 8
GM_CHUNK = 128
XA_HEADS = 4
MOE_GROUPS = 8
MOE_PER_GROUP = 8
MOE_TOPK = 2

V7X_LANES = 128
V7X_VMEM_BYTES = 64 * 1024 * 1024
VMEM_LIMIT_BYTES = 56 * 1024 * 1024

RET_BLOCK = 512
MOE_ROW_BLOCK = 512

F32 = jnp.float32
BF16 = jnp.bfloat16
U32 = jnp.uint32
I32 = jnp.int32


def _params(sem):
    return pltpu.CompilerParams(dimension_semantics=sem, vmem_limit_bytes=VMEM_LIMIT_BYTES)


def _tile(dim, target):
    t = min(dim, target)
    while dim % t:
        t -= V7X_LANES
    assert t > 0, (dim, target)
    return t


def _pack_halves(x_f32):
    k = x_f32.shape[-1] // 2
    bits = lax.bitcast_convert_type(x_f32.astype(BF16).astype(F32), U32)
    return (bits[:, k:] & jnp.uint32(0xFFFF0000)) | (bits[:, :k] >> 16)


def _unpack_halves(w_u32):
    lo = lax.bitcast_convert_type(w_u32 << 16, F32)
    hi = lax.bitcast_convert_type(w_u32 & jnp.uint32(0xFFFF0000), F32)
    return lo, hi


SIDE_CAST_SLABS = 64
BF16_SUBLANES = 16


def _side_slabs(side, steps):
    n = SIDE_CAST_SLABS
    while n > steps or any(a.shape[0] % (BF16_SUBLANES * n) for a in side):
        n //= 2
    assert n >= 1, [a.shape for a in side]
    return n


def _side_specs(side, n_j, slabs):
    return [pl.BlockSpec((a.shape[0] // slabs, a.shape[1]), lambda i, j: (jnp.minimum(i * n_j + j, slabs - 1), 0))
            for a in side]


def _side_cast(in_refs, out_refs, n_j, slabs):
    if not in_refs:
        return

    @pl.when(pl.program_id(0) * n_j + pl.program_id(1) < slabs)
    def _():
        for src, dst in zip(in_refs, out_refs):
            dst[...] = src[...].astype(dst.dtype)


def _side_out_shapes(side):
    return [jax.ShapeDtypeStruct(a.shape, BF16) for a in side]


def _norm_matmul_body(*refs, n_side, n_j, slabs):
    x_ref, g_ref, w_ref = refs[:3]
    side_in = refs[3:3 + n_side]
    o_ref = refs[3 + n_side]
    side_out = refs[4 + n_side:4 + 2 * n_side]
    xn_ref = refs[4 + 2 * n_side]

    @pl.when(pl.program_id(1) == 0)
    def _():
        x = x_ref[...]
        ms = jnp.mean(x * x, axis=-1, keepdims=True)
        xn_ref[...] = ((x * lax.rsqrt(ms + NORM_EPS)) * g_ref[...]).astype(BF16)

    o_ref[...] = jnp.dot(xn_ref[...], w_ref[...], preferred_element_type=F32).astype(o_ref.dtype)
    _side_cast(side_in, side_out, n_j, slabs)


def norm_matmul(x, g, w, *, tm, tn, side=()):
    m, k = x.shape
    n = w.shape[1]
    tm, tn = _tile(m, tm), _tile(n, tn)
    n_j = n // tn
    slabs = _side_slabs(side, (m // tm) * n_j)
    side_specs = _side_specs(side, n_j, slabs)
    out = pl.pallas_call(
        functools.partial(_norm_matmul_body, n_side=len(side), n_j=n_j, slabs=slabs),
        grid=(m // tm, n_j),
        in_specs=[
            pl.BlockSpec((tm, k), lambda i, j: (i, 0)),
            pl.BlockSpec((1, k), lambda i, j: (0, 0)),
            pl.BlockSpec((k, tn), lambda i, j: (0, j)),
        ] + side_specs,
        out_specs=[pl.BlockSpec((tm, tn), lambda i, j: (i, j))] + side_specs,
        out_shape=[jax.ShapeDtypeStruct((m, n), BF16)] + _side_out_shapes(side),
        scratch_shapes=[pltpu.VMEM((tm, k), BF16)],
        compiler_params=_params(("arbitrary", "arbitrary") if side else ("parallel", "arbitrary")),
        name="norm_matmul",
    )(x, g.reshape(1, k), w, *side)
    return out[0], out[1:]


def _rmsnorm_body(x_ref, g_ref, o_ref):
    x = x_ref[...]
    ms = jnp.mean(x * x, axis=-1, keepdims=True)
    o_ref[...] = ((x * lax.rsqrt(ms + NORM_EPS)) * g_ref[...]).astype(o_ref.dtype)


def rmsnorm_bf16(x, g, *, tm):
    m, k = x.shape
    tm = _tile(m, tm)
    return pl.pallas_call(
        _rmsnorm_body,
        grid=(m // tm,),
        in_specs=[pl.BlockSpec((tm, k), lambda i: (i, 0)), pl.BlockSpec((1, k), lambda i: (0, 0))],
        out_specs=pl.BlockSpec((tm, k), lambda i: (i, 0)),
        out_shape=jax.ShapeDtypeStruct((m, k), BF16),
        compiler_params=_params(("parallel",)),
        name="rmsnorm",
    )(x, g.reshape(1, k))


def _matmul_wcast_body(*refs, n_side, n_i, slabs):
    a_ref, w_ref = refs[:2]
    side_in = refs[2:2 + n_side]
    o_ref = refs[2 + n_side]
    side_out = refs[3 + n_side:3 + 2 * n_side]
    wb_ref = refs[3 + 2 * n_side]

    @pl.when(pl.program_id(1) == 0)
    def _():
        wb_ref[...] = w_ref[...].astype(BF16)

    o_ref[...] = jnp.dot(a_ref[...], wb_ref[...], preferred_element_type=F32).astype(o_ref.dtype)
    _side_cast(side_in, side_out, n_i, slabs)


def matmul_wcast(a, w, *, tm, tn, side=()):
    m, k = a.shape
    n = w.shape[1]
    tm, tn = _tile(m, tm), _tile(n, tn)
    n_i = m // tm
    slabs = _side_slabs(side, (n // tn) * n_i)
    side_specs = _side_specs(side, n_i, slabs)
    out = pl.pallas_call(
        functools.partial(_matmul_wcast_body, n_side=len(side), n_i=n_i, slabs=slabs),
        grid=(n // tn, n_i),
        in_specs=[pl.BlockSpec((tm, k), lambda j, i: (i, 0)), pl.BlockSpec((k, tn), lambda j, i: (0, j))] + side_specs,
        out_specs=[pl.BlockSpec((tm, tn), lambda j, i: (i, j))] + side_specs,
        out_shape=[jax.ShapeDtypeStruct((m, n), BF16)] + _side_out_shapes(side),
        scratch_shapes=[pltpu.VMEM((k, tn), BF16)],
        compiler_params=_params(("arbitrary", "arbitrary")),
        name="matmul_wcast",
    )(a, w, *side)
    return out[0], out[1:]


RET_HEADS_PER_STEP = 4


def _retention_body(lg_ref, q_ref, k_ref, v_ref, g_ref, cos_ref, sin_ref, gn_ref, o_ref, state_ref, dec_ref,
                    *, blk, dk, hps):
    c = pl.program_id(2)
    first_of_pair = jnp.logical_and(pl.program_id(1) == 0, c == 0)
    half = dk // 2
    cos = cos_ref[...]
    sin = sin_ref[...]
    pos = lax.broadcasted_iota(I32, (blk, 1), 0).astype(F32)

    def rot(t):
        t1, t2 = t[:, :half], t[:, half:]
        return jnp.concatenate([t1 * cos - t2 * sin, t1 * sin + t2 * cos], axis=-1)

    @pl.when(c == 0)
    def _():
        state_ref[...] = jnp.zeros_like(state_ref)

    for j in range(hps):
        lg = lg_ref[pl.program_id(0) * hps + j]
        cols = slice(j * dk, (j + 1) * dk)

        @pl.when(first_of_pair)
        def _(j=j, lg=lg):
            ri = lax.broadcasted_iota(I32, (blk, blk), 0)
            ci = lax.broadcasted_iota(I32, (blk, blk), 1)
            diff = (ri - ci).astype(F32)
            dec_ref[j] = jnp.where(diff >= 0.0, jnp.exp(jnp.maximum(diff, 0.0) * lg), 0.0)

        qr = rot(q_ref[:, cols].astype(F32))
        kr = rot(k_ref[:, cols].astype(F32)) * (dk ** -0.5)
        v = v_ref[:, cols]

        q_dec = jnp.exp((pos + 1.0) * lg)
        k_dec = jnp.exp((blk - 1.0 - pos) * lg)
        blk_dec = jnp.exp(jnp.full((1, dk), blk * lg, F32))

        s = lax.dot_general(qr.astype(BF16), kr.astype(BF16), (((1,), (1,)), ((), ())), preferred_element_type=F32)
        inner = jnp.dot((s * dec_ref[j]).astype(BF16), v, preferred_element_type=F32)

        state = state_ref[j]
        cross = jnp.dot((qr * q_dec).astype(BF16), state.astype(BF16), preferred_element_type=F32)
        kd_t = jnp.transpose(kr * k_dec).astype(BF16)
        state_ref[j] = state * blk_dec + jnp.dot(kd_t, v, preferred_element_type=F32)

        out = inner + cross
        mu = jnp.mean(out, axis=-1, keepdims=True)
        cen = out - mu
        var = jnp.mean(cen * cen, axis=-1, keepdims=True)
        y = cen * lax.rsqrt(var + NORM_EPS) * gn_ref[:, cols]
        gate = g_ref[:, cols].astype(F32)
        o_ref[:, cols] = (y * (gate * jax.nn.sigmoid(gate))).astype(o_ref.dtype)


def retention(proj, cos, sin, ret_norm_g, *, batch, seq, ret_width):
    heads = RET_HEADS
    hps = min(RET_HEADS_PER_STEP, heads)
    assert heads % hps == 0
    dk = ret_width // heads
    blk = min(RET_BLOCK, seq)
    nblk = seq // blk
    hb = heads // hps
    log_gamma = jnp.log(1.0 - jnp.exp2(-5.0 - jnp.arange(heads, dtype=F32)))

    def col(seg):
        return pl.BlockSpec((blk, hps * dk), lambda h, b, c, lg, seg=seg: (b * nblk + c, seg * hb + h))

    rowspec = pl.BlockSpec((blk, dk // 2), lambda h, b, c, lg: (b * nblk + c, 0))
    return pl.pallas_call(
        functools.partial(_retention_body, blk=blk, dk=dk, hps=hps),
        grid_spec=pltpu.PrefetchScalarGridSpec(
            num_scalar_prefetch=1,
            grid=(heads // hps, batch, nblk),
            in_specs=[col(0), col(1), col(2), col(3), rowspec, rowspec,
                      pl.BlockSpec((1, hps * dk), lambda h, b, c, lg: (0, h))],
            out_specs=pl.BlockSpec((blk, hps * dk), lambda h, b, c, lg: (b * nblk + c, h)),
            scratch_shapes=[pltpu.VMEM((hps, dk, dk), F32), pltpu.VMEM((hps, blk, blk), F32)],
        ),
        out_shape=jax.ShapeDtypeStruct((batch * seq, ret_width), BF16),
        compiler_params=_params(("arbitrary", "arbitrary", "arbitrary")),
        name="retention",
    )(log_gamma, proj, proj, proj, proj, cos, sin, ret_norm_g.reshape(1, ret_width))


def _gmlp_body(u_ref, v_ref, lng_ref, lnb_ref, ws_ref, bst_ref, o_ref, vn_ref, *, rows, groups, cg, chunk):
    v = jax.nn.gelu(v_ref[...].astype(F32))
    mu = jnp.mean(v, axis=-1, keepdims=True)
    cen = v - mu
    var = jnp.mean(cen * cen, axis=-1, keepdims=True)
    vn_ref[...] = (cen * lax.rsqrt(var + NORM_EPS) * lng_ref[...] + lnb_ref[...]).astype(BF16)

    ri = lax.broadcasted_iota(I32, (chunk, chunk), 0)
    ci = lax.broadcasted_iota(I32, (chunk, chunk), 1)
    causal = ri >= ci
    for g in range(groups):
        w = jnp.where(causal, ws_ref[g], 0.0).astype(BF16)
        bias = bst_ref[:, g:g + 1]
        cols = slice(g * cg, (g + 1) * cg)
        for t in range(rows // chunk):
            rws = slice(t * chunk, (t + 1) * chunk)
            sp = jnp.dot(w, vn_ref[rws, cols], preferred_element_type=F32) + bias
            u = jax.nn.gelu(u_ref[rws, cols].astype(F32))
            o_ref[rws, cols] = (u * sp).astype(o_ref.dtype)


def gmlp(proj, ln_g, ln_b, ws, bs, *, n_rows, gm_width, u_block, v_block):
    groups, chunk = GM_GROUPS, GM_CHUNK
    cg = gm_width // groups
    rows = 2 * chunk
    return pl.pallas_call(
        functools.partial(_gmlp_body, rows=rows, groups=groups, cg=cg, chunk=chunk),
        grid=(n_rows // rows,),
        in_specs=[
            pl.BlockSpec((rows, gm_width), lambda i: (i, u_block)),
            pl.BlockSpec((rows, gm_width), lambda i: (i, v_block)),
            pl.BlockSpec((1, gm_width), lambda i: (0, 0)),
            pl.BlockSpec((1, gm_width), lambda i: (0, 0)),
            pl.BlockSpec((groups, chunk, chunk), lambda i: (0, 0, 0)),
            pl.BlockSpec((chunk, groups), lambda i: (0, 0)),
        ],
        out_specs=pl.BlockSpec((rows, gm_width), lambda i: (i, 0)),
        out_shape=jax.ShapeDtypeStruct((n_rows, gm_width), BF16),
        scratch_shapes=[pltpu.VMEM((rows, gm_width), BF16)],
        compiler_params=_params(("parallel",)),
        name="gmlp",
    )(proj, proj, ln_g.reshape(1, gm_width), ln_b.reshape(1, gm_width), ws, bs.T)


def _matmul_residual_body(*refs, n_parts, n_side, n_j, slabs):
    a_refs, w_refs = refs[:n_parts], refs[n_parts:2 * n_parts]
    r_ref = refs[2 * n_parts]
    side_in = refs[2 * n_parts + 1:2 * n_parts + 1 + n_side]
    o_ref = refs[2 * n_parts + 1 + n_side]
    side_out = refs[2 * n_parts + 2 + n_side:]
    acc = r_ref[...]
    for a_ref, w_ref in zip(a_refs, w_refs):
        acc = acc + jnp.dot(a_ref[...], w_ref[...], preferred_element_type=F32)
    o_ref[...] = acc
    _side_cast(side_in, side_out, n_j, slabs)


def matmul_residual(parts, w, res, *, tm, tn, side=()):
    m, n = res.shape
    tm, tn = _tile(m, tm), _tile(n, tn)
    n_j = n // tn
    slabs = _side_slabs(side, (m // tm) * n_j)
    kp = parts[0].shape[1]
    n_parts = len(parts)
    a_specs = [pl.BlockSpec((tm, kp), lambda i, j: (i, 0)) for _ in parts]
    w_specs = [pl.BlockSpec((kp, tn), lambda i, j, p=p: (p, j)) for p in range(n_parts)]
    side_specs = _side_specs(side, n_j, slabs)
    out = pl.pallas_call(
        functools.partial(_matmul_residual_body, n_parts=n_parts, n_side=len(side), n_j=n_j, slabs=slabs),
        grid=(m // tm, n_j),
        in_specs=a_specs + w_specs + [pl.BlockSpec((tm, tn), lambda i, j: (i, j))] + side_specs,
        out_specs=[pl.BlockSpec((tm, tn), lambda i, j: (i, j))] + side_specs,
        out_shape=[jax.ShapeDtypeStruct((m, n), F32)] + _side_out_shapes(side),
        compiler_params=_params(("arbitrary", "arbitrary") if side else ("parallel", "parallel")),
        name="matmul_residual",
    )(*parts, *([w] * n_parts), res, *side)
    return out[0], out[1:]


def _cross_attention_body(h_ref, g_ref, wq_ref, kv_ref, wo_ref, o_ref, *, heads, dh):
    width = heads * dh
    x = h_ref[...]
    ms = jnp.mean(x * x, axis=-1, keepdims=True)
    xn = ((x * lax.rsqrt(ms + NORM_EPS)) * g_ref[...]).astype(BF16)
    q = jnp.dot(xn, wq_ref[...], preferred_element_type=F32).astype(BF16)
    outs = []
    for h in range(heads):
        k = kv_ref[:, h * dh:(h + 1) * dh]
        v = kv_ref[:, width + h * dh:width + (h + 1) * dh]
        s = lax.dot_general(q[:, h * dh:(h + 1) * dh], k, (((1,), (1,)), ((), ())),
                            preferred_element_type=F32) * (dh ** -0.5)
        e = jnp.exp(s - jnp.max(s, axis=-1, keepdims=True))
        p = e / jnp.sum(e, axis=-1, keepdims=True)
        outs.append(jnp.dot(p.astype(BF16), v, preferred_element_type=F32).astype(BF16))
    o = jnp.concatenate(outs, axis=-1)
    o_ref[...] = x + jnp.dot(o, wo_ref[...], preferred_element_type=F32)


def cross_attention_block(h, g, wq, kv, wo, *, seq, mem_len, tm):
    n, d = h.shape
    heads = XA_HEADS
    width = wq.shape[1]
    dh = width // heads
    tm = _tile(seq, tm)
    per_batch = seq // tm

    def resident(shape):
        return pl.BlockSpec(shape, lambda i: (0, 0), pipeline_mode=pl.Buffered(1))

    return pl.pallas_call(
        functools.partial(_cross_attention_body, heads=heads, dh=dh),
        grid=(n // tm,),
        in_specs=[
            pl.BlockSpec((tm, d), lambda i: (i, 0)),
            resident((1, d)),
            resident((d, width)),
            pl.BlockSpec((mem_len, 2 * width), lambda i: (i // per_batch, 0)),
            resident((width, d)),
        ],
        out_specs=pl.BlockSpec((tm, d), lambda i: (i, 0)),
        out_shape=jax.ShapeDtypeStruct((n, d), F32),
        compiler_params=_params(("parallel",)),
        name="cross_attention",
    )(h, g.reshape(1, d), wq, kv, wo)


def _router_body(h_ref, g_ref, wr_ref, br_ref, xp_ref, meta_ref, meta_t_ref, cnt_ref, base_ref, *, tm, n_grp, per):
    @pl.when(pl.program_id(0) == 0)
    def _():
        base_ref[...] = jnp.zeros_like(base_ref)

    x = h_ref[...]
    ms = jnp.mean(x * x, axis=-1, keepdims=True)
    xn = (x * lax.rsqrt(ms + NORM_EPS)) * g_ref[...]
    xp_ref[...] = _pack_halves(xn)

    x_hi = xn.astype(BF16)
    x_lo = (xn - x_hi.astype(F32)).astype(BF16)
    both = jnp.dot(x_hi, wr_ref[...], preferred_element_type=F32)
    corr = jnp.dot(x_lo, wr_ref[:, :V7X_LANES], preferred_element_type=F32)
    logits = both[:, :V7X_LANES] + (both[:, V7X_LANES:] + corr) + br_ref[...]
    lane = lax.broadcasted_iota(I32, logits.shape, 1)
    neg = jnp.float32(-1e30)
    big = jnp.int32(V7X_LANES)

    gl = jnp.where(lane < n_grp, logits, neg)
    gmax = jnp.max(gl, axis=-1, keepdims=True)
    gidx = jnp.min(jnp.where(gl == gmax, lane, big), axis=-1, keepdims=True)
    grp_gate = 1.0 / jnp.sum(jnp.exp(gl - gmax), axis=-1, keepdims=True)

    lo = n_grp + gidx * per
    el = jnp.where((lane >= lo) & (lane < lo + per), logits, neg)
    v1 = jnp.max(el, axis=-1, keepdims=True)
    i1 = jnp.min(jnp.where(el == v1, lane, big), axis=-1, keepdims=True)
    el2 = jnp.where(lane == i1, neg, el)
    v2 = jnp.max(el2, axis=-1, keepdims=True)
    i2 = jnp.min(jnp.where(el2 == v2, lane, big), axis=-1, keepdims=True)
    t = jnp.exp(v2 - v1)
    den = 1.0 + t
    g1 = grp_gate / den
    g2 = grp_gate * (t / den)

    oh1 = jnp.where(lane == i1, 1.0, 0.0)
    oh2 = jnp.where(lane == i2, 1.0, 0.0)
    ri = lax.broadcasted_iota(I32, (tm, tm), 0)
    ci = lax.broadcasted_iota(I32, (tm, tm), 1)
    lower = jnp.where(ri > ci, 1.0, 0.0).astype(BF16)
    pre1 = jnp.dot(lower, oh1.astype(BF16), preferred_element_type=F32)
    pre2 = jnp.dot(lower, oh2.astype(BF16), preferred_element_type=F32)
    cnt1 = jnp.sum(oh1, axis=0, keepdims=True)
    cnt2 = jnp.sum(oh2, axis=0, keepdims=True)
    base = base_ref[...]
    rank1 = jnp.sum(oh1 * (pre1 + base), axis=-1, keepdims=True)
    rank2 = jnp.sum(oh2 * (pre2 + base + cnt1), axis=-1, keepdims=True)
    total = base + cnt1 + cnt2
    base_ref[...] = total
    cnt_ref[...] = total

    e1 = (i1 - n_grp).astype(F32)
    e2 = (i2 - n_grp).astype(F32)
    meta = jnp.zeros(logits.shape, F32)
    for idx, val in enumerate((e1, e2, rank1, rank2, g1, g2)):
        meta = jnp.where(lane == idx, val, meta)
    meta_ref[...] = meta
    meta_t_ref[...] = jnp.transpose(meta)[:META_ROWS, :]


META_ROWS = 8


def moe_router(h, g, wr, br, *, tm):
    n, d = h.shape
    tm = min(tm, n)
    return pl.pallas_call(
        functools.partial(_router_body, tm=tm, n_grp=MOE_GROUPS, per=MOE_PER_GROUP),
        grid=(n // tm,),
        in_specs=[
            pl.BlockSpec((tm, d), lambda i: (i, 0)),
            pl.BlockSpec((1, d), lambda i: (0, 0)),
            pl.BlockSpec((d, 2 * V7X_LANES), lambda i: (0, 0)),
            pl.BlockSpec((1, V7X_LANES), lambda i: (0, 0)),
        ],
        out_specs=[
            pl.BlockSpec((tm, d // 2), lambda i: (i, 0)),
            pl.BlockSpec((tm, V7X_LANES), lambda i: (i, 0)),
            pl.BlockSpec((META_ROWS, tm), lambda i: (0, i)),
            pl.BlockSpec((1, V7X_LANES), lambda i: (0, 0)),
        ],
        out_shape=[
            jax.ShapeDtypeStruct((n, d // 2), U32),
            jax.ShapeDtypeStruct((n, V7X_LANES), F32),
            jax.ShapeDtypeStruct((META_ROWS, n), F32),
            jax.ShapeDtypeStruct((1, V7X_LANES), F32),
        ],
        scratch_shapes=[pltpu.VMEM((1, V7X_LANES), F32)],
        compiler_params=_params(("arbitrary",)),
        name="moe_router",
    )(h, g.reshape(1, d), wr, br)


MOE_CODE_SHIFT = 6
assert MOE_GROUPS * MOE_PER_GROUP == 1 << MOE_CODE_SHIFT


def _order_body(code_ref, start_ref, order_ref):
    def body(a, carry):
        code = code_ref[a]
        order_ref[start_ref[code & ((1 << MOE_CODE_SHIFT) - 1)] + (code >> MOE_CODE_SHIFT)] = a
        return carry

    lax.fori_loop(0, code_ref.shape[0], body, 0, unroll=8)


def moe_order(code, start):
    return pl.pallas_call(
        _order_body,
        grid_spec=pltpu.PrefetchScalarGridSpec(
            num_scalar_prefetch=2,
            grid=(1,),
            in_specs=[],
            out_specs=pl.BlockSpec(memory_space=pltpu.SMEM),
        ),
        out_shape=jax.ShapeDtypeStruct(code.shape, I32),
        compiler_params=_params(("arbitrary",)),
        name="moe_order",
    )(code, start)


EXPERT_STEPS = 4
ROW_DMA_UNROLL = 8
WEIGHT_DMA_PRIORITY = 1
assert MOE_TOPK == 2


def _experts_body(exp_ref, nxt_ref, rows_ref, off_ref, order_ref, nact_ref,
                  xp_hbm, wg_hbm, wu_hbm, wdn_hbm, ya_hbm,
                  xbuf_ref, ybuf_ref, xl_ref, xh_ref, hg_ref, hu_ref, hd_ref, wa_ref, wb_ref, wc_ref, wd_ref,
                  ring_in, ring_dn, gsem, ssem, sem_in, sem_dn, *, rb, n_tok, n_blocks):
    v = pl.program_id(0)
    s = pl.program_id(1)
    nact = nact_ref[0]
    active = v < nact
    slot = lax.rem(v, 2)
    half = rb // 2
    quarter = rb // EXPERT_STEPS
    oc = wc_ref.shape[1]
    dq = ring_in.shape[1]

    def in_copy(e, i):
        src = (wg_hbm, wg_hbm, wu_hbm, wu_hbm)[i % 4]
        q = 2 * (i // 4) + i % 2
        return pltpu.make_async_copy(src.at[e, pl.ds(q * dq, dq), :], ring_in.at[i], sem_in.at[i])

    def dn_copy(e, i):
        col = (i % 2) * 2 * oc + (i // 2) * oc
        return pltpu.make_async_copy(wdn_hbm.at[e, :, pl.ds(col, oc)], ring_dn.at[i], sem_dn.at[i])

    def refill(copy, slots):
        @pl.when(v + 1 < nact)
        def _():
            for i in slots:
                copy(nxt_ref[v], i).start(priority=WEIGHT_DMA_PRIORITY)

    def decode(base, r):
        a = order_ref[base + r]
        k = jnp.where(a >= n_tok, 1, 0)
        return k, a - k * n_tok

    def gather_copy(base, buf, r):
        _, tok = decode(base, r)
        return pltpu.make_async_copy(xp_hbm.at[pl.ds(tok, 1)], xbuf_ref.at[buf, pl.ds(r, 1)], gsem.at[buf])

    def scatter_copy(base, r):
        k, tok = decode(base, r)
        return pltpu.make_async_copy(ybuf_ref.at[pl.ds(r, 1)], ya_hbm.at[k, pl.ds(tok, 1)], ssem)

    def for_rows(lo, hi, fn):
        groups = lax.shift_right_logical(jnp.maximum(hi - lo, 0), ROW_DMA_UNROLL.bit_length() - 1)

        def group(g, carry):
            for u in range(ROW_DMA_UNROLL):
                fn(lo + g * ROW_DMA_UNROLL + u)
            return carry

        def single(r, carry):
            fn(r)
            return carry

        lax.fori_loop(0, groups, group, 0)
        lax.fori_loop(lo + groups * ROW_DMA_UNROLL, hi, single, 0)

    @pl.when(jnp.logical_and(v == 0, s == 0))
    def _():
        for i in range(ring_in.shape[0]):
            in_copy(exp_ref[0], i).start(priority=WEIGHT_DMA_PRIORITY)
        for i in range(ring_dn.shape[0]):
            dn_copy(exp_ref[0], i).start(priority=WEIGHT_DMA_PRIORITY)
        xbuf_ref[...] = jnp.zeros_like(xbuf_ref)
        base = off_ref[0]
        for_rows(0, rows_ref[0], lambda r: gather_copy(base, 0, r).start())

    @pl.when(jnp.logical_and(active, s == 0))
    def _():
        base = off_ref[v]
        for_rows(0, rows_ref[v], lambda r: gather_copy(base, slot, r).wait())

    @pl.when(jnp.logical_and(active, v + 1 < nact))
    def _():
        nxt = jnp.minimum(v + 1, n_blocks - 1)
        base = off_ref[nxt]
        for_rows(s * quarter, jnp.minimum((s + 1) * quarter, rows_ref[nxt]),
                 lambda r: gather_copy(base, 1 - slot, r).start())

    @pl.when(jnp.logical_and(active, jnp.logical_and(s == 2, v >= 1)))
    def _():
        prev = jnp.maximum(v - 1, 0)
        base = off_ref[prev]
        for_rows(0, rows_ref[prev], lambda r: scatter_copy(base, r).wait())

    halves = [(pl.ds(0, half), None), (pl.ds(half, half), rows_ref[v] > half)]

    def for_halves(fn):
        for rows, cond in halves:
            if cond is None:
                fn(rows)
            else:
                pl.when(cond)(functools.partial(fn, rows))

    def load_in(step):
        slots = range(4 * step, 4 * step + 4)
        for i in slots:
            in_copy(exp_ref[v], i).wait()
        for i in slots:
            dst = (wa_ref, wa_ref, wb_ref, wb_ref)[i % 4]
            dst[pl.ds((i % 2) * dq, dq), :] = ring_in[i].astype(BF16)
        refill(in_copy, slots)

    def load_dn(step):
        slots = range(2 * (step - 2), 2 * (step - 2) + 2)
        for i in slots:
            dn_copy(exp_ref[v], i).wait()
        for i in slots:
            (wc_ref, wd_ref)[i % 2][...] = ring_dn[i].astype(BF16)
        refill(dn_copy, slots)

    @pl.when(jnp.logical_and(active, s == 0))
    def _():
        load_in(0)

        def step0(rows):
            lo, hi = _unpack_halves(xbuf_ref[slot, rows, :])
            xl = lo.astype(BF16)
            xl_ref[rows, :] = xl
            xh_ref[rows, :] = hi.astype(BF16)
            hg_ref[rows, :] = jnp.dot(xl, wa_ref[...], preferred_element_type=F32)
            hu_ref[rows, :] = jnp.dot(xl, wb_ref[...], preferred_element_type=F32)
        for_halves(step0)

    @pl.when(jnp.logical_and(active, s == 1))
    def _():
        load_in(1)

        def step1(rows):
            xh = xh_ref[rows, :]
            hg = hg_ref[rows, :] + jnp.dot(xh, wa_ref[...], preferred_element_type=F32)
            hu = hu_ref[rows, :] + jnp.dot(xh, wb_ref[...], preferred_element_type=F32)
            hd_ref[rows, :] = (hg * jax.nn.sigmoid(hg) * hu).astype(BF16)
        for_halves(step1)

    def down(step, cols):
        load_dn(step)

        def step2(rows):
            hd = hd_ref[rows, :]
            y_lo = jnp.dot(hd, wc_ref[...], preferred_element_type=F32)
            y_hi = jnp.dot(hd, wd_ref[...], preferred_element_type=F32)
            ybuf_ref[rows, cols] = _pack_halves(jnp.concatenate([y_lo, y_hi], axis=-1))
        for_halves(step2)

    @pl.when(jnp.logical_and(active, s == 2))
    def _():
        down(2, pl.ds(0, oc))

    @pl.when(jnp.logical_and(active, s == 3))
    def _():
        down(3, pl.ds(oc, oc))
        base = off_ref[v]
        for_rows(0, rows_ref[v], lambda r: scatter_copy(base, r).start())

    @pl.when(jnp.logical_and(v == n_blocks - 1, s == EXPERT_STEPS - 1))
    def _():
        last = jnp.maximum(nact - 1, 0)
        base = off_ref[last]
        for_rows(0, rows_ref[last], lambda r: scatter_copy(base, r).wait())


def moe_experts(xp, w_gate, w_up, w_down, blk_exp, blk_next, blk_rows, blk_off, order, nact, *, n_blocks, rb):
    n_exp, d, ff = w_gate.shape
    n_tok, kw = xp.shape
    assert d == 2 * kw
    oc = kw // 2
    n_in, n_dn = 8, 4
    any_spec = pl.BlockSpec(memory_space=pl.ANY)
    return pl.pallas_call(
        functools.partial(_experts_body, rb=rb, n_tok=n_tok, n_blocks=n_blocks),
        grid_spec=pltpu.PrefetchScalarGridSpec(
            num_scalar_prefetch=6,
            grid=(n_blocks, EXPERT_STEPS),
            in_specs=[any_spec, any_spec, any_spec, any_spec],
            out_specs=any_spec,
            scratch_shapes=[pltpu.VMEM((2, rb, kw), U32), pltpu.VMEM((rb, kw), U32),
                            pltpu.VMEM((rb, kw), BF16), pltpu.VMEM((rb, kw), BF16),
                            pltpu.VMEM((rb, ff), F32), pltpu.VMEM((rb, ff), F32), pltpu.VMEM((rb, ff), BF16),
                            pltpu.VMEM((d // 2, ff), BF16), pltpu.VMEM((d // 2, ff), BF16),
                            pltpu.VMEM((ff, oc), BF16), pltpu.VMEM((ff, oc), BF16),
                            pltpu.VMEM((n_in, d // 4, ff), F32), pltpu.VMEM((n_dn, ff, oc), F32),
                            pltpu.SemaphoreType.DMA((2,)), pltpu.SemaphoreType.DMA(()),
                            pltpu.SemaphoreType.DMA((n_in,)), pltpu.SemaphoreType.DMA((n_dn,))],
        ),
        out_shape=jax.ShapeDtypeStruct((MOE_TOPK, n_tok, kw), U32),
        compiler_params=_params(("arbitrary", "arbitrary")),
        name="moe_experts",
    )(blk_exp, blk_next, blk_rows, blk_off, order, nact, xp, w_gate, w_up, w_down)


def _combine_body(h_ref, meta_ref, g_ref, y0_ref, y1_ref, o_ref):
    meta = meta_ref[...]
    acc_lo = None
    acc_hi = None
    for k, y_ref in enumerate((y0_ref, y1_ref)):
        gate = meta[:, 2 * MOE_TOPK + k:2 * MOE_TOPK + k + 1]
        lo, hi = _unpack_halves(y_ref[0])
        lo, hi = lo * gate, hi * gate
        acc_lo = lo if acc_lo is None else acc_lo + lo
        acc_hi = hi if acc_hi is None else acc_hi + hi
    kh = acc_lo.shape[1]
    h_lo = h_ref[:, :kh] + acc_lo
    h_hi = h_ref[:, kh:] + acc_hi
    ms = (jnp.sum(h_lo * h_lo, axis=-1, keepdims=True) + jnp.sum(h_hi * h_hi, axis=-1, keepdims=True)) / (2 * kh)
    scale = lax.rsqrt(ms + NORM_EPS)
    o_ref[:, :kh] = (h_lo * scale) * g_ref[:, :kh]
    o_ref[:, kh:] = (h_hi * scale) * g_ref[:, kh:]


def moe_combine(h, meta, g, ya, *, tb):
    n, d = h.shape
    kw = ya.shape[2]
    tb = _tile(n, tb)

    def slot(k):
        return pl.BlockSpec((1, tb, kw), lambda i, k=k: (k, i, 0))

    return pl.pallas_call(
        _combine_body,
        grid=(n // tb,),
        in_specs=[
            pl.BlockSpec((tb, d), lambda i: (i, 0)),
            pl.BlockSpec((tb, V7X_LANES), lambda i: (i, 0)),
            pl.BlockSpec((1, d), lambda i: (0, 0)),
            slot(0), slot(1),
        ],
        out_specs=pl.BlockSpec((tb, d), lambda i: (i, 0)),
        out_shape=jax.ShapeDtypeStruct((n, d), F32),
        compiler_params=_params(("parallel",)),
        name="moe_combine",
    )(h, meta, g.reshape(1, d), ya, ya)


def _moe_plan(counts, *, rb, n_blocks):
    n_exp = counts.shape[0]
    per_exp = (counts + rb - 1) // rb
    blk_end = jnp.cumsum(per_exp)
    start = jnp.cumsum(counts) - counts
    nact = blk_end[-1].astype(I32)
    blk = jnp.arange(n_blocks, dtype=I32)
    blk_exp = jnp.minimum(jnp.searchsorted(blk_end, blk, side="right"), n_exp - 1).astype(I32)
    active = blk < nact
    last = jnp.maximum(nact - 1, 0)
    blk_exp = jnp.where(active, blk_exp, blk_exp[last]).astype(I32)
    blk_next = blk_exp[jnp.minimum(blk + 1, last)]
    j = blk - (blk_end[blk_exp] - per_exp[blk_exp])
    blk_off = jnp.where(active, start[blk_exp] + j * rb, 0).astype(I32)
    blk_rows = jnp.where(active, jnp.clip(counts[blk_exp] - j * rb, 0, rb), 0).astype(I32)
    return start.astype(I32), blk_exp, blk_next, blk_rows, blk_off, nact.reshape(1)


def kernel(x, mem, positions, norm_mix_g, w_in, ret_norm_g, gm_ln_g, gm_ln_b, gm_ws, gm_bs, w_out, norm_xa_g, norm_mem_g, xa_wq, xa_wkv, xa_wo, norm_moe_g, router_grp_w, router_grp_b, router_exp_w, router_exp_b, moe_w_gate, moe_w_up, moe_w_down, norm_final_g):
    batch, seq, d = x.shape
    mem_len = mem.shape[1]
    n = batch * seq
    ret_width = ret_norm_g.shape[0]
    gm_width = gm_ln_g.shape[0]
    assert ret_width == gm_width and w_in.shape[1] == 4 * ret_width + 2 * gm_width
    dk = ret_width // RET_HEADS
    n_exp = moe_w_gate.shape[0]
    assert n_exp == MOE_GROUPS * MOE_PER_GROUP and MOE_GROUPS + n_exp <= V7X_LANES

    inv_freq = ROPE_BASE ** (-jnp.arange(0, dk, 2, dtype=F32) / dk)
    ang = positions.astype(F32).reshape(n, 1) * inv_freq
    cos, sin = jnp.cos(ang), jnp.sin(ang)

    x2 = x.reshape(n, d)
    xn = rmsnorm_bf16(x2, norm_mix_g, tm=512)
    proj, (w_out_b,) = matmul_wcast(xn, w_in, tm=1024, tn=512, side=(w_out,))
    ret = retention(proj, cos, sin, ret_norm_g, batch=batch, seq=seq, ret_width=ret_width)
    gm = gmlp(proj, gm_ln_g, gm_ln_b, gm_ws, gm_bs, n_rows=n, gm_width=gm_width,
              u_block=4 * ret_width // gm_width, v_block=4 * ret_width // gm_width + 1)
    h1, (wq_b, wkv_b, wo_b) = matmul_residual([ret, gm], w_out_b, x2, tm=512, tn=1024,
                                              side=(xa_wq, xa_wkv, xa_wo))

    kv, _ = norm_matmul(mem.reshape(batch * mem_len, d), norm_mem_g, wkv_b, tm=512, tn=1024)
    h2 = cross_attention_block(h1, norm_xa_g, wq_b, kv, wo_b, seq=seq, mem_len=mem_len, tm=256)

    pad = V7X_LANES - MOE_GROUPS - n_exp
    wr = jnp.concatenate([router_grp_w, router_exp_w, jnp.zeros((d, pad), F32)], axis=1)
    br = jnp.concatenate([router_grp_b, router_exp_b, jnp.zeros((pad,), F32)]).reshape(1, V7X_LANES)
    wr_hi = wr.astype(BF16)
    wr_lo = (wr - wr_hi.astype(F32)).astype(BF16)
    xp, meta, meta_t, cnt = moe_router(h2, norm_moe_g, jnp.concatenate([wr_hi, wr_lo], axis=1), br, tm=256)

    rb = MOE_ROW_BLOCK
    n_blocks = -(-(n * MOE_TOPK) // rb) + n_exp
    counts = cnt[0, MOE_GROUPS:MOE_GROUPS + n_exp].astype(I32)
    start, blk_exp, blk_next, blk_rows, blk_off, nact = _moe_plan(counts, rb=rb, n_blocks=n_blocks)
    code = (meta_t[MOE_TOPK:2 * MOE_TOPK].astype(I32) * (1 << MOE_CODE_SHIFT)
            + meta_t[0:MOE_TOPK].astype(I32)).reshape(-1)
    order = moe_order(code, start)
    ya = moe_experts(xp, moe_w_gate, moe_w_up, moe_w_down, blk_exp, blk_next, blk_rows, blk_off, order, nact,
                     n_blocks=n_blocks, rb=rb)
    y = moe_combine(h2, meta, norm_final_g, ya, tb=256)
    return y.reshape(batch, seq, d)
```

```python
import functools

import jax
import jax.numpy as jnp
from jax import lax
from jax.experimental import pallas as pl
from jax.experimental.pallas import tpu as pltpu

NORM_EPS = 1e-6
RET_HEADS = 8
ROPE_BASE = 10000.0
GM_GROUPS = 8
GM_CHUNK = 128
XA_HEADS = 4
MOE_GROUPS = 8
MOE_PER_GROUP = 8
MOE_TOPK = 2

V7X_LANES = 128
V7X_VMEM_BYTES = 64 * 1024 * 1024
VMEM_LIMIT_BYTES = 56 * 1024 * 1024
EXPERTS_VMEM_LIMIT_BYTES = 58 * 1024 * 1024

RET_BLOCK = 512
MOE_ROW_BLOCK = 512

F32 = jnp.float32
BF16 = jnp.bfloat16
U32 = jnp.uint32
I32 = jnp.int32


def _params(sem, vmem_limit_bytes=VMEM_LIMIT_BYTES):
    return pltpu.CompilerParams(dimension_semantics=sem, vmem_limit_bytes=vmem_limit_bytes)


def _tile(dim, target):
    t = min(dim, target)
    while dim % t:
        t -= V7X_LANES
    assert t > 0, (dim, target)
    return t


def _pack_halves(x_f32):
    k = x_f32.shape[-1] // 2
    bits = lax.bitcast_convert_type(x_f32.astype(BF16).astype(F32), U32)
    return (bits[:, k:] & jnp.uint32(0xFFFF0000)) | (bits[:, :k] >> 16)


def _unpack_halves(w_u32):
    lo = lax.bitcast_convert_type(w_u32 << 16, F32)
    hi = lax.bitcast_convert_type(w_u32 & jnp.uint32(0xFFFF0000), F32)
    return lo, hi


SIDE_CAST_SLABS = 64
BF16_SUBLANES = 16


def _side_slabs(side, steps):
    n = SIDE_CAST_SLABS
    while n > steps or any(a.shape[0] % (BF16_SUBLANES * n) for a in side):
        n //= 2
    assert n >= 1, [a.shape for a in side]
    return n


def _side_specs(side, n_j, slabs):
    return [pl.BlockSpec((a.shape[0] // slabs, a.shape[1]), lambda i, j: (jnp.minimum(i * n_j + j, slabs - 1), 0))
            for a in side]


def _side_cast(in_refs, out_refs, n_j, slabs):
    if not in_refs:
        return

    @pl.when(pl.program_id(0) * n_j + pl.program_id(1) < slabs)
    def _():
        for src, dst in zip(in_refs, out_refs):
            dst[...] = src[...].astype(dst.dtype)


def _side_out_shapes(side):
    return [jax.ShapeDtypeStruct(a.shape, BF16) for a in side]


def _norm_matmul_body(*refs, n_side, n_j, slabs):
    x_ref, g_ref, w_ref = refs[:3]
    side_in = refs[3:3 + n_side]
    o_ref = refs[3 + n_side]
    side_out = refs[4 + n_side:4 + 2 * n_side]
    xn_ref = refs[4 + 2 * n_side]

    @pl.when(pl.program_id(1) == 0)
    def _():
        x = x_ref[...]
        ms = jnp.mean(x * x, axis=-1, keepdims=True)
        xn_ref[...] = ((x * lax.rsqrt(ms + NORM_EPS)) * g_ref[...]).astype(BF16)

    o_ref[...] = jnp.dot(xn_ref[...], w_ref[...], preferred_element_type=F32).astype(o_ref.dtype)
    _side_cast(side_in, side_out, n_j, slabs)


def norm_matmul(x, g, w, *, tm, tn, side=()):
    m, k = x.shape
    n = w.shape[1]
    tm, tn = _tile(m, tm), _tile(n, tn)
    n_j = n // tn
    slabs = _side_slabs(side, (m // tm) * n_j)
    side_specs = _side_specs(side, n_j, slabs)
    out = pl.pallas_call(
        functools.partial(_norm_matmul_body, n_side=len(side), n_j=n_j, slabs=slabs),
        grid=(m // tm, n_j),
        in_specs=[
            pl.BlockSpec((tm, k), lambda i, j: (i, 0)),
            pl.BlockSpec((1, k), lambda i, j: (0, 0)),
            pl.BlockSpec((k, tn), lambda i, j: (0, j)),
        ] + side_specs,
        out_specs=[pl.BlockSpec((tm, tn), lambda i, j: (i, j))] + side_specs,
        out_shape=[jax.ShapeDtypeStruct((m, n), BF16)] + _side_out_shapes(side),
        scratch_shapes=[pltpu.VMEM((tm, k), BF16)],
        compiler_params=_params(("arbitrary", "arbitrary") if side else ("parallel", "arbitrary")),
        name="norm_matmul",
    )(x, g.reshape(1, k), w, *side)
    return out[0], out[1:]


def _rmsnorm_body(x_ref, g_ref, o_ref):
    x = x_ref[...]
    ms = jnp.mean(x * x, axis=-1, keepdims=True)
    o_ref[...] = ((x * lax.rsqrt(ms + NORM_EPS)) * g_ref[...]).astype(o_ref.dtype)


def rmsnorm_bf16(x, g, *, tm):
    m, k = x.shape
    tm = _tile(m, tm)
    return pl.pallas_call(
        _rmsnorm_body,
        grid=(m // tm,),
        in_specs=[pl.BlockSpec((tm, k), lambda i: (i, 0)), pl.BlockSpec((1, k), lambda i: (0, 0))],
        out_specs=pl.BlockSpec((tm, k), lambda i: (i, 0)),
        out_shape=jax.ShapeDtypeStruct((m, k), BF16),
        compiler_params=_params(("parallel",)),
        name="rmsnorm",
    )(x, g.reshape(1, k))


def _matmul_wcast_body(*refs, n_side, n_i, slabs):
    a_ref, w_ref = refs[:2]
    side_in = refs[2:2 + n_side]
    o_ref = refs[2 + n_side]
    side_out = refs[3 + n_side:3 + 2 * n_side]
    wb_ref = refs[3 + 2 * n_side]

    @pl.when(pl.program_id(1) == 0)
    def _():
        wb_ref[...] = w_ref[...].astype(BF16)

    o_ref[...] = jnp.dot(a_ref[...], wb_ref[...], preferred_element_type=F32).astype(o_ref.dtype)
    _side_cast(side_in, side_out, n_i, slabs)


def matmul_wcast(a, w, *, tm, tn, side=()):
    m, k = a.shape
    n = w.shape[1]
    tm, tn = _tile(m, tm), _tile(n, tn)
    n_i = m // tm
    slabs = _side_slabs(side, (n // tn) * n_i)
    side_specs = _side_specs(side, n_i, slabs)
    out = pl.pallas_call(
        functools.partial(_matmul_wcast_body, n_side=len(side), n_i=n_i, slabs=slabs),
        grid=(n // tn, n_i),
        in_specs=[pl.BlockSpec((tm, k), lambda j, i: (i, 0)), pl.BlockSpec((k, tn), lambda j, i: (0, j))] + side_specs,
        out_specs=[pl.BlockSpec((tm, tn), lambda j, i: (i, j))] + side_specs,
        out_shape=[jax.ShapeDtypeStruct((m, n), BF16)] + _side_out_shapes(side),
        scratch_shapes=[pltpu.VMEM((k, tn), BF16)],
        compiler_params=_params(("arbitrary", "arbitrary")),
        name="matmul_wcast",
    )(a, w, *side)
    return out[0], out[1:]


RET_HEADS_PER_STEP = 4


def _retention_body(lg_ref, q_ref, k_ref, v_ref, g_ref, cos_ref, sin_ref, gn_ref, o_ref, state_ref, dec_ref,
                    *, blk, dk, hps):
    c = pl.program_id(2)
    first_of_pair = jnp.logical_and(pl.program_id(1) == 0, c == 0)
    half = dk // 2
    cos = cos_ref[...]
    sin = sin_ref[...]
    pos = lax.broadcasted_iota(I32, (blk, 1), 0).astype(F32)

    def rot(t):
        t1, t2 = t[:, :half], t[:, half:]
        return jnp.concatenate([t1 * cos - t2 * sin, t1 * sin + t2 * cos], axis=-1)

    @pl.when(c == 0)
    def _():
        state_ref[...] = jnp.zeros_like(state_ref)

    for j in range(hps):
        lg = lg_ref[pl.program_id(0) * hps + j]
        cols = slice(j * dk, (j + 1) * dk)

        @pl.when(first_of_pair)
        def _(j=j, lg=lg):
            ri = lax.broadcasted_iota(I32, (blk, blk), 0)
            ci = lax.broadcasted_iota(I32, (blk, blk), 1)
            diff = (ri - ci).astype(F32)
            dec_ref[j] = jnp.where(diff >= 0.0, jnp.exp(jnp.maximum(diff, 0.0) * lg), 0.0)

        qr = rot(q_ref[:, cols].astype(F32))
        kr = rot(k_ref[:, cols].astype(F32)) * (dk ** -0.5)
        v = v_ref[:, cols]

        q_dec = jnp.exp((pos + 1.0) * lg)
        k_dec = jnp.exp((blk - 1.0 - pos) * lg)
        blk_dec = jnp.exp(jnp.full((1, dk), blk * lg, F32))

        s = lax.dot_general(qr.astype(BF16), kr.astype(BF16), (((1,), (1,)), ((), ())), preferred_element_type=F32)
        inner = jnp.dot((s * dec_ref[j]).astype(BF16), v, preferred_element_type=F32)

        state = state_ref[j]
        cross = jnp.dot((qr * q_dec).astype(BF16), state.astype(BF16), preferred_element_type=F32)
        kd_t = jnp.transpose(kr * k_dec).astype(BF16)
        state_ref[j] = state * blk_dec + jnp.dot(kd_t, v, preferred_element_type=F32)

        out = inner + cross
        mu = jnp.mean(out, axis=-1, keepdims=True)
        cen = out - mu
        var = jnp.mean(cen * cen, axis=-1, keepdims=True)
        y = cen * lax.rsqrt(var + NORM_EPS) * gn_ref[:, cols]
        gate = g_ref[:, cols].astype(F32)
        o_ref[:, cols] = (y * (gate * jax.nn.sigmoid(gate))).astype(o_ref.dtype)


def retention(proj, cos, sin, ret_norm_g, *, batch, seq, ret_width):
    heads = RET_HEADS
    hps = min(RET_HEADS_PER_STEP, heads)
    assert heads % hps == 0
    dk = ret_width // heads
    blk = min(RET_BLOCK, seq)
    nblk = seq // blk
    hb = heads // hps
    log_gamma = jnp.log(1.0 - jnp.exp2(-5.0 - jnp.arange(heads, dtype=F32)))

    def col(seg):
        return pl.BlockSpec((blk, hps * dk), lambda h, b, c, lg, seg=seg: (b * nblk + c, seg * hb + h))

    rowspec = pl.BlockSpec((blk, dk // 2), lambda h, b, c, lg: (b * nblk + c, 0))
    return pl.pallas_call(
        functools.partial(_retention_body, blk=blk, dk=dk, hps=hps),
        grid_spec=pltpu.PrefetchScalarGridSpec(
            num_scalar_prefetch=1,
            grid=(heads // hps, batch, nblk),
            in_specs=[col(0), col(1), col(2), col(3), rowspec, rowspec,
                      pl.BlockSpec((1, hps * dk), lambda h, b, c, lg: (0, h))],
            out_specs=pl.BlockSpec((blk, hps * dk), lambda h, b, c, lg: (b * nblk + c, h)),
            scratch_shapes=[pltpu.VMEM((hps, dk, dk), F32), pltpu.VMEM((hps, blk, blk), F32)],
        ),
        out_shape=jax.ShapeDtypeStruct((batch * seq, ret_width), BF16),
        compiler_params=_params(("arbitrary", "arbitrary", "arbitrary")),
        name="retention",
    )(log_gamma, proj, proj, proj, proj, cos, sin, ret_norm_g.reshape(1, ret_width))


def _gmlp_body(u_ref, v_ref, lng_ref, lnb_ref, ws_ref, bst_ref, o_ref, vn_ref, *, rows, groups, cg, chunk):
    v = jax.nn.gelu(v_ref[...].astype(F32))
    mu = jnp.mean(v, axis=-1, keepdims=True)
    cen = v - mu
    var = jnp.mean(cen * cen, axis=-1, keepdims=True)
    vn_ref[...] = (cen * lax.rsqrt(var + NORM_EPS) * lng_ref[...] + lnb_ref[...]).astype(BF16)

    ri = lax.broadcasted_iota(I32, (chunk, chunk), 0)
    ci = lax.broadcasted_iota(I32, (chunk, chunk), 1)
    causal = ri >= ci
    for g in range(groups):
        w = jnp.where(causal, ws_ref[g], 0.0).astype(BF16)
        bias = bst_ref[:, g:g + 1]
        cols = slice(g * cg, (g + 1) * cg)
        for t in range(rows // chunk):
            rws = slice(t * chunk, (t + 1) * chunk)
            sp = jnp.dot(w, vn_ref[rws, cols], preferred_element_type=F32) + bias
            u = jax.nn.gelu(u_ref[rws, cols].astype(F32))
            o_ref[rws, cols] = (u * sp).astype(o_ref.dtype)


def gmlp(proj, ln_g, ln_b, ws, bs, *, n_rows, gm_width, u_block, v_block):
    groups, chunk = GM_GROUPS, GM_CHUNK
    cg = gm_width // groups
    rows = 2 * chunk
    return pl.pallas_call(
        functools.partial(_gmlp_body, rows=rows, groups=groups, cg=cg, chunk=chunk),
        grid=(n_rows // rows,),
        in_specs=[
            pl.BlockSpec((rows, gm_width), lambda i: (i, u_block)),
            pl.BlockSpec((rows, gm_width), lambda i: (i, v_block)),
            pl.BlockSpec((1, gm_width), lambda i: (0, 0)),
            pl.BlockSpec((1, gm_width), lambda i: (0, 0)),
            pl.BlockSpec((groups, chunk, chunk), lambda i: (0, 0, 0)),
            pl.BlockSpec((chunk, groups), lambda i: (0, 0)),
        ],
        out_specs=pl.BlockSpec((rows, gm_width), lambda i: (i, 0)),
        out_shape=jax.ShapeDtypeStruct((n_rows, gm_width), BF16),
        scratch_shapes=[pltpu.VMEM((rows, gm_width), BF16)],
        compiler_params=_params(("parallel",)),
        name="gmlp",
    )(proj, proj, ln_g.reshape(1, gm_width), ln_b.reshape(1, gm_width), ws, bs.T)


def _matmul_residual_body(*refs, n_parts, n_side, n_j, slabs):
    a_refs, w_refs = refs[:n_parts], refs[n_parts:2 * n_parts]
    r_ref = refs[2 * n_parts]
    side_in = refs[2 * n_parts + 1:2 * n_parts + 1 + n_side]
    o_ref = refs[2 * n_parts + 1 + n_side]
    side_out = refs[2 * n_parts + 2 + n_side:]
    acc = r_ref[...]
    for a_ref, w_ref in zip(a_refs, w_refs):
        acc = acc + jnp.dot(a_ref[...], w_ref[...], preferred_element_type=F32)
    o_ref[...] = acc
    _side_cast(side_in, side_out, n_j, slabs)


def matmul_residual(parts, w, res, *, tm, tn, side=()):
    m, n = res.shape
    tm, tn = _tile(m, tm), _tile(n, tn)
    n_j = n // tn
    slabs = _side_slabs(side, (m // tm) * n_j)
    kp = parts[0].shape[1]
    n_parts = len(parts)
    a_specs = [pl.BlockSpec((tm, kp), lambda i, j: (i, 0)) for _ in parts]
    w_specs = [pl.BlockSpec((kp, tn), lambda i, j, p=p: (p, j)) for p in range(n_parts)]
    side_specs = _side_specs(side, n_j, slabs)
    out = pl.pallas_call(
        functools.partial(_matmul_residual_body, n_parts=n_parts, n_side=len(side), n_j=n_j, slabs=slabs),
        grid=(m // tm, n_j),
        in_specs=a_specs + w_specs + [pl.BlockSpec((tm, tn), lambda i, j: (i, j))] + side_specs,
        out_specs=[pl.BlockSpec((tm, tn), lambda i, j: (i, j))] + side_specs,
        out_shape=[jax.ShapeDtypeStruct((m, n), F32)] + _side_out_shapes(side),
        compiler_params=_params(("arbitrary", "arbitrary") if side else ("parallel", "parallel")),
        name="matmul_residual",
    )(*parts, *([w] * n_parts), res, *side)
    return out[0], out[1:]


def _cross_attention_body(h_ref, g_ref, wq_ref, kv_ref, wo_ref, o_ref, *, heads, dh):
    width = heads * dh
    x = h_ref[...]
    ms = jnp.mean(x * x, axis=-1, keepdims=True)
    xn = ((x * lax.rsqrt(ms + NORM_EPS)) * g_ref[...]).astype(BF16)
    q = jnp.dot(xn, wq_ref[...], preferred_element_type=F32).astype(BF16)
    outs = []
    for h in range(heads):
        k = kv_ref[:, h * dh:(h + 1) * dh]
        v = kv_ref[:, width + h * dh:width + (h + 1) * dh]
        s = lax.dot_general(q[:, h * dh:(h + 1) * dh], k, (((1,), (1,)), ((), ())),
                            preferred_element_type=F32) * (dh ** -0.5)
        e = jnp.exp(s - jnp.max(s, axis=-1, keepdims=True))
        p = e / jnp.sum(e, axis=-1, keepdims=True)
        outs.append(jnp.dot(p.astype(BF16), v, preferred_element_type=F32).astype(BF16))
    o = jnp.concatenate(outs, axis=-1)
    o_ref[...] = x + jnp.dot(o, wo_ref[...], preferred_element_type=F32)


def cross_attention_block(h, g, wq, kv, wo, *, seq, mem_len, tm):
    n, d = h.shape
    heads = XA_HEADS
    width = wq.shape[1]
    dh = width // heads
    tm = _tile(seq, tm)
    per_batch = seq // tm

    def resident(shape):
        return pl.BlockSpec(shape, lambda i: (0, 0), pipeline_mode=pl.Buffered(1))

    return pl.pallas_call(
        functools.partial(_cross_attention_body, heads=heads, dh=dh),
        grid=(n // tm,),
        in_specs=[
            pl.BlockSpec((tm, d), lambda i: (i, 0)),
            resident((1, d)),
            resident((d, width)),
            pl.BlockSpec((mem_len, 2 * width), lambda i: (i // per_batch, 0)),
            resident((width, d)),
        ],
        out_specs=pl.BlockSpec((tm, d), lambda i: (i, 0)),
        out_shape=jax.ShapeDtypeStruct((n, d), F32),
        compiler_params=_params(("parallel",)),
        name="cross_attention",
    )(h, g.reshape(1, d), wq, kv, wo)


def _router_body(h_ref, g_ref, wr_ref, br_ref, xp_ref, meta_ref, meta_t_ref, cnt_ref, base_ref, *, tm, n_grp, per):
    @pl.when(pl.program_id(0) == 0)
    def _():
        base_ref[...] = jnp.zeros_like(base_ref)

    x = h_ref[...]
    ms = jnp.mean(x * x, axis=-1, keepdims=True)
    xn = (x * lax.rsqrt(ms + NORM_EPS)) * g_ref[...]
    xp_ref[...] = _pack_halves(xn)

    x_hi = xn.astype(BF16)
    x_lo = (xn - x_hi.astype(F32)).astype(BF16)
    both = jnp.dot(x_hi, wr_ref[...], preferred_element_type=F32)
    corr = jnp.dot(x_lo, wr_ref[:, :V7X_LANES], preferred_element_type=F32)
    logits = both[:, :V7X_LANES] + (both[:, V7X_LANES:] + corr) + br_ref[...]
    lane = lax.broadcasted_iota(I32, logits.shape, 1)
    neg = jnp.float32(-1e30)
    big = jnp.int32(V7X_LANES)

    gl = jnp.where(lane < n_grp, logits, neg)
    gmax = jnp.max(gl, axis=-1, keepdims=True)
    gidx = jnp.min(jnp.where(gl == gmax, lane, big), axis=-1, keepdims=True)
    grp_gate = 1.0 / jnp.sum(jnp.exp(gl - gmax), axis=-1, keepdims=True)

    lo = n_grp + gidx * per
    el = jnp.where((lane >= lo) & (lane < lo + per), logits, neg)
    v1 = jnp.max(el, axis=-1, keepdims=True)
    i1 = jnp.min(jnp.where(el == v1, lane, big), axis=-1, keepdims=True)
    el2 = jnp.where(lane == i1, neg, el)
    v2 = jnp.max(el2, axis=-1, keepdims=True)
    i2 = jnp.min(jnp.where(el2 == v2, lane, big), axis=-1, keepdims=True)
    t = jnp.exp(v2 - v1)
    den = 1.0 + t
    g1 = grp_gate / den
    g2 = grp_gate * (t / den)

    oh1 = jnp.where(lane == i1, 1.0, 0.0)
    oh2 = jnp.where(lane == i2, 1.0, 0.0)
    ri = lax.broadcasted_iota(I32, (tm, tm), 0)
    ci = lax.broadcasted_iota(I32, (tm, tm), 1)
    lower = jnp.where(ri > ci, 1.0, 0.0).astype(BF16)
    pre1 = jnp.dot(lower, oh1.astype(BF16), preferred_element_type=F32)
    pre2 = jnp.dot(lower, oh2.astype(BF16), preferred_element_type=F32)
    cnt1 = jnp.sum(oh1, axis=0, keepdims=True)
    cnt2 = jnp.sum(oh2, axis=0, keepdims=True)
    base = base_ref[...]
    rank1 = jnp.sum(oh1 * (pre1 + base), axis=-1, keepdims=True)
    rank2 = jnp.sum(oh2 * (pre2 + base + cnt1), axis=-1, keepdims=True)
    total = base + cnt1 + cnt2
    base_ref[...] = total
    cnt_ref[...] = total

    e1 = (i1 - n_grp).astype(F32)
    e2 = (i2 - n_grp).astype(F32)
    meta = jnp.zeros(logits.shape, F32)
    for idx, val in enumerate((e1, e2, rank1, rank2, g1, g2)):
        meta = jnp.where(lane == idx, val, meta)
    meta_ref[...] = meta
    meta_t_ref[...] = jnp.transpose(meta)[:META_ROWS, :]


META_ROWS = 8


def moe_router(h, g, wr, br, *, tm):
    n, d = h.shape
    tm = min(tm, n)
    return pl.pallas_call(
        functools.partial(_router_body, tm=tm, n_grp=MOE_GROUPS, per=MOE_PER_GROUP),
        grid=(n // tm,),
        in_specs=[
            pl.BlockSpec((tm, d), lambda i: (i, 0)),
            pl.BlockSpec((1, d), lambda i: (0, 0)),
            pl.BlockSpec((d, 2 * V7X_LANES), lambda i: (0, 0)),
            pl.BlockSpec((1, V7X_LANES), lambda i: (0, 0)),
        ],
        out_specs=[
            pl.BlockSpec((tm, d // 2), lambda i: (i, 0)),
            pl.BlockSpec((tm, V7X_LANES), lambda i: (i, 0)),
            pl.BlockSpec((META_ROWS, tm), lambda i: (0, i)),
            pl.BlockSpec((1, V7X_LANES), lambda i: (0, 0)),
        ],
        out_shape=[
            jax.ShapeDtypeStruct((n, d // 2), U32),
            jax.ShapeDtypeStruct((n, V7X_LANES), F32),
            jax.ShapeDtypeStruct((META_ROWS, n), F32),
            jax.ShapeDtypeStruct((1, V7X_LANES), F32),
        ],
        scratch_shapes=[pltpu.VMEM((1, V7X_LANES), F32)],
        compiler_params=_params(("arbitrary",)),
        name="moe_router",
    )(h, g.reshape(1, d), wr, br)


MOE_CODE_SHIFT = 6
assert MOE_GROUPS * MOE_PER_GROUP == 1 << MOE_CODE_SHIFT


def _order_body(code_ref, start_ref, order_ref):
    def body(a, carry):
        code = code_ref[a]
        order_ref[start_ref[code & ((1 << MOE_CODE_SHIFT) - 1)] + (code >> MOE_CODE_SHIFT)] = a
        return carry

    lax.fori_loop(0, code_ref.shape[0], body, 0, unroll=8)


def moe_order(code, start):
    return pl.pallas_call(
        _order_body,
        grid_spec=pltpu.PrefetchScalarGridSpec(
            num_scalar_prefetch=2,
            grid=(1,),
            in_specs=[],
            out_specs=pl.BlockSpec(memory_space=pltpu.SMEM),
        ),
        out_shape=jax.ShapeDtypeStruct(code.shape, I32),
        compiler_params=_params(("arbitrary",)),
        name="moe_order",
    )(code, start)


EXPERT_STEPS = 2
ROW_DMA_UNROLL = 8
WEIGHT_DMA_PRIORITY = 1
assert MOE_TOPK == 2


def _experts_body(exp_ref, nxt_ref, rows_ref, off_ref, order_ref, nact_ref,
                  xp_hbm, wg_hbm, wu_hbm, wdn_hbm, ya_hbm,
                  xbuf_ref, ybuf_ref, hd_ref, wa_ref, wb_ref, wc_ref, wd_ref,
                  ring_in, ring_dn, gsem, ssem, sem_in, sem_dn, *, rb, n_tok, n_blocks):
    v = pl.program_id(0)
    s = pl.program_id(1)
    nact = nact_ref[0]
    active = v < nact
    slot = lax.rem(v, 2)
    half = rb // 2
    oc = ring_dn.shape[2]
    dq = ring_in.shape[1]

    def in_copy(e, i):
        src = (wg_hbm, wg_hbm, wu_hbm, wu_hbm)[i % 4]
        q = 2 * (i // 4) + i % 2
        return pltpu.make_async_copy(src.at[e, pl.ds(q * dq, dq), :], ring_in.at[i], sem_in.at[i])

    def dn_copy(e, i):
        col = (i % 2) * 2 * oc + (i // 2) * oc
        return pltpu.make_async_copy(wdn_hbm.at[e, :, pl.ds(col, oc)], ring_dn.at[i], sem_dn.at[i])

    def refill(copy, slots):
        @pl.when(v + 1 < nact)
        def _():
            for i in slots:
                copy(nxt_ref[v], i).start(priority=WEIGHT_DMA_PRIORITY)

    def decode(base, r):
        a = order_ref[base + r]
        k = jnp.where(a >= n_tok, 1, 0)
        return k, a - k * n_tok

    def gather_copy(base, buf, r):
        _, tok = decode(base, r)
        return pltpu.make_async_copy(xp_hbm.at[pl.ds(tok, 1)], xbuf_ref.at[buf, pl.ds(r, 1)], gsem.at[buf])

    def scatter_copy(base, r):
        k, tok = decode(base, r)
        return pltpu.make_async_copy(ybuf_ref.at[pl.ds(r, 1)], ya_hbm.at[k, pl.ds(tok, 1)], ssem)

    def for_rows(lo, hi, fn):
        groups = lax.shift_right_logical(jnp.maximum(hi - lo, 0), ROW_DMA_UNROLL.bit_length() - 1)

        def group(g, carry):
            for u in range(ROW_DMA_UNROLL):
                fn(lo + g * ROW_DMA_UNROLL + u)
            return carry

        def single(r, carry):
            fn(r)
            return carry

        lax.fori_loop(0, groups, group, 0)
        lax.fori_loop(lo + groups * ROW_DMA_UNROLL, hi, single, 0)

    @pl.when(jnp.logical_and(v == 0, s == 0))
    def _():
        for i in range(ring_in.shape[0]):
            in_copy(exp_ref[0], i).start(priority=WEIGHT_DMA_PRIORITY)
        for i in range(ring_dn.shape[0]):
            dn_copy(exp_ref[0], i).start(priority=WEIGHT_DMA_PRIORITY)
        xbuf_ref[...] = jnp.zeros_like(xbuf_ref)
        base = off_ref[0]
        for_rows(0, rows_ref[0], lambda r: gather_copy(base, 0, r).start())

    @pl.when(jnp.logical_and(active, s == 0))
    def _():
        base = off_ref[v]
        for_rows(0, rows_ref[v], lambda r: gather_copy(base, slot, r).wait())

    @pl.when(jnp.logical_and(active, v + 1 < nact))
    def _():
        nxt = jnp.minimum(v + 1, n_blocks - 1)
        base = off_ref[nxt]
        for_rows(s * half, jnp.minimum((s + 1) * half, rows_ref[nxt]),
                 lambda r: gather_copy(base, 1 - slot, r).start())

    @pl.when(jnp.logical_and(active, jnp.logical_and(s == 1, v >= 1)))
    def _():
        prev = jnp.maximum(v - 1, 0)
        base = off_ref[prev]
        for_rows(0, rows_ref[prev], lambda r: scatter_copy(base, r).wait())

    halves = [(pl.ds(0, half), None), (pl.ds(half, half), rows_ref[v] > half)]

    def for_halves(fn):
        for rows, cond in halves:
            if cond is None:
                fn(rows)
            else:
                pl.when(cond)(functools.partial(fn, rows))

    @pl.when(jnp.logical_and(active, s == 0))
    def _():
        slots = range(ring_in.shape[0])
        for i in slots:
            in_copy(exp_ref[v], i).wait()
        for i in slots:
            dst = (wa_ref, wa_ref, wb_ref, wb_ref)[i % 4]
            dst[pl.ds((2 * (i // 4) + i % 2) * dq, dq), :] = ring_in[i].astype(BF16)
        refill(in_copy, slots)

        def gate_up(rows):
            lo, hi = _unpack_halves(xbuf_ref[slot, rows, :])
            xl, xh = lo.astype(BF16), hi.astype(BF16)
            kh = xl.shape[1]
            hg = (jnp.dot(xl, wa_ref[:kh, :], preferred_element_type=F32)
                  + jnp.dot(xh, wa_ref[kh:, :], preferred_element_type=F32))
            hu = (jnp.dot(xl, wb_ref[:kh, :], preferred_element_type=F32)
                  + jnp.dot(xh, wb_ref[kh:, :], preferred_element_type=F32))
            hd_ref[rows, :] = (hg * jax.nn.sigmoid(hg) * hu).astype(BF16)
        for_halves(gate_up)

    @pl.when(jnp.logical_and(active, s == 1))
    def _():
        slots = range(ring_dn.shape[0])
        for i in slots:
            dn_copy(exp_ref[v], i).wait()
        for i in slots:
            (wc_ref, wd_ref)[i % 2][:, pl.ds((i // 2) * oc, oc)] = ring_dn[i].astype(BF16)
        refill(dn_copy, slots)

        def down(rows):
            hd = hd_ref[rows, :]
            for j in range(ring_dn.shape[0] // 2):
                cols = pl.ds(j * oc, oc)
                y_lo = jnp.dot(hd, wc_ref[:, cols], preferred_element_type=F32)
                y_hi = jnp.dot(hd, wd_ref[:, cols], preferred_element_type=F32)
                ybuf_ref[rows, cols] = _pack_halves(jnp.concatenate([y_lo, y_hi], axis=-1))
        for_halves(down)
        base = off_ref[v]
        for_rows(0, rows_ref[v], lambda r: scatter_copy(base, r).start())

    @pl.when(jnp.logical_and(v == n_blocks - 1, s == EXPERT_STEPS - 1))
    def _():
        last = jnp.maximum(nact - 1, 0)
        base = off_ref[last]
        for_rows(0, rows_ref[last], lambda r: scatter_copy(base, r).wait())


def moe_experts(xp, w_gate, w_up, w_down, blk_exp, blk_next, blk_rows, blk_off, order, nact, *, n_blocks, rb):
    n_exp, d, ff = w_gate.shape
    n_tok, kw = xp.shape
    assert d == 2 * kw
    oc = kw // 2
    n_in, n_dn = 8, 4
    any_spec = pl.BlockSpec(memory_space=pl.ANY)
    return pl.pallas_call(
        functools.partial(_experts_body, rb=rb, n_tok=n_tok, n_blocks=n_blocks),
        grid_spec=pltpu.PrefetchScalarGridSpec(
            num_scalar_prefetch=6,
            grid=(n_blocks, EXPERT_STEPS),
            in_specs=[any_spec, any_spec, any_spec, any_spec],
            out_specs=any_spec,
            scratch_shapes=[pltpu.VMEM((2, rb, kw), U32), pltpu.VMEM((rb, kw), U32),
                            pltpu.VMEM((rb, ff), BF16),
                            pltpu.VMEM((d, ff), BF16), pltpu.VMEM((d, ff), BF16),
                            pltpu.VMEM((ff, kw), BF16), pltpu.VMEM((ff, kw), BF16),
                            pltpu.VMEM((n_in, d // 4, ff), F32), pltpu.VMEM((n_dn, ff, oc), F32),
                            pltpu.SemaphoreType.DMA((2,)), pltpu.SemaphoreType.DMA(()),
                            pltpu.SemaphoreType.DMA((n_in,)), pltpu.SemaphoreType.DMA((n_dn,))],
        ),
        out_shape=jax.ShapeDtypeStruct((MOE_TOPK, n_tok, kw), U32),
        compiler_params=_params(("arbitrary", "arbitrary"), EXPERTS_VMEM_LIMIT_BYTES),
        name="moe_experts",
    )(blk_exp, blk_next, blk_rows, blk_off, order, nact, xp, w_gate, w_up, w_down)


def _combine_body(h_ref, meta_ref, g_ref, y0_ref, y1_ref, o_ref):
    meta = meta_ref[...]
    acc_lo = None
    acc_hi = None
    for k, y_ref in enumerate((y0_ref, y1_ref)):
        gate = meta[:, 2 * MOE_TOPK + k:2 * MOE_TOPK + k + 1]
        lo, hi = _unpack_halves(y_ref[0])
        lo, hi = lo * gate, hi * gate
        acc_lo = lo if acc_lo is None else acc_lo + lo
        acc_hi = hi if acc_hi is None else acc_hi + hi
    kh = acc_lo.shape[1]
    h_lo = h_ref[:, :kh] + acc_lo
    h_hi = h_ref[:, kh:] + acc_hi
    ms = (jnp.sum(h_lo * h_lo, axis=-1, keepdims=True) + jnp.sum(h_hi * h_hi, axis=-1, keepdims=True)) / (2 * kh)
    scale = lax.rsqrt(ms + NORM_EPS)
    o_ref[:, :kh] = (h_lo * scale) * g_ref[:, :kh]
    o_ref[:, kh:] = (h_hi * scale) * g_ref[:, kh:]


def moe_combine(h, meta, g, ya, *, tb):
    n, d = h.shape
    kw = ya.shape[2]
    tb = _tile(n, tb)

    def slot(k):
        return pl.BlockSpec((1, tb, kw), lambda i, k=k: (k, i, 0))

    return pl.pallas_call(
        _combine_body,
        grid=(n // tb,),
        in_specs=[
            pl.BlockSpec((tb, d), lambda i: (i, 0)),
            pl.BlockSpec((tb, V7X_LANES), lambda i: (i, 0)),
            pl.BlockSpec((1, d), lambda i: (0, 0)),
            slot(0), slot(1),
        ],
        out_specs=pl.BlockSpec((tb, d), lambda i: (i, 0)),
        out_shape=jax.ShapeDtypeStruct((n, d), F32),
        compiler_params=_params(("parallel",)),
        name="moe_combine",
    )(h, meta, g.reshape(1, d), ya, ya)


def _moe_plan(counts, *, rb, n_blocks):
    n_exp = counts.shape[0]
    per_exp = (counts + rb - 1) // rb
    blk_end = jnp.cumsum(per_exp)
    start = jnp.cumsum(counts) - counts
    nact = blk_end[-1].astype(I32)
    blk = jnp.arange(n_blocks, dtype=I32)
    blk_exp = jnp.minimum(jnp.sum(blk_end[None, :] <= blk[:, None], axis=1), n_exp - 1).astype(I32)
    active = blk < nact
    last = jnp.maximum(nact - 1, 0)
    blk_exp = jnp.where(active, blk_exp, blk_exp[last]).astype(I32)
    blk_next = blk_exp[jnp.minimum(blk + 1, last)]
    j = blk - (blk_end[blk_exp] - per_exp[blk_exp])
    blk_off = jnp.where(active, start[blk_exp] + j * rb, 0).astype(I32)
    blk_rows = jnp.where(active, jnp.clip(counts[blk_exp] - j * rb, 0, rb), 0).astype(I32)
    return start.astype(I32), blk_exp, blk_next, blk_rows, blk_off, nact.reshape(1)


def kernel(x, mem, positions, norm_mix_g, w_in, ret_norm_g, gm_ln_g, gm_ln_b, gm_ws, gm_bs, w_out, norm_xa_g, norm_mem_g, xa_wq, xa_wkv, xa_wo, norm_moe_g, router_grp_w, router_grp_b, router_exp_w, router_exp_b, moe_w_gate, moe_w_up, moe_w_down, norm_final_g):
    batch, seq, d = x.shape
    mem_len = mem.shape[1]
    n = batch * seq
    ret_width = ret_norm_g.shape[0]
    gm_width = gm_ln_g.shape[0]
    assert ret_width == gm_width and w_in.shape[1] == 4 * ret_width + 2 * gm_width
    dk = ret_width // RET_HEADS
    n_exp = moe_w_gate.shape[0]
    assert n_exp == MOE_GROUPS * MOE_PER_GROUP and MOE_GROUPS + n_exp <= V7X_LANES

    inv_freq = ROPE_BASE ** (-jnp.arange(0, dk, 2, dtype=F32) / dk)
    ang = positions.astype(F32).reshape(n, 1) * inv_freq
    cos, sin = jnp.cos(ang), jnp.sin(ang)

    x2 = x.reshape(n, d)
    xn = rmsnorm_bf16(x2, norm_mix_g, tm=512)
    proj, (w_out_b,) = matmul_wcast(xn, w_in, tm=1024, tn=512, side=(w_out,))
    ret = retention(proj, cos, sin, ret_norm_g, batch=batch, seq=seq, ret_width=ret_width)
    gm = gmlp(proj, gm_ln_g, gm_ln_b, gm_ws, gm_bs, n_rows=n, gm_width=gm_width,
              u_block=4 * ret_width // gm_width, v_block=4 * ret_width // gm_width + 1)
    h1, (wq_b, wkv_b, wo_b) = matmul_residual([ret, gm], w_out_b, x2, tm=512, tn=1024,
                                              side=(xa_wq, xa_wkv, xa_wo))

    kv, _ = norm_matmul(mem.reshape(batch * mem_len, d), norm_mem_g, wkv_b, tm=512, tn=1024)
    h2 = cross_attention_block(h1, norm_xa_g, wq_b, kv, wo_b, seq=seq, mem_len=mem_len, tm=256)

    pad = V7X_LANES - MOE_GROUPS - n_exp
    wr = jnp.concatenate([router_grp_w, router_exp_w, jnp.zeros((d, pad), F32)], axis=1)
    br = jnp.concatenate([router_grp_b, router_exp_b, jnp.zeros((pad,), F32)]).reshape(1, V7X_LANES)
    wr_hi = wr.astype(BF16)
    wr_lo = (wr - wr_hi.astype(F32)).astype(BF16)
    xp, meta, meta_t, cnt = moe_router(h2, norm_moe_g, jnp.concatenate([wr_hi, wr_lo], axis=1), br, tm=256)

    rb = MOE_ROW_BLOCK
    n_blocks = -(-(n * MOE_TOPK) // rb) + n_exp
    counts = cnt[0, MOE_GROUPS:MOE_GROUPS + n_exp].astype(I32)
    start, blk_exp, blk_next, blk_rows, blk_off, nact = _moe_plan(counts, rb=rb, n_blocks=n_blocks)
    code = (meta_t[MOE_TOPK:2 * MOE_TOPK].astype(I32) * (1 << MOE_CODE_SHIFT)
            + meta_t[0:MOE_TOPK].astype(I32)).reshape(-1)
    order = moe_order(code, start)
    ya = moe_experts(xp, moe_w_gate, moe_w_up, moe_w_down, blk_exp, blk_next, blk_rows, blk_off, order, nact,
                     n_blocks=n_blocks, rb=rb)
    y = moe_combine(h2, meta, norm_final_g, ya, tb=256)
    return y.reshape(batch, seq, d)
```

```python
import functools

import jax
import jax.numpy as jnp
from jax import lax
from jax.experimental import pallas as pl
from jax.experimental.pallas import tpu as pltpu

NORM_EPS = 1e-6
RET_HEADS = 8
ROPE_BASE = 10000.0
GM_GROUPS = 8
GM_CHUNK = 128
XA_HEADS = 4
MOE_GROUPS = 8
MOE_PER_GROUP = 8
MOE_TOPK = 2

V7X_LANES = 128
V7X_VMEM_BYTES = 64 * 1024 * 1024
VMEM_LIMIT_BYTES = 56 * 1024 * 1024
EXPERTS_VMEM_LIMIT_BYTES = 58 * 1024 * 1024

RET_BLOCK = 512
MOE_ROW_BLOCK = 512

F32 = jnp.float32
BF16 = jnp.bfloat16
U32 = jnp.uint32
I32 = jnp.int32


def _params(sem, vmem_limit_bytes=VMEM_LIMIT_BYTES):
    return pltpu.CompilerParams(dimension_semantics=sem, vmem_limit_bytes=vmem_limit_bytes)


def _tile(dim, target):
    t = min(dim, target)
    while dim % t:
        t -= V7X_LANES
    assert t > 0, (dim, target)
    return t


def _pack_halves(x_f32):
    k = x_f32.shape[-1] // 2
    bits = lax.bitcast_convert_type(x_f32.astype(BF16).astype(F32), U32)
    return (bits[:, k:] & jnp.uint32(0xFFFF0000)) | (bits[:, :k] >> 16)


def _unpack_halves(w_u32):
    lo = lax.bitcast_convert_type(w_u32 << 16, F32)
    hi = lax.bitcast_convert_type(w_u32 & jnp.uint32(0xFFFF0000), F32)
    return lo, hi


SIDE_CAST_SLABS = 64
BF16_SUBLANES = 16


def _side_slabs(side, steps):
    n = SIDE_CAST_SLABS
    while n > steps or any(a.shape[0] % (BF16_SUBLANES * n) for a in side):
        n //= 2
    assert n >= 1, [a.shape for a in side]
    return n


def _side_specs(side, n_j, slabs):
    return [pl.BlockSpec((a.shape[0] // slabs, a.shape[1]), lambda i, j: (jnp.minimum(i * n_j + j, slabs - 1), 0))
            for a in side]


def _side_cast(in_refs, out_refs, n_j, slabs):
    if not in_refs:
        return

    @pl.when(pl.program_id(0) * n_j + pl.program_id(1) < slabs)
    def _():
        for src, dst in zip(in_refs, out_refs):
            dst[...] = src[...].astype(dst.dtype)


def _side_out_shapes(side):
    return [jax.ShapeDtypeStruct(a.shape, BF16) for a in side]


def _norm_matmul_body(*refs, n_side, n_j, slabs):
    x_ref, g_ref, w_ref = refs[:3]
    side_in = refs[3:3 + n_side]
    o_ref = refs[3 + n_side]
    side_out = refs[4 + n_side:4 + 2 * n_side]
    xn_ref = refs[4 + 2 * n_side]

    @pl.when(pl.program_id(1) == 0)
    def _():
        x = x_ref[...]
        ms = jnp.mean(x * x, axis=-1, keepdims=True)
        xn_ref[...] = ((x * lax.rsqrt(ms + NORM_EPS)) * g_ref[...]).astype(BF16)

    o_ref[...] = jnp.dot(xn_ref[...], w_ref[...], preferred_element_type=F32).astype(o_ref.dtype)
    _side_cast(side_in, side_out, n_j, slabs)


def norm_matmul(x, g, w, *, tm, tn, side=()):
    m, k = x.shape
    n = w.shape[1]
    tm, tn = _tile(m, tm), _tile(n, tn)
    n_j = n // tn
    slabs = _side_slabs(side, (m // tm) * n_j)
    side_specs = _side_specs(side, n_j, slabs)
    out = pl.pallas_call(
        functools.partial(_norm_matmul_body, n_side=len(side), n_j=n_j, slabs=slabs),
        grid=(m // tm, n_j),
        in_specs=[
            pl.BlockSpec((tm, k), lambda i, j: (i, 0)),
            pl.BlockSpec((1, k), lambda i, j: (0, 0)),
            pl.BlockSpec((k, tn), lambda i, j: (0, j)),
        ] + side_specs,
        out_specs=[pl.BlockSpec((tm, tn), lambda i, j: (i, j))] + side_specs,
        out_shape=[jax.ShapeDtypeStruct((m, n), BF16)] + _side_out_shapes(side),
        scratch_shapes=[pltpu.VMEM((tm, k), BF16)],
        compiler_params=_params(("arbitrary", "arbitrary") if side else ("parallel", "arbitrary")),
        name="norm_matmul",
    )(x, g.reshape(1, k), w, *side)
    return out[0], out[1:]


def _rmsnorm_body(x_ref, g_ref, o_ref):
    x = x_ref[...]
    ms = jnp.mean(x * x, axis=-1, keepdims=True)
    o_ref[...] = ((x * lax.rsqrt(ms + NORM_EPS)) * g_ref[...]).astype(o_ref.dtype)


def rmsnorm_bf16(x, g, *, tm):
    m, k = x.shape
    tm = _tile(m, tm)
    return pl.pallas_call(
        _rmsnorm_body,
        grid=(m // tm,),
        in_specs=[pl.BlockSpec((tm, k), lambda i: (i, 0)), pl.BlockSpec((1, k), lambda i: (0, 0))],
        out_specs=pl.BlockSpec((tm, k), lambda i: (i, 0)),
        out_shape=jax.ShapeDtypeStruct((m, k), BF16),
        compiler_params=_params(("parallel",)),
        name="rmsnorm",
    )(x, g.reshape(1, k))


def _matmul_wcast_body(*refs, n_side, n_i, slabs):
    a_ref, w_ref = refs[:2]
    side_in = refs[2:2 + n_side]
    o_ref = refs[2 + n_side]
    side_out = refs[3 + n_side:3 + 2 * n_side]
    wb_ref = refs[3 + 2 * n_side]

    @pl.when(pl.program_id(1) == 0)
    def _():
        wb_ref[...] = w_ref[...].astype(BF16)

    o_ref[...] = jnp.dot(a_ref[...], wb_ref[...], preferred_element_type=F32).astype(o_ref.dtype)
    _side_cast(side_in, side_out, n_i, slabs)


def matmul_wcast(a, w, *, tm, tn, side=()):
    m, k = a.shape
    n = w.shape[1]
    tm, tn = _tile(m, tm), _tile(n, tn)
    n_i = m // tm
    slabs = _side_slabs(side, (n // tn) * n_i)
    side_specs = _side_specs(side, n_i, slabs)
    out = pl.pallas_call(
        functools.partial(_matmul_wcast_body, n_side=len(side), n_i=n_i, slabs=slabs),
        grid=(n // tn, n_i),
        in_specs=[pl.BlockSpec((tm, k), lambda j, i: (i, 0)), pl.BlockSpec((k, tn), lambda j, i: (0, j))] + side_specs,
        out_specs=[pl.BlockSpec((tm, tn), lambda j, i: (i, j))] + side_specs,
        out_shape=[jax.ShapeDtypeStruct((m, n), BF16)] + _side_out_shapes(side),
        scratch_shapes=[pltpu.VMEM((k, tn), BF16)],
        compiler_params=_params(("arbitrary", "arbitrary")),
        name="matmul_wcast",
    )(a, w, *side)
    return out[0], out[1:]


RET_HEADS_PER_STEP = 4


def _retention_body(lg_ref, q_ref, k_ref, v_ref, g_ref, cos_ref, sin_ref, gn_ref, o_ref, state_ref, dec_ref,
                    *, blk, dk, hps):
    c = pl.program_id(2)
    first_of_pair = jnp.logical_and(pl.program_id(1) == 0, c == 0)
    half = dk // 2
    cos = cos_ref[...]
    sin = sin_ref[...]
    pos = lax.broadcasted_iota(I32, (blk, 1), 0).astype(F32)

    def rot(t):
        t1, t2 = t[:, :half], t[:, half:]
        return jnp.concatenate([t1 * cos - t2 * sin, t1 * sin + t2 * cos], axis=-1)

    @pl.when(c == 0)
    def _():
        state_ref[...] = jnp.zeros_like(state_ref)

    for j in range(hps):
        lg = lg_ref[pl.program_id(0) * hps + j]
        cols = slice(j * dk, (j + 1) * dk)

        @pl.when(first_of_pair)
        def _(j=j, lg=lg):
            ri = lax.broadcasted_iota(I32, (blk, blk), 0)
            ci = lax.broadcasted_iota(I32, (blk, blk), 1)
            diff = (ri - ci).astype(F32)
            dec_ref[j] = jnp.where(diff >= 0.0, jnp.exp(jnp.maximum(diff, 0.0) * lg), 0.0)

        qr = rot(q_ref[:, cols].astype(F32))
        kr = rot(k_ref[:, cols].astype(F32)) * (dk ** -0.5)
        v = v_ref[:, cols]

        q_dec = jnp.exp((pos + 1.0) * lg)
        k_dec = jnp.exp((blk - 1.0 - pos) * lg)
        blk_dec = jnp.exp(jnp.full((1, dk), blk * lg, F32))

        s = lax.dot_general(qr.astype(BF16), kr.astype(BF16), (((1,), (1,)), ((), ())), preferred_element_type=F32)
        inner = jnp.dot((s * dec_ref[j]).astype(BF16), v, preferred_element_type=F32)

        state = state_ref[j]
        cross = jnp.dot((qr * q_dec).astype(BF16), state.astype(BF16), preferred_element_type=F32)
        kd_t = jnp.transpose(kr * k_dec).astype(BF16)
        state_ref[j] = state * blk_dec + jnp.dot(kd_t, v, preferred_element_type=F32)

        out = inner + cross
        mu = jnp.mean(out, axis=-1, keepdims=True)
        cen = out - mu
        var = jnp.mean(cen * cen, axis=-1, keepdims=True)
        y = cen * lax.rsqrt(var + NORM_EPS) * gn_ref[:, cols]
        gate = g_ref[:, cols].astype(F32)
        o_ref[:, cols] = (y * (gate * jax.nn.sigmoid(gate))).astype(o_ref.dtype)


def retention(proj, cos, sin, ret_norm_g, *, batch, seq, ret_width):
    heads = RET_HEADS
    hps = min(RET_HEADS_PER_STEP, heads)
    assert heads % hps == 0
    dk = ret_width // heads
    blk = min(RET_BLOCK, seq)
    nblk = seq // blk
    hb = heads // hps
    log_gamma = jnp.log(1.0 - jnp.exp2(-5.0 - jnp.arange(heads, dtype=F32)))

    def col(seg):
        return pl.BlockSpec((blk, hps * dk), lambda h, b, c, lg, seg=seg: (b * nblk + c, seg * hb + h))

    rowspec = pl.BlockSpec((blk, dk // 2), lambda h, b, c, lg: (b * nblk + c, 0))
    return pl.pallas_call(
        functools.partial(_retention_body, blk=blk, dk=dk, hps=hps),
        grid_spec=pltpu.PrefetchScalarGridSpec(
            num_scalar_prefetch=1,
            grid=(heads // hps, batch, nblk),
            in_specs=[col(0), col(1), col(2), col(3), rowspec, rowspec,
                      pl.BlockSpec((1, hps * dk), lambda h, b, c, lg: (0, h))],
            out_specs=pl.BlockSpec((blk, hps * dk), lambda h, b, c, lg: (b * nblk + c, h)),
            scratch_shapes=[pltpu.VMEM((hps, dk, dk), F32), pltpu.VMEM((hps, blk, blk), F32)],
        ),
        out_shape=jax.ShapeDtypeStruct((batch * seq, ret_width), BF16),
        compiler_params=_params(("arbitrary", "arbitrary", "arbitrary")),
        name="retention",
    )(log_gamma, proj, proj, proj, proj, cos, sin, ret_norm_g.reshape(1, ret_width))


def _gmlp_body(u_ref, v_ref, lng_ref, lnb_ref, ws_ref, bst_ref, o_ref, vn_ref, *, rows, groups, cg, chunk):
    v = jax.nn.gelu(v_ref[...].astype(F32))
    mu = jnp.mean(v, axis=-1, keepdims=True)
    cen = v - mu
    var = jnp.mean(cen * cen, axis=-1, keepdims=True)
    vn_ref[...] = (cen * lax.rsqrt(var + NORM_EPS) * lng_ref[...] + lnb_ref[...]).astype(BF16)

    ri = lax.broadcasted_iota(I32, (chunk, chunk), 0)
    ci = lax.broadcasted_iota(I32, (chunk, chunk), 1)
    causal = ri >= ci
    for g in range(groups):
        w = jnp.where(causal, ws_ref[g], 0.0).astype(BF16)
        bias = bst_ref[:, g:g + 1]
        cols = slice(g * cg, (g + 1) * cg)
        for t in range(rows // chunk):
            rws = slice(t * chunk, (t + 1) * chunk)
            sp = jnp.dot(w, vn_ref[rws, cols], preferred_element_type=F32) + bias
            u = jax.nn.gelu(u_ref[rws, cols].astype(F32))
            o_ref[rws, cols] = (u * sp).astype(o_ref.dtype)


def gmlp(proj, ln_g, ln_b, ws, bs, *, n_rows, gm_width, u_block, v_block):
    groups, chunk = GM_GROUPS, GM_CHUNK
    cg = gm_width // groups
    rows = 2 * chunk
    return pl.pallas_call(
        functools.partial(_gmlp_body, rows=rows, groups=groups, cg=cg, chunk=chunk),
        grid=(n_rows // rows,),
        in_specs=[
            pl.BlockSpec((rows, gm_width), lambda i: (i, u_block)),
            pl.BlockSpec((rows, gm_width), lambda i: (i, v_block)),
            pl.BlockSpec((1, gm_width), lambda i: (0, 0)),
            pl.BlockSpec((1, gm_width), lambda i: (0, 0)),
            pl.BlockSpec((groups, chunk, chunk), lambda i: (0, 0, 0)),
            pl.BlockSpec((chunk, groups), lambda i: (0, 0)),
        ],
        out_specs=pl.BlockSpec((rows, gm_width), lambda i: (i, 0)),
        out_shape=jax.ShapeDtypeStruct((n_rows, gm_width), BF16),
        scratch_shapes=[pltpu.VMEM((rows, gm_width), BF16)],
        compiler_params=_params(("parallel",)),
        name="gmlp",
    )(proj, proj, ln_g.reshape(1, gm_width), ln_b.reshape(1, gm_width), ws, bs.T)


def _matmul_residual_body(*refs, n_parts, n_side, n_j, slabs):
    a_refs, w_refs = refs[:n_parts], refs[n_parts:2 * n_parts]
    r_ref = refs[2 * n_parts]
    side_in = refs[2 * n_parts + 1:2 * n_parts + 1 + n_side]
    o_ref = refs[2 * n_parts + 1 + n_side]
    side_out = refs[2 * n_parts + 2 + n_side:]
    acc = r_ref[...]
    for a_ref, w_ref in zip(a_refs, w_refs):
        acc = acc + jnp.dot(a_ref[...], w_ref[...], preferred_element_type=F32)
    o_ref[...] = acc
    _side_cast(side_in, side_out, n_j, slabs)


def matmul_residual(parts, w, res, *, tm, tn, side=()):
    m, n = res.shape
    tm, tn = _tile(m, tm), _tile(n, tn)
    n_j = n // tn
    slabs = _side_slabs(side, (m // tm) * n_j)
    kp = parts[0].shape[1]
    n_parts = len(parts)
    a_specs = [pl.BlockSpec((tm, kp), lambda i, j: (i, 0)) for _ in parts]
    w_specs = [pl.BlockSpec((kp, tn), lambda i, j, p=p: (p, j)) for p in range(n_parts)]
    side_specs = _side_specs(side, n_j, slabs)
    out = pl.pallas_call(
        functools.partial(_matmul_residual_body, n_parts=n_parts, n_side=len(side), n_j=n_j, slabs=slabs),
        grid=(m // tm, n_j),
        in_specs=a_specs + w_specs + [pl.BlockSpec((tm, tn), lambda i, j: (i, j))] + side_specs,
        out_specs=[pl.BlockSpec((tm, tn), lambda i, j: (i, j))] + side_specs,
        out_shape=[jax.ShapeDtypeStruct((m, n), F32)] + _side_out_shapes(side),
        compiler_params=_params(("arbitrary", "arbitrary") if side else ("parallel", "parallel")),
        name="matmul_residual",
    )(*parts, *([w] * n_parts), res, *side)
    return out[0], out[1:]


def _cross_attention_body(h_ref, g_ref, wq_ref, kv_ref, wo_ref, o_ref, *, heads, dh):
    width = heads * dh
    x = h_ref[...]
    ms = jnp.mean(x * x, axis=-1, keepdims=True)
    xn = ((x * lax.rsqrt(ms + NORM_EPS)) * g_ref[...]).astype(BF16)
    q = jnp.dot(xn, wq_ref[...], preferred_element_type=F32).astype(BF16)
    outs = []
    for h in range(heads):
        k = kv_ref[:, h * dh:(h + 1) * dh]
        v = kv_ref[:, width + h * dh:width + (h + 1) * dh]
        s = lax.dot_general(q[:, h * dh:(h + 1) * dh], k, (((1,), (1,)), ((), ())),
                            preferred_element_type=F32) * (dh ** -0.5)
        e = jnp.exp(s - jnp.max(s, axis=-1, keepdims=True))
        p = e / jnp.sum(e, axis=-1, keepdims=True)
        outs.append(jnp.dot(p.astype(BF16), v, preferred_element_type=F32).astype(BF16))
    o = jnp.concatenate(outs, axis=-1)
    o_ref[...] = x + jnp.dot(o, wo_ref[...], preferred_element_type=F32)


def cross_attention_block(h, g, wq, kv, wo, *, seq, mem_len, tm):
    n, d = h.shape
    heads = XA_HEADS
    width = wq.shape[1]
    dh = width // heads
    tm = _tile(seq, tm)
    per_batch = seq // tm

    def resident(shape):
        return pl.BlockSpec(shape, lambda i: (0, 0), pipeline_mode=pl.Buffered(1))

    return pl.pallas_call(
        functools.partial(_cross_attention_body, heads=heads, dh=dh),
        grid=(n // tm,),
        in_specs=[
            pl.BlockSpec((tm, d), lambda i: (i, 0)),
            resident((1, d)),
            resident((d, width)),
            pl.BlockSpec((mem_len, 2 * width), lambda i: (i // per_batch, 0)),
            resident((width, d)),
        ],
        out_specs=pl.BlockSpec((tm, d), lambda i: (i, 0)),
        out_shape=jax.ShapeDtypeStruct((n, d), F32),
        compiler_params=_params(("parallel",)),
        name="cross_attention",
    )(h, g.reshape(1, d), wq, kv, wo)


def _router_body(h_ref, g_ref, wr_ref, br_ref, xp_ref, meta_ref, meta_t_ref, cnt_ref, base_ref, *, tm, n_grp, per):
    @pl.when(pl.program_id(0) == 0)
    def _():
        base_ref[...] = jnp.zeros_like(base_ref)

    x = h_ref[...]
    ms = jnp.mean(x * x, axis=-1, keepdims=True)
    xn = (x * lax.rsqrt(ms + NORM_EPS)) * g_ref[...]
    xp_ref[...] = _pack_halves(xn)

    x_hi = xn.astype(BF16)
    x_lo = (xn - x_hi.astype(F32)).astype(BF16)
    both = jnp.dot(x_hi, wr_ref[...], preferred_element_type=F32)
    corr = jnp.dot(x_lo, wr_ref[:, :V7X_LANES], preferred_element_type=F32)
    logits = both[:, :V7X_LANES] + (both[:, V7X_LANES:] + corr) + br_ref[...]
    lane = lax.broadcasted_iota(I32, logits.shape, 1)
    neg = jnp.float32(-1e30)
    big = jnp.int32(V7X_LANES)

    gl = jnp.where(lane < n_grp, logits, neg)
    gmax = jnp.max(gl, axis=-1, keepdims=True)
    gidx = jnp.min(jnp.where(gl == gmax, lane, big), axis=-1, keepdims=True)
    grp_gate = 1.0 / jnp.sum(jnp.exp(gl - gmax), axis=-1, keepdims=True)

    lo = n_grp + gidx * per
    el = jnp.where((lane >= lo) & (lane < lo + per), logits, neg)
    v1 = jnp.max(el, axis=-1, keepdims=True)
    i1 = jnp.min(jnp.where(el == v1, lane, big), axis=-1, keepdims=True)
    el2 = jnp.where(lane == i1, neg, el)
    v2 = jnp.max(el2, axis=-1, keepdims=True)
    i2 = jnp.min(jnp.where(el2 == v2, lane, big), axis=-1, keepdims=True)
    t = jnp.exp(v2 - v1)
    den = 1.0 + t
    g1 = grp_gate / den
    g2 = grp_gate * (t / den)

    oh1 = jnp.where(lane == i1, 1.0, 0.0)
    oh2 = jnp.where(lane == i2, 1.0, 0.0)
    ri = lax.broadcasted_iota(I32, (tm, tm), 0)
    ci = lax.broadcasted_iota(I32, (tm, tm), 1)
    lower = jnp.where(ri > ci, 1.0, 0.0).astype(BF16)
    pre1 = jnp.dot(lower, oh1.astype(BF16), preferred_element_type=F32)
    pre2 = jnp.dot(lower, oh2.astype(BF16), preferred_element_type=F32)
    cnt1 = jnp.sum(oh1, axis=0, keepdims=True)
    cnt2 = jnp.sum(oh2, axis=0, keepdims=True)
    base = base_ref[...]
    rank1 = jnp.sum(oh1 * (pre1 + base), axis=-1, keepdims=True)
    rank2 = jnp.sum(oh2 * (pre2 + base + cnt1), axis=-1, keepdims=True)
    total = base + cnt1 + cnt2
    base_ref[...] = total
    cnt_ref[...] = total

    e1 = (i1 - n_grp).astype(F32)
    e2 = (i2 - n_grp).astype(F32)
    meta = jnp.zeros(logits.shape, F32)
    for idx, val in enumerate((e1, e2, rank1, rank2, g1, g2)):
        meta = jnp.where(lane == idx, val, meta)
    meta_ref[...] = meta
    meta_t_ref[...] = jnp.transpose(meta)[:META_ROWS, :]


META_ROWS = 8


def moe_router(h, g, wr, br, *, tm):
    n, d = h.shape
    tm = min(tm, n)
    return pl.pallas_call(
        functools.partial(_router_body, tm=tm, n_grp=MOE_GROUPS, per=MOE_PER_GROUP),
        grid=(n // tm,),
        in_specs=[
            pl.BlockSpec((tm, d), lambda i: (i, 0)),
            pl.BlockSpec((1, d), lambda i: (0, 0)),
            pl.BlockSpec((d, 2 * V7X_LANES), lambda i: (0, 0)),
            pl.BlockSpec((1, V7X_LANES), lambda i: (0, 0)),
        ],
        out_specs=[
            pl.BlockSpec((tm, d // 2), lambda i: (i, 0)),
            pl.BlockSpec((tm, V7X_LANES), lambda i: (i, 0)),
            pl.BlockSpec((META_ROWS, tm), lambda i: (0, i)),
            pl.BlockSpec((1, V7X_LANES), lambda i: (0, 0)),
        ],
        out_shape=[
            jax.ShapeDtypeStruct((n, d // 2), U32),
            jax.ShapeDtypeStruct((n, V7X_LANES), F32),
            jax.ShapeDtypeStruct((META_ROWS, n), F32),
            jax.ShapeDtypeStruct((1, V7X_LANES), F32),
        ],
        scratch_shapes=[pltpu.VMEM((1, V7X_LANES), F32)],
        compiler_params=_params(("arbitrary",)),
        name="moe_router",
    )(h, g.reshape(1, d), wr, br)


MOE_CODE_SHIFT = 6
assert MOE_GROUPS * MOE_PER_GROUP == 1 << MOE_CODE_SHIFT


def _order_body(code_ref, start_ref, order_ref):
    def body(a, carry):
        code = code_ref[a]
        order_ref[start_ref[code & ((1 << MOE_CODE_SHIFT) - 1)] + (code >> MOE_CODE_SHIFT)] = a
        return carry

    lax.fori_loop(0, code_ref.shape[0], body, 0, unroll=8)


def moe_order(code, start):
    return pl.pallas_call(
        _order_body,
        grid_spec=pltpu.PrefetchScalarGridSpec(
            num_scalar_prefetch=2,
            grid=(1,),
            in_specs=[],
            out_specs=pl.BlockSpec(memory_space=pltpu.SMEM),
        ),
        out_shape=jax.ShapeDtypeStruct(code.shape, I32),
        compiler_params=_params(("arbitrary",)),
        name="moe_order",
    )(code, start)


EXPERT_STEPS = 1
ROW_DMA_UNROLL = 8
WEIGHT_DMA_PRIORITY = 1
assert MOE_TOPK == 2


def _experts_body(exp_ref, nxt_ref, rows_ref, off_ref, order_ref, nact_ref,
                  xp_hbm, wg_hbm, wu_hbm, wdn_hbm, ya_hbm,
                  xbuf_ref, ybuf_ref, hd_ref, wa_ref, wb_ref, wc_ref, wd_ref,
                  ring_in, ring_dn, gsem, ssem, sem_in, sem_dn, *, rb, n_tok, n_blocks):
    v = pl.program_id(0)
    s = pl.program_id(1)
    nact = nact_ref[0]
    active = v < nact
    slot = lax.rem(v, 2)
    half = rb // 2
    oc = ring_dn.shape[2]
    dq = ring_in.shape[1]

    def in_copy(e, i):
        src = (wg_hbm, wg_hbm, wu_hbm, wu_hbm)[i % 4]
        q = 2 * (i // 4) + i % 2
        return pltpu.make_async_copy(src.at[e, pl.ds(q * dq, dq), :], ring_in.at[i], sem_in.at[i])

    def dn_copy(e, i):
        col = (i % 2) * 2 * oc + (i // 2) * oc
        return pltpu.make_async_copy(wdn_hbm.at[e, :, pl.ds(col, oc)], ring_dn.at[i], sem_dn.at[i])

    def refill(copy, slots):
        @pl.when(v + 1 < nact)
        def _():
            for i in slots:
                copy(nxt_ref[v], i).start(priority=WEIGHT_DMA_PRIORITY)

    def decode(base, r):
        a = order_ref[base + r]
        k = jnp.where(a >= n_tok, 1, 0)
        return k, a - k * n_tok

    def gather_copy(base, buf, r):
        _, tok = decode(base, r)
        return pltpu.make_async_copy(xp_hbm.at[pl.ds(tok, 1)], xbuf_ref.at[buf, pl.ds(r, 1)], gsem.at[buf])

    def scatter_copy(base, r):
        k, tok = decode(base, r)
        return pltpu.make_async_copy(ybuf_ref.at[pl.ds(r, 1)], ya_hbm.at[k, pl.ds(tok, 1)], ssem)

    def for_rows(lo, hi, fn):
        groups = lax.shift_right_logical(jnp.maximum(hi - lo, 0), ROW_DMA_UNROLL.bit_length() - 1)

        def group(g, carry):
            for u in range(ROW_DMA_UNROLL):
                fn(lo + g * ROW_DMA_UNROLL + u)
            return carry

        def single(r, carry):
            fn(r)
            return carry

        lax.fori_loop(0, groups, group, 0)
        lax.fori_loop(lo + groups * ROW_DMA_UNROLL, hi, single, 0)

    @pl.when(jnp.logical_and(v == 0, s == 0))
    def _():
        for i in range(ring_in.shape[0]):
            in_copy(exp_ref[0], i).start(priority=WEIGHT_DMA_PRIORITY)
        for i in range(ring_dn.shape[0]):
            dn_copy(exp_ref[0], i).start(priority=WEIGHT_DMA_PRIORITY)
        xbuf_ref[...] = jnp.zeros_like(xbuf_ref)
        base = off_ref[0]
        for_rows(0, rows_ref[0], lambda r: gather_copy(base, 0, r).start())

    @pl.when(jnp.logical_and(active, s == 0))
    def _():
        base = off_ref[v]
        for_rows(0, rows_ref[v], lambda r: gather_copy(base, slot, r).wait())

    @pl.when(jnp.logical_and(active, v + 1 < nact))
    def _():
        nxt = jnp.minimum(v + 1, n_blocks - 1)
        base = off_ref[nxt]
        per_step = rb // EXPERT_STEPS
        for_rows(s * per_step, jnp.minimum((s + 1) * per_step, rows_ref[nxt]),
                 lambda r: gather_copy(base, 1 - slot, r).start())

    halves = [(pl.ds(0, half), None), (pl.ds(half, half), rows_ref[v] > half)]

    def for_halves(fn):
        for rows, cond in halves:
            if cond is None:
                fn(rows)
            else:
                pl.when(cond)(functools.partial(fn, rows))

    @pl.when(jnp.logical_and(active, s == 0))
    def _():
        slots = range(ring_in.shape[0])
        for i in slots:
            in_copy(exp_ref[v], i).wait()
        for i in slots:
            dst = (wa_ref, wa_ref, wb_ref, wb_ref)[i % 4]
            dst[pl.ds((2 * (i // 4) + i % 2) * dq, dq), :] = ring_in[i].astype(BF16)
        refill(in_copy, slots)

        def gate_up(rows):
            lo, hi = _unpack_halves(xbuf_ref[slot, rows, :])
            xl, xh = lo.astype(BF16), hi.astype(BF16)
            kh = xl.shape[1]
            hg = (jnp.dot(xl, wa_ref[:kh, :], preferred_element_type=F32)
                  + jnp.dot(xh, wa_ref[kh:, :], preferred_element_type=F32))
            hu = (jnp.dot(xl, wb_ref[:kh, :], preferred_element_type=F32)
                  + jnp.dot(xh, wb_ref[kh:, :], preferred_element_type=F32))
            hd_ref[rows, :] = (hg * jax.nn.sigmoid(hg) * hu).astype(BF16)
        for_halves(gate_up)

    @pl.when(jnp.logical_and(active, jnp.logical_and(s == EXPERT_STEPS - 1, v >= 1)))
    def _():
        prev = jnp.maximum(v - 1, 0)
        base = off_ref[prev]
        for_rows(0, rows_ref[prev], lambda r: scatter_copy(base, r).wait())

    @pl.when(jnp.logical_and(active, s == EXPERT_STEPS - 1))
    def _():
        slots = range(ring_dn.shape[0])
        for i in slots:
            dn_copy(exp_ref[v], i).wait()
        for i in slots:
            (wc_ref, wd_ref)[i % 2][:, pl.ds((i // 2) * oc, oc)] = ring_dn[i].astype(BF16)
        refill(dn_copy, slots)

        def down(rows):
            hd = hd_ref[rows, :]
            for j in range(ring_dn.shape[0] // 2):
                cols = pl.ds(j * oc, oc)
                y_lo = jnp.dot(hd, wc_ref[:, cols], preferred_element_type=F32)
                y_hi = jnp.dot(hd, wd_ref[:, cols], preferred_element_type=F32)
                ybuf_ref[rows, cols] = _pack_halves(jnp.concatenate([y_lo, y_hi], axis=-1))
        for_halves(down)
        base = off_ref[v]
        for_rows(0, rows_ref[v], lambda r: scatter_copy(base, r).start())

    @pl.when(jnp.logical_and(v == n_blocks - 1, s == EXPERT_STEPS - 1))
    def _():
        last = jnp.maximum(nact - 1, 0)
        base = off_ref[last]
        for_rows(0, rows_ref[last], lambda r: scatter_copy(base, r).wait())


def moe_experts(xp, w_gate, w_up, w_down, blk_exp, blk_next, blk_rows, blk_off, order, nact, *, n_blocks, rb):
    n_exp, d, ff = w_gate.shape
    n_tok, kw = xp.shape
    assert d == 2 * kw
    oc = kw // 2
    n_in, n_dn = 8, 4
    any_spec = pl.BlockSpec(memory_space=pl.ANY)
    return pl.pallas_call(
        functools.partial(_experts_body, rb=rb, n_tok=n_tok, n_blocks=n_blocks),
        grid_spec=pltpu.PrefetchScalarGridSpec(
            num_scalar_prefetch=6,
            grid=(n_blocks, EXPERT_STEPS),
            in_specs=[any_spec, any_spec, any_spec, any_spec],
            out_specs=any_spec,
            scratch_shapes=[pltpu.VMEM((2, rb, kw), U32), pltpu.VMEM((rb, kw), U32),
                            pltpu.VMEM((rb, ff), BF16),
                            pltpu.VMEM((d, ff), BF16), pltpu.VMEM((d, ff), BF16),
                            pltpu.VMEM((ff, kw), BF16), pltpu.VMEM((ff, kw), BF16),
                            pltpu.VMEM((n_in, d // 4, ff), F32), pltpu.VMEM((n_dn, ff, oc), F32),
                            pltpu.SemaphoreType.DMA((2,)), pltpu.SemaphoreType.DMA(()),
                            pltpu.SemaphoreType.DMA((n_in,)), pltpu.SemaphoreType.DMA((n_dn,))],
        ),
        out_shape=jax.ShapeDtypeStruct((MOE_TOPK, n_tok, kw), U32),
        compiler_params=_params(("arbitrary", "arbitrary"), EXPERTS_VMEM_LIMIT_BYTES),
        name="moe_experts",
    )(blk_exp, blk_next, blk_rows, blk_off, order, nact, xp, w_gate, w_up, w_down)


def _combine_body(h_ref, meta_ref, g_ref, y0_ref, y1_ref, o_ref):
    meta = meta_ref[...]
    acc_lo = None
    acc_hi = None
    for k, y_ref in enumerate((y0_ref, y1_ref)):
        gate = meta[:, 2 * MOE_TOPK + k:2 * MOE_TOPK + k + 1]
        lo, hi = _unpack_halves(y_ref[0])
        lo, hi = lo * gate, hi * gate
        acc_lo = lo if acc_lo is None else acc_lo + lo
        acc_hi = hi if acc_hi is None else acc_hi + hi
    kh = acc_lo.shape[1]
    h_lo = h_ref[:, :kh] + acc_lo
    h_hi = h_ref[:, kh:] + acc_hi
    ms = (jnp.sum(h_lo * h_lo, axis=-1, keepdims=True) + jnp.sum(h_hi * h_hi, axis=-1, keepdims=True)) / (2 * kh)
    scale = lax.rsqrt(ms + NORM_EPS)
    o_ref[:, :kh] = (h_lo * scale) * g_ref[:, :kh]
    o_ref[:, kh:] = (h_hi * scale) * g_ref[:, kh:]


def moe_combine(h, meta, g, ya, *, tb):
    n, d = h.shape
    kw = ya.shape[2]
    tb = _tile(n, tb)

    def slot(k):
        return pl.BlockSpec((1, tb, kw), lambda i, k=k: (k, i, 0))

    return pl.pallas_call(
        _combine_body,
        grid=(n // tb,),
        in_specs=[
            pl.BlockSpec((tb, d), lambda i: (i, 0)),
            pl.BlockSpec((tb, V7X_LANES), lambda i: (i, 0)),
            pl.BlockSpec((1, d), lambda i: (0, 0)),
            slot(0), slot(1),
        ],
        out_specs=pl.BlockSpec((tb, d), lambda i: (i, 0)),
        out_shape=jax.ShapeDtypeStruct((n, d), F32),
        compiler_params=_params(("parallel",)),
        name="moe_combine",
    )(h, meta, g.reshape(1, d), ya, ya)


def _moe_plan(counts, *, rb, n_blocks):
    n_exp = counts.shape[0]
    per_exp = (counts + rb - 1) // rb
    blk_end = jnp.cumsum(per_exp)
    start = jnp.cumsum(counts) - counts
    nact = blk_end[-1].astype(I32)
    blk = jnp.arange(n_blocks, dtype=I32)
    blk_exp = jnp.minimum(jnp.sum(blk_end[None, :] <= blk[:, None], axis=1), n_exp - 1).astype(I32)
    active = blk < nact
    last = jnp.maximum(nact - 1, 0)
    blk_exp = jnp.where(active, blk_exp, blk_exp[last]).astype(I32)
    blk_next = blk_exp[jnp.minimum(blk + 1, last)]
    j = blk - (blk_end[blk_exp] - per_exp[blk_exp])
    blk_off = jnp.where(active, start[blk_exp] + j * rb, 0).astype(I32)
    blk_rows = jnp.where(active, jnp.clip(counts[blk_exp] - j * rb, 0, rb), 0).astype(I32)
    return start.astype(I32), blk_exp, blk_next, blk_rows, blk_off, nact.reshape(1)


def kernel(x, mem, positions, norm_mix_g, w_in, ret_norm_g, gm_ln_g, gm_ln_b, gm_ws, gm_bs, w_out, norm_xa_g, norm_mem_g, xa_wq, xa_wkv, xa_wo, norm_moe_g, router_grp_w, router_grp_b, router_exp_w, router_exp_b, moe_w_gate, moe_w_up, moe_w_down, norm_final_g):
    batch, seq, d = x.shape
    mem_len = mem.shape[1]
    n = batch * seq
    ret_width = ret_norm_g.shape[0]
    gm_width = gm_ln_g.shape[0]
    assert ret_width == gm_width and w_in.shape[1] == 4 * ret_width + 2 * gm_width
    dk = ret_width // RET_HEADS
    n_exp = moe_w_gate.shape[0]
    assert n_exp == MOE_GROUPS * MOE_PER_GROUP and MOE_GROUPS + n_exp <= V7X_LANES

    inv_freq = ROPE_BASE ** (-jnp.arange(0, dk, 2, dtype=F32) / dk)
    ang = positions.astype(F32).reshape(n, 1) * inv_freq
    cos, sin = jnp.cos(ang), jnp.sin(ang)

    x2 = x.reshape(n, d)
    xn = rmsnorm_bf16(x2, norm_mix_g, tm=512)
    proj, (w_out_b,) = matmul_wcast(xn, w_in, tm=1024, tn=512, side=(w_out,))
    ret = retention(proj, cos, sin, ret_norm_g, batch=batch, seq=seq, ret_width=ret_width)
    gm = gmlp(proj, gm_ln_g, gm_ln_b, gm_ws, gm_bs, n_rows=n, gm_width=gm_width,
              u_block=4 * ret_width // gm_width, v_block=4 * ret_width // gm_width + 1)
    h1, (wq_b, wkv_b, wo_b) = matmul_residual([ret, gm], w_out_b, x2, tm=512, tn=1024,
                                              side=(xa_wq, xa_wkv, xa_wo))

    kv, _ = norm_matmul(mem.reshape(batch * mem_len, d), norm_mem_g, wkv_b, tm=512, tn=1024)
    h2 = cross_attention_block(h1, norm_xa_g, wq_b, kv, wo_b, seq=seq, mem_len=mem_len, tm=256)

    pad = V7X_LANES - MOE_GROUPS - n_exp
    wr = jnp.concatenate([router_grp_w, router_exp_w, jnp.zeros((d, pad), F32)], axis=1)
    br = jnp.concatenate([router_grp_b, router_exp_b, jnp.zeros((pad,), F32)]).reshape(1, V7X_LANES)
    wr_hi = wr.astype(BF16)
    wr_lo = (wr - wr_hi.astype(F32)).astype(BF16)
    xp, meta, meta_t, cnt = moe_router(h2, norm_moe_g, jnp.concatenate([wr_hi, wr_lo], axis=1), br, tm=256)

    rb = MOE_ROW_BLOCK
    n_blocks = -(-(n * MOE_TOPK) // rb) + n_exp
    counts = cnt[0, MOE_GROUPS:MOE_GROUPS + n_exp].astype(I32)
    start, blk_exp, blk_next, blk_rows, blk_off, nact = _moe_plan(counts, rb=rb, n_blocks=n_blocks)
    code = (meta_t[MOE_TOPK:2 * MOE_TOPK].astype(I32) * (1 << MOE_CODE_SHIFT)
            + meta_t[0:MOE_TOPK].astype(I32)).reshape(-1)
    order = moe_order(code, start)
    ya = moe_experts(xp, moe_w_gate, moe_w_up, moe_w_down, blk_exp, blk_next, blk_rows, blk_off, order, nact,
                     n_blocks=n_blocks, rb=rb)
    y = moe_combine(h2, meta, norm_final_g, ya, tb=256)
    return y.reshape(batch, seq, d)
```

```python
import functools

import jax
import jax.numpy as jnp
from jax import lax
from jax.experimental import pallas as pl
from jax.experimental.pallas import tpu as pltpu

NORM_EPS = 1e-6
RET_HEADS = 8
ROPE_BASE = 10000.0
GM_GROUPS = 8
GM_CHUNK = 128
XA_HEADS = 4
MOE_GROUPS = 8
MOE_PER_GROUP = 8
MOE_TOPK = 2

V7X_LANES = 128
V7X_VMEM_BYTES = 64 * 1024 * 1024
VMEM_LIMIT_BYTES = 56 * 1024 * 1024

RET_BLOCK = 512
MOE_ROW_BLOCK = 512

F32 = jnp.float32
BF16 = jnp.bfloat16
U32 = jnp.uint32
I32 = jnp.int32


def _params(sem):
    return pltpu.CompilerParams(dimension_semantics=sem, vmem_limit_bytes=VMEM_LIMIT_BYTES)


def _tile(dim, target):
    t = min(dim, target)
    while dim % t:
        t -= V7X_LANES
    assert t > 0, (dim, target)
    return t


def _pack_halves(x_f32):
    k = x_f32.shape[-1] // 2
    bits = lax.bitcast_convert_type(x_f32.astype(BF16).astype(F32), U32)
    return (bits[:, k:] & jnp.uint32(0xFFFF0000)) | (bits[:, :k] >> 16)


def _unpack_halves(w_u32):
    lo = lax.bitcast_convert_type(w_u32 << 16, F32)
    hi = lax.bitcast_convert_type(w_u32 & jnp.uint32(0xFFFF0000), F32)
    return lo, hi


SIDE_CAST_SLABS = 64
BF16_SUBLANES = 16


def _side_slabs(side, steps):
    n = SIDE_CAST_SLABS
    while n > steps or any(a.shape[0] % (BF16_SUBLANES * n) for a in side):
        n //= 2
    assert n >= 1, [a.shape for a in side]
    return n


def _side_specs(side, n_j, slabs):
    return [pl.BlockSpec((a.shape[0] // slabs, a.shape[1]), lambda i, j: (jnp.minimum(i * n_j + j, slabs - 1), 0))
            for a in side]


def _side_cast(in_refs, out_refs, n_j, slabs):
    if not in_refs:
        return

    @pl.when(pl.program_id(0) * n_j + pl.program_id(1) < slabs)
    def _():
        for src, dst in zip(in_refs, out_refs):
            dst[...] = src[...].astype(dst.dtype)


def _side_out_shapes(side):
    return [jax.ShapeDtypeStruct(a.shape, BF16) for a in side]


def _norm_matmul_body(*refs, n_side, n_j, slabs):
    x_ref, g_ref, w_ref = refs[:3]
    side_in = refs[3:3 + n_side]
    o_ref = refs[3 + n_side]
    side_out = refs[4 + n_side:4 + 2 * n_side]
    xn_ref = refs[4 + 2 * n_side]

    @pl.when(pl.program_id(1) == 0)
    def _():
        x = x_ref[...]
        ms = jnp.mean(x * x, axis=-1, keepdims=True)
        xn_ref[...] = ((x * lax.rsqrt(ms + NORM_EPS)) * g_ref[...]).astype(BF16)

    o_ref[...] = jnp.dot(xn_ref[...], w_ref[...], preferred_element_type=F32).astype(o_ref.dtype)
    _side_cast(side_in, side_out, n_j, slabs)


def norm_matmul(x, g, w, *, tm, tn, side=()):
    m, k = x.shape
    n = w.shape[1]
    tm, tn = _tile(m, tm), _tile(n, tn)
    n_j = n // tn
    slabs = _side_slabs(side, (m // tm) * n_j)
    side_specs = _side_specs(side, n_j, slabs)
    out = pl.pallas_call(
        functools.partial(_norm_matmul_body, n_side=len(side), n_j=n_j, slabs=slabs),
        grid=(m // tm, n_j),
        in_specs=[
            pl.BlockSpec((tm, k), lambda i, j: (i, 0)),
            pl.BlockSpec((1, k), lambda i, j: (0, 0)),
            pl.BlockSpec((k, tn), lambda i, j: (0, j)),
        ] + side_specs,
        out_specs=[pl.BlockSpec((tm, tn), lambda i, j: (i, j))] + side_specs,
        out_shape=[jax.ShapeDtypeStruct((m, n), BF16)] + _side_out_shapes(side),
        scratch_shapes=[pltpu.VMEM((tm, k), BF16)],
        compiler_params=_params(("arbitrary", "arbitrary") if side else ("parallel", "arbitrary")),
        name="norm_matmul",
    )(x, g.reshape(1, k), w, *side)
    return out[0], out[1:]


def _rmsnorm_body(x_ref, g_ref, o_ref):
    x = x_ref[...]
    ms = jnp.mean(x * x, axis=-1, keepdims=True)
    o_ref[...] = ((x * lax.rsqrt(ms + NORM_EPS)) * g_ref[...]).astype(o_ref.dtype)


def rmsnorm_bf16(x, g, *, tm):
    m, k = x.shape
    tm = _tile(m, tm)
    return pl.pallas_call(
        _rmsnorm_body,
        grid=(m // tm,),
        in_specs=[pl.BlockSpec((tm, k), lambda i: (i, 0)), pl.BlockSpec((1, k), lambda i: (0, 0))],
        out_specs=pl.BlockSpec((tm, k), lambda i: (i, 0)),
        out_shape=jax.ShapeDtypeStruct((m, k), BF16),
        compiler_params=_params(("parallel",)),
        name="rmsnorm",
    )(x, g.reshape(1, k))


def _matmul_wcast_body(*refs, n_side, n_i, slabs):
    a_ref, w_ref = refs[:2]
    side_in = refs[2:2 + n_side]
    o_ref = refs[2 + n_side]
    side_out = refs[3 + n_side:3 + 2 * n_side]
    wb_ref = refs[3 + 2 * n_side]

    @pl.when(pl.program_id(1) == 0)
    def _():
        wb_ref[...] = w_ref[...].astype(BF16)

    o_ref[...] = jnp.dot(a_ref[...], wb_ref[...], preferred_element_type=F32).astype(o_ref.dtype)
    _side_cast(side_in, side_out, n_i, slabs)


def matmul_wcast(a, w, *, tm, tn, side=()):
    m, k = a.shape
    n = w.shape[1]
    tm, tn = _tile(m, tm), _tile(n, tn)
    n_i = m // tm
    slabs = _side_slabs(side, (n // tn) * n_i)
    side_specs = _side_specs(side, n_i, slabs)
    out = pl.pallas_call(
        functools.partial(_matmul_wcast_body, n_side=len(side), n_i=n_i, slabs=slabs),
        grid=(n // tn, n_i),
        in_specs=[pl.BlockSpec((tm, k), lambda j, i: (i, 0)), pl.BlockSpec((k, tn), lambda j, i: (0, j))] + side_specs,
        out_specs=[pl.BlockSpec((tm, tn), lambda j, i: (i, j))] + side_specs,
        out_shape=[jax.ShapeDtypeStruct((m, n), BF16)] + _side_out_shapes(side),
        scratch_shapes=[pltpu.VMEM((k, tn), BF16)],
        compiler_params=_params(("arbitrary", "arbitrary")),
        name="matmul_wcast",
    )(a, w, *side)
    return out[0], out[1:]


RET_HEADS_PER_STEP = 4


def _retention_body(lg_ref, q_ref, k_ref, v_ref, g_ref, cos_ref, sin_ref, gn_ref, o_ref, state_ref, dec_ref,
                    *, blk, dk, hps):
    c = pl.program_id(2)
    first_of_pair = jnp.logical_and(pl.program_id(1) == 0, c == 0)
    half = dk // 2
    cos = cos_ref[...]
    sin = sin_ref[...]
    pos = lax.broadcasted_iota(I32, (blk, 1), 0).astype(F32)

    def rot(t):
        t1, t2 = t[:, :half], t[:, half:]
        return jnp.concatenate([t1 * cos - t2 * sin, t1 * sin + t2 * cos], axis=-1)

    @pl.when(c == 0)
    def _():
        state_ref[...] = jnp.zeros_like(state_ref)

    for j in range(hps):
        lg = lg_ref[pl.program_id(0) * hps + j]
        cols = slice(j * dk, (j + 1) * dk)

        @pl.when(first_of_pair)
        def _(j=j, lg=lg):
            ri = lax.broadcasted_iota(I32, (blk, blk), 0)
            ci = lax.broadcasted_iota(I32, (blk, blk), 1)
            diff = (ri - ci).astype(F32)
            dec_ref[j] = jnp.where(diff >= 0.0, jnp.exp(jnp.maximum(diff, 0.0) * lg), 0.0)

        qr = rot(q_ref[:, cols].astype(F32))
        kr = rot(k_ref[:, cols].astype(F32)) * (dk ** -0.5)
        v = v_ref[:, cols]

        q_dec = jnp.exp((pos + 1.0) * lg)
        k_dec = jnp.exp((blk - 1.0 - pos) * lg)
        blk_dec = jnp.exp(jnp.full((1, dk), blk * lg, F32))

        s = lax.dot_general(qr.astype(BF16), kr.astype(BF16), (((1,), (1,)), ((), ())), preferred_element_type=F32)
        inner = jnp.dot((s * dec_ref[j]).astype(BF16), v, preferred_element_type=F32)

        state = state_ref[j]
        cross = jnp.dot((qr * q_dec).astype(BF16), state.astype(BF16), preferred_element_type=F32)
        kd_t = jnp.transpose(kr * k_dec).astype(BF16)
        state_ref[j] = state * blk_dec + jnp.dot(kd_t, v, preferred_element_type=F32)

        out = inner + cross
        mu = jnp.mean(out, axis=-1, keepdims=True)
        cen = out - mu
        var = jnp.mean(cen * cen, axis=-1, keepdims=True)
        y = cen * lax.rsqrt(var + NORM_EPS) * gn_ref[:, cols]
        gate = g_ref[:, cols].astype(F32)
        o_ref[:, cols] = (y * (gate * jax.nn.sigmoid(gate))).astype(o_ref.dtype)


def retention(proj, cos, sin, ret_norm_g, *, batch, seq, ret_width):
    heads = RET_HEADS
    hps = min(RET_HEADS_PER_STEP, heads)
    assert heads % hps == 0
    dk = ret_width // heads
    blk = min(RET_BLOCK, seq)
    nblk = seq // blk
    hb = heads // hps
    log_gamma = jnp.log(1.0 - jnp.exp2(-5.0 - jnp.arange(heads, dtype=F32)))

    def col(seg):
        return pl.BlockSpec((blk, hps * dk), lambda h, b, c, lg, seg=seg: (b * nblk + c, seg * hb + h))

    rowspec = pl.BlockSpec((blk, dk // 2), lambda h, b, c, lg: (b * nblk + c, 0))
    return pl.pallas_call(
        functools.partial(_retention_body, blk=blk, dk=dk, hps=hps),
        grid_spec=pltpu.PrefetchScalarGridSpec(
            num_scalar_prefetch=1,
            grid=(heads // hps, batch, nblk),
            in_specs=[col(0), col(1), col(2), col(3), rowspec, rowspec,
                      pl.BlockSpec((1, hps * dk), lambda h, b, c, lg: (0, h))],
            out_specs=pl.BlockSpec((blk, hps * dk), lambda h, b, c, lg: (b * nblk + c, h)),
            scratch_shapes=[pltpu.VMEM((hps, dk, dk), F32), pltpu.VMEM((hps, blk, blk), F32)],
        ),
        out_shape=jax.ShapeDtypeStruct((batch * seq, ret_width), BF16),
        compiler_params=_params(("arbitrary", "arbitrary", "arbitrary")),
        name="retention",
    )(log_gamma, proj, proj, proj, proj, cos, sin, ret_norm_g.reshape(1, ret_width))


def _gmlp_body(u_ref, v_ref, lng_ref, lnb_ref, ws_ref, bst_ref, o_ref, vn_ref, *, rows, groups, cg, chunk):
    v = jax.nn.gelu(v_ref[...].astype(F32))
    mu = jnp.mean(v, axis=-1, keepdims=True)
    cen = v - mu
    var = jnp.mean(cen * cen, axis=-1, keepdims=True)
    vn_ref[...] = (cen * lax.rsqrt(var + NORM_EPS) * lng_ref[...] + lnb_ref[...]).astype(BF16)

    ri = lax.broadcasted_iota(I32, (chunk, chunk), 0)
    ci = lax.broadcasted_iota(I32, (chunk, chunk), 1)
    causal = ri >= ci
    for g in range(groups):
        w = jnp.where(causal, ws_ref[g], 0.0).astype(BF16)
        bias = bst_ref[:, g:g + 1]
        cols = slice(g * cg, (g + 1) * cg)
        for t in range(rows // chunk):
            rws = slice(t * chunk, (t + 1) * chunk)
            sp = jnp.dot(w, vn_ref[rws, cols], preferred_element_type=F32) + bias
            u = jax.nn.gelu(u_ref[rws, cols].astype(F32))
            o_ref[rws, cols] = (u * sp).astype(o_ref.dtype)


def gmlp(proj, ln_g, ln_b, ws, bs, *, n_rows, gm_width, u_block, v_block):
    groups, chunk = GM_GROUPS, GM_CHUNK
    cg = gm_width // groups
    rows = 2 * chunk
    return pl.pallas_call(
        functools.partial(_gmlp_body, rows=rows, groups=groups, cg=cg, chunk=chunk),
        grid=(n_rows // rows,),
        in_specs=[
            pl.BlockSpec((rows, gm_width), lambda i: (i, u_block)),
            pl.BlockSpec((rows, gm_width), lambda i: (i, v_block)),
            pl.BlockSpec((1, gm_width), lambda i: (0, 0)),
            pl.BlockSpec((1, gm_width), lambda i: (0, 0)),
            pl.BlockSpec((groups, chunk, chunk), lambda i: (0, 0, 0)),
            pl.BlockSpec((chunk, groups), lambda i: (0, 0)),
        ],
        out_specs=pl.BlockSpec((rows, gm_width), lambda i: (i, 0)),
        out_shape=jax.ShapeDtypeStruct((n_rows, gm_width), BF16),
        scratch_shapes=[pltpu.VMEM((rows, gm_width), BF16)],
        compiler_params=_params(("parallel",)),
        name="gmlp",
    )(proj, proj, ln_g.reshape(1, gm_width), ln_b.reshape(1, gm_width), ws, bs.T)


def _matmul_residual_body(*refs, n_parts, n_side, n_j, slabs):
    a_refs, w_refs = refs[:n_parts], refs[n_parts:2 * n_parts]
    r_ref = refs[2 * n_parts]
    side_in = refs[2 * n_parts + 1:2 * n_parts + 1 + n_side]
    o_ref = refs[2 * n_parts + 1 + n_side]
    side_out = refs[2 * n_parts + 2 + n_side:]
    acc = r_ref[...]
    for a_ref, w_ref in zip(a_refs, w_refs):
        acc = acc + jnp.dot(a_ref[...], w_ref[...], preferred_element_type=F32)
    o_ref[...] = acc
    _side_cast(side_in, side_out, n_j, slabs)


def matmul_residual(parts, w, res, *, tm, tn, side=()):
    m, n = res.shape
    tm, tn = _tile(m, tm), _tile(n, tn)
    n_j = n // tn
    slabs = _side_slabs(side, (m // tm) * n_j)
    kp = parts[0].shape[1]
    n_parts = len(parts)
    a_specs = [pl.BlockSpec((tm, kp), lambda i, j: (i, 0)) for _ in parts]
    w_specs = [pl.BlockSpec((kp, tn), lambda i, j, p=p: (p, j)) for p in range(n_parts)]
    side_specs = _side_specs(side, n_j, slabs)
    out = pl.pallas_call(
        functools.partial(_matmul_residual_body, n_parts=n_parts, n_side=len(side), n_j=n_j, slabs=slabs),
        grid=(m // tm, n_j),
        in_specs=a_specs + w_specs + [pl.BlockSpec((tm, tn), lambda i, j: (i, j))] + side_specs,
        out_specs=[pl.BlockSpec((tm, tn), lambda i, j: (i, j))] + side_specs,
        out_shape=[jax.ShapeDtypeStruct((m, n), F32)] + _side_out_shapes(side),
        compiler_params=_params(("arbitrary", "arbitrary") if side else ("parallel", "parallel")),
        name="matmul_residual",
    )(*parts, *([w] * n_parts), res, *side)
    return out[0], out[1:]


def _cross_attention_body(h_ref, g_ref, wq_ref, kv_ref, wo_ref, o_ref, *, heads, dh):
    width = heads * dh
    x = h_ref[...]
    ms = jnp.mean(x * x, axis=-1, keepdims=True)
    xn = ((x * lax.rsqrt(ms + NORM_EPS)) * g_ref[...]).astype(BF16)
    q = jnp.dot(xn, wq_ref[...], preferred_element_type=F32).astype(BF16)
    outs = []
    for h in range(heads):
        k = kv_ref[:, h * dh:(h + 1) * dh]
        v = kv_ref[:, width + h * dh:width + (h + 1) * dh]
        s = lax.dot_general(q[:, h * dh:(h + 1) * dh], k, (((1,), (1,)), ((), ())),
                            preferred_element_type=F32) * (dh ** -0.5)
        e = jnp.exp(s - jnp.max(s, axis=-1, keepdims=True))
        p = e / jnp.sum(e, axis=-1, keepdims=True)
        outs.append(jnp.dot(p.astype(BF16), v, preferred_element_type=F32).astype(BF16))
    o = jnp.concatenate(outs, axis=-1)
    o_ref[...] = x + jnp.dot(o, wo_ref[...], preferred_element_type=F32)


def cross_attention_block(h, g, wq, kv, wo, *, seq, mem_len, tm):
    n, d = h.shape
    heads = XA_HEADS
    width = wq.shape[1]
    dh = width // heads
    tm = _tile(seq, tm)
    per_batch = seq // tm

    def resident(shape):
        return pl.BlockSpec(shape, lambda i: (0, 0), pipeline_mode=pl.Buffered(1))

    return pl.pallas_call(
        functools.partial(_cross_attention_body, heads=heads, dh=dh),
        grid=(n // tm,),
        in_specs=[
            pl.BlockSpec((tm, d), lambda i: (i, 0)),
            resident((1, d)),
            resident((d, width)),
            pl.BlockSpec((mem_len, 2 * width), lambda i: (i // per_batch, 0)),
            resident((width, d)),
        ],
        out_specs=pl.BlockSpec((tm, d), lambda i: (i, 0)),
        out_shape=jax.ShapeDtypeStruct((n, d), F32),
        compiler_params=_params(("parallel",)),
        name="cross_attention",
    )(h, g.reshape(1, d), wq, kv, wo)


def _router_body(h_ref, g_ref, wr_ref, br_ref, xp_ref, meta_ref, meta_t_ref, cnt_ref, base_ref, *, tm, n_grp, per):
    @pl.when(pl.program_id(0) == 0)
    def _():
        base_ref[...] = jnp.zeros_like(base_ref)

    x = h_ref[...]
    ms = jnp.mean(x * x, axis=-1, keepdims=True)
    xn = (x * lax.rsqrt(ms + NORM_EPS)) * g_ref[...]
    xp_ref[...] = _pack_halves(xn)

    x_hi = xn.astype(BF16)
    x_lo = (xn - x_hi.astype(F32)).astype(BF16)
    both = jnp.dot(x_hi, wr_ref[...], preferred_element_type=F32)
    corr = jnp.dot(x_lo, wr_ref[:, :V7X_LANES], preferred_element_type=F32)
    logits = both[:, :V7X_LANES] + (both[:, V7X_LANES:] + corr) + br_ref[...]
    lane = lax.broadcasted_iota(I32, logits.shape, 1)
    neg = jnp.float32(-1e30)
    big = jnp.int32(V7X_LANES)

    gl = jnp.where(lane < n_grp, logits, neg)
    gmax = jnp.max(gl, axis=-1, keepdims=True)
    gidx = jnp.min(jnp.where(gl == gmax, lane, big), axis=-1, keepdims=True)
    grp_gate = 1.0 / jnp.sum(jnp.exp(gl - gmax), axis=-1, keepdims=True)

    lo = n_grp + gidx * per
    el = jnp.where((lane >= lo) & (lane < lo + per), logits, neg)
    v1 = jnp.max(el, axis=-1, keepdims=True)
    i1 = jnp.min(jnp.where(el == v1, lane, big), axis=-1, keepdims=True)
    el2 = jnp.where(lane == i1, neg, el)
    v2 = jnp.max(el2, axis=-1, keepdims=True)
    i2 = jnp.min(jnp.where(el2 == v2, lane, big), axis=-1, keepdims=True)
    t = jnp.exp(v2 - v1)
    den = 1.0 + t
    g1 = grp_gate / den
    g2 = grp_gate * (t / den)

    oh1 = jnp.where(lane == i1, 1.0, 0.0)
    oh2 = jnp.where(lane == i2, 1.0, 0.0)
    ri = lax.broadcasted_iota(I32, (tm, tm), 0)
    ci = lax.broadcasted_iota(I32, (tm, tm), 1)
    lower = jnp.where(ri > ci, 1.0, 0.0).astype(BF16)
    pre1 = jnp.dot(lower, oh1.astype(BF16), preferred_element_type=F32)
    pre2 = jnp.dot(lower, oh2.astype(BF16), preferred_element_type=F32)
    cnt1 = jnp.sum(oh1, axis=0, keepdims=True)
    cnt2 = jnp.sum(oh2, axis=0, keepdims=True)
    base = base_ref[...]
    rank1 = jnp.sum(oh1 * (pre1 + base), axis=-1, keepdims=True)
    rank2 = jnp.sum(oh2 * (pre2 + base + cnt1), axis=-1, keepdims=True)
    total = base + cnt1 + cnt2
    base_ref[...] = total
    cnt_ref[...] = total

    e1 = (i1 - n_grp).astype(F32)
    e2 = (i2 - n_grp).astype(F32)
    meta = jnp.zeros(logits.shape, F32)
    for idx, val in enumerate((e1, e2, rank1, rank2, g1, g2)):
        meta = jnp.where(lane == idx, val, meta)
    meta_ref[...] = meta
    meta_t_ref[...] = jnp.transpose(meta)[:META_ROWS, :]


META_ROWS = 8


def moe_router(h, g, wr, br, *, tm):
    n, d = h.shape
    tm = min(tm, n)
    return pl.pallas_call(
        functools.partial(_router_body, tm=tm, n_grp=MOE_GROUPS, per=MOE_PER_GROUP),
        grid=(n // tm,),
        in_specs=[
            pl.BlockSpec((tm, d), lambda i: (i, 0)),
            pl.BlockSpec((1, d), lambda i: (0, 0)),
            pl.BlockSpec((d, 2 * V7X_LANES), lambda i: (0, 0)),
            pl.BlockSpec((1, V7X_LANES), lambda i: (0, 0)),
        ],
        out_specs=[
            pl.BlockSpec((tm, d // 2), lambda i: (i, 0)),
            pl.BlockSpec((tm, V7X_LANES), lambda i: (i, 0)),
            pl.BlockSpec((META_ROWS, tm), lambda i: (0, i)),
            pl.BlockSpec((1, V7X_LANES), lambda i: (0, 0)),
        ],
        out_shape=[
            jax.ShapeDtypeStruct((n, d // 2), U32),
            jax.ShapeDtypeStruct((n, V7X_LANES), F32),
            jax.ShapeDtypeStruct((META_ROWS, n), F32),
            jax.ShapeDtypeStruct((1, V7X_LANES), F32),
        ],
        scratch_shapes=[pltpu.VMEM((1, V7X_LANES), F32)],
        compiler_params=_params(("arbitrary",)),
        name="moe_router",
    )(h, g.reshape(1, d), wr, br)


MOE_CODE_SHIFT = 6
assert MOE_GROUPS * MOE_PER_GROUP == 1 << MOE_CODE_SHIFT


def _order_body(code_ref, start_ref, order_ref):
    def body(a, carry):
        code = code_ref[a]
        order_ref[start_ref[code & ((1 << MOE_CODE_SHIFT) - 1)] + (code >> MOE_CODE_SHIFT)] = a
        return carry

    lax.fori_loop(0, code_ref.shape[0], body, 0, unroll=8)


def moe_order(code, start):
    return pl.pallas_call(
        _order_body,
        grid_spec=pltpu.PrefetchScalarGridSpec(
            num_scalar_prefetch=2,
            grid=(1,),
            in_specs=[],
            out_specs=pl.BlockSpec(memory_space=pltpu.SMEM),
        ),
        out_shape=jax.ShapeDtypeStruct(code.shape, I32),
        compiler_params=_params(("arbitrary",)),
        name="moe_order",
    )(code, start)


EXPERT_STEPS = 1
ROW_DMA_UNROLL = 8
WEIGHT_DMA_PRIORITY = 1
assert MOE_TOPK == 2


def _experts_body(exp_ref, nxt_ref, rows_ref, off_ref, order_ref, nact_ref,
                  xp_hbm, wg_hbm, wu_hbm, wdn_hbm, ya_hbm,
                  xbuf_ref, ybuf_ref, hd_ref, ring_in, ring_dn, gsem, ssem, sem_in, sem_dn,
                  *, rb, n_tok, n_blocks):
    v = pl.program_id(0)
    s = pl.program_id(1)
    nact = nact_ref[0]
    active = v < nact
    slot = lax.rem(v, 2)
    half = rb // 2
    oc = ring_dn.shape[2]
    dq = ring_in.shape[1]

    def in_copy(e, i):
        src = (wg_hbm, wg_hbm, wu_hbm, wu_hbm)[i % 4]
        q = 2 * (i // 4) + i % 2
        return pltpu.make_async_copy(src.at[e, pl.ds(q * dq, dq), :], ring_in.at[i], sem_in.at[i])

    def dn_copy(e, i):
        col = (i % 2) * 2 * oc + (i // 2) * oc
        return pltpu.make_async_copy(wdn_hbm.at[e, :, pl.ds(col, oc)], ring_dn.at[i], sem_dn.at[i])

    def refill(copy, slots):
        @pl.when(v + 1 < nact)
        def _():
            for i in slots:
                copy(nxt_ref[v], i).start(priority=WEIGHT_DMA_PRIORITY)

    def decode(base, r):
        a = order_ref[base + r]
        k = jnp.where(a >= n_tok, 1, 0)
        return k, a - k * n_tok

    def gather_copy(base, buf, r):
        _, tok = decode(base, r)
        return pltpu.make_async_copy(xp_hbm.at[pl.ds(tok, 1)], xbuf_ref.at[buf, pl.ds(r, 1)], gsem.at[buf])

    def scatter_copy(base, r):
        k, tok = decode(base, r)
        return pltpu.make_async_copy(ybuf_ref.at[pl.ds(r, 1)], ya_hbm.at[k, pl.ds(tok, 1)], ssem)

    def for_rows(lo, hi, fn):
        groups = lax.shift_right_logical(jnp.maximum(hi - lo, 0), ROW_DMA_UNROLL.bit_length() - 1)

        def group(g, carry):
            for u in range(ROW_DMA_UNROLL):
                fn(lo + g * ROW_DMA_UNROLL + u)
            return carry

        def single(r, carry):
            fn(r)
            return carry

        lax.fori_loop(0, groups, group, 0)
        lax.fori_loop(lo + groups * ROW_DMA_UNROLL, hi, single, 0)

    @pl.when(jnp.logical_and(v == 0, s == 0))
    def _():
        for i in range(ring_in.shape[0]):
            in_copy(exp_ref[0], i).start(priority=WEIGHT_DMA_PRIORITY)
        for i in range(ring_dn.shape[0]):
            dn_copy(exp_ref[0], i).start(priority=WEIGHT_DMA_PRIORITY)
        xbuf_ref[...] = jnp.zeros_like(xbuf_ref)
        base = off_ref[0]
        for_rows(0, rows_ref[0], lambda r: gather_copy(base, 0, r).start())

    @pl.when(jnp.logical_and(active, s == 0))
    def _():
        base = off_ref[v]
        for_rows(0, rows_ref[v], lambda r: gather_copy(base, slot, r).wait())

    @pl.when(jnp.logical_and(active, v + 1 < nact))
    def _():
        nxt = jnp.minimum(v + 1, n_blocks - 1)
        base = off_ref[nxt]
        per_step = rb // EXPERT_STEPS
        for_rows(s * per_step, jnp.minimum((s + 1) * per_step, rows_ref[nxt]),
                 lambda r: gather_copy(base, 1 - slot, r).start())

    halves = [(pl.ds(0, half), None), (pl.ds(half, half), rows_ref[v] > half)]

    def for_halves(fn):
        for rows, cond in halves:
            if cond is None:
                fn(rows)
            else:
                pl.when(cond)(functools.partial(fn, rows))

    @pl.when(jnp.logical_and(active, s == 0))
    def _():
        slots = range(ring_in.shape[0])
        for i in slots:
            in_copy(exp_ref[v], i).wait()

        def gate_up(rows):
            lo, hi = _unpack_halves(xbuf_ref[slot, rows, :])
            x_half = (lo.astype(BF16), hi.astype(BF16))
            per_half = x_half[0].shape[1] // dq
            acc = [None, None]
            for i in slots:
                q = 2 * (i // 4) + i % 2
                xq = x_half[q // per_half][:, (q % per_half) * dq:(q % per_half + 1) * dq]
                part = jnp.dot(xq, ring_in[i].astype(BF16), preferred_element_type=F32)
                which = (i % 4) // 2
                acc[which] = part if acc[which] is None else acc[which] + part
            hg, hu = acc
            hd_ref[rows, :] = (hg * jax.nn.sigmoid(hg) * hu).astype(BF16)
        for_halves(gate_up)
        refill(in_copy, slots)

    @pl.when(jnp.logical_and(active, jnp.logical_and(s == EXPERT_STEPS - 1, v >= 1)))
    def _():
        prev = jnp.maximum(v - 1, 0)
        base = off_ref[prev]
        for_rows(0, rows_ref[prev], lambda r: scatter_copy(base, r).wait())

    @pl.when(jnp.logical_and(active, s == EXPERT_STEPS - 1))
    def _():
        slots = range(ring_dn.shape[0])
        for i in slots:
            dn_copy(exp_ref[v], i).wait()

        def down(rows):
            hd = hd_ref[rows, :]
            for j in range(ring_dn.shape[0] // 2):
                y_lo = jnp.dot(hd, ring_dn[2 * j].astype(BF16), preferred_element_type=F32)
                y_hi = jnp.dot(hd, ring_dn[2 * j + 1].astype(BF16), preferred_element_type=F32)
                ybuf_ref[rows, pl.ds(j * oc, oc)] = _pack_halves(jnp.concatenate([y_lo, y_hi], axis=-1))
        for_halves(down)
        refill(dn_copy, slots)
        base = off_ref[v]
        for_rows(0, rows_ref[v], lambda r: scatter_copy(base, r).start())

    @pl.when(jnp.logical_and(v == n_blocks - 1, s == EXPERT_STEPS - 1))
    def _():
        last = jnp.maximum(nact - 1, 0)
        base = off_ref[last]
        for_rows(0, rows_ref[last], lambda r: scatter_copy(base, r).wait())


def moe_experts(xp, w_gate, w_up, w_down, blk_exp, blk_next, blk_rows, blk_off, order, nact, *, n_blocks, rb):
    n_exp, d, ff = w_gate.shape
    n_tok, kw = xp.shape
    assert d == 2 * kw
    oc = kw // 2
    n_in, n_dn = 8, 4
    any_spec = pl.BlockSpec(memory_space=pl.ANY)
    return pl.pallas_call(
        functools.partial(_experts_body, rb=rb, n_tok=n_tok, n_blocks=n_blocks),
        grid_spec=pltpu.PrefetchScalarGridSpec(
            num_scalar_prefetch=6,
            grid=(n_blocks, EXPERT_STEPS),
            in_specs=[any_spec, any_spec, any_spec, any_spec],
            out_specs=any_spec,
            scratch_shapes=[pltpu.VMEM((2, rb, kw), U32), pltpu.VMEM((rb, kw), U32),
                            pltpu.VMEM((rb, ff), BF16),
                            pltpu.VMEM((n_in, d // 4, ff), F32), pltpu.VMEM((n_dn, ff, oc), F32),
                            pltpu.SemaphoreType.DMA((2,)), pltpu.SemaphoreType.DMA(()),
                            pltpu.SemaphoreType.DMA((n_in,)), pltpu.SemaphoreType.DMA((n_dn,))],
        ),
        out_shape=jax.ShapeDtypeStruct((MOE_TOPK, n_tok, kw), U32),
        compiler_params=_params(("arbitrary", "arbitrary")),
        name="moe_experts",
    )(blk_exp, blk_next, blk_rows, blk_off, order, nact, xp, w_gate, w_up, w_down)


def _combine_body(h_ref, meta_ref, g_ref, y0_ref, y1_ref, o_ref):
    meta = meta_ref[...]
    acc_lo = None
    acc_hi = None
    for k, y_ref in enumerate((y0_ref, y1_ref)):
        gate = meta[:, 2 * MOE_TOPK + k:2 * MOE_TOPK + k + 1]
        lo, hi = _unpack_halves(y_ref[0])
        lo, hi = lo * gate, hi * gate
        acc_lo = lo if acc_lo is None else acc_lo + lo
        acc_hi = hi if acc_hi is None else acc_hi + hi
    kh = acc_lo.shape[1]
    h_lo = h_ref[:, :kh] + acc_lo
    h_hi = h_ref[:, kh:] + acc_hi
    ms = (jnp.sum(h_lo * h_lo, axis=-1, keepdims=True) + jnp.sum(h_hi * h_hi, axis=-1, keepdims=True)) / (2 * kh)
    scale = lax.rsqrt(ms + NORM_EPS)
    o_ref[:, :kh] = (h_lo * scale) * g_ref[:, :kh]
    o_ref[:, kh:] = (h_hi * scale) * g_ref[:, kh:]


def moe_combine(h, meta, g, ya, *, tb):
    n, d = h.shape
    kw = ya.shape[2]
    tb = _tile(n, tb)

    def slot(k):
        return pl.BlockSpec((1, tb, kw), lambda i, k=k: (k, i, 0))

    return pl.pallas_call(
        _combine_body,
        grid=(n // tb,),
        in_specs=[
            pl.BlockSpec((tb, d), lambda i: (i, 0)),
            pl.BlockSpec((tb, V7X_LANES), lambda i: (i, 0)),
            pl.BlockSpec((1, d), lambda i: (0, 0)),
            slot(0), slot(1),
        ],
        out_specs=pl.BlockSpec((tb, d), lambda i: (i, 0)),
        out_shape=jax.ShapeDtypeStruct((n, d), F32),
        compiler_params=_params(("parallel",)),
        name="moe_combine",
    )(h, meta, g.reshape(1, d), ya, ya)


def _moe_plan(counts, *, rb, n_blocks):
    n_exp = counts.shape[0]
    per_exp = (counts + rb - 1) // rb
    blk_end = jnp.cumsum(per_exp)
    start = jnp.cumsum(counts) - counts
    nact = blk_end[-1].astype(I32)
    blk = jnp.arange(n_blocks, dtype=I32)
    blk_exp = jnp.minimum(jnp.sum(blk_end[None, :] <= blk[:, None], axis=1), n_exp - 1).astype(I32)
    active = blk < nact
    last = jnp.maximum(nact - 1, 0)
    blk_exp = jnp.where(active, blk_exp, blk_exp[last]).astype(I32)
    blk_next = blk_exp[jnp.minimum(blk + 1, last)]
    j = blk - (blk_end[blk_exp] - per_exp[blk_exp])
    blk_off = jnp.where(active, start[blk_exp] + j * rb, 0).astype(I32)
    blk_rows = jnp.where(active, jnp.clip(counts[blk_exp] - j * rb, 0, rb), 0).astype(I32)
    return start.astype(I32), blk_exp, blk_next, blk_rows, blk_off, nact.reshape(1)


def kernel(x, mem, positions, norm_mix_g, w_in, ret_norm_g, gm_ln_g, gm_ln_b, gm_ws, gm_bs, w_out, norm_xa_g, norm_mem_g, xa_wq, xa_wkv, xa_wo, norm_moe_g, router_grp_w, router_grp_b, router_exp_w, router_exp_b, moe_w_gate, moe_w_up, moe_w_down, norm_final_g):
    batch, seq, d = x.shape
    mem_len = mem.shape[1]
    n = batch * seq
    ret_width = ret_norm_g.shape[0]
    gm_width = gm_ln_g.shape[0]
    assert ret_width == gm_width and w_in.shape[1] == 4 * ret_width + 2 * gm_width
    dk = ret_width // RET_HEADS
    n_exp = moe_w_gate.shape[0]
    assert n_exp == MOE_GROUPS * MOE_PER_GROUP and MOE_GROUPS + n_exp <= V7X_LANES

    inv_freq = ROPE_BASE ** (-jnp.arange(0, dk, 2, dtype=F32) / dk)
    ang = positions.astype(F32).reshape(n, 1) * inv_freq
    cos, sin = jnp.cos(ang), jnp.sin(ang)

    x2 = x.reshape(n, d)
    xn = rmsnorm_bf16(x2, norm_mix_g, tm=512)
    proj, (w_out_b,) = matmul_wcast(xn, w_in, tm=1024, tn=512, side=(w_out,))
    ret = retention(proj, cos, sin, ret_norm_g, batch=batch, seq=seq, ret_width=ret_width)
    gm = gmlp(proj, gm_ln_g, gm_ln_b, gm_ws, gm_bs, n_rows=n, gm_width=gm_width,
              u_block=4 * ret_width // gm_width, v_block=4 * ret_width // gm_width + 1)
    h1, (wq_b, wkv_b, wo_b) = matmul_residual([ret, gm], w_out_b, x2, tm=512, tn=1024,
                                              side=(xa_wq, xa_wkv, xa_wo))

    kv, _ = norm_matmul(mem.reshape(batch * mem_len, d), norm_mem_g, wkv_b, tm=512, tn=1024)
    h2 = cross_attention_block(h1, norm_xa_g, wq_b, kv, wo_b, seq=seq, mem_len=mem_len, tm=256)

    pad = V7X_LANES - MOE_GROUPS - n_exp
    wr = jnp.concatenate([router_grp_w, router_exp_w, jnp.zeros((d, pad), F32)], axis=1)
    br = jnp.concatenate([router_grp_b, router_exp_b, jnp.zeros((pad,), F32)]).reshape(1, V7X_LANES)
    wr_hi = wr.astype(BF16)
    wr_lo = (wr - wr_hi.astype(F32)).astype(BF16)
    xp, meta, meta_t, cnt = moe_router(h2, norm_moe_g, jnp.concatenate([wr_hi, wr_lo], axis=1), br, tm=256)

    rb = MOE_ROW_BLOCK
    n_blocks = -(-(n * MOE_TOPK) // rb) + n_exp
    counts = cnt[0, MOE_GROUPS:MOE_GROUPS + n_exp].astype(I32)
    start, blk_exp, blk_next, blk_rows, blk_off, nact = _moe_plan(counts, rb=rb, n_blocks=n_blocks)
    code = (meta_t[MOE_TOPK:2 * MOE_TOPK].astype(I32) * (1 << MOE_CODE_SHIFT)
            + meta_t[0:MOE_TOPK].astype(I32)).reshape(-1)
    order = moe_order(code, start)
    ya = moe_experts(xp, moe_w_gate, moe_w_up, moe_w_down, blk_exp, blk_next, blk_rows, blk_off, order, nact,
                     n_blocks=n_blocks, rb=rb)
    y = moe_combine(h2, meta, norm_final_g, ya, tb=256)
    return y.reshape(batch, seq, d)
```

```python
import functools

import jax
import jax.numpy as jnp
from jax import lax
from jax.experimental import pallas as pl
from jax.experimental.pallas import tpu as pltpu

NORM_EPS = 1e-6
RET_HEADS = 8
ROPE_BASE = 10000.0
GM_GROUPS = 8
GM_CHUNK = 128
XA_HEADS = 4
MOE_GROUPS = 8
MOE_PER_GROUP = 8
MOE_TOPK = 2

V7X_LANES = 128
V7X_VMEM_BYTES = 64 * 1024 * 1024
VMEM_LIMIT_BYTES = 56 * 1024 * 1024
EXPERTS_VMEM_LIMIT_BYTES = 58 * 1024 * 1024

RET_BLOCK = 512
MOE_ROW_BLOCK = 512

F32 = jnp.float32
BF16 = jnp.bfloat16
U32 = jnp.uint32
I32 = jnp.int32


def _params(sem, vmem_limit_bytes=VMEM_LIMIT_BYTES):
    return pltpu.CompilerParams(dimension_semantics=sem, vmem_limit_bytes=vmem_limit_bytes)


def _tile(dim, target):
    t = min(dim, target)
    while dim % t:
        t -= V7X_LANES
    assert t > 0, (dim, target)
    return t


def _pack_halves(x_f32):
    k = x_f32.shape[-1] // 2
    bits = lax.bitcast_convert_type(x_f32.astype(BF16).astype(F32), U32)
    return (bits[:, k:] & jnp.uint32(0xFFFF0000)) | (bits[:, :k] >> 16)


def _unpack_halves(w_u32):
    lo = lax.bitcast_convert_type(w_u32 << 16, F32)
    hi = lax.bitcast_convert_type(w_u32 & jnp.uint32(0xFFFF0000), F32)
    return lo, hi


SIDE_CAST_SLABS = 64
BF16_SUBLANES = 16


def _side_slabs(side, steps):
    n = SIDE_CAST_SLABS
    while n > steps or any(a.shape[0] % (BF16_SUBLANES * n) for a in side):
        n //= 2
    assert n >= 1, [a.shape for a in side]
    return n


def _side_specs(side, n_j, slabs):
    return [pl.BlockSpec((a.shape[0] // slabs, a.shape[1]), lambda i, j: (jnp.minimum(i * n_j + j, slabs - 1), 0))
            for a in side]


def _side_cast(in_refs, out_refs, n_j, slabs):
    if not in_refs:
        return

    @pl.when(pl.program_id(0) * n_j + pl.program_id(1) < slabs)
    def _():
        for src, dst in zip(in_refs, out_refs):
            dst[...] = src[...].astype(dst.dtype)


def _side_out_shapes(side):
    return [jax.ShapeDtypeStruct(a.shape, BF16) for a in side]


def _norm_matmul_body(*refs, n_side, n_j, slabs):
    x_ref, g_ref, w_ref = refs[:3]
    side_in = refs[3:3 + n_side]
    o_ref = refs[3 + n_side]
    side_out = refs[4 + n_side:4 + 2 * n_side]
    xn_ref = refs[4 + 2 * n_side]

    @pl.when(pl.program_id(1) == 0)
    def _():
        x = x_ref[...]
        ms = jnp.mean(x * x, axis=-1, keepdims=True)
        xn_ref[...] = ((x * lax.rsqrt(ms + NORM_EPS)) * g_ref[...]).astype(BF16)

    o_ref[...] = jnp.dot(xn_ref[...], w_ref[...], preferred_element_type=F32).astype(o_ref.dtype)
    _side_cast(side_in, side_out, n_j, slabs)


def norm_matmul(x, g, w, *, tm, tn, side=()):
    m, k = x.shape
    n = w.shape[1]
    tm, tn = _tile(m, tm), _tile(n, tn)
    n_j = n // tn
    slabs = _side_slabs(side, (m // tm) * n_j)
    side_specs = _side_specs(side, n_j, slabs)
    out = pl.pallas_call(
        functools.partial(_norm_matmul_body, n_side=len(side), n_j=n_j, slabs=slabs),
        grid=(m // tm, n_j),
        in_specs=[
            pl.BlockSpec((tm, k), lambda i, j: (i, 0)),
            pl.BlockSpec((1, k), lambda i, j: (0, 0)),
            pl.BlockSpec((k, tn), lambda i, j: (0, j)),
        ] + side_specs,
        out_specs=[pl.BlockSpec((tm, tn), lambda i, j: (i, j))] + side_specs,
        out_shape=[jax.ShapeDtypeStruct((m, n), BF16)] + _side_out_shapes(side),
        scratch_shapes=[pltpu.VMEM((tm, k), BF16)],
        compiler_params=_params(("arbitrary", "arbitrary") if side else ("parallel", "arbitrary")),
        name="norm_matmul",
    )(x, g.reshape(1, k), w, *side)
    return out[0], out[1:]


def _rmsnorm_body(x_ref, g_ref, o_ref):
    x = x_ref[...]
    ms = jnp.mean(x * x, axis=-1, keepdims=True)
    o_ref[...] = ((x * lax.rsqrt(ms + NORM_EPS)) * g_ref[...]).astype(o_ref.dtype)


def rmsnorm_bf16(x, g, *, tm):
    m, k = x.shape
    tm = _tile(m, tm)
    return pl.pallas_call(
        _rmsnorm_body,
        grid=(m // tm,),
        in_specs=[pl.BlockSpec((tm, k), lambda i: (i, 0)), pl.BlockSpec((1, k), lambda i: (0, 0))],
        out_specs=pl.BlockSpec((tm, k), lambda i: (i, 0)),
        out_shape=jax.ShapeDtypeStruct((m, k), BF16),
        compiler_params=_params(("parallel",)),
        name="rmsnorm",
    )(x, g.reshape(1, k))


def _matmul_wcast_body(*refs, n_side, n_i, slabs):
    a_ref, w_ref = refs[:2]
    side_in = refs[2:2 + n_side]
    o_ref = refs[2 + n_side]
    side_out = refs[3 + n_side:3 + 2 * n_side]
    wb_ref = refs[3 + 2 * n_side]

    @pl.when(pl.program_id(1) == 0)
    def _():
        wb_ref[...] = w_ref[...].astype(BF16)

    o_ref[...] = jnp.dot(a_ref[...], wb_ref[...], preferred_element_type=F32).astype(o_ref.dtype)
    _side_cast(side_in, side_out, n_i, slabs)


def matmul_wcast(a, w, *, tm, tn, side=()):
    m, k = a.shape
    n = w.shape[1]
    tm, tn = _tile(m, tm), _tile(n, tn)
    n_i = m // tm
    slabs = _side_slabs(side, (n // tn) * n_i)
    side_specs = _side_specs(side, n_i, slabs)
    out = pl.pallas_call(
        functools.partial(_matmul_wcast_body, n_side=len(side), n_i=n_i, slabs=slabs),
        grid=(n // tn, n_i),
        in_specs=[pl.BlockSpec((tm, k), lambda j, i: (i, 0)), pl.BlockSpec((k, tn), lambda j, i: (0, j))] + side_specs,
        out_specs=[pl.BlockSpec((tm, tn), lambda j, i: (i, j))] + side_specs,
        out_shape=[jax.ShapeDtypeStruct((m, n), BF16)] + _side_out_shapes(side),
        scratch_shapes=[pltpu.VMEM((k, tn), BF16)],
        compiler_params=_params(("arbitrary", "arbitrary")),
        name="matmul_wcast",
    )(a, w, *side)
    return out[0], out[1:]


RET_HEADS_PER_STEP = 4


def _retention_body(lg_ref, q_ref, k_ref, v_ref, g_ref, cos_ref, sin_ref, gn_ref, o_ref, state_ref, dec_ref,
                    *, blk, dk, hps):
    c = pl.program_id(2)
    first_of_pair = jnp.logical_and(pl.program_id(1) == 0, c == 0)
    half = dk // 2
    cos = cos_ref[...]
    sin = sin_ref[...]
    pos = lax.broadcasted_iota(I32, (blk, 1), 0).astype(F32)

    def rot(t):
        t1, t2 = t[:, :half], t[:, half:]
        return jnp.concatenate([t1 * cos - t2 * sin, t1 * sin + t2 * cos], axis=-1)

    @pl.when(c == 0)
    def _():
        state_ref[...] = jnp.zeros_like(state_ref)

    for j in range(hps):
        lg = lg_ref[pl.program_id(0) * hps + j]
        cols = slice(j * dk, (j + 1) * dk)

        @pl.when(first_of_pair)
        def _(j=j, lg=lg):
            ri = lax.broadcasted_iota(I32, (blk, blk), 0)
            ci = lax.broadcasted_iota(I32, (blk, blk), 1)
            diff = (ri - ci).astype(F32)
            dec_ref[j] = jnp.where(diff >= 0.0, jnp.exp(jnp.maximum(diff, 0.0) * lg), 0.0)

        qr = rot(q_ref[:, cols].astype(F32))
        kr = rot(k_ref[:, cols].astype(F32)) * (dk ** -0.5)
        v = v_ref[:, cols]

        q_dec = jnp.exp((pos + 1.0) * lg)
        k_dec = jnp.exp((blk - 1.0 - pos) * lg)
        blk_dec = jnp.exp(jnp.full((1, dk), blk * lg, F32))

        s = lax.dot_general(qr.astype(BF16), kr.astype(BF16), (((1,), (1,)), ((), ())), preferred_element_type=F32)
        inner = jnp.dot((s * dec_ref[j]).astype(BF16), v, preferred_element_type=F32)

        state = state_ref[j]
        cross = jnp.dot((qr * q_dec).astype(BF16), state.astype(BF16), preferred_element_type=F32)
        kd_t = jnp.transpose(kr * k_dec).astype(BF16)
        state_ref[j] = state * blk_dec + jnp.dot(kd_t, v, preferred_element_type=F32)

        out = inner + cross
        mu = jnp.mean(out, axis=-1, keepdims=True)
        cen = out - mu
        var = jnp.mean(cen * cen, axis=-1, keepdims=True)
        y = cen * lax.rsqrt(var + NORM_EPS) * gn_ref[:, cols]
        gate = g_ref[:, cols].astype(F32)
        o_ref[:, cols] = (y * (gate * jax.nn.sigmoid(gate))).astype(o_ref.dtype)


def retention(proj, cos, sin, ret_norm_g, *, batch, seq, ret_width):
    heads = RET_HEADS
    hps = min(RET_HEADS_PER_STEP, heads)
    assert heads % hps == 0
    dk = ret_width // heads
    blk = min(RET_BLOCK, seq)
    nblk = seq // blk
    hb = heads // hps
    log_gamma = jnp.log(1.0 - jnp.exp2(-5.0 - jnp.arange(heads, dtype=F32)))

    def col(seg):
        return pl.BlockSpec((blk, hps * dk), lambda h, b, c, lg, seg=seg: (b * nblk + c, seg * hb + h))

    rowspec = pl.BlockSpec((blk, dk // 2), lambda h, b, c, lg: (b * nblk + c, 0))
    return pl.pallas_call(
        functools.partial(_retention_body, blk=blk, dk=dk, hps=hps),
        grid_spec=pltpu.PrefetchScalarGridSpec(
            num_scalar_prefetch=1,
            grid=(heads // hps, batch, nblk),
            in_specs=[col(0), col(1), col(2), col(3), rowspec, rowspec,
                      pl.BlockSpec((1, hps * dk), lambda h, b, c, lg: (0, h))],
            out_specs=pl.BlockSpec((blk, hps * dk), lambda h, b, c, lg: (b * nblk + c, h)),
            scratch_shapes=[pltpu.VMEM((hps, dk, dk), F32), pltpu.VMEM((hps, blk, blk), F32)],
        ),
        out_shape=jax.ShapeDtypeStruct((batch * seq, ret_width), BF16),
        compiler_params=_params(("arbitrary", "arbitrary", "arbitrary")),
        name="retention",
    )(log_gamma, proj, proj, proj, proj, cos, sin, ret_norm_g.reshape(1, ret_width))


def _gmlp_body(u_ref, v_ref, lng_ref, lnb_ref, ws_ref, bst_ref, o_ref, vn_ref, *, rows, groups, cg, chunk):
    v = jax.nn.gelu(v_ref[...].astype(F32))
    mu = jnp.mean(v, axis=-1, keepdims=True)
    cen = v - mu
    var = jnp.mean(cen * cen, axis=-1, keepdims=True)
    vn_ref[...] = (cen * lax.rsqrt(var + NORM_EPS) * lng_ref[...] + lnb_ref[...]).astype(BF16)

    ri = lax.broadcasted_iota(I32, (chunk, chunk), 0)
    ci = lax.broadcasted_iota(I32, (chunk, chunk), 1)
    causal = ri >= ci
    for g in range(groups):
        w = jnp.where(causal, ws_ref[g], 0.0).astype(BF16)
        bias = bst_ref[:, g:g + 1]
        cols = slice(g * cg, (g + 1) * cg)
        for t in range(rows // chunk):
            rws = slice(t * chunk, (t + 1) * chunk)
            sp = jnp.dot(w, vn_ref[rws, cols], preferred_element_type=F32) + bias
            u = jax.nn.gelu(u_ref[rws, cols].astype(F32))
            o_ref[rws, cols] = (u * sp).astype(o_ref.dtype)


def gmlp(proj, ln_g, ln_b, ws, bs, *, n_rows, gm_width, u_block, v_block):
    groups, chunk = GM_GROUPS, GM_CHUNK
    cg = gm_width // groups
    rows = 2 * chunk
    return pl.pallas_call(
        functools.partial(_gmlp_body, rows=rows, groups=groups, cg=cg, chunk=chunk),
        grid=(n_rows // rows,),
        in_specs=[
            pl.BlockSpec((rows, gm_width), lambda i: (i, u_block)),
            pl.BlockSpec((rows, gm_width), lambda i: (i, v_block)),
            pl.BlockSpec((1, gm_width), lambda i: (0, 0)),
            pl.BlockSpec((1, gm_width), lambda i: (0, 0)),
            pl.BlockSpec((groups, chunk, chunk), lambda i: (0, 0, 0)),
            pl.BlockSpec((chunk, groups), lambda i: (0, 0)),
        ],
        out_specs=pl.BlockSpec((rows, gm_width), lambda i: (i, 0)),
        out_shape=jax.ShapeDtypeStruct((n_rows, gm_width), BF16),
        scratch_shapes=[pltpu.VMEM((rows, gm_width), BF16)],
        compiler_params=_params(("parallel",)),
        name="gmlp",
    )(proj, proj, ln_g.reshape(1, gm_width), ln_b.reshape(1, gm_width), ws, bs.T)


def _matmul_residual_body(*refs, n_parts, n_side, n_j, slabs):
    a_refs, w_refs = refs[:n_parts], refs[n_parts:2 * n_parts]
    r_ref = refs[2 * n_parts]
    side_in = refs[2 * n_parts + 1:2 * n_parts + 1 + n_side]
    o_ref = refs[2 * n_parts + 1 + n_side]
    side_out = refs[2 * n_parts + 2 + n_side:]
    acc = r_ref[...]
    for a_ref, w_ref in zip(a_refs, w_refs):
        acc = acc + jnp.dot(a_ref[...], w_ref[...], preferred_element_type=F32)
    o_ref[...] = acc
    _side_cast(side_in, side_out, n_j, slabs)


def matmul_residual(parts, w, res, *, tm, tn, side=()):
    m, n = res.shape
    tm, tn = _tile(m, tm), _tile(n, tn)
    n_j = n // tn
    slabs = _side_slabs(side, (m // tm) * n_j)
    kp = parts[0].shape[1]
    n_parts = len(parts)
    a_specs = [pl.BlockSpec((tm, kp), lambda i, j: (i, 0)) for _ in parts]
    w_specs = [pl.BlockSpec((kp, tn), lambda i, j, p=p: (p, j)) for p in range(n_parts)]
    side_specs = _side_specs(side, n_j, slabs)
    out = pl.pallas_call(
        functools.partial(_matmul_residual_body, n_parts=n_parts, n_side=len(side), n_j=n_j, slabs=slabs),
        grid=(m // tm, n_j),
        in_specs=a_specs + w_specs + [pl.BlockSpec((tm, tn), lambda i, j: (i, j))] + side_specs,
        out_specs=[pl.BlockSpec((tm, tn), lambda i, j: (i, j))] + side_specs,
        out_shape=[jax.ShapeDtypeStruct((m, n), F32)] + _side_out_shapes(side),
        compiler_params=_params(("arbitrary", "arbitrary") if side else ("parallel", "parallel")),
        name="matmul_residual",
    )(*parts, *([w] * n_parts), res, *side)
    return out[0], out[1:]


def _cross_attention_body(h_ref, g_ref, wq_ref, kv_ref, wo_ref, o_ref, *, heads, dh):
    width = heads * dh
    x = h_ref[...]
    ms = jnp.mean(x * x, axis=-1, keepdims=True)
    xn = ((x * lax.rsqrt(ms + NORM_EPS)) * g_ref[...]).astype(BF16)
    q = jnp.dot(xn, wq_ref[...], preferred_element_type=F32).astype(BF16)
    outs = []
    for h in range(heads):
        k = kv_ref[:, h * dh:(h + 1) * dh]
        v = kv_ref[:, width + h * dh:width + (h + 1) * dh]
        s = lax.dot_general(q[:, h * dh:(h + 1) * dh], k, (((1,), (1,)), ((), ())),
                            preferred_element_type=F32) * (dh ** -0.5)
        e = jnp.exp(s - jnp.max(s, axis=-1, keepdims=True))
        p = e / jnp.sum(e, axis=-1, keepdims=True)
        outs.append(jnp.dot(p.astype(BF16), v, preferred_element_type=F32).astype(BF16))
    o = jnp.concatenate(outs, axis=-1)
    o_ref[...] = x + jnp.dot(o, wo_ref[...], preferred_element_type=F32)


def cross_attention_block(h, g, wq, kv, wo, *, seq, mem_len, tm):
    n, d = h.shape
    heads = XA_HEADS
    width = wq.shape[1]
    dh = width // heads
    tm = _tile(seq, tm)
    per_batch = seq // tm

    def resident(shape):
        return pl.BlockSpec(shape, lambda i: (0, 0), pipeline_mode=pl.Buffered(1))

    return pl.pallas_call(
        functools.partial(_cross_attention_body, heads=heads, dh=dh),
        grid=(n // tm,),
        in_specs=[
            pl.BlockSpec((tm, d), lambda i: (i, 0)),
            resident((1, d)),
            resident((d, width)),
            pl.BlockSpec((mem_len, 2 * width), lambda i: (i // per_batch, 0)),
            resident((width, d)),
        ],
        out_specs=pl.BlockSpec((tm, d), lambda i: (i, 0)),
        out_shape=jax.ShapeDtypeStruct((n, d), F32),
        compiler_params=_params(("parallel",)),
        name="cross_attention",
    )(h, g.reshape(1, d), wq, kv, wo)


def _router_body(h_ref, g_ref, wr_ref, br_ref, xp_ref, meta_ref, meta_t_ref, cnt_ref, base_ref, *, tm, n_grp, per):
    @pl.when(pl.program_id(0) == 0)
    def _():
        base_ref[...] = jnp.zeros_like(base_ref)

    x = h_ref[...]
    ms = jnp.mean(x * x, axis=-1, keepdims=True)
    xn = (x * lax.rsqrt(ms + NORM_EPS)) * g_ref[...]
    xp_ref[...] = _pack_halves(xn)

    x_hi = xn.astype(BF16)
    x_lo = (xn - x_hi.astype(F32)).astype(BF16)
    both = jnp.dot(x_hi, wr_ref[...], preferred_element_type=F32)
    corr = jnp.dot(x_lo, wr_ref[:, :V7X_LANES], preferred_element_type=F32)
    logits = both[:, :V7X_LANES] + (both[:, V7X_LANES:] + corr) + br_ref[...]
    lane = lax.broadcasted_iota(I32, logits.shape, 1)
    neg = jnp.float32(-1e30)
    big = jnp.int32(V7X_LANES)

    gl = jnp.where(lane < n_grp, logits, neg)
    gmax = jnp.max(gl, axis=-1, keepdims=True)
    gidx = jnp.min(jnp.where(gl == gmax, lane, big), axis=-1, keepdims=True)
    grp_gate = 1.0 / jnp.sum(jnp.exp(gl - gmax), axis=-1, keepdims=True)

    lo = n_grp + gidx * per
    el = jnp.where((lane >= lo) & (lane < lo + per), logits, neg)
    v1 = jnp.max(el, axis=-1, keepdims=True)
    i1 = jnp.min(jnp.where(el == v1, lane, big), axis=-1, keepdims=True)
    el2 = jnp.where(lane == i1, neg, el)
    v2 = jnp.max(el2, axis=-1, keepdims=True)
    i2 = jnp.min(jnp.where(el2 == v2, lane, big), axis=-1, keepdims=True)
    t = jnp.exp(v2 - v1)
    den = 1.0 + t
    g1 = grp_gate / den
    g2 = grp_gate * (t / den)

    oh1 = jnp.where(lane == i1, 1.0, 0.0)
    oh2 = jnp.where(lane == i2, 1.0, 0.0)
    ri = lax.broadcasted_iota(I32, (tm, tm), 0)
    ci = lax.broadcasted_iota(I32, (tm, tm), 1)
    lower = jnp.where(ri > ci, 1.0, 0.0).astype(BF16)
    pre1 = jnp.dot(lower, oh1.astype(BF16), preferred_element_type=F32)
    pre2 = jnp.dot(lower, oh2.astype(BF16), preferred_element_type=F32)
    cnt1 = jnp.sum(oh1, axis=0, keepdims=True)
    cnt2 = jnp.sum(oh2, axis=0, keepdims=True)
    base = base_ref[...]
    rank1 = jnp.sum(oh1 * (pre1 + base), axis=-1, keepdims=True)
    rank2 = jnp.sum(oh2 * (pre2 + base + cnt1), axis=-1, keepdims=True)
    total = base + cnt1 + cnt2
    base_ref[...] = total
    cnt_ref[...] = total

    e1 = (i1 - n_grp).astype(F32)
    e2 = (i2 - n_grp).astype(F32)
    meta = jnp.zeros(logits.shape, F32)
    for idx, val in enumerate((e1, e2, rank1, rank2, g1, g2)):
        meta = jnp.where(lane == idx, val, meta)
    meta_ref[...] = meta
    meta_t_ref[...] = jnp.transpose(meta)[:META_ROWS, :]


META_ROWS = 8


def moe_router(h, g, wr, br, *, tm):
    n, d = h.shape
    tm = min(tm, n)
    return pl.pallas_call(
        functools.partial(_router_body, tm=tm, n_grp=MOE_GROUPS, per=MOE_PER_GROUP),
        grid=(n // tm,),
        in_specs=[
            pl.BlockSpec((tm, d), lambda i: (i, 0)),
            pl.BlockSpec((1, d), lambda i: (0, 0)),
            pl.BlockSpec((d, 2 * V7X_LANES), lambda i: (0, 0)),
            pl.BlockSpec((1, V7X_LANES), lambda i: (0, 0)),
        ],
        out_specs=[
            pl.BlockSpec((tm, d // 2), lambda i: (i, 0)),
            pl.BlockSpec((tm, V7X_LANES), lambda i: (i, 0)),
            pl.BlockSpec((META_ROWS, tm), lambda i: (0, i)),
            pl.BlockSpec((1, V7X_LANES), lambda i: (0, 0)),
        ],
        out_shape=[
            jax.ShapeDtypeStruct((n, d // 2), U32),
            jax.ShapeDtypeStruct((n, V7X_LANES), F32),
            jax.ShapeDtypeStruct((META_ROWS, n), F32),
            jax.ShapeDtypeStruct((1, V7X_LANES), F32),
        ],
        scratch_shapes=[pltpu.VMEM((1, V7X_LANES), F32)],
        compiler_params=_params(("arbitrary",)),
        name="moe_router",
    )(h, g.reshape(1, d), wr, br)


MOE_CODE_SHIFT = 6
assert MOE_GROUPS * MOE_PER_GROUP == 1 << MOE_CODE_SHIFT


def _order_body(code_ref, start_ref, order_ref):
    def body(a, carry):
        code = code_ref[a]
        order_ref[start_ref[code & ((1 << MOE_CODE_SHIFT) - 1)] + (code >> MOE_CODE_SHIFT)] = a
        return carry

    lax.fori_loop(0, code_ref.shape[0], body, 0, unroll=8)


def moe_order(code, start):
    return pl.pallas_call(
        _order_body,
        grid_spec=pltpu.PrefetchScalarGridSpec(
            num_scalar_prefetch=2,
            grid=(1,),
            in_specs=[],
            out_specs=pl.BlockSpec(memory_space=pltpu.SMEM),
        ),
        out_shape=jax.ShapeDtypeStruct(code.shape, I32),
        compiler_params=_params(("arbitrary",)),
        name="moe_order",
    )(code, start)


EXPERT_STEPS = 1
ROW_DMA_UNROLL = 8
WEIGHT_DMA_PRIORITY = 1
assert MOE_TOPK == 2


def _experts_body(exp_ref, nxt_ref, rows_ref, off_ref, order_ref, nact_ref,
                  xp_hbm, wg_hbm, wu_hbm, wdn_hbm, ya_hbm,
                  xbuf_ref, ybuf_ref, hd_ref, wa_ref, wb_ref, wc_ref, wd_ref,
                  ring_in, ring_dn, gsem, ssem, sem_in, sem_dn, *, rb, n_tok, n_blocks):
    v = pl.program_id(0)
    s = pl.program_id(1)
    nact = nact_ref[0]
    active = v < nact
    slot = lax.rem(v, 2)
    half = rb // 2
    oc = ring_dn.shape[2]
    dq = ring_in.shape[1]

    def in_copy(e, i):
        src = (wg_hbm, wg_hbm, wu_hbm, wu_hbm)[i % 4]
        q = 2 * (i // 4) + i % 2
        return pltpu.make_async_copy(src.at[e, pl.ds(q * dq, dq), :], ring_in.at[i], sem_in.at[i])

    def dn_copy(e, i):
        col = (i % 2) * 2 * oc + (i // 2) * oc
        return pltpu.make_async_copy(wdn_hbm.at[e, :, pl.ds(col, oc)], ring_dn.at[i], sem_dn.at[i])

    def refill(copy, slots):
        @pl.when(v + 1 < nact)
        def _():
            for i in slots:
                copy(nxt_ref[v], i).start(priority=WEIGHT_DMA_PRIORITY)

    def decode(base, r):
        a = order_ref[base + r]
        k = jnp.where(a >= n_tok, 1, 0)
        return k, a - k * n_tok

    def gather_copy(base, buf, r):
        _, tok = decode(base, r)
        return pltpu.make_async_copy(xp_hbm.at[pl.ds(tok, 1)], xbuf_ref.at[buf, pl.ds(r, 1)], gsem.at[buf])

    def scatter_copy(base, r):
        k, tok = decode(base, r)
        return pltpu.make_async_copy(ybuf_ref.at[pl.ds(r, 1)], ya_hbm.at[k, pl.ds(tok, 1)], ssem)

    def for_rows(lo, hi, fn):
        groups = lax.shift_right_logical(jnp.maximum(hi - lo, 0), ROW_DMA_UNROLL.bit_length() - 1)

        def group(g, carry):
            for u in range(ROW_DMA_UNROLL):
                fn(lo + g * ROW_DMA_UNROLL + u)
            return carry

        def single(r, carry):
            fn(r)
            return carry

        lax.fori_loop(0, groups, group, 0)
        lax.fori_loop(lo + groups * ROW_DMA_UNROLL, hi, single, 0)

    @pl.when(jnp.logical_and(v == 0, s == 0))
    def _():
        for i in range(ring_in.shape[0]):
            in_copy(exp_ref[0], i).start(priority=WEIGHT_DMA_PRIORITY)
        for i in range(ring_dn.shape[0]):
            dn_copy(exp_ref[0], i).start(priority=WEIGHT_DMA_PRIORITY)
        xbuf_ref[...] = jnp.zeros_like(xbuf_ref)
        base = off_ref[0]
        for_rows(0, rows_ref[0], lambda r: gather_copy(base, 0, r).start())

    @pl.when(jnp.logical_and(active, s == 0))
    def _():
        base = off_ref[v]
        for_rows(0, rows_ref[v], lambda r: gather_copy(base, slot, r).wait())

    @pl.when(jnp.logical_and(active, v + 1 < nact))
    def _():
        nxt = jnp.minimum(v + 1, n_blocks - 1)
        base = off_ref[nxt]
        per_step = rb // EXPERT_STEPS
        for_rows(s * per_step, jnp.minimum((s + 1) * per_step, rows_ref[nxt]),
                 lambda r: gather_copy(base, 1 - slot, r).start())

    halves = [(pl.ds(0, half), None), (pl.ds(half, half), rows_ref[v] > half)]

    def for_halves(fn):
        for rows, cond in halves:
            if cond is None:
                fn(rows)
            else:
                pl.when(cond)(functools.partial(fn, rows))

    @pl.when(jnp.logical_and(active, s == 0))
    def _():
        slots = range(ring_in.shape[0])
        for i in slots:
            in_copy(exp_ref[v], i).wait()
        for i in slots:
            dst = (wa_ref, wa_ref, wb_ref, wb_ref)[i % 4]
            dst[pl.ds((2 * (i // 4) + i % 2) * dq, dq), :] = ring_in[i].astype(BF16)
        refill(in_copy, slots)

        def gate_up(rows):
            lo, hi = _unpack_halves(xbuf_ref[slot, rows, :])
            xl, xh = lo.astype(BF16), hi.astype(BF16)
            kh = xl.shape[1]
            hg = (jnp.dot(xl, wa_ref[:kh, :], preferred_element_type=F32)
                  + jnp.dot(xh, wa_ref[kh:, :], preferred_element_type=F32))
            hu = (jnp.dot(xl, wb_ref[:kh, :], preferred_element_type=F32)
                  + jnp.dot(xh, wb_ref[kh:, :], preferred_element_type=F32))
            hd_ref[rows, :] = (hg * jax.nn.sigmoid(hg) * hu).astype(BF16)
        for_halves(gate_up)

    @pl.when(jnp.logical_and(active, jnp.logical_and(s == EXPERT_STEPS - 1, v >= 1)))
    def _():
        prev = jnp.maximum(v - 1, 0)
        base = off_ref[prev]
        for_rows(0, rows_ref[prev], lambda r: scatter_copy(base, r).wait())

    @pl.when(jnp.logical_and(active, s == EXPERT_STEPS - 1))
    def _():
        slots = range(ring_dn.shape[0])
        for i in slots:
            dn_copy(exp_ref[v], i).wait()
        for i in slots:
            (wc_ref, wd_ref)[i % 2][:, pl.ds((i // 2) * oc, oc)] = ring_dn[i].astype(BF16)
        refill(dn_copy, slots)

        def down(rows):
            hd = hd_ref[rows, :]
            for j in range(ring_dn.shape[0] // 2):
                cols = pl.ds(j * oc, oc)
                y_lo = jnp.dot(hd, wc_ref[:, cols], preferred_element_type=F32)
                y_hi = jnp.dot(hd, wd_ref[:, cols], preferred_element_type=F32)
                ybuf_ref[rows, cols] = _pack_halves(jnp.concatenate([y_lo, y_hi], axis=-1))
        for_halves(down)
        base = off_ref[v]
        for_rows(0, rows_ref[v], lambda r: scatter_copy(base, r).start())

    @pl.when(jnp.logical_and(v == n_blocks - 1, s == EXPERT_STEPS - 1))
    def _():
        last = jnp.maximum(nact - 1, 0)
        base = off_ref[last]
        for_rows(0, rows_ref[last], lambda r: scatter_copy(base, r).wait())


def moe_experts(xp, w_gate, w_up, w_down, blk_exp, blk_next, blk_rows, blk_off, order, nact, *, n_blocks, rb):
    n_exp, d, ff = w_gate.shape
    n_tok, kw = xp.shape
    assert d == 2 * kw
    oc = kw // 2
    n_in, n_dn = 8, 4
    any_spec = pl.BlockSpec(memory_space=pl.ANY)
    return pl.pallas_call(
        functools.partial(_experts_body, rb=rb, n_tok=n_tok, n_blocks=n_blocks),
        grid_spec=pltpu.PrefetchScalarGridSpec(
            num_scalar_prefetch=6,
            grid=(n_blocks, EXPERT_STEPS),
            in_specs=[any_spec, any_spec, any_spec, any_spec],
            out_specs=any_spec,
            scratch_shapes=[pltpu.VMEM((2, rb, kw), U32), pltpu.VMEM((rb, kw), U32),
                            pltpu.VMEM((rb, ff), BF16),
                            pltpu.VMEM((d, ff), BF16), pltpu.VMEM((d, ff), BF16),
                            pltpu.VMEM((ff, kw), BF16), pltpu.VMEM((ff, kw), BF16),
                            pltpu.VMEM((n_in, d // 4, ff), F32), pltpu.VMEM((n_dn, ff, oc), F32),
                            pltpu.SemaphoreType.DMA((2,)), pltpu.SemaphoreType.DMA(()),
                            pltpu.SemaphoreType.DMA((n_in,)), pltpu.SemaphoreType.DMA((n_dn,))],
        ),
        out_shape=jax.ShapeDtypeStruct((MOE_TOPK, n_tok, kw), U32),
        compiler_params=_params(("arbitrary", "arbitrary"), EXPERTS_VMEM_LIMIT_BYTES),
        name="moe_experts",
    )(blk_exp, blk_next, blk_rows, blk_off, order, nact, xp, w_gate, w_up, w_down)


def _combine_body(h_ref, meta_ref, g_ref, y0_ref, y1_ref, o_ref):
    meta = meta_ref[...]
    acc_lo = None
    acc_hi = None
    for k, y_ref in enumerate((y0_ref, y1_ref)):
        gate = meta[:, 2 * MOE_TOPK + k:2 * MOE_TOPK + k + 1]
        lo, hi = _unpack_halves(y_ref[0])
        lo, hi = lo * gate, hi * gate
        acc_lo = lo if acc_lo is None else acc_lo + lo
        acc_hi = hi if acc_hi is None else acc_hi + hi
    kh = acc_lo.shape[1]
    h_lo = h_ref[:, :kh] + acc_lo
    h_hi = h_ref[:, kh:] + acc_hi
    ms = (jnp.sum(h_lo * h_lo, axis=-1, keepdims=True) + jnp.sum(h_hi * h_hi, axis=-1, keepdims=True)) / (2 * kh)
    scale = lax.rsqrt(ms + NORM_EPS)
    o_ref[:, :kh] = (h_lo * scale) * g_ref[:, :kh]
    o_ref[:, kh:] = (h_hi * scale) * g_ref[:, kh:]


def moe_combine(h, meta, g, ya, *, tb):
    n, d = h.shape
    kw = ya.shape[2]
    tb = _tile(n, tb)

    def slot(k):
        return pl.BlockSpec((1, tb, kw), lambda i, k=k: (k, i, 0))

    return pl.pallas_call(
        _combine_body,
        grid=(n // tb,),
        in_specs=[
            pl.BlockSpec((tb, d), lambda i: (i, 0)),
            pl.BlockSpec((tb, V7X_LANES), lambda i: (i, 0)),
            pl.BlockSpec((1, d), lambda i: (0, 0)),
            slot(0), slot(1),
        ],
        out_specs=pl.BlockSpec((tb, d), lambda i: (i, 0)),
        out_shape=jax.ShapeDtypeStruct((n, d), F32),
        compiler_params=_params(("parallel",)),
        name="moe_combine",
    )(h, meta, g.reshape(1, d), ya, ya)


def _moe_plan(counts, *, rb, n_blocks):
    n_exp = counts.shape[0]
    per_exp = (counts + rb - 1) // rb
    blk_end = jnp.cumsum(per_exp)
    start = jnp.cumsum(counts) - counts
    nact = blk_end[-1].astype(I32)
    blk = jnp.arange(n_blocks, dtype=I32)
    blk_exp = jnp.minimum(jnp.sum(blk_end[None, :] <= blk[:, None], axis=1), n_exp - 1).astype(I32)
    active = blk < nact
    last = jnp.maximum(nact - 1, 0)
    blk_exp = jnp.where(active, blk_exp, blk_exp[last]).astype(I32)
    blk_next = blk_exp[jnp.minimum(blk + 1, last)]
    j = blk - (blk_end[blk_exp] - per_exp[blk_exp])
    blk_off = jnp.where(active, start[blk_exp] + j * rb, 0).astype(I32)
    blk_rows = jnp.where(active, jnp.clip(counts[blk_exp] - j * rb, 0, rb), 0).astype(I32)
    return start.astype(I32), blk_exp, blk_next, blk_rows, blk_off, nact.reshape(1)


def kernel(x, mem, positions, norm_mix_g, w_in, ret_norm_g, gm_ln_g, gm_ln_b, gm_ws, gm_bs, w_out, norm_xa_g, norm_mem_g, xa_wq, xa_wkv, xa_wo, norm_moe_g, router_grp_w, router_grp_b, router_exp_w, router_exp_b, moe_w_gate, moe_w_up, moe_w_down, norm_final_g):
    batch, seq, d = x.shape
    mem_len = mem.shape[1]
    n = batch * seq
    ret_width = ret_norm_g.shape[0]
    gm_width = gm_ln_g.shape[0]
    assert ret_width == gm_width and w_in.shape[1] == 4 * ret_width + 2 * gm_width
    dk = ret_width // RET_HEADS
    n_exp = moe_w_gate.shape[0]
    assert n_exp == MOE_GROUPS * MOE_PER_GROUP and MOE_GROUPS + n_exp <= V7X_LANES

    inv_freq = ROPE_BASE ** (-jnp.arange(0, dk, 2, dtype=F32) / dk)
    ang = positions.astype(F32).reshape(n, 1) * inv_freq
    cos, sin = jnp.cos(ang), jnp.sin(ang)

    x2 = x.reshape(n, d)
    xn = rmsnorm_bf16(x2, norm_mix_g, tm=512)
    proj, (w_out_b,) = matmul_wcast(xn, w_in, tm=1024, tn=512, side=(w_out,))
    ret = retention(proj, cos, sin, ret_norm_g, batch=batch, seq=seq, ret_width=ret_width)
    gm = gmlp(proj, gm_ln_g, gm_ln_b, gm_ws, gm_bs, n_rows=n, gm_width=gm_width,
              u_block=4 * ret_width // gm_width, v_block=4 * ret_width // gm_width + 1)
    h1, (wq_b, wkv_b, wo_b) = matmul_residual([ret, gm], w_out_b, x2, tm=1024, tn=512,
                                              side=(xa_wq, xa_wkv, xa_wo))

    kv, _ = norm_matmul(mem.reshape(batch * mem_len, d), norm_mem_g, wkv_b, tm=512, tn=1024)
    h2 = cross_attention_block(h1, norm_xa_g, wq_b, kv, wo_b, seq=seq, mem_len=mem_len, tm=256)

    pad = V7X_LANES - MOE_GROUPS - n_exp
    wr = jnp.concatenate([router_grp_w, router_exp_w, jnp.zeros((d, pad), F32)], axis=1)
    br = jnp.concatenate([router_grp_b, router_exp_b, jnp.zeros((pad,), F32)]).reshape(1, V7X_LANES)
    wr_hi = wr.astype(BF16)
    wr_lo = (wr - wr_hi.astype(F32)).astype(BF16)
    xp, meta, meta_t, cnt = moe_router(h2, norm_moe_g, jnp.concatenate([wr_hi, wr_lo], axis=1), br, tm=256)

    rb = MOE_ROW_BLOCK
    n_blocks = -(-(n * MOE_TOPK) // rb) + n_exp
    counts = cnt[0, MOE_GROUPS:MOE_GROUPS + n_exp].astype(I32)
    start, blk_exp, blk_next, blk_rows, blk_off, nact = _moe_plan(counts, rb=rb, n_blocks=n_blocks)
    code = (meta_t[MOE_TOPK:2 * MOE_TOPK].astype(I32) * (1 << MOE_CODE_SHIFT)
            + meta_t[0:MOE_TOPK].astype(I32)).reshape(-1)
    order = moe_order(code, start)
    ya = moe_experts(xp, moe_w_gate, moe_w_up, moe_w_down, blk_exp, blk_next, blk_rows, blk_off, order, nact,
                     n_blocks=n_blocks, rb=rb)
    y = moe_combine(h2, meta, norm_final_g, ya, tb=256)
    return y.reshape(batch, seq, d)
```

```python
import functools

import jax
import jax.numpy as jnp
from jax import lax
from jax.experimental import pallas as pl
from jax.experimental.pallas import tpu as pltpu

NORM_EPS = 1e-6
RET_HEADS = 8
ROPE_BASE = 10000.0
GM_GROUPS = 8
GM_CHUNK = 128
XA_HEADS = 4
MOE_GROUPS = 8
MOE_PER_GROUP = 8
MOE_TOPK = 2

V7X_LANES = 128
V7X_VMEM_BYTES = 64 * 1024 * 1024
VMEM_LIMIT_BYTES = 56 * 1024 * 1024
EXPERTS_VMEM_LIMIT_BYTES = 58 * 1024 * 1024

RET_BLOCK = 512
MOE_ROW_BLOCK = 512

F32 = jnp.float32
BF16 = jnp.bfloat16
U32 = jnp.uint32
I32 = jnp.int32


def _params(sem, vmem_limit_bytes=VMEM_LIMIT_BYTES):
    return pltpu.CompilerParams(dimension_semantics=sem, vmem_limit_bytes=vmem_limit_bytes)


def _tile(dim, target):
    t = min(dim, target)
    while dim % t:
        t -= V7X_LANES
    assert t > 0, (dim, target)
    return t


def _pack_halves(x_f32):
    k = x_f32.shape[-1] // 2
    bits = lax.bitcast_convert_type(x_f32.astype(BF16).astype(F32), U32)
    return (bits[:, k:] & jnp.uint32(0xFFFF0000)) | (bits[:, :k] >> 16)


def _unpack_halves(w_u32):
    lo = lax.bitcast_convert_type(w_u32 << 16, F32)
    hi = lax.bitcast_convert_type(w_u32 & jnp.uint32(0xFFFF0000), F32)
    return lo, hi


SIDE_CAST_SLABS = 64
BF16_SUBLANES = 16


def _side_slabs(side, steps):
    n = SIDE_CAST_SLABS
    while n > steps or any(a.shape[0] % (BF16_SUBLANES * n) for a in side):
        n //= 2
    assert n >= 1, [a.shape for a in side]
    return n


def _side_specs(side, n_j, slabs):
    return [pl.BlockSpec((a.shape[0] // slabs, a.shape[1]), lambda i, j: (jnp.minimum(i * n_j + j, slabs - 1), 0))
            for a in side]


def _side_cast(in_refs, out_refs, n_j, slabs):
    if not in_refs:
        return

    @pl.when(pl.program_id(0) * n_j + pl.program_id(1) < slabs)
    def _():
        for src, dst in zip(in_refs, out_refs):
            dst[...] = src[...].astype(dst.dtype)


def _side_out_shapes(side):
    return [jax.ShapeDtypeStruct(a.shape, BF16) for a in side]


def _norm_matmul_body(*refs, n_side, n_j, slabs):
    x_ref, g_ref, w_ref = refs[:3]
    side_in = refs[3:3 + n_side]
    o_ref = refs[3 + n_side]
    side_out = refs[4 + n_side:4 + 2 * n_side]
    xn_ref = refs[4 + 2 * n_side]

    @pl.when(pl.program_id(1) == 0)
    def _():
        x = x_ref[...]
        ms = jnp.mean(x * x, axis=-1, keepdims=True)
        xn_ref[...] = ((x * lax.rsqrt(ms + NORM_EPS)) * g_ref[...]).astype(BF16)

    o_ref[...] = jnp.dot(xn_ref[...], w_ref[...], preferred_element_type=F32).astype(o_ref.dtype)
    _side_cast(side_in, side_out, n_j, slabs)


def norm_matmul(x, g, w, *, tm, tn, side=()):
    m, k = x.shape
    n = w.shape[1]
    tm, tn = _tile(m, tm), _tile(n, tn)
    n_j = n // tn
    slabs = _side_slabs(side, (m // tm) * n_j)
    side_specs = _side_specs(side, n_j, slabs)
    out = pl.pallas_call(
        functools.partial(_norm_matmul_body, n_side=len(side), n_j=n_j, slabs=slabs),
        grid=(m // tm, n_j),
        in_specs=[
            pl.BlockSpec((tm, k), lambda i, j: (i, 0)),
            pl.BlockSpec((1, k), lambda i, j: (0, 0)),
            pl.BlockSpec((k, tn), lambda i, j: (0, j)),
        ] + side_specs,
        out_specs=[pl.BlockSpec((tm, tn), lambda i, j: (i, j))] + side_specs,
        out_shape=[jax.ShapeDtypeStruct((m, n), BF16)] + _side_out_shapes(side),
        scratch_shapes=[pltpu.VMEM((tm, k), BF16)],
        compiler_params=_params(("arbitrary", "arbitrary") if side else ("parallel", "arbitrary")),
        name="norm_matmul",
    )(x, g.reshape(1, k), w, *side)
    return out[0], out[1:]


def _rmsnorm_body(x_ref, g_ref, o_ref):
    x = x_ref[...]
    ms = jnp.mean(x * x, axis=-1, keepdims=True)
    o_ref[...] = ((x * lax.rsqrt(ms + NORM_EPS)) * g_ref[...]).astype(o_ref.dtype)


def rmsnorm_bf16(x, g, *, tm):
    m, k = x.shape
    tm = _tile(m, tm)
    return pl.pallas_call(
        _rmsnorm_body,
        grid=(m // tm,),
        in_specs=[pl.BlockSpec((tm, k), lambda i: (i, 0)), pl.BlockSpec((1, k), lambda i: (0, 0))],
        out_specs=pl.BlockSpec((tm, k), lambda i: (i, 0)),
        out_shape=jax.ShapeDtypeStruct((m, k), BF16),
        compiler_params=_params(("parallel",)),
        name="rmsnorm",
    )(x, g.reshape(1, k))


def _matmul_wcast_body(*refs, n_side, n_i, n_j, slabs, tn):
    a_ref, w_hbm = refs[:2]
    side_in = refs[2:2 + n_side]
    o_ref = refs[2 + n_side]
    side_out = refs[3 + n_side:3 + 2 * n_side]
    wb_ref, ring_ref, sem = refs[3 + 2 * n_side:]
    j = pl.program_id(0)
    i = pl.program_id(1)
    pieces, rows = ring_ref.shape[0], ring_ref.shape[1]

    def piece_copy(col_block, p):
        return pltpu.make_async_copy(
            w_hbm.at[pl.ds(p * rows, rows), pl.ds(pl.multiple_of(col_block * tn, V7X_LANES), tn)],
            ring_ref.at[p], sem.at[p])

    @pl.when(jnp.logical_and(j == 0, i == 0))
    def _():
        for p in range(pieces):
            piece_copy(0, p).start()

    @pl.when(i == 0)
    def _():
        for p in range(pieces):
            piece_copy(j, p).wait()
            wb_ref[pl.ds(p * rows, rows), :] = ring_ref[p].astype(BF16)

    for p in range(pieces):
        @pl.when(jnp.logical_and(i == p * n_i // pieces, j + 1 < n_j))
        def _(p=p):
            piece_copy(j + 1, p).start()

    o_ref[...] = jnp.dot(a_ref[...], wb_ref[...], preferred_element_type=F32).astype(o_ref.dtype)
    _side_cast(side_in, side_out, n_i, slabs)


WCAST_PIECES = 8
WCAST_VMEM_LIMIT_BYTES = 60 * 1024 * 1024


def matmul_wcast(a, w, *, tm, tn, side=()):
    m, k = a.shape
    n = w.shape[1]
    tm, tn = _tile(m, tm), _tile(n, tn)
    n_i, n_j = m // tm, n // tn
    pieces = min(WCAST_PIECES, n_i)
    assert k % pieces == 0 and n_i % pieces == 0
    slabs = _side_slabs(side, n_j * n_i)
    side_specs = _side_specs(side, n_i, slabs)
    out = pl.pallas_call(
        functools.partial(_matmul_wcast_body, n_side=len(side), n_i=n_i, n_j=n_j, slabs=slabs, tn=tn),
        grid=(n_j, n_i),
        in_specs=[pl.BlockSpec((tm, k), lambda j, i: (i, 0)), pl.BlockSpec(memory_space=pl.ANY)] + side_specs,
        out_specs=[pl.BlockSpec((tm, tn), lambda j, i: (i, j))] + side_specs,
        out_shape=[jax.ShapeDtypeStruct((m, n), BF16)] + _side_out_shapes(side),
        scratch_shapes=[pltpu.VMEM((k, tn), BF16), pltpu.VMEM((pieces, k // pieces, tn), F32),
                        pltpu.SemaphoreType.DMA((pieces,))],
        compiler_params=_params(("arbitrary", "arbitrary"), WCAST_VMEM_LIMIT_BYTES),
        name="matmul_wcast",
    )(a, w, *side)
    return out[0], out[1:]


RET_HEADS_PER_STEP = 4


def _retention_body(lg_ref, q_ref, k_ref, v_ref, g_ref, cos_ref, sin_ref, gn_ref, o_ref, state_ref, dec_ref,
                    *, blk, dk, hps):
    c = pl.program_id(2)
    first_of_pair = jnp.logical_and(pl.program_id(1) == 0, c == 0)
    half = dk // 2
    cos = cos_ref[...]
    sin = sin_ref[...]
    pos = lax.broadcasted_iota(I32, (blk, 1), 0).astype(F32)

    def rot(t):
        t1, t2 = t[:, :half], t[:, half:]
        return jnp.concatenate([t1 * cos - t2 * sin, t1 * sin + t2 * cos], axis=-1)

    @pl.when(c == 0)
    def _():
        state_ref[...] = jnp.zeros_like(state_ref)

    for j in range(hps):
        lg = lg_ref[pl.program_id(0) * hps + j]
        cols = slice(j * dk, (j + 1) * dk)

        @pl.when(first_of_pair)
        def _(j=j, lg=lg):
            ri = lax.broadcasted_iota(I32, (blk, blk), 0)
            ci = lax.broadcasted_iota(I32, (blk, blk), 1)
            diff = (ri - ci).astype(F32)
            dec_ref[j] = jnp.where(diff >= 0.0, jnp.exp(jnp.maximum(diff, 0.0) * lg), 0.0)

        qr = rot(q_ref[:, cols].astype(F32))
        kr = rot(k_ref[:, cols].astype(F32)) * (dk ** -0.5)
        v = v_ref[:, cols]

        q_dec = jnp.exp((pos + 1.0) * lg)
        k_dec = jnp.exp((blk - 1.0 - pos) * lg)
        blk_dec = jnp.exp(jnp.full((1, dk), blk * lg, F32))

        s = lax.dot_general(qr.astype(BF16), kr.astype(BF16), (((1,), (1,)), ((), ())), preferred_element_type=F32)
        inner = jnp.dot((s * dec_ref[j]).astype(BF16), v, preferred_element_type=F32)

        state = state_ref[j]
        cross = jnp.dot((qr * q_dec).astype(BF16), state.astype(BF16), preferred_element_type=F32)
        kd_t = jnp.transpose(kr * k_dec).astype(BF16)
        state_ref[j] = state * blk_dec + jnp.dot(kd_t, v, preferred_element_type=F32)

        out = inner + cross
        mu = jnp.mean(out, axis=-1, keepdims=True)
        cen = out - mu
        var = jnp.mean(cen * cen, axis=-1, keepdims=True)
        y = cen * lax.rsqrt(var + NORM_EPS) * gn_ref[:, cols]
        gate = g_ref[:, cols].astype(F32)
        o_ref[:, cols] = (y * (gate * jax.nn.sigmoid(gate))).astype(o_ref.dtype)


def retention(proj, cos, sin, ret_norm_g, *, batch, seq, ret_width):
    heads = RET_HEADS
    hps = min(RET_HEADS_PER_STEP, heads)
    assert heads % hps == 0
    dk = ret_width // heads
    blk = min(RET_BLOCK, seq)
    nblk = seq // blk
    hb = heads // hps
    log_gamma = jnp.log(1.0 - jnp.exp2(-5.0 - jnp.arange(heads, dtype=F32)))

    def col(seg):
        return pl.BlockSpec((blk, hps * dk), lambda h, b, c, lg, seg=seg: (b * nblk + c, seg * hb + h))

    rowspec = pl.BlockSpec((blk, dk // 2), lambda h, b, c, lg: (b * nblk + c, 0))
    return pl.pallas_call(
        functools.partial(_retention_body, blk=blk, dk=dk, hps=hps),
        grid_spec=pltpu.PrefetchScalarGridSpec(
            num_scalar_prefetch=1,
            grid=(heads // hps, batch, nblk),
            in_specs=[col(0), col(1), col(2), col(3), rowspec, rowspec,
                      pl.BlockSpec((1, hps * dk), lambda h, b, c, lg: (0, h))],
            out_specs=pl.BlockSpec((blk, hps * dk), lambda h, b, c, lg: (b * nblk + c, h)),
            scratch_shapes=[pltpu.VMEM((hps, dk, dk), F32), pltpu.VMEM((hps, blk, blk), F32)],
        ),
        out_shape=jax.ShapeDtypeStruct((batch * seq, ret_width), BF16),
        compiler_params=_params(("arbitrary", "arbitrary", "arbitrary")),
        name="retention",
    )(log_gamma, proj, proj, proj, proj, cos, sin, ret_norm_g.reshape(1, ret_width))


def _gmlp_body(u_ref, v_ref, lng_ref, lnb_ref, ws_ref, bst_ref, o_ref, vn_ref, *, rows, groups, cg, chunk):
    v = jax.nn.gelu(v_ref[...].astype(F32))
    mu = jnp.mean(v, axis=-1, keepdims=True)
    cen = v - mu
    var = jnp.mean(cen * cen, axis=-1, keepdims=True)
    vn_ref[...] = (cen * lax.rsqrt(var + NORM_EPS) * lng_ref[...] + lnb_ref[...]).astype(BF16)

    ri = lax.broadcasted_iota(I32, (chunk, chunk), 0)
    ci = lax.broadcasted_iota(I32, (chunk, chunk), 1)
    causal = ri >= ci
    for g in range(groups):
        w = jnp.where(causal, ws_ref[g], 0.0).astype(BF16)
        bias = bst_ref[:, g:g + 1]
        cols = slice(g * cg, (g + 1) * cg)
        for t in range(rows // chunk):
            rws = slice(t * chunk, (t + 1) * chunk)
            sp = jnp.dot(w, vn_ref[rws, cols], preferred_element_type=F32) + bias
            u = jax.nn.gelu(u_ref[rws, cols].astype(F32))
            o_ref[rws, cols] = (u * sp).astype(o_ref.dtype)


def gmlp(proj, ln_g, ln_b, ws, bs, *, n_rows, gm_width, u_block, v_block):
    groups, chunk = GM_GROUPS, GM_CHUNK
    cg = gm_width // groups
    rows = 2 * chunk
    return pl.pallas_call(
        functools.partial(_gmlp_body, rows=rows, groups=groups, cg=cg, chunk=chunk),
        grid=(n_rows // rows,),
        in_specs=[
            pl.BlockSpec((rows, gm_width), lambda i: (i, u_block)),
            pl.BlockSpec((rows, gm_width), lambda i: (i, v_block)),
            pl.BlockSpec((1, gm_width), lambda i: (0, 0)),
            pl.BlockSpec((1, gm_width), lambda i: (0, 0)),
            pl.BlockSpec((groups, chunk, chunk), lambda i: (0, 0, 0)),
            pl.BlockSpec((chunk, groups), lambda i: (0, 0)),
        ],
        out_specs=pl.BlockSpec((rows, gm_width), lambda i: (i, 0)),
        out_shape=jax.ShapeDtypeStruct((n_rows, gm_width), BF16),
        scratch_shapes=[pltpu.VMEM((rows, gm_width), BF16)],
        compiler_params=_params(("parallel",)),
        name="gmlp",
    )(proj, proj, ln_g.reshape(1, gm_width), ln_b.reshape(1, gm_width), ws, bs.T)


def _matmul_residual_body(*refs, n_parts, n_side, n_j, slabs):
    a_refs, w_refs = refs[:n_parts], refs[n_parts:2 * n_parts]
    r_ref = refs[2 * n_parts]
    side_in = refs[2 * n_parts + 1:2 * n_parts + 1 + n_side]
    o_ref = refs[2 * n_parts + 1 + n_side]
    side_out = refs[2 * n_parts + 2 + n_side:]
    acc = r_ref[...]
    for a_ref, w_ref in zip(a_refs, w_refs):
        acc = acc + jnp.dot(a_ref[...], w_ref[...], preferred_element_type=F32)
    o_ref[...] = acc
    _side_cast(side_in, side_out, n_j, slabs)


def matmul_residual(parts, w, res, *, tm, tn, side=()):
    m, n = res.shape
    tm, tn = _tile(m, tm), _tile(n, tn)
    n_j = n // tn
    slabs = _side_slabs(side, (m // tm) * n_j)
    kp = parts[0].shape[1]
    n_parts = len(parts)
    a_specs = [pl.BlockSpec((tm, kp), lambda i, j: (i, 0)) for _ in parts]
    w_specs = [pl.BlockSpec((kp, tn), lambda i, j, p=p: (p, j)) for p in range(n_parts)]
    side_specs = _side_specs(side, n_j, slabs)
    out = pl.pallas_call(
        functools.partial(_matmul_residual_body, n_parts=n_parts, n_side=len(side), n_j=n_j, slabs=slabs),
        grid=(m // tm, n_j),
        in_specs=a_specs + w_specs + [pl.BlockSpec((tm, tn), lambda i, j: (i, j))] + side_specs,
        out_specs=[pl.BlockSpec((tm, tn), lambda i, j: (i, j))] + side_specs,
        out_shape=[jax.ShapeDtypeStruct((m, n), F32)] + _side_out_shapes(side),
        compiler_params=_params(("arbitrary", "arbitrary") if side else ("parallel", "parallel")),
        name="matmul_residual",
    )(*parts, *([w] * n_parts), res, *side)
    return out[0], out[1:]


def _cross_attention_body(h_ref, g_ref, wq_ref, kv_ref, wo_ref, o_ref, *, heads, dh):
    width = heads * dh
    x = h_ref[...]
    ms = jnp.mean(x * x, axis=-1, keepdims=True)
    xn = ((x * lax.rsqrt(ms + NORM_EPS)) * g_ref[...]).astype(BF16)
    q = jnp.dot(xn, wq_ref[...], preferred_element_type=F32).astype(BF16)
    outs = []
    for h in range(heads):
        k = kv_ref[:, h * dh:(h + 1) * dh]
        v = kv_ref[:, width + h * dh:width + (h + 1) * dh]
        s = lax.dot_general(q[:, h * dh:(h + 1) * dh], k, (((1,), (1,)), ((), ())),
                            preferred_element_type=F32) * (dh ** -0.5)
        e = jnp.exp(s - jnp.max(s, axis=-1, keepdims=True))
        p = e / jnp.sum(e, axis=-1, keepdims=True)
        outs.append(jnp.dot(p.astype(BF16), v, preferred_element_type=F32).astype(BF16))
    o = jnp.concatenate(outs, axis=-1)
    o_ref[...] = x + jnp.dot(o, wo_ref[...], preferred_element_type=F32)


def cross_attention_block(h, g, wq, kv, wo, *, seq, mem_len, tm):
    n, d = h.shape
    heads = XA_HEADS
    width = wq.shape[1]
    dh = width // heads
    tm = _tile(seq, tm)
    per_batch = seq // tm

    def resident(shape):
        return pl.BlockSpec(shape, lambda i: (0, 0), pipeline_mode=pl.Buffered(1))

    return pl.pallas_call(
        functools.partial(_cross_attention_body, heads=heads, dh=dh),
        grid=(n // tm,),
        in_specs=[
            pl.BlockSpec((tm, d), lambda i: (i, 0)),
            resident((1, d)),
            resident((d, width)),
            pl.BlockSpec((mem_len, 2 * width), lambda i: (i // per_batch, 0)),
            resident((width, d)),
        ],
        out_specs=pl.BlockSpec((tm, d), lambda i: (i, 0)),
        out_shape=jax.ShapeDtypeStruct((n, d), F32),
        compiler_params=_params(("parallel",)),
        name="cross_attention",
    )(h, g.reshape(1, d), wq, kv, wo)


def _router_body(h_ref, g_ref, wr_ref, br_ref, xp_ref, meta_ref, meta_t_ref, cnt_ref, base_ref, *, tm, n_grp, per):
    @pl.when(pl.program_id(0) == 0)
    def _():
        base_ref[...] = jnp.zeros_like(base_ref)

    x = h_ref[...]
    ms = jnp.mean(x * x, axis=-1, keepdims=True)
    xn = (x * lax.rsqrt(ms + NORM_EPS)) * g_ref[...]
    xp_ref[...] = _pack_halves(xn)

    x_hi = xn.astype(BF16)
    x_lo = (xn - x_hi.astype(F32)).astype(BF16)
    both = jnp.dot(x_hi, wr_ref[...], preferred_element_type=F32)
    corr = jnp.dot(x_lo, wr_ref[:, :V7X_LANES], preferred_element_type=F32)
    logits = both[:, :V7X_LANES] + (both[:, V7X_LANES:] + corr) + br_ref[...]
    lane = lax.broadcasted_iota(I32, logits.shape, 1)
    neg = jnp.float32(-1e30)
    big = jnp.int32(V7X_LANES)

    gl = jnp.where(lane < n_grp, logits, neg)
    gmax = jnp.max(gl, axis=-1, keepdims=True)
    gidx = jnp.min(jnp.where(gl == gmax, lane, big), axis=-1, keepdims=True)
    grp_gate = 1.0 / jnp.sum(jnp.exp(gl - gmax), axis=-1, keepdims=True)

    lo = n_grp + gidx * per
    el = jnp.where((lane >= lo) & (lane < lo + per), logits, neg)
    v1 = jnp.max(el, axis=-1, keepdims=True)
    i1 = jnp.min(jnp.where(el == v1, lane, big), axis=-1, keepdims=True)
    el2 = jnp.where(lane == i1, neg, el)
    v2 = jnp.max(el2, axis=-1, keepdims=True)
    i2 = jnp.min(jnp.where(el2 == v2, lane, big), axis=-1, keepdims=True)
    t = jnp.exp(v2 - v1)
    den = 1.0 + t
    g1 = grp_gate / den
    g2 = grp_gate * (t / den)

    oh1 = jnp.where(lane == i1, 1.0, 0.0)
    oh2 = jnp.where(lane == i2, 1.0, 0.0)
    ri = lax.broadcasted_iota(I32, (tm, tm), 0)
    ci = lax.broadcasted_iota(I32, (tm, tm), 1)
    lower = jnp.where(ri > ci, 1.0, 0.0).astype(BF16)
    pre1 = jnp.dot(lower, oh1.astype(BF16), preferred_element_type=F32)
    pre2 = jnp.dot(lower, oh2.astype(BF16), preferred_element_type=F32)
    cnt1 = jnp.sum(oh1, axis=0, keepdims=True)
    cnt2 = jnp.sum(oh2, axis=0, keepdims=True)
    base = base_ref[...]
    rank1 = jnp.sum(oh1 * (pre1 + base), axis=-1, keepdims=True)
    rank2 = jnp.sum(oh2 * (pre2 + base + cnt1), axis=-1, keepdims=True)
    total = base + cnt1 + cnt2
    base_ref[...] = total
    cnt_ref[...] = total

    e1 = (i1 - n_grp).astype(F32)
    e2 = (i2 - n_grp).astype(F32)
    meta = jnp.zeros(logits.shape, F32)
    for idx, val in enumerate((e1, e2, rank1, rank2, g1, g2)):
        meta = jnp.where(lane == idx, val, meta)
    meta_ref[...] = meta
    meta_t_ref[...] = jnp.transpose(meta)[:META_ROWS, :]


META_ROWS = 8


def moe_router(h, g, wr, br, *, tm):
    n, d = h.shape
    tm = min(tm, n)
    return pl.pallas_call(
        functools.partial(_router_body, tm=tm, n_grp=MOE_GROUPS, per=MOE_PER_GROUP),
        grid=(n // tm,),
        in_specs=[
            pl.BlockSpec((tm, d), lambda i: (i, 0)),
            pl.BlockSpec((1, d), lambda i: (0, 0)),
            pl.BlockSpec((d, 2 * V7X_LANES), lambda i: (0, 0)),
            pl.BlockSpec((1, V7X_LANES), lambda i: (0, 0)),
        ],
        out_specs=[
            pl.BlockSpec((tm, d // 2), lambda i: (i, 0)),
            pl.BlockSpec((tm, V7X_LANES), lambda i: (i, 0)),
            pl.BlockSpec((META_ROWS, tm), lambda i: (0, i)),
            pl.BlockSpec((1, V7X_LANES), lambda i: (0, 0)),
        ],
        out_shape=[
            jax.ShapeDtypeStruct((n, d // 2), U32),
            jax.ShapeDtypeStruct((n, V7X_LANES), F32),
            jax.ShapeDtypeStruct((META_ROWS, n), F32),
            jax.ShapeDtypeStruct((1, V7X_LANES), F32),
        ],
        scratch_shapes=[pltpu.VMEM((1, V7X_LANES), F32)],
        compiler_params=_params(("arbitrary",)),
        name="moe_router",
    )(h, g.reshape(1, d), wr, br)


MOE_CODE_SHIFT = 6
assert MOE_GROUPS * MOE_PER_GROUP == 1 << MOE_CODE_SHIFT


def _order_body(code_ref, start_ref, order_ref):
    def body(a, carry):
        code = code_ref[a]
        order_ref[start_ref[code & ((1 << MOE_CODE_SHIFT) - 1)] + (code >> MOE_CODE_SHIFT)] = a
        return carry

    lax.fori_loop(0, code_ref.shape[0], body, 0, unroll=8)


def moe_order(code, start):
    return pl.pallas_call(
        _order_body,
        grid_spec=pltpu.PrefetchScalarGridSpec(
            num_scalar_prefetch=2,
            grid=(1,),
            in_specs=[],
            out_specs=pl.BlockSpec(memory_space=pltpu.SMEM),
        ),
        out_shape=jax.ShapeDtypeStruct(code.shape, I32),
        compiler_params=_params(("arbitrary",)),
        name="moe_order",
    )(code, start)


EXPERT_STEPS = 1
ROW_DMA_UNROLL = 8
WEIGHT_DMA_PRIORITY = 1
assert MOE_TOPK == 2


def _experts_body(exp_ref, nxt_ref, rows_ref, off_ref, order_ref, nact_ref,
                  xp_hbm, wg_hbm, wu_hbm, wdn_hbm, ya_hbm,
                  xbuf_ref, ybuf_ref, hd_ref, wa_ref, wb_ref, wc_ref, wd_ref,
                  ring_in, ring_dn, gsem, ssem, sem_in, sem_dn, *, rb, n_tok, n_blocks):
    v = pl.program_id(0)
    s = pl.program_id(1)
    nact = nact_ref[0]
    active = v < nact
    slot = lax.rem(v, 2)
    half = rb // 2
    oc = ring_dn.shape[2]
    dq = ring_in.shape[1]

    def in_copy(e, i):
        src = (wg_hbm, wg_hbm, wu_hbm, wu_hbm)[i % 4]
        q = 2 * (i // 4) + i % 2
        return pltpu.make_async_copy(src.at[e, pl.ds(q * dq, dq), :], ring_in.at[i], sem_in.at[i])

    def dn_copy(e, i):
        col = (i % 2) * 2 * oc + (i // 2) * oc
        return pltpu.make_async_copy(wdn_hbm.at[e, :, pl.ds(col, oc)], ring_dn.at[i], sem_dn.at[i])

    def refill(copy, slots):
        @pl.when(v + 1 < nact)
        def _():
            for i in slots:
                copy(nxt_ref[v], i).start(priority=WEIGHT_DMA_PRIORITY)

    def decode(base, r):
        a = order_ref[base + r]
        k = jnp.where(a >= n_tok, 1, 0)
        return k, a - k * n_tok

    def gather_copy(base, buf, r):
        _, tok = decode(base, r)
        return pltpu.make_async_copy(xp_hbm.at[pl.ds(tok, 1)], xbuf_ref.at[buf, pl.ds(r, 1)], gsem.at[buf])

    def scatter_copy(base, r):
        k, tok = decode(base, r)
        return pltpu.make_async_copy(ybuf_ref.at[pl.ds(r, 1)], ya_hbm.at[k, pl.ds(tok, 1)], ssem)

    def for_rows(lo, hi, fn):
        groups = lax.shift_right_logical(jnp.maximum(hi - lo, 0), ROW_DMA_UNROLL.bit_length() - 1)

        def group(g, carry):
            for u in range(ROW_DMA_UNROLL):
                fn(lo + g * ROW_DMA_UNROLL + u)
            return carry

        def single(r, carry):
            fn(r)
            return carry

        lax.fori_loop(0, groups, group, 0)
        lax.fori_loop(lo + groups * ROW_DMA_UNROLL, hi, single, 0)

    @pl.when(jnp.logical_and(v == 0, s == 0))
    def _():
        for i in range(ring_in.shape[0]):
            in_copy(exp_ref[0], i).start(priority=WEIGHT_DMA_PRIORITY)
        for i in range(ring_dn.shape[0]):
            dn_copy(exp_ref[0], i).start(priority=WEIGHT_DMA_PRIORITY)
        xbuf_ref[...] = jnp.zeros_like(xbuf_ref)
        base = off_ref[0]
        for_rows(0, rows_ref[0], lambda r: gather_copy(base, 0, r).start())

    @pl.when(jnp.logical_and(active, s == 0))
    def _():
        base = off_ref[v]
        for_rows(0, rows_ref[v], lambda r: gather_copy(base, slot, r).wait())

    @pl.when(jnp.logical_and(active, v + 1 < nact))
    def _():
        nxt = jnp.minimum(v + 1, n_blocks - 1)
        base = off_ref[nxt]
        per_step = rb // EXPERT_STEPS
        for_rows(s * per_step, jnp.minimum((s + 1) * per_step, rows_ref[nxt]),
                 lambda r: gather_copy(base, 1 - slot, r).start())

    halves = [(pl.ds(0, half), None), (pl.ds(half, half), rows_ref[v] > half)]

    def for_halves(fn):
        for rows, cond in halves:
            if cond is None:
                fn(rows)
            else:
                pl.when(cond)(functools.partial(fn, rows))

    @pl.when(jnp.logical_and(active, s == 0))
    def _():
        slots = range(ring_in.shape[0])
        for i in slots:
            in_copy(exp_ref[v], i).wait()
        for i in slots:
            dst = (wa_ref, wa_ref, wb_ref, wb_ref)[i % 4]
            dst[pl.ds((2 * (i // 4) + i % 2) * dq, dq), :] = ring_in[i].astype(BF16)
        refill(in_copy, slots)

        def gate_up(rows):
            lo, hi = _unpack_halves(xbuf_ref[slot, rows, :])
            xl, xh = lo.astype(BF16), hi.astype(BF16)
            kh = xl.shape[1]
            hg = (jnp.dot(xl, wa_ref[:kh, :], preferred_element_type=F32)
                  + jnp.dot(xh, wa_ref[kh:, :], preferred_element_type=F32))
            hu = (jnp.dot(xl, wb_ref[:kh, :], preferred_element_type=F32)
                  + jnp.dot(xh, wb_ref[kh:, :], preferred_element_type=F32))
            hd_ref[rows, :] = (hg * jax.nn.sigmoid(hg) * hu).astype(BF16)
        for_halves(gate_up)

    @pl.when(jnp.logical_and(active, jnp.logical_and(s == EXPERT_STEPS - 1, v >= 1)))
    def _():
        prev = jnp.maximum(v - 1, 0)
        base = off_ref[prev]
        for_rows(0, rows_ref[prev], lambda r: scatter_copy(base, r).wait())

    @pl.when(jnp.logical_and(active, s == EXPERT_STEPS - 1))
    def _():
        slots = range(ring_dn.shape[0])
        for i in slots:
            dn_copy(exp_ref[v], i).wait()
        for i in slots:
            (wc_ref, wd_ref)[i % 2][:, pl.ds((i // 2) * oc, oc)] = ring_dn[i].astype(BF16)
        refill(dn_copy, slots)

        def down(rows):
            hd = hd_ref[rows, :]
            for j in range(ring_dn.shape[0] // 2):
                cols = pl.ds(j * oc, oc)
                y_lo = jnp.dot(hd, wc_ref[:, cols], preferred_element_type=F32)
                y_hi = jnp.dot(hd, wd_ref[:, cols], preferred_element_type=F32)
                ybuf_ref[rows, cols] = _pack_halves(jnp.concatenate([y_lo, y_hi], axis=-1))
        for_halves(down)
        base = off_ref[v]
        for_rows(0, rows_ref[v], lambda r: scatter_copy(base, r).start())

    @pl.when(jnp.logical_and(v == n_blocks - 1, s == EXPERT_STEPS - 1))
    def _():
        last = jnp.maximum(nact - 1, 0)
        base = off_ref[last]
        for_rows(0, rows_ref[last], lambda r: scatter_copy(base, r).wait())


def moe_experts(xp, w_gate, w_up, w_down, blk_exp, blk_next, blk_rows, blk_off, order, nact, *, n_blocks, rb):
    n_exp, d, ff = w_gate.shape
    n_tok, kw = xp.shape
    assert d == 2 * kw
    oc = kw // 2
    n_in, n_dn = 8, 4
    any_spec = pl.BlockSpec(memory_space=pl.ANY)
    return pl.pallas_call(
        functools.partial(_experts_body, rb=rb, n_tok=n_tok, n_blocks=n_blocks),
        grid_spec=pltpu.PrefetchScalarGridSpec(
            num_scalar_prefetch=6,
            grid=(n_blocks, EXPERT_STEPS),
            in_specs=[any_spec, any_spec, any_spec, any_spec],
            out_specs=any_spec,
            scratch_shapes=[pltpu.VMEM((2, rb, kw), U32), pltpu.VMEM((rb, kw), U32),
                            pltpu.VMEM((rb, ff), BF16),
                            pltpu.VMEM((d, ff), BF16), pltpu.VMEM((d, ff), BF16),
                            pltpu.VMEM((ff, kw), BF16), pltpu.VMEM((ff, kw), BF16),
                            pltpu.VMEM((n_in, d // 4, ff), F32), pltpu.VMEM((n_dn, ff, oc), F32),
                            pltpu.SemaphoreType.DMA((2,)), pltpu.SemaphoreType.DMA(()),
                            pltpu.SemaphoreType.DMA((n_in,)), pltpu.SemaphoreType.DMA((n_dn,))],
        ),
        out_shape=jax.ShapeDtypeStruct((MOE_TOPK, n_tok, kw), U32),
        compiler_params=_params(("arbitrary", "arbitrary"), EXPERTS_VMEM_LIMIT_BYTES),
        name="moe_experts",
    )(blk_exp, blk_next, blk_rows, blk_off, order, nact, xp, w_gate, w_up, w_down)


def _combine_body(h_ref, meta_ref, g_ref, y0_ref, y1_ref, o_ref):
    meta = meta_ref[...]
    acc_lo = None
    acc_hi = None
    for k, y_ref in enumerate((y0_ref, y1_ref)):
        gate = meta[:, 2 * MOE_TOPK + k:2 * MOE_TOPK + k + 1]
        lo, hi = _unpack_halves(y_ref[0])
        lo, hi = lo * gate, hi * gate
        acc_lo = lo if acc_lo is None else acc_lo + lo
        acc_hi = hi if acc_hi is None else acc_hi + hi
    kh = acc_lo.shape[1]
    h_lo = h_ref[:, :kh] + acc_lo
    h_hi = h_ref[:, kh:] + acc_hi
    ms = (jnp.sum(h_lo * h_lo, axis=-1, keepdims=True) + jnp.sum(h_hi * h_hi, axis=-1, keepdims=True)) / (2 * kh)
    scale = lax.rsqrt(ms + NORM_EPS)
    o_ref[:, :kh] = (h_lo * scale) * g_ref[:, :kh]
    o_ref[:, kh:] = (h_hi * scale) * g_ref[:, kh:]


def moe_combine(h, meta, g, ya, *, tb):
    n, d = h.shape
    kw = ya.shape[2]
    tb = _tile(n, tb)

    def slot(k):
        return pl.BlockSpec((1, tb, kw), lambda i, k=k: (k, i, 0))

    return pl.pallas_call(
        _combine_body,
        grid=(n // tb,),
        in_specs=[
            pl.BlockSpec((tb, d), lambda i: (i, 0)),
            pl.BlockSpec((tb, V7X_LANES), lambda i: (i, 0)),
            pl.BlockSpec((1, d), lambda i: (0, 0)),
            slot(0), slot(1),
        ],
        out_specs=pl.BlockSpec((tb, d), lambda i: (i, 0)),
        out_shape=jax.ShapeDtypeStruct((n, d), F32),
        compiler_params=_params(("parallel",)),
        name="moe_combine",
    )(h, meta, g.reshape(1, d), ya, ya)


def _moe_plan(counts, *, rb, n_blocks):
    n_exp = counts.shape[0]
    per_exp = (counts + rb - 1) // rb
    blk_end = jnp.cumsum(per_exp)
    start = jnp.cumsum(counts) - counts
    nact = blk_end[-1].astype(I32)
    blk = jnp.arange(n_blocks, dtype=I32)
    blk_exp = jnp.minimum(jnp.sum(blk_end[None, :] <= blk[:, None], axis=1), n_exp - 1).astype(I32)
    active = blk < nact
    last = jnp.maximum(nact - 1, 0)
    blk_exp = jnp.where(active, blk_exp, blk_exp[last]).astype(I32)
    blk_next = blk_exp[jnp.minimum(blk + 1, last)]
    j = blk - (blk_end[blk_exp] - per_exp[blk_exp])
    blk_off = jnp.where(active, start[blk_exp] + j * rb, 0).astype(I32)
    blk_rows = jnp.where(active, jnp.clip(counts[blk_exp] - j * rb, 0, rb), 0).astype(I32)
    return start.astype(I32), blk_exp, blk_next, blk_rows, blk_off, nact.reshape(1)


def kernel(x, mem, positions, norm_mix_g, w_in, ret_norm_g, gm_ln_g, gm_ln_b, gm_ws, gm_bs, w_out, norm_xa_g, norm_mem_g, xa_wq, xa_wkv, xa_wo, norm_moe_g, router_grp_w, router_grp_b, router_exp_w, router_exp_b, moe_w_gate, moe_w_up, moe_w_down, norm_final_g):
    batch, seq, d = x.shape
    mem_len = mem.shape[1]
    n = batch * seq
    ret_width = ret_norm_g.shape[0]
    gm_width = gm_ln_g.shape[0]
    assert ret_width == gm_width and w_in.shape[1] == 4 * ret_width + 2 * gm_width
    dk = ret_width // RET_HEADS
    n_exp = moe_w_gate.shape[0]
    assert n_exp == MOE_GROUPS * MOE_PER_GROUP and MOE_GROUPS + n_exp <= V7X_LANES

    inv_freq = ROPE_BASE ** (-jnp.arange(0, dk, 2, dtype=F32) / dk)
    ang = positions.astype(F32).reshape(n, 1) * inv_freq
    cos, sin = jnp.cos(ang), jnp.sin(ang)

    x2 = x.reshape(n, d)
    xn = rmsnorm_bf16(x2, norm_mix_g, tm=512)
    proj, (w_out_b,) = matmul_wcast(xn, w_in, tm=1024, tn=1024, side=(w_out,))
    ret = retention(proj, cos, sin, ret_norm_g, batch=batch, seq=seq, ret_width=ret_width)
    gm = gmlp(proj, gm_ln_g, gm_ln_b, gm_ws, gm_bs, n_rows=n, gm_width=gm_width,
              u_block=4 * ret_width // gm_width, v_block=4 * ret_width // gm_width + 1)
    h1, (wq_b, wkv_b, wo_b) = matmul_residual([ret, gm], w_out_b, x2, tm=1024, tn=512,
                                              side=(xa_wq, xa_wkv, xa_wo))

    kv, _ = norm_matmul(mem.reshape(batch * mem_len, d), norm_mem_g, wkv_b, tm=512, tn=1024)
    h2 = cross_attention_block(h1, norm_xa_g, wq_b, kv, wo_b, seq=seq, mem_len=mem_len, tm=256)

    pad = V7X_LANES - MOE_GROUPS - n_exp
    wr = jnp.concatenate([router_grp_w, router_exp_w, jnp.zeros((d, pad), F32)], axis=1)
    br = jnp.concatenate([router_grp_b, router_exp_b, jnp.zeros((pad,), F32)]).reshape(1, V7X_LANES)
    wr_hi = wr.astype(BF16)
    wr_lo = (wr - wr_hi.astype(F32)).astype(BF16)
    xp, meta, meta_t, cnt = moe_router(h2, norm_moe_g, jnp.concatenate([wr_hi, wr_lo], axis=1), br, tm=256)

    rb = MOE_ROW_BLOCK
    n_blocks = -(-(n * MOE_TOPK) // rb) + n_exp
    counts = cnt[0, MOE_GROUPS:MOE_GROUPS + n_exp].astype(I32)
    start, blk_exp, blk_next, blk_rows, blk_off, nact = _moe_plan(counts, rb=rb, n_blocks=n_blocks)
    code = (meta_t[MOE_TOPK:2 * MOE_TOPK].astype(I32) * (1 << MOE_CODE_SHIFT)
            + meta_t[0:MOE_TOPK].astype(I32)).reshape(-1)
    order = moe_order(code, start)
    ya = moe_experts(xp, moe_w_gate, moe_w_up, moe_w_down, blk_exp, blk_next, blk_rows, blk_off, order, nact,
                     n_blocks=n_blocks, rb=rb)
    y = moe_combine(h2, meta, norm_final_g, ya, tb=256)
    return y.reshape(batch, seq, d)
```

```python
import functools

import jax
import jax.numpy as jnp
from jax import lax
from jax.experimental import pallas as pl
from jax.experimental.pallas import tpu as pltpu

NORM_EPS = 1e-6
RET_HEADS = 8
ROPE_BASE = 10000.0
GM_GROUPS = 8
GM_CHUNK = 128
XA_HEADS = 4
MOE_GROUPS = 8
MOE_PER_GROUP = 8
MOE_TOPK = 2

V7X_LANES = 128
V7X_VMEM_BYTES = 64 * 1024 * 1024
VMEM_LIMIT_BYTES = 56 * 1024 * 1024
EXPERTS_VMEM_LIMIT_BYTES = 58 * 1024 * 1024

RET_BLOCK = 512
MOE_ROW_BLOCK = 512

F32 = jnp.float32
BF16 = jnp.bfloat16
U32 = jnp.uint32
I32 = jnp.int32


def _params(sem, vmem_limit_bytes=VMEM_LIMIT_BYTES):
    return pltpu.CompilerParams(dimension_semantics=sem, vmem_limit_bytes=vmem_limit_bytes)


def _tile(dim, target):
    t = min(dim, target)
    while dim % t:
        t -= V7X_LANES
    assert t > 0, (dim, target)
    return t


def _pack_halves(x_f32):
    k = x_f32.shape[-1] // 2
    bits = lax.bitcast_convert_type(x_f32.astype(BF16).astype(F32), U32)
    return (bits[:, k:] & jnp.uint32(0xFFFF0000)) | (bits[:, :k] >> 16)


def _unpack_halves(w_u32):
    lo = lax.bitcast_convert_type(w_u32 << 16, F32)
    hi = lax.bitcast_convert_type(w_u32 & jnp.uint32(0xFFFF0000), F32)
    return lo, hi


SIDE_CAST_SLABS = 64
BF16_SUBLANES = 16


def _side_slabs(side, steps):
    n = SIDE_CAST_SLABS
    while n > steps or any(a.shape[0] % (BF16_SUBLANES * n) for a in side):
        n //= 2
    assert n >= 1, [a.shape for a in side]
    return n


def _side_specs(side, n_j, slabs):
    return [pl.BlockSpec((a.shape[0] // slabs, a.shape[1]), lambda i, j: (jnp.minimum(i * n_j + j, slabs - 1), 0))
            for a in side]


def _side_cast(in_refs, out_refs, n_j, slabs):
    if not in_refs:
        return

    @pl.when(pl.program_id(0) * n_j + pl.program_id(1) < slabs)
    def _():
        for src, dst in zip(in_refs, out_refs):
            dst[...] = src[...].astype(dst.dtype)


def _side_out_shapes(side):
    return [jax.ShapeDtypeStruct(a.shape, BF16) for a in side]


def _norm_matmul_body(*refs, n_side, n_j, slabs):
    x_ref, g_ref, w_ref = refs[:3]
    side_in = refs[3:3 + n_side]
    o_ref = refs[3 + n_side]
    side_out = refs[4 + n_side:4 + 2 * n_side]
    xn_ref = refs[4 + 2 * n_side]

    @pl.when(pl.program_id(1) == 0)
    def _():
        x = x_ref[...]
        ms = jnp.mean(x * x, axis=-1, keepdims=True)
        xn_ref[...] = ((x * lax.rsqrt(ms + NORM_EPS)) * g_ref[...]).astype(BF16)

    o_ref[...] = jnp.dot(xn_ref[...], w_ref[...], preferred_element_type=F32).astype(o_ref.dtype)
    _side_cast(side_in, side_out, n_j, slabs)


def norm_matmul(x, g, w, *, tm, tn, side=()):
    m, k = x.shape
    n = w.shape[1]
    tm, tn = _tile(m, tm), _tile(n, tn)
    n_j = n // tn
    slabs = _side_slabs(side, (m // tm) * n_j)
    side_specs = _side_specs(side, n_j, slabs)
    out = pl.pallas_call(
        functools.partial(_norm_matmul_body, n_side=len(side), n_j=n_j, slabs=slabs),
        grid=(m // tm, n_j),
        in_specs=[
            pl.BlockSpec((tm, k), lambda i, j: (i, 0)),
            pl.BlockSpec((1, k), lambda i, j: (0, 0)),
            pl.BlockSpec((k, tn), lambda i, j: (0, j)),
        ] + side_specs,
        out_specs=[pl.BlockSpec((tm, tn), lambda i, j: (i, j))] + side_specs,
        out_shape=[jax.ShapeDtypeStruct((m, n), BF16)] + _side_out_shapes(side),
        scratch_shapes=[pltpu.VMEM((tm, k), BF16)],
        compiler_params=_params(("arbitrary", "arbitrary") if side else ("parallel", "arbitrary")),
        name="norm_matmul",
    )(x, g.reshape(1, k), w, *side)
    return out[0], out[1:]


def _rmsnorm_body(x_ref, g_ref, o_ref):
    x = x_ref[...]
    ms = jnp.mean(x * x, axis=-1, keepdims=True)
    o_ref[...] = ((x * lax.rsqrt(ms + NORM_EPS)) * g_ref[...]).astype(o_ref.dtype)


def rmsnorm_bf16(x, g, *, tm):
    m, k = x.shape
    tm = _tile(m, tm)
    return pl.pallas_call(
        _rmsnorm_body,
        grid=(m // tm,),
        in_specs=[pl.BlockSpec((tm, k), lambda i: (i, 0)), pl.BlockSpec((1, k), lambda i: (0, 0))],
        out_specs=pl.BlockSpec((tm, k), lambda i: (i, 0)),
        out_shape=jax.ShapeDtypeStruct((m, k), BF16),
        compiler_params=_params(("parallel",)),
        name="rmsnorm",
    )(x, g.reshape(1, k))


def _matmul_wcast_body(*refs, n_side, n_i, n_j, slabs, tn):
    a_ref, w_hbm = refs[:2]
    side_in = refs[2:2 + n_side]
    o_ref = refs[2 + n_side]
    side_out = refs[3 + n_side:3 + 2 * n_side]
    wb_ref, ring_ref, sem = refs[3 + 2 * n_side:]
    j = pl.program_id(0)
    i = pl.program_id(1)
    pieces, rows = ring_ref.shape[0], ring_ref.shape[1]

    def piece_copy(col_block, p):
        return pltpu.make_async_copy(
            w_hbm.at[pl.ds(p * rows, rows), pl.ds(pl.multiple_of(col_block * tn, V7X_LANES), tn)],
            ring_ref.at[p], sem.at[p])

    @pl.when(jnp.logical_and(j == 0, i == 0))
    def _():
        for p in range(pieces):
            piece_copy(0, p).start()

    @pl.when(i == 0)
    def _():
        for p in range(pieces):
            piece_copy(j, p).wait()
            wb_ref[pl.ds(p * rows, rows), :] = ring_ref[p].astype(BF16)

    for p in range(pieces):
        @pl.when(jnp.logical_and(i == p * n_i // pieces, j + 1 < n_j))
        def _(p=p):
            piece_copy(j + 1, p).start()

    o_ref[...] = jnp.dot(a_ref[...], wb_ref[...], preferred_element_type=F32).astype(o_ref.dtype)
    _side_cast(side_in, side_out, n_i, slabs)


WCAST_PIECES = 8
WCAST_VMEM_LIMIT_BYTES = 60 * 1024 * 1024


def matmul_wcast(a, w, *, tm, tn, side=()):
    m, k = a.shape
    n = w.shape[1]
    tm, tn = _tile(m, tm), _tile(n, tn)
    n_i, n_j = m // tm, n // tn
    pieces = min(WCAST_PIECES, n_i)
    assert k % pieces == 0 and n_i % pieces == 0
    slabs = _side_slabs(side, n_j * n_i)
    side_specs = _side_specs(side, n_i, slabs)
    out = pl.pallas_call(
        functools.partial(_matmul_wcast_body, n_side=len(side), n_i=n_i, n_j=n_j, slabs=slabs, tn=tn),
        grid=(n_j, n_i),
        in_specs=[pl.BlockSpec((tm, k), lambda j, i: (i, 0)), pl.BlockSpec(memory_space=pl.ANY)] + side_specs,
        out_specs=[pl.BlockSpec((tm, tn), lambda j, i: (i, j))] + side_specs,
        out_shape=[jax.ShapeDtypeStruct((m, n), BF16)] + _side_out_shapes(side),
        scratch_shapes=[pltpu.VMEM((k, tn), BF16), pltpu.VMEM((pieces, k // pieces, tn), F32),
                        pltpu.SemaphoreType.DMA((pieces,))],
        compiler_params=_params(("arbitrary", "arbitrary"), WCAST_VMEM_LIMIT_BYTES),
        name="matmul_wcast",
    )(a, w, *side)
    return out[0], out[1:]


RET_HEADS_PER_STEP = 8


def _retention_body(lg_ref, q_ref, k_ref, v_ref, g_ref, cos_ref, sin_ref, gn_ref, o_ref, state_ref, dec_ref,
                    *, blk, dk, hps):
    c = pl.program_id(2)
    first_of_pair = jnp.logical_and(pl.program_id(1) == 0, c == 0)
    half = dk // 2
    cos = cos_ref[...]
    sin = sin_ref[...]
    pos = lax.broadcasted_iota(I32, (blk, 1), 0).astype(F32)

    def rot(t):
        t1, t2 = t[:, :half], t[:, half:]
        return jnp.concatenate([t1 * cos - t2 * sin, t1 * sin + t2 * cos], axis=-1)

    @pl.when(c == 0)
    def _():
        state_ref[...] = jnp.zeros_like(state_ref)

    for j in range(hps):
        lg = lg_ref[pl.program_id(0) * hps + j]
        cols = slice(j * dk, (j + 1) * dk)

        @pl.when(first_of_pair)
        def _(j=j, lg=lg):
            ri = lax.broadcasted_iota(I32, (blk, blk), 0)
            ci = lax.broadcasted_iota(I32, (blk, blk), 1)
            diff = (ri - ci).astype(F32)
            dec_ref[j] = jnp.where(diff >= 0.0, jnp.exp(jnp.maximum(diff, 0.0) * lg), 0.0)

        qr = rot(q_ref[:, cols].astype(F32))
        kr = rot(k_ref[:, cols].astype(F32)) * (dk ** -0.5)
        v = v_ref[:, cols]

        q_dec = jnp.exp((pos + 1.0) * lg)
        k_dec = jnp.exp((blk - 1.0 - pos) * lg)
        blk_dec = jnp.exp(jnp.full((1, dk), blk * lg, F32))

        s = lax.dot_general(qr.astype(BF16), kr.astype(BF16), (((1,), (1,)), ((), ())), preferred_element_type=F32)
        inner = jnp.dot((s * dec_ref[j]).astype(BF16), v, preferred_element_type=F32)

        state = state_ref[j]
        cross = jnp.dot((qr * q_dec).astype(BF16), state.astype(BF16), preferred_element_type=F32)
        kd_t = jnp.transpose(kr * k_dec).astype(BF16)
        state_ref[j] = state * blk_dec + jnp.dot(kd_t, v, preferred_element_type=F32)

        out = inner + cross
        mu = jnp.mean(out, axis=-1, keepdims=True)
        cen = out - mu
        var = jnp.mean(cen * cen, axis=-1, keepdims=True)
        y = cen * lax.rsqrt(var + NORM_EPS) * gn_ref[:, cols]
        gate = g_ref[:, cols].astype(F32)
        o_ref[:, cols] = (y * (gate * jax.nn.sigmoid(gate))).astype(o_ref.dtype)


def retention(proj, cos, sin, ret_norm_g, *, batch, seq, ret_width):
    heads = RET_HEADS
    hps = min(RET_HEADS_PER_STEP, heads)
    assert heads % hps == 0
    dk = ret_width // heads
    blk = min(RET_BLOCK, seq)
    nblk = seq // blk
    hb = heads // hps
    log_gamma = jnp.log(1.0 - jnp.exp2(-5.0 - jnp.arange(heads, dtype=F32)))

    def col(seg):
        return pl.BlockSpec((blk, hps * dk), lambda h, b, c, lg, seg=seg: (b * nblk + c, seg * hb + h))

    rowspec = pl.BlockSpec((blk, dk // 2), lambda h, b, c, lg: (b * nblk + c, 0))
    return pl.pallas_call(
        functools.partial(_retention_body, blk=blk, dk=dk, hps=hps),
        grid_spec=pltpu.PrefetchScalarGridSpec(
            num_scalar_prefetch=1,
            grid=(heads // hps, batch, nblk),
            in_specs=[col(0), col(1), col(2), col(3), rowspec, rowspec,
                      pl.BlockSpec((1, hps * dk), lambda h, b, c, lg: (0, h))],
            out_specs=pl.BlockSpec((blk, hps * dk), lambda h, b, c, lg: (b * nblk + c, h)),
            scratch_shapes=[pltpu.VMEM((hps, dk, dk), F32), pltpu.VMEM((hps, blk, blk), F32)],
        ),
        out_shape=jax.ShapeDtypeStruct((batch * seq, ret_width), BF16),
        compiler_params=_params(("arbitrary", "arbitrary", "arbitrary")),
        name="retention",
    )(log_gamma, proj, proj, proj, proj, cos, sin, ret_norm_g.reshape(1, ret_width))


def _gmlp_body(u_ref, v_ref, lng_ref, lnb_ref, ws_ref, bst_ref, o_ref, vn_ref, *, rows, groups, cg, chunk):
    v = jax.nn.gelu(v_ref[...].astype(F32))
    mu = jnp.mean(v, axis=-1, keepdims=True)
    cen = v - mu
    var = jnp.mean(cen * cen, axis=-1, keepdims=True)
    vn_ref[...] = (cen * lax.rsqrt(var + NORM_EPS) * lng_ref[...] + lnb_ref[...]).astype(BF16)

    ri = lax.broadcasted_iota(I32, (chunk, chunk), 0)
    ci = lax.broadcasted_iota(I32, (chunk, chunk), 1)
    causal = ri >= ci
    for g in range(groups):
        w = jnp.where(causal, ws_ref[g], 0.0).astype(BF16)
        bias = bst_ref[:, g:g + 1]
        cols = slice(g * cg, (g + 1) * cg)
        for t in range(rows // chunk):
            rws = slice(t * chunk, (t + 1) * chunk)
            sp = jnp.dot(w, vn_ref[rws, cols], preferred_element_type=F32) + bias
            u = jax.nn.gelu(u_ref[rws, cols].astype(F32))
            o_ref[rws, cols] = (u * sp).astype(o_ref.dtype)


def gmlp(proj, ln_g, ln_b, ws, bs, *, n_rows, gm_width, u_block, v_block):
    groups, chunk = GM_GROUPS, GM_CHUNK
    cg = gm_width // groups
    rows = 2 * chunk
    return pl.pallas_call(
        functools.partial(_gmlp_body, rows=rows, groups=groups, cg=cg, chunk=chunk),
        grid=(n_rows // rows,),
        in_specs=[
            pl.BlockSpec((rows, gm_width), lambda i: (i, u_block)),
            pl.BlockSpec((rows, gm_width), lambda i: (i, v_block)),
            pl.BlockSpec((1, gm_width), lambda i: (0, 0)),
            pl.BlockSpec((1, gm_width), lambda i: (0, 0)),
            pl.BlockSpec((groups, chunk, chunk), lambda i: (0, 0, 0)),
            pl.BlockSpec((chunk, groups), lambda i: (0, 0)),
        ],
        out_specs=pl.BlockSpec((rows, gm_width), lambda i: (i, 0)),
        out_shape=jax.ShapeDtypeStruct((n_rows, gm_width), BF16),
        scratch_shapes=[pltpu.VMEM((rows, gm_width), BF16)],
        compiler_params=_params(("parallel",)),
        name="gmlp",
    )(proj, proj, ln_g.reshape(1, gm_width), ln_b.reshape(1, gm_width), ws, bs.T)


def _matmul_residual_body(*refs, n_parts, n_side, n_j, slabs):
    a_refs, w_refs = refs[:n_parts], refs[n_parts:2 * n_parts]
    r_ref = refs[2 * n_parts]
    side_in = refs[2 * n_parts + 1:2 * n_parts + 1 + n_side]
    o_ref = refs[2 * n_parts + 1 + n_side]
    side_out = refs[2 * n_parts + 2 + n_side:]
    acc = r_ref[...]
    for a_ref, w_ref in zip(a_refs, w_refs):
        acc = acc + jnp.dot(a_ref[...], w_ref[...], preferred_element_type=F32)
    o_ref[...] = acc
    _side_cast(side_in, side_out, n_j, slabs)


def matmul_residual(parts, w, res, *, tm, tn, side=()):
    m, n = res.shape
    tm, tn = _tile(m, tm), _tile(n, tn)
    n_j = n // tn
    slabs = _side_slabs(side, (m // tm) * n_j)
    kp = parts[0].shape[1]
    n_parts = len(parts)
    a_specs = [pl.BlockSpec((tm, kp), lambda i, j: (i, 0)) for _ in parts]
    w_specs = [pl.BlockSpec((kp, tn), lambda i, j, p=p: (p, j)) for p in range(n_parts)]
    side_specs = _side_specs(side, n_j, slabs)
    out = pl.pallas_call(
        functools.partial(_matmul_residual_body, n_parts=n_parts, n_side=len(side), n_j=n_j, slabs=slabs),
        grid=(m // tm, n_j),
        in_specs=a_specs + w_specs + [pl.BlockSpec((tm, tn), lambda i, j: (i, j))] + side_specs,
        out_specs=[pl.BlockSpec((tm, tn), lambda i, j: (i, j))] + side_specs,
        out_shape=[jax.ShapeDtypeStruct((m, n), F32)] + _side_out_shapes(side),
        compiler_params=_params(("arbitrary", "arbitrary") if side else ("parallel", "parallel")),
        name="matmul_residual",
    )(*parts, *([w] * n_parts), res, *side)
    return out[0], out[1:]


def _cross_attention_body(h_ref, g_ref, wq_ref, kv_ref, wo_ref, o_ref, *, heads, dh):
    width = heads * dh
    x = h_ref[...]
    ms = jnp.mean(x * x, axis=-1, keepdims=True)
    xn = ((x * lax.rsqrt(ms + NORM_EPS)) * g_ref[...]).astype(BF16)
    q = jnp.dot(xn, wq_ref[...], preferred_element_type=F32).astype(BF16)
    outs = []
    for h in range(heads):
        k = kv_ref[:, h * dh:(h + 1) * dh]
        v = kv_ref[:, width + h * dh:width + (h + 1) * dh]
        s = lax.dot_general(q[:, h * dh:(h + 1) * dh], k, (((1,), (1,)), ((), ())),
                            preferred_element_type=F32) * (dh ** -0.5)
        e = jnp.exp(s - jnp.max(s, axis=-1, keepdims=True))
        p = e / jnp.sum(e, axis=-1, keepdims=True)
        outs.append(jnp.dot(p.astype(BF16), v, preferred_element_type=F32).astype(BF16))
    o = jnp.concatenate(outs, axis=-1)
    o_ref[...] = x + jnp.dot(o, wo_ref[...], preferred_element_type=F32)


def cross_attention_block(h, g, wq, kv, wo, *, seq, mem_len, tm):
    n, d = h.shape
    heads = XA_HEADS
    width = wq.shape[1]
    dh = width // heads
    tm = _tile(seq, tm)
    per_batch = seq // tm

    def resident(shape):
        return pl.BlockSpec(shape, lambda i: (0, 0), pipeline_mode=pl.Buffered(1))

    return pl.pallas_call(
        functools.partial(_cross_attention_body, heads=heads, dh=dh),
        grid=(n // tm,),
        in_specs=[
            pl.BlockSpec((tm, d), lambda i: (i, 0)),
            resident((1, d)),
            resident((d, width)),
            pl.BlockSpec((mem_len, 2 * width), lambda i: (i // per_batch, 0)),
            resident((width, d)),
        ],
        out_specs=pl.BlockSpec((tm, d), lambda i: (i, 0)),
        out_shape=jax.ShapeDtypeStruct((n, d), F32),
        compiler_params=_params(("parallel",)),
        name="cross_attention",
    )(h, g.reshape(1, d), wq, kv, wo)


def _router_body(h_ref, g_ref, wr_ref, br_ref, xp_ref, meta_ref, meta_t_ref, cnt_ref, base_ref, *, tm, n_grp, per):
    @pl.when(pl.program_id(0) == 0)
    def _():
        base_ref[...] = jnp.zeros_like(base_ref)

    x = h_ref[...]
    ms = jnp.mean(x * x, axis=-1, keepdims=True)
    xn = (x * lax.rsqrt(ms + NORM_EPS)) * g_ref[...]
    xp_ref[...] = _pack_halves(xn)

    x_hi = xn.astype(BF16)
    x_lo = (xn - x_hi.astype(F32)).astype(BF16)
    both = jnp.dot(x_hi, wr_ref[...], preferred_element_type=F32)
    corr = jnp.dot(x_lo, wr_ref[:, :V7X_LANES], preferred_element_type=F32)
    logits = both[:, :V7X_LANES] + (both[:, V7X_LANES:] + corr) + br_ref[...]
    lane = lax.broadcasted_iota(I32, logits.shape, 1)
    neg = jnp.float32(-1e30)
    big = jnp.int32(V7X_LANES)

    gl = jnp.where(lane < n_grp, logits, neg)
    gmax = jnp.max(gl, axis=-1, keepdims=True)
    gidx = jnp.min(jnp.where(gl == gmax, lane, big), axis=-1, keepdims=True)
    grp_gate = 1.0 / jnp.sum(jnp.exp(gl - gmax), axis=-1, keepdims=True)

    lo = n_grp + gidx * per
    el = jnp.where((lane >= lo) & (lane < lo + per), logits, neg)
    v1 = jnp.max(el, axis=-1, keepdims=True)
    i1 = jnp.min(jnp.where(el == v1, lane, big), axis=-1, keepdims=True)
    el2 = jnp.where(lane == i1, neg, el)
    v2 = jnp.max(el2, axis=-1, keepdims=True)
    i2 = jnp.min(jnp.where(el2 == v2, lane, big), axis=-1, keepdims=True)
    t = jnp.exp(v2 - v1)
    den = 1.0 + t
    g1 = grp_gate / den
    g2 = grp_gate * (t / den)

    oh1 = jnp.where(lane == i1, 1.0, 0.0)
    oh2 = jnp.where(lane == i2, 1.0, 0.0)
    ri = lax.broadcasted_iota(I32, (tm, tm), 0)
    ci = lax.broadcasted_iota(I32, (tm, tm), 1)
    lower = jnp.where(ri > ci, 1.0, 0.0).astype(BF16)
    pre1 = jnp.dot(lower, oh1.astype(BF16), preferred_element_type=F32)
    pre2 = jnp.dot(lower, oh2.astype(BF16), preferred_element_type=F32)
    cnt1 = jnp.sum(oh1, axis=0, keepdims=True)
    cnt2 = jnp.sum(oh2, axis=0, keepdims=True)
    base = base_ref[...]
    rank1 = jnp.sum(oh1 * (pre1 + base), axis=-1, keepdims=True)
    rank2 = jnp.sum(oh2 * (pre2 + base + cnt1), axis=-1, keepdims=True)
    total = base + cnt1 + cnt2
    base_ref[...] = total
    cnt_ref[...] = total

    e1 = (i1 - n_grp).astype(F32)
    e2 = (i2 - n_grp).astype(F32)
    meta = jnp.zeros(logits.shape, F32)
    for idx, val in enumerate((e1, e2, rank1, rank2, g1, g2)):
        meta = jnp.where(lane == idx, val, meta)
    meta_ref[...] = meta
    meta_t_ref[...] = jnp.transpose(meta)[:META_ROWS, :]


META_ROWS = 8


def moe_router(h, g, wr, br, *, tm):
    n, d = h.shape
    tm = min(tm, n)
    return pl.pallas_call(
        functools.partial(_router_body, tm=tm, n_grp=MOE_GROUPS, per=MOE_PER_GROUP),
        grid=(n // tm,),
        in_specs=[
            pl.BlockSpec((tm, d), lambda i: (i, 0)),
            pl.BlockSpec((1, d), lambda i: (0, 0)),
            pl.BlockSpec((d, 2 * V7X_LANES), lambda i: (0, 0)),
            pl.BlockSpec((1, V7X_LANES), lambda i: (0, 0)),
        ],
        out_specs=[
            pl.BlockSpec((tm, d // 2), lambda i: (i, 0)),
            pl.BlockSpec((tm, V7X_LANES), lambda i: (i, 0)),
            pl.BlockSpec((META_ROWS, tm), lambda i: (0, i)),
            pl.BlockSpec((1, V7X_LANES), lambda i: (0, 0)),
        ],
        out_shape=[
            jax.ShapeDtypeStruct((n, d // 2), U32),
            jax.ShapeDtypeStruct((n, V7X_LANES), F32),
            jax.ShapeDtypeStruct((META_ROWS, n), F32),
            jax.ShapeDtypeStruct((1, V7X_LANES), F32),
        ],
        scratch_shapes=[pltpu.VMEM((1, V7X_LANES), F32)],
        compiler_params=_params(("arbitrary",)),
        name="moe_router",
    )(h, g.reshape(1, d), wr, br)


MOE_CODE_SHIFT = 6
assert MOE_GROUPS * MOE_PER_GROUP == 1 << MOE_CODE_SHIFT


def _order_body(code_ref, start_ref, order_ref):
    def body(a, carry):
        code = code_ref[a]
        order_ref[start_ref[code & ((1 << MOE_CODE_SHIFT) - 1)] + (code >> MOE_CODE_SHIFT)] = a
        return carry

    lax.fori_loop(0, code_ref.shape[0], body, 0, unroll=8)


def moe_order(code, start):
    return pl.pallas_call(
        _order_body,
        grid_spec=pltpu.PrefetchScalarGridSpec(
            num_scalar_prefetch=2,
            grid=(1,),
            in_specs=[],
            out_specs=pl.BlockSpec(memory_space=pltpu.SMEM),
        ),
        out_shape=jax.ShapeDtypeStruct(code.shape, I32),
        compiler_params=_params(("arbitrary",)),
        name="moe_order",
    )(code, start)


EXPERT_STEPS = 1
ROW_DMA_UNROLL = 8
WEIGHT_DMA_PRIORITY = 1
assert MOE_TOPK == 2


def _experts_body(exp_ref, nxt_ref, rows_ref, off_ref, order_ref, nact_ref,
                  xp_hbm, wg_hbm, wu_hbm, wdn_hbm, ya_hbm,
                  xbuf_ref, ybuf_ref, hd_ref, wa_ref, wb_ref, wc_ref, wd_ref,
                  ring_in, ring_dn, gsem, ssem, sem_in, sem_dn, *, rb, n_tok, n_blocks):
    v = pl.program_id(0)
    s = pl.program_id(1)
    nact = nact_ref[0]
    active = v < nact
    slot = lax.rem(v, 2)
    half = rb // 2
    oc = ring_dn.shape[2]
    dq = ring_in.shape[1]

    def in_copy(e, i):
        src = (wg_hbm, wg_hbm, wu_hbm, wu_hbm)[i % 4]
        q = 2 * (i // 4) + i % 2
        return pltpu.make_async_copy(src.at[e, pl.ds(q * dq, dq), :], ring_in.at[i], sem_in.at[i])

    def dn_copy(e, i):
        col = (i % 2) * 2 * oc + (i // 2) * oc
        return pltpu.make_async_copy(wdn_hbm.at[e, :, pl.ds(col, oc)], ring_dn.at[i], sem_dn.at[i])

    def refill(copy, slots):
        @pl.when(v + 1 < nact)
        def _():
            for i in slots:
                copy(nxt_ref[v], i).start(priority=WEIGHT_DMA_PRIORITY)

    def decode(base, r):
        a = order_ref[base + r]
        k = jnp.where(a >= n_tok, 1, 0)
        return k, a - k * n_tok

    def gather_copy(base, buf, r):
        _, tok = decode(base, r)
        return pltpu.make_async_copy(xp_hbm.at[pl.ds(tok, 1)], xbuf_ref.at[buf, pl.ds(r, 1)], gsem.at[buf])

    def scatter_copy(base, r):
        k, tok = decode(base, r)
        return pltpu.make_async_copy(ybuf_ref.at[pl.ds(r, 1)], ya_hbm.at[k, pl.ds(tok, 1)], ssem)

    def for_rows(lo, hi, fn):
        groups = lax.shift_right_logical(jnp.maximum(hi - lo, 0), ROW_DMA_UNROLL.bit_length() - 1)

        def group(g, carry):
            for u in range(ROW_DMA_UNROLL):
                fn(lo + g * ROW_DMA_UNROLL + u)
            return carry

        def single(r, carry):
            fn(r)
            return carry

        lax.fori_loop(0, groups, group, 0)
        lax.fori_loop(lo + groups * ROW_DMA_UNROLL, hi, single, 0)

    @pl.when(jnp.logical_and(v == 0, s == 0))
    def _():
        for i in range(ring_in.shape[0]):
            in_copy(exp_ref[0], i).start(priority=WEIGHT_DMA_PRIORITY)
        for i in range(ring_dn.shape[0]):
            dn_copy(exp_ref[0], i).start(priority=WEIGHT_DMA_PRIORITY)
        xbuf_ref[...] = jnp.zeros_like(xbuf_ref)
        base = off_ref[0]
        for_rows(0, rows_ref[0], lambda r: gather_copy(base, 0, r).start())

    @pl.when(jnp.logical_and(active, s == 0))
    def _():
        base = off_ref[v]
        for_rows(0, rows_ref[v], lambda r: gather_copy(base, slot, r).wait())

    @pl.when(jnp.logical_and(active, v + 1 < nact))
    def _():
        nxt = jnp.minimum(v + 1, n_blocks - 1)
        base = off_ref[nxt]
        per_step = rb // EXPERT_STEPS
        for_rows(s * per_step, jnp.minimum((s + 1) * per_step, rows_ref[nxt]),
                 lambda r: gather_copy(base, 1 - slot, r).start())

    halves = [(pl.ds(0, half), None), (pl.ds(half, half), rows_ref[v] > half)]

    def for_halves(fn):
        for rows, cond in halves:
            if cond is None:
                fn(rows)
            else:
                pl.when(cond)(functools.partial(fn, rows))

    @pl.when(jnp.logical_and(active, s == 0))
    def _():
        slots = range(ring_in.shape[0])
        for i in slots:
            in_copy(exp_ref[v], i).wait()
        for i in slots:
            dst = (wa_ref, wa_ref, wb_ref, wb_ref)[i % 4]
            dst[pl.ds((2 * (i // 4) + i % 2) * dq, dq), :] = ring_in[i].astype(BF16)
        refill(in_copy, slots)

        def gate_up(rows):
            lo, hi = _unpack_halves(xbuf_ref[slot, rows, :])
            xl, xh = lo.astype(BF16), hi.astype(BF16)
            kh = xl.shape[1]
            hg = (jnp.dot(xl, wa_ref[:kh, :], preferred_element_type=F32)
                  + jnp.dot(xh, wa_ref[kh:, :], preferred_element_type=F32))
            hu = (jnp.dot(xl, wb_ref[:kh, :], preferred_element_type=F32)
                  + jnp.dot(xh, wb_ref[kh:, :], preferred_element_type=F32))
            hd_ref[rows, :] = (hg * jax.nn.sigmoid(hg) * hu).astype(BF16)
        for_halves(gate_up)

    @pl.when(jnp.logical_and(active, jnp.logical_and(s == EXPERT_STEPS - 1, v >= 1)))
    def _():
        prev = jnp.maximum(v - 1, 0)
        base = off_ref[prev]
        for_rows(0, rows_ref[prev], lambda r: scatter_copy(base, r).wait())

    @pl.when(jnp.logical_and(active, s == EXPERT_STEPS - 1))
    def _():
        slots = range(ring_dn.shape[0])
        for i in slots:
            dn_copy(exp_ref[v], i).wait()
        for i in slots:
            (wc_ref, wd_ref)[i % 2][:, pl.ds((i // 2) * oc, oc)] = ring_dn[i].astype(BF16)
        refill(dn_copy, slots)

        def down(rows):
            hd = hd_ref[rows, :]
            for j in range(ring_dn.shape[0] // 2):
                cols = pl.ds(j * oc, oc)
                y_lo = jnp.dot(hd, wc_ref[:, cols], preferred_element_type=F32)
                y_hi = jnp.dot(hd, wd_ref[:, cols], preferred_element_type=F32)
                ybuf_ref[rows, cols] = _pack_halves(jnp.concatenate([y_lo, y_hi], axis=-1))
        for_halves(down)
        base = off_ref[v]
        for_rows(0, rows_ref[v], lambda r: scatter_copy(base, r).start())

    @pl.when(jnp.logical_and(v == n_blocks - 1, s == EXPERT_STEPS - 1))
    def _():
        last = jnp.maximum(nact - 1, 0)
        base = off_ref[last]
        for_rows(0, rows_ref[last], lambda r: scatter_copy(base, r).wait())


def moe_experts(xp, w_gate, w_up, w_down, blk_exp, blk_next, blk_rows, blk_off, order, nact, *, n_blocks, rb):
    n_exp, d, ff = w_gate.shape
    n_tok, kw = xp.shape
    assert d == 2 * kw
    oc = kw // 2
    n_in, n_dn = 8, 4
    any_spec = pl.BlockSpec(memory_space=pl.ANY)
    return pl.pallas_call(
        functools.partial(_experts_body, rb=rb, n_tok=n_tok, n_blocks=n_blocks),
        grid_spec=pltpu.PrefetchScalarGridSpec(
            num_scalar_prefetch=6,
            grid=(n_blocks, EXPERT_STEPS),
            in_specs=[any_spec, any_spec, any_spec, any_spec],
            out_specs=any_spec,
            scratch_shapes=[pltpu.VMEM((2, rb, kw), U32), pltpu.VMEM((rb, kw), U32),
                            pltpu.VMEM((rb, ff), BF16),
                            pltpu.VMEM((d, ff), BF16), pltpu.VMEM((d, ff), BF16),
                            pltpu.VMEM((ff, kw), BF16), pltpu.VMEM((ff, kw), BF16),
                            pltpu.VMEM((n_in, d // 4, ff), F32), pltpu.VMEM((n_dn, ff, oc), F32),
                            pltpu.SemaphoreType.DMA((2,)), pltpu.SemaphoreType.DMA(()),
                            pltpu.SemaphoreType.DMA((n_in,)), pltpu.SemaphoreType.DMA((n_dn,))],
        ),
        out_shape=jax.ShapeDtypeStruct((MOE_TOPK, n_tok, kw), U32),
        compiler_params=_params(("arbitrary", "arbitrary"), EXPERTS_VMEM_LIMIT_BYTES),
        name="moe_experts",
    )(blk_exp, blk_next, blk_rows, blk_off, order, nact, xp, w_gate, w_up, w_down)


def _combine_body(h_ref, meta_ref, g_ref, y0_ref, y1_ref, o_ref):
    meta = meta_ref[...]
    acc_lo = None
    acc_hi = None
    for k, y_ref in enumerate((y0_ref, y1_ref)):
        gate = meta[:, 2 * MOE_TOPK + k:2 * MOE_TOPK + k + 1]
        lo, hi = _unpack_halves(y_ref[0])
        lo, hi = lo * gate, hi * gate
        acc_lo = lo if acc_lo is None else acc_lo + lo
        acc_hi = hi if acc_hi is None else acc_hi + hi
    kh = acc_lo.shape[1]
    h_lo = h_ref[:, :kh] + acc_lo
    h_hi = h_ref[:, kh:] + acc_hi
    ms = (jnp.sum(h_lo * h_lo, axis=-1, keepdims=True) + jnp.sum(h_hi * h_hi, axis=-1, keepdims=True)) / (2 * kh)
    scale = lax.rsqrt(ms + NORM_EPS)
    o_ref[:, :kh] = (h_lo * scale) * g_ref[:, :kh]
    o_ref[:, kh:] = (h_hi * scale) * g_ref[:, kh:]


def moe_combine(h, meta, g, ya, *, tb):
    n, d = h.shape
    kw = ya.shape[2]
    tb = _tile(n, tb)

    def slot(k):
        return pl.BlockSpec((1, tb, kw), lambda i, k=k: (k, i, 0))

    return pl.pallas_call(
        _combine_body,
        grid=(n // tb,),
        in_specs=[
            pl.BlockSpec((tb, d), lambda i: (i, 0)),
            pl.BlockSpec((tb, V7X_LANES), lambda i: (i, 0)),
            pl.BlockSpec((1, d), lambda i: (0, 0)),
            slot(0), slot(1),
        ],
        out_specs=pl.BlockSpec((tb, d), lambda i: (i, 0)),
        out_shape=jax.ShapeDtypeStruct((n, d), F32),
        compiler_params=_params(("parallel",)),
        name="moe_combine",
    )(h, meta, g.reshape(1, d), ya, ya)


def _moe_plan(counts, *, rb, n_blocks):
    n_exp = counts.shape[0]
    per_exp = (counts + rb - 1) // rb
    blk_end = jnp.cumsum(per_exp)
    start = jnp.cumsum(counts) - counts
    nact = blk_end[-1].astype(I32)
    blk = jnp.arange(n_blocks, dtype=I32)
    blk_exp = jnp.minimum(jnp.sum(blk_end[None, :] <= blk[:, None], axis=1), n_exp - 1).astype(I32)
    active = blk < nact
    last = jnp.maximum(nact - 1, 0)
    blk_exp = jnp.where(active, blk_exp, blk_exp[last]).astype(I32)
    blk_next = blk_exp[jnp.minimum(blk + 1, last)]
    j = blk - (blk_end[blk_exp] - per_exp[blk_exp])
    blk_off = jnp.where(active, start[blk_exp] + j * rb, 0).astype(I32)
    blk_rows = jnp.where(active, jnp.clip(counts[blk_exp] - j * rb, 0, rb), 0).astype(I32)
    return start.astype(I32), blk_exp, blk_next, blk_rows, blk_off, nact.reshape(1)


def kernel(x, mem, positions, norm_mix_g, w_in, ret_norm_g, gm_ln_g, gm_ln_b, gm_ws, gm_bs, w_out, norm_xa_g, norm_mem_g, xa_wq, xa_wkv, xa_wo, norm_moe_g, router_grp_w, router_grp_b, router_exp_w, router_exp_b, moe_w_gate, moe_w_up, moe_w_down, norm_final_g):
    batch, seq, d = x.shape
    mem_len = mem.shape[1]
    n = batch * seq
    ret_width = ret_norm_g.shape[0]
    gm_width = gm_ln_g.shape[0]
    assert ret_width == gm_width and w_in.shape[1] == 4 * ret_width + 2 * gm_width
    dk = ret_width // RET_HEADS
    n_exp = moe_w_gate.shape[0]
    assert n_exp == MOE_GROUPS * MOE_PER_GROUP and MOE_GROUPS + n_exp <= V7X_LANES

    inv_freq = ROPE_BASE ** (-jnp.arange(0, dk, 2, dtype=F32) / dk)
    ang = positions.astype(F32).reshape(n, 1) * inv_freq
    cos, sin = jnp.cos(ang), jnp.sin(ang)

    x2 = x.reshape(n, d)
    xn = rmsnorm_bf16(x2, norm_mix_g, tm=512)
    proj, (w_out_b,) = matmul_wcast(xn, w_in, tm=1024, tn=1024, side=(w_out,))
    ret = retention(proj, cos, sin, ret_norm_g, batch=batch, seq=seq, ret_width=ret_width)
    gm = gmlp(proj, gm_ln_g, gm_ln_b, gm_ws, gm_bs, n_rows=n, gm_width=gm_width,
              u_block=4 * ret_width // gm_width, v_block=4 * ret_width // gm_width + 1)
    h1, (wq_b, wkv_b, wo_b) = matmul_residual([ret, gm], w_out_b, x2, tm=1024, tn=512,
                                              side=(xa_wq, xa_wkv, xa_wo))

    kv, _ = norm_matmul(mem.reshape(batch * mem_len, d), norm_mem_g, wkv_b, tm=512, tn=1024)
    h2 = cross_attention_block(h1, norm_xa_g, wq_b, kv, wo_b, seq=seq, mem_len=mem_len, tm=256)

    pad = V7X_LANES - MOE_GROUPS - n_exp
    wr = jnp.concatenate([router_grp_w, router_exp_w, jnp.zeros((d, pad), F32)], axis=1)
    br = jnp.concatenate([router_grp_b, router_exp_b, jnp.zeros((pad,), F32)]).reshape(1, V7X_LANES)
    wr_hi = wr.astype(BF16)
    wr_lo = (wr - wr_hi.astype(F32)).astype(BF16)
    xp, meta, meta_t, cnt = moe_router(h2, norm_moe_g, jnp.concatenate([wr_hi, wr_lo], axis=1), br, tm=256)

    rb = MOE_ROW_BLOCK
    n_blocks = -(-(n * MOE_TOPK) // rb) + n_exp
    counts = cnt[0, MOE_GROUPS:MOE_GROUPS + n_exp].astype(I32)
    start, blk_exp, blk_next, blk_rows, blk_off, nact = _moe_plan(counts, rb=rb, n_blocks=n_blocks)
    code = (meta_t[MOE_TOPK:2 * MOE_TOPK].astype(I32) * (1 << MOE_CODE_SHIFT)
            + meta_t[0:MOE_TOPK].astype(I32)).reshape(-1)
    order = moe_order(code, start)
    ya = moe_experts(xp, moe_w_gate, moe_w_up, moe_w_down, blk_exp, blk_next, blk_rows, blk_off, order, nact,
                     n_blocks=n_blocks, rb=rb)
    y = moe_combine(h2, meta, norm_final_g, ya, tb=256)
    return y.reshape(batch, seq, d)
```
